```python
import jax
import jax.numpy as jnp
from jax import lax
import numpy as np

D_MODEL = 1024
BATCH = 1
SEQ = 16384
DEPTH = 4

GRID_W = 64
CTX_LEN = 256
N_MIXERS = 4
N_MOD = 6

FN_GROUPS = 4

HEAD_DIM = 64
FA_Q_HEADS = 16
FA_KV_HEADS = 4
WA_Q_HEADS = 16
WA_KV_HEADS = 2
WINDOW = 128
Q_BLOCK = 128
ROPE_THETA = 10000.0

GM_CHUNK = 128
GM_DFFN = 4 * D_MODEL
GM_GROUPS = 8

N_EXPERTS = 32
TOP_K = 4
EXPERT_FF = D_MODEL
SWIGLU_LIMIT = 7.0
SWIGLU_ALPHA = 1.702
MOE_BLOCK = 128

LN_EPS = 1e-5
RMS_EPS = 1e-6
NEG_INF = -1e30
DEEPNORM_ALPHA = (2 * DEPTH) ** 0.25
DEEPNORM_BETA = (8 * DEPTH) ** -0.25

N_FN = len(range(0, DEPTH, N_MIXERS))
N_FA = len(range(1, DEPTH, N_MIXERS))
N_GM = len(range(2, DEPTH, N_MIXERS))
N_WA = len(range(3, DEPTH, N_MIXERS))

kernel_name = 'hybrid_dit_fourier_gqa_gmlp_swa_moe'


def _layer_norm(x, g, b):
    xf = x.astype(jnp.float32)
    mu = jnp.mean(xf, axis=-1, keepdims=True)
    var = jnp.mean(jnp.square(xf - mu), axis=-1, keepdims=True)
    y = (xf - mu) * lax.rsqrt(var + LN_EPS) * g.astype(jnp.float32) + b.astype(jnp.float32)
    return y.astype(x.dtype)


def _rms_norm(x, g):
    xf = x.astype(jnp.float32)
    y = xf * lax.rsqrt(jnp.mean(xf * xf, axis=-1, keepdims=True) + RMS_EPS) * g.astype(jnp.float32)
    return y.astype(x.dtype)


def _modulate(x, shift, scale):
    return x * (1 + scale) + shift


def _axial_rope_tables(rows):
    row = jnp.repeat(jnp.arange(rows, dtype=jnp.float32), GRID_W)
    col = jnp.tile(jnp.arange(GRID_W, dtype=jnp.float32), rows)
    n_freq = HEAD_DIM // 4
    inv = ROPE_THETA ** (-jnp.arange(n_freq, dtype=jnp.float32) / n_freq)
    ang = jnp.concatenate([row[:, None] * inv, col[:, None] * inv], axis=-1)
    return jnp.cos(ang), jnp.sin(ang)


def _apply_rope(x, cos, sin):
    xf = x.astype(jnp.float32)
    x1, x2 = xf[..., 0::2], xf[..., 1::2]
    out = jnp.stack([x1 * cos - x2 * sin, x1 * sin + x2 * cos], axis=-1).reshape(xf.shape)
    return out.astype(x.dtype)


def _qkv(h, w_qkv, b_qkv, n_q, n_kv):
    bsz, n, _ = h.shape
    grp = n_q // n_kv
    qkv = h @ w_qkv + b_qkv
    q, k, v = jnp.split(qkv, [n_q * HEAD_DIM, (n_q + n_kv) * HEAD_DIM], axis=-1)
    q = q.reshape(bsz, n, n_kv, grp, HEAD_DIM).transpose(0, 2, 3, 1, 4)
    k = k.reshape(bsz, n, n_kv, HEAD_DIM).transpose(0, 2, 1, 3)
    v = v.reshape(bsz, n, n_kv, HEAD_DIM).transpose(0, 2, 1, 3)
    return q, k, v


def _merge_heads(o):
    bsz, hk, grp, n, dh = o.shape
    return o.transpose(0, 3, 1, 2, 4).reshape(bsz, n, hk * grp * dh)


def _gqa_softmax(q, k, v, mask=None, sink=None):
    s = jnp.einsum('bhgqd,bhkd->bhgqk', q, k, preferred_element_type=jnp.float32) * (HEAD_DIM ** -0.5)
    if mask is not None:
        s = jnp.where(mask, s, NEG_INF)
    if sink is not None:
        sk = jnp.broadcast_to(sink.astype(jnp.float32)[None, :, :, None, None], s.shape[:-1] + (1,))
        s = jnp.concatenate([s, sk], axis=-1)
    p = jax.nn.softmax(s, axis=-1)
    if sink is not None:
        p = p[..., :-1]
    return jnp.einsum('bhgqk,bhkd->bhgqd', p.astype(v.dtype), v)


def _fourier_mixer(h_lat, h_ctx, w_out, b_out, with_ctx):
    def run(h):
        bsz, n, d = h.shape
        hg = h.astype(jnp.float32).reshape(bsz, n, FN_GROUPS, d // FN_GROUPS)
        mixed = jnp.fft.fft2(hg, axes=(1, 3), norm='ortho').real.astype(h.dtype).reshape(bsz, n, d)
        return mixed @ w_out + b_out
    return run(h_lat), (run(h_ctx) if with_ctx else None)


def _full_attention_mixer(h_lat, h_ctx, cos, sin, w_qkv, b_qkv, q_norm, k_norm, w_out, b_out, with_ctx):
    bsz, n, _ = h_lat.shape
    grp = FA_Q_HEADS // FA_KV_HEADS
    nb = n // Q_BLOCK
    q, k, v = _qkv(h_lat, w_qkv, b_qkv, FA_Q_HEADS, FA_KV_HEADS)
    q = _apply_rope(_rms_norm(q, q_norm), cos, sin)
    k = _apply_rope(_rms_norm(k, k_norm), cos, sin)
    qc, kc, vc = _qkv(h_ctx, w_qkv, b_qkv, FA_Q_HEADS, FA_KV_HEADS)
    qc = _rms_norm(qc, q_norm)
    kc = _rms_norm(kc, k_norm)
    k_all = jnp.concatenate([k, kc], axis=2)
    v_all = jnp.concatenate([v, vc], axis=2)
    q_blocks = jnp.moveaxis(q.reshape(bsz, FA_KV_HEADS, grp, nb, Q_BLOCK, HEAD_DIM), 3, 0)
    o = lax.map(lambda qb: _gqa_softmax(qb, k_all, v_all), q_blocks)
    o = jnp.moveaxis(o, 0, 3).reshape(bsz, FA_KV_HEADS, grp, n, HEAD_DIM)
    y_lat = _merge_heads(o) @ w_out + b_out
    y_ctx = (_merge_heads(_gqa_softmax(qc, kc, vc)) @ w_out + b_out) if with_ctx else None
    return y_lat, y_ctx


def _gmlp_mixer(h_lat, h_ctx, w_in, b_in, v_norm_g, v_norm_b, w_s, b_s, w_out, b_out, with_ctx):
    half = GM_DFFN // 2
    def run(h):
        bsz, n, _ = h.shape
        z = jax.nn.gelu(h @ w_in + b_in, approximate=False)
        u, v = jnp.split(z, 2, axis=-1)
        v = _layer_norm(v, v_norm_g, v_norm_b)
        v = v.reshape(bsz, n // GM_CHUNK, GM_CHUNK, GM_GROUPS, half // GM_GROUPS)
        v = jnp.einsum('hpq,bnqhc->bnphc', w_s, v) + b_s.T[None, None, :, :, None]
        return (u * v.reshape(bsz, n, half)) @ w_out + b_out
    return run(h_lat), (run(h_ctx) if with_ctx else None)


def _window_attention_mixer(h_lat, h_ctx, cos, sin, w_qkv, b_qkv, sink, w_out, b_out, with_ctx):
    bsz, n, _ = h_lat.shape
    grp = WA_Q_HEADS // WA_KV_HEADS
    nb = n // Q_BLOCK
    span = Q_BLOCK + 2 * WINDOW
    q, k, v = _qkv(h_lat, w_qkv, b_qkv, WA_Q_HEADS, WA_KV_HEADS)
    q = _apply_rope(q, cos, sin)
    k = _apply_rope(k, cos, sin)
    qc, kc, vc = _qkv(h_ctx, w_qkv, b_qkv, WA_Q_HEADS, WA_KV_HEADS)
    sink_hg = sink.reshape(WA_KV_HEADS, grp)
    pad = ((0, 0), (0, 0), (WINDOW, WINDOW), (0, 0))
    k_pad = jnp.pad(k, pad)
    v_pad = jnp.pad(v, pad)
    qi = jnp.arange(Q_BLOCK)[:, None]
    kj = jnp.arange(span)[None, :]
    band = jnp.abs(kj - WINDOW - qi) <= WINDOW
    ctx_ok = jnp.ones((Q_BLOCK, kc.shape[2]), dtype=bool)
    q_blocks = jnp.moveaxis(q.reshape(bsz, WA_KV_HEADS, grp, nb, Q_BLOCK, HEAD_DIM), 3, 0)

    def block(args):
        qb, b_idx = args
        start = b_idx * Q_BLOCK
        kw = lax.dynamic_slice_in_dim(k_pad, start, span, axis=2)
        vw = lax.dynamic_slice_in_dim(v_pad, start, span, axis=2)
        kpos = start - WINDOW + jnp.arange(span)
        valid = band & ((kpos >= 0) & (kpos < n))[None, :]
        mask = jnp.concatenate([valid, ctx_ok], axis=-1)
        return _gqa_softmax(qb, jnp.concatenate([kw, kc], axis=2), jnp.concatenate([vw, vc], axis=2), mask, sink_hg)

    o = lax.map(block, (q_blocks, jnp.arange(nb)))
    o = jnp.moveaxis(o, 0, 3).reshape(bsz, WA_KV_HEADS, grp, n, HEAD_DIM)
    y_lat = _merge_heads(o) @ w_out + b_out
    y_ctx = (_merge_heads(_gqa_softmax(qc, kc, vc, None, sink_hg)) @ w_out + b_out) if with_ctx else None
    return y_lat, y_ctx


def _moe(h, router_w, router_b, w_gate_up, b_gate_up, w_down, b_down):
    n_tok, d = h.shape
    logits = (h @ router_w + router_b).astype(jnp.float32)
    top_val, top_idx = lax.top_k(logits, TOP_K)
    gates = jax.nn.softmax(top_val, axis=-1)
    n_assign = n_tok * TOP_K
    n_blocks = -(-n_assign // MOE_BLOCK) + N_EXPERTS
    n_slots = n_blocks * MOE_BLOCK
    flat_e = top_idx.reshape(-1)
    flat_tok = jnp.repeat(jnp.arange(n_tok, dtype=jnp.int32), TOP_K)
    flat_g = gates.reshape(-1)
    order = jnp.argsort(flat_e, stable=True)
    e_sorted = flat_e[order]
    counts = jnp.bincount(flat_e, length=N_EXPERTS)
    padded = (counts + MOE_BLOCK - 1) // MOE_BLOCK * MOE_BLOCK
    ends_pad = jnp.cumsum(padded)
    starts = jnp.cumsum(counts) - counts
    dest = (ends_pad - padded)[e_sorted] + jnp.arange(n_assign) - starts[e_sorted]
    slot_tok = jnp.full((n_slots,), n_tok, dtype=jnp.int32).at[dest].set(flat_tok[order])
    slot_gate = jnp.zeros((n_slots,), dtype=jnp.float32).at[dest].set(flat_g[order])
    block_expert = jnp.minimum(jnp.searchsorted(ends_pad, jnp.arange(n_blocks) * MOE_BLOCK, side='right'), N_EXPERTS - 1)
    h_pad = jnp.concatenate([h, jnp.zeros((1, d), h.dtype)], axis=0)
    xb = h_pad[slot_tok].reshape(n_blocks, MOE_BLOCK, d)

    def expert_block(args):
        xe, e = args
        gu = xe @ w_gate_up[e] + b_gate_up[e]
        x_glu = jnp.minimum(gu[:, 0::2], SWIGLU_LIMIT)
        x_lin = jnp.clip(gu[:, 1::2], -SWIGLU_LIMIT, SWIGLU_LIMIT)
        act = x_glu * jax.nn.sigmoid(SWIGLU_ALPHA * x_glu) * (x_lin + 1)
        return act @ w_down[e] + b_down[e]

    yb = lax.map(expert_block, (xb, block_expert)).reshape(n_slots, d)
    y = jax.ops.segment_sum(yb * slot_gate[:, None].astype(yb.dtype), slot_tok, num_segments=n_tok + 1)
    return y[:n_tok]


def setup_inputs(seed: int = 0) -> dict:
    key = jax.random.key(seed)
    keys = iter(jax.random.split(key, 48))

    def nrm(shape, scale):
        return jax.random.normal(next(keys), shape, jnp.float32) * scale

    d = D_MODEL
    fa_qkv = (FA_Q_HEADS + 2 * FA_KV_HEADS) * HEAD_DIM
    wa_qkv = (WA_Q_HEADS + 2 * WA_KV_HEADS) * HEAD_DIM
    gm_half = GM_DFFN // 2
    return {
        'x': nrm((BATCH, SEQ, d), 1.0),
        'c': nrm((BATCH, d), 1.0),
        'ctx': nrm((BATCH, CTX_LEN, d), 1.0),
        'c_ctx': nrm((d,), 1.0),
        'ada_w': nrm((DEPTH, d, N_MOD * d), d ** -0.5),
        'ada_b': nrm((DEPTH, N_MOD * d), 0.02),
        'ln_mix_g': 1.0 + nrm((DEPTH, d), 0.02),
        'ln_mix_b': nrm((DEPTH, d), 0.02),
        'ln_ffn_g': 1.0 + nrm((DEPTH, d), 0.02),
        'ln_ffn_b': nrm((DEPTH, d), 0.02),
        'fn_w_out': nrm((N_FN, d, d), DEEPNORM_BETA * d ** -0.5),
        'fn_b_out': nrm((N_FN, d), 0.02),
        'fa_w_qkv': nrm((N_FA, d, fa_qkv), d ** -0.5),
        'fa_b_qkv': nrm((N_FA, fa_qkv), 0.02),
        'fa_q_norm': 1.0 + nrm((N_FA, HEAD_DIM), 0.02),
        'fa_k_norm': 1.0 + nrm((N_FA, HEAD_DIM), 0.02),
        'fa_w_out': nrm((N_FA, FA_Q_HEADS * HEAD_DIM, d), DEEPNORM_BETA * (FA_Q_HEADS * HEAD_DIM) ** -0.5),
        'fa_b_out': nrm((N_FA, d), 0.02),
        'gm_w_in': nrm((N_GM, d, GM_DFFN), d ** -0.5),
        'gm_b_in': nrm((N_GM, GM_DFFN), 0.02),
        'gm_v_norm_g': 1.0 + nrm((N_GM, gm_half), 0.02),
        'gm_v_norm_b': nrm((N_GM, gm_half), 0.02),
        'gm_w_s': nrm((N_GM, GM_GROUPS, GM_CHUNK, GM_CHUNK), GM_CHUNK ** -0.5),
        'gm_b_s': 1.0 + nrm((N_GM, GM_GROUPS, GM_CHUNK), 0.02),
        'gm_w_out': nrm((N_GM, gm_half, d), DEEPNORM_BETA * gm_half ** -0.5),
        'gm_b_out': nrm((N_GM, d), 0.02),
        'wa_w_qkv': nrm((N_WA, d, wa_qkv), d ** -0.5),
        'wa_b_qkv': nrm((N_WA, wa_qkv), 0.02),
        'wa_sink': nrm((N_WA, WA_Q_HEADS), 0.5),
        'wa_w_out': nrm((N_WA, WA_Q_HEADS * HEAD_DIM, d), DEEPNORM_BETA * (WA_Q_HEADS * HEAD_DIM) ** -0.5),
        'wa_b_out': nrm((N_WA, d), 0.02),
        'router_w': nrm((DEPTH, d, N_EXPERTS), d ** -0.5),
        'router_b': nrm((DEPTH, N_EXPERTS), 0.01),
        'exp_w_gate_up': nrm((DEPTH, N_EXPERTS, d, 2 * EXPERT_FF), d ** -0.5),
        'exp_b_gate_up': nrm((DEPTH, N_EXPERTS, 2 * EXPERT_FF), 0.02),
        'exp_w_down': nrm((DEPTH, N_EXPERTS, EXPERT_FF, d), DEEPNORM_BETA * EXPERT_FF ** -0.5),
        'exp_b_down': nrm((DEPTH, N_EXPERTS, d), 0.02),
    }


def reference(x, c, ctx, c_ctx, ada_w, ada_b, ln_mix_g, ln_mix_b, ln_ffn_g, ln_ffn_b,
              fn_w_out, fn_b_out,
              fa_w_qkv, fa_b_qkv, fa_q_norm, fa_k_norm, fa_w_out, fa_b_out,
              gm_w_in, gm_b_in, gm_v_norm_g, gm_v_norm_b, gm_w_s, gm_b_s, gm_w_out, gm_b_out,
              wa_w_qkv, wa_b_qkv, wa_sink, wa_w_out, wa_b_out,
              router_w, router_b, exp_w_gate_up, exp_b_gate_up, exp_w_down, exp_b_down):
    bsz, n_lat, d = x.shape
    n_ctx = ctx.shape[1]
    rows = n_lat // GRID_W
    cos, sin = _axial_rope_tables(rows)
    silu_c = jax.nn.silu(c)
    silu_cc = jax.nn.silu(c_ctx)
    for i in range(DEPTH):
        kind, j = i % N_MIXERS, i // N_MIXERS
        with_ctx = i < DEPTH - 1
        sh1, sc1, g1, sh2, sc2, g2 = jnp.split((silu_c @ ada_w[i] + ada_b[i])[:, None, :], N_MOD, axis=-1)
        sh1c, sc1c, g1c, sh2c, sc2c, g2c = jnp.split((silu_cc @ ada_w[i] + ada_b[i])[None, None, :], N_MOD, axis=-1)
        h_lat = _modulate(x, sh1, sc1)
        h_ctx = _modulate(ctx, sh1c, sc1c)
        if kind == 0:
            y_lat, y_ctx = _fourier_mixer(h_lat, h_ctx, fn_w_out[j], fn_b_out[j], with_ctx)
        elif kind == 1:
            y_lat, y_ctx = _full_attention_mixer(h_lat, h_ctx, cos, sin, fa_w_qkv[j], fa_b_qkv[j], fa_q_norm[j],
                                                 fa_k_norm[j], fa_w_out[j], fa_b_out[j], with_ctx)
        elif kind == 2:
            y_lat, y_ctx = _gmlp_mixer(h_lat, h_ctx, gm_w_in[j], gm_b_in[j], gm_v_norm_g[j], gm_v_norm_b[j],
                                       gm_w_s[j], gm_b_s[j], gm_w_out[j], gm_b_out[j], with_ctx)
        else:
            y_lat, y_ctx = _window_attention_mixer(h_lat, h_ctx, cos, sin, wa_w_qkv[j], wa_b_qkv[j], wa_sink[j],
                                                   wa_w_out[j], wa_b_out[j], with_ctx)
        x = _layer_norm(DEEPNORM_ALPHA * x + g1 * y_lat, ln_mix_g[i], ln_mix_b[i])
        h_lat = _modulate(x, sh2, sc2)
        moe_args = (router_w[i], router_b[i], exp_w_gate_up[i], exp_b_gate_up[i], exp_w_down[i], exp_b_down[i])
        if with_ctx:
            ctx = _layer_norm(DEEPNORM_ALPHA * ctx + g1c * y_ctx, ln_mix_g[i], ln_mix_b[i])
            h_ctx = _modulate(ctx, sh2c, sc2c)
            tok = jnp.concatenate([h_ctx, h_lat], axis=1).reshape(-1, d)
            f = _moe(tok, *moe_args).reshape(bsz, n_ctx + n_lat, d)
            f_ctx, f_lat = f[:, :n_ctx], f[:, n_ctx:]
            ctx = _layer_norm(DEEPNORM_ALPHA * ctx + g2c * f_ctx, ln_ffn_g[i], ln_ffn_b[i])
        else:
            f_lat = _moe(h_lat.reshape(-1, d), *moe_args).reshape(bsz, n_lat, d)
        x = _layer_norm(DEEPNORM_ALPHA * x + g2 * f_lat, ln_ffn_g[i], ln_ffn_b[i])
    return x
```

```python
import functools
import math

import numpy as np
import jax
import jax.numpy as jnp
from jax import lax
from jax.experimental import pallas as pl
from jax.experimental.pallas import tpu as pltpu

F32, BF16, I32 = jnp.float32, jnp.bfloat16, jnp.int32

D_MODEL = 1024
DEPTH = 4
GRID_W = 64
N_MOD = 6
FN_GROUPS = 4
HEAD_DIM = 64
FA_Q_HEADS, FA_KV_HEADS = 16, 4
WA_Q_HEADS, WA_KV_HEADS = 16, 2
WINDOW = 128
Q_BLOCK = 128
ROPE_THETA = 10000.0
GM_CHUNK = 128
GM_GROUPS = 8
N_EXPERTS = 32
TOP_K = 4
SWIGLU_LIMIT = 7.0
SWIGLU_ALPHA = 1.702
MOE_BLOCK = 128
LN_EPS = 1e-5
RMS_EPS = 1e-6
NEG = -1e30
ALPHA = (2 * DEPTH) ** 0.25
LOG2E = math.log2(math.e)
Q_SCALE = HEAD_DIM ** -0.5 * LOG2E

LANE = 128
TM = 256
FLASH_TQ = 256
FLASH_TK = 1280
VMEM_LIMIT = 56 * 2 ** 20


def _params(*sem):
    return pltpu.CompilerParams(dimension_semantics=sem, vmem_limit_bytes=VMEM_LIMIT)


def _dot(a, b):
    return jnp.dot(a, b, preferred_element_type=F32)


def _split(a):
    hi = a.astype(BF16)
    lo = (a - hi.astype(F32)).astype(BF16)
    return hi, lo


def _dot3(a_hi, a_lo, b_hi, b_lo):
    return _dot(a_hi, b_hi) + (_dot(a_hi, b_lo) + _dot(a_lo, b_hi))


def _layer_norm(x, g, b):
    mu = jnp.mean(x, axis=-1, keepdims=True)
    xc = x - mu
    var = jnp.mean(xc * xc, axis=-1, keepdims=True)
    return xc * lax.rsqrt(var + LN_EPS) * g + b


def _top4(logits):
    lane = lax.broadcasted_iota(I32, logits.shape, 1).astype(F32)
    cur = logits
    vals, idxs = [], []
    for _ in range(TOP_K):
        m = jnp.max(cur, axis=-1, keepdims=True)
        i = jnp.min(jnp.where(cur == m, lane, float(LANE)), axis=-1, keepdims=True)
        vals.append(m)
        idxs.append(i)
        cur = jnp.where(lane == i, -jnp.inf, cur)
    exps = [jnp.exp(v - vals[0]) for v in vals]
    inv = 1.0 / (exps[0] + exps[1] + exps[2] + exps[3])
    idx_out = jnp.zeros_like(logits)
    gate_out = jnp.zeros_like(logits)
    for k in range(TOP_K):
        idx_out = jnp.where(lane == float(k), idxs[k], idx_out)
        gate_out = jnp.where(lane == float(k), exps[k] * inv, gate_out)
    return idx_out.astype(I32), gate_out


def _post(y, x, mod, lng, lnb, rw_hi, rw_lo, rb):
    x1 = _layer_norm(ALPHA * x + mod[2:3] * y, lng, lnb)
    h2 = x1 * (1.0 + mod[4:5]) + mod[3:4]
    hh, hl = _split(h2)
    logits = _dot3(hh, hl, rw_hi, rw_lo) + rb
    idx, gates = _top4(logits)
    return x1, h2, idx, gates


def _stack_heads(qb, grp):
    return jnp.concatenate([qb[:, g * HEAD_DIM:(g + 1) * HEAD_DIM] for g in range(grp)], axis=0)


def _unstack_heads(o, grp):
    t = o.shape[0] // grp
    return jnp.concatenate([o[g * t:(g + 1) * t] for g in range(grp)], axis=1)


def _ada_kernel(cs_ref, w_ref, b_ref, o_ref):
    cs = cs_ref[...]
    s = cs * (1.0 / (1.0 + jnp.exp(-cs)))
    sh, sl = _split(s)
    wh, wl = _split(w_ref[0])
    o_ref[0] = _dot3(sh, sl, wh, wl) + b_ref[0]


def _ada(c, c_ctx, ada_w, ada_b):
    d = c.shape[-1]
    nm = ada_w.shape[-1]
    tn = nm // 4
    cs = jnp.zeros((8, d), F32).at[0].set(c[0]).at[1].set(c_ctx)
    out = pl.pallas_call(
        _ada_kernel,
        grid=(DEPTH, nm // tn),
        in_specs=[
            pl.BlockSpec((8, d), lambda i, j: (0, 0)),
            pl.BlockSpec((1, d, tn), lambda i, j: (i, 0, j)),
            pl.BlockSpec((1, 1, tn), lambda i, j: (i, 0, j)),
        ],
        out_specs=pl.BlockSpec((1, 8, tn), lambda i, j: (i, 0, j)),
        out_shape=jax.ShapeDtypeStruct((DEPTH, 8, nm), F32),
        compiler_params=_params("arbitrary", "arbitrary"),
        name="ada",
    )(cs, ada_w, ada_b.reshape(DEPTH, 1, nm))
    return out[:, :2].reshape(DEPTH, 2, N_MOD, d)


def _tile_specs(n_lat_tiles):
    tok = pl.BlockSpec((TM, D_MODEL), lambda t: (t, 0))
    mod = pl.BlockSpec((1, N_MOD, D_MODEL), lambda t: (jnp.where(t >= n_lat_tiles, 1, 0), 0, 0))
    return tok, mod


def _full(shape):
    nd = len(shape)
    return pl.BlockSpec(shape, lambda *_: (0,) * nd)


def _post_out(n_rows):
    shapes = (
        jax.ShapeDtypeStruct((n_rows, D_MODEL), F32),
        jax.ShapeDtypeStruct((n_rows, D_MODEL), F32),
        jax.ShapeDtypeStruct((n_rows, LANE), I32),
        jax.ShapeDtypeStruct((n_rows, LANE), F32),
    )
    specs = (
        pl.BlockSpec((TM, D_MODEL), lambda t: (t, 0)),
        pl.BlockSpec((TM, D_MODEL), lambda t: (t, 0)),
        pl.BlockSpec((TM, LANE), lambda t: (t, 0)),
        pl.BlockSpec((TM, LANE), lambda t: (t, 0)),
    )
    return shapes, specs


def _router_operands(router_w, router_b):
    rw = jnp.zeros((D_MODEL, LANE), F32).at[:, :N_EXPERTS].set(router_w)
    rw_hi = rw.astype(BF16)
    rw_lo = (rw - rw_hi.astype(F32)).astype(BF16)
    rb = jnp.full((1, LANE), NEG, F32).at[0, :N_EXPERTS].set(router_b)
    return rw_hi, rw_lo, rb


def _proj_post_kernel(a_ref, w_ref, b_ref, x_ref, mod_ref, lng_ref, lnb_ref, rwh_ref, rwl_ref, rb_ref,
                      x1_ref, h2_ref, idx_ref, gate_ref):
    y = _dot(a_ref[...], w_ref[...]) + b_ref[...]
    x1, h2, idx, gates = _post(y, x_ref[...], mod_ref[0], lng_ref[...], lnb_ref[...],
                               rwh_ref[...], rwl_ref[...], rb_ref[...])
    x1_ref[...] = x1
    h2_ref[...] = h2
    idx_ref[...] = idx
    gate_ref[...] = gates


def _proj_post(a, w_out, b_out, x, mods, lng, lnb, router, n_rows, n_lat_tiles):
    k = a.shape[1]
    tok, mod = _tile_specs(n_lat_tiles)
    shapes, specs = _post_out(n_rows)
    return pl.pallas_call(
        _proj_post_kernel,
        grid=(n_rows // TM,),
        in_specs=[
            pl.BlockSpec((TM, k), lambda t: (t, 0)),
            _full((k, D_MODEL)), _full((1, D_MODEL)),
            tok, mod, _full((1, D_MODEL)), _full((1, D_MODEL)),
            _full((D_MODEL, LANE)), _full((D_MODEL, LANE)), _full((1, LANE)),
        ],
        out_specs=specs,
        out_shape=shapes,
        compiler_params=_params("arbitrary"),
        name="proj_post",
    )(a, w_out.astype(BF16), b_out.reshape(1, -1), x, mods, lng.reshape(1, -1), lnb.reshape(1, -1), *router)


def _dft_mats(n):
    jk = np.outer(np.arange(n), np.arange(n)) % n
    ang = 2.0 * np.pi * jk / n
    out = []
    for m in (np.cos(ang), np.sin(ang)):
        m32 = jnp.asarray(m, F32)
        hi = m32.astype(BF16)
        out += [hi, (m32 - hi.astype(F32)).astype(BF16)]
    return out


def _channel_dft(h, cc, sc):
    cw = cc[0].shape[0]
    a_parts, b_parts = [], []
    for g in range(h.shape[1] // cw):
        hh, hl = _split(h[:, g * cw:(g + 1) * cw])
        a_parts.append(_dot3(hh, hl, cc[0][...], cc[1][...]))
        b_parts.append(_dot3(hh, hl, sc[0][...], sc[1][...]))
    return jnp.concatenate(a_parts, axis=1), jnp.concatenate(b_parts, axis=1)


def _fourier1_kernel(x_ref, mod_ref, cch_ref, ccl_ref, sch_ref, scl_ref, tc_ref, ts_ref, ur_ref, ui_ref):
    mod = mod_ref[0]
    h = x_ref[...] * (1.0 + mod[1:2]) + mod[0:1]
    a, b = _channel_dft(h, (cch_ref, ccl_ref), (sch_ref, scl_ref))
    tch, tcl = _split(tc_ref[0])
    tsh, tsl = _split(ts_ref[0])
    ah, al = _split(a)
    bh, bl = _split(b)
    ur_ref[...] = _dot3(tch, tcl, ah, al) - _dot3(tsh, tsl, bh, bl)
    ui_ref[...] = -(_dot3(tch, tcl, bh, bl) + _dot3(tsh, tsl, ah, al))


def _fourier2_kernel(ur_ref, ui_ref, c2h_ref, c2l_ref, s2h_ref, s2l_ref, w_ref, b_ref, x_ref, mod_ref,
                     lng_ref, lnb_ref, rwh_ref, rwl_ref, rb_ref, x1_ref, h2_ref, idx_ref, gate_ref, *, norm):
    urh, url = _split(ur_ref[0])
    uih, uil = _split(ui_ref[0])
    mixed = (_dot3(c2h_ref[...], c2l_ref[...], urh, url) + _dot3(s2h_ref[...], s2l_ref[...], uih, uil)) * norm
    y = _dot(mixed.astype(BF16), w_ref[...]) + b_ref[...]
    x1, h2, idx, gates = _post(y, x_ref[...], mod_ref[0], lng_ref[...], lnb_ref[...],
                               rwh_ref[...], rwl_ref[...], rb_ref[...])
    x1_ref[...] = x1
    h2_ref[...] = h2
    idx_ref[...] = idx
    gate_ref[...] = gates


def _fourier_ctx_kernel(x_ref, mod_ref, cch_ref, ccl_ref, sch_ref, scl_ref, cnh_ref, cnl_ref, snh_ref, snl_ref,
                        w_ref, b_ref, lng_ref, lnb_ref, rwh_ref, rwl_ref, rb_ref,
                        x1_ref, h2_ref, idx_ref, gate_ref, *, norm):
    mod = mod_ref[0]
    x = x_ref[...]
    h = x * (1.0 + mod[1:2]) + mod[0:1]
    a, b = _channel_dft(h, (cch_ref, ccl_ref), (sch_ref, scl_ref))
    ah, al = _split(a)
    bh, bl = _split(b)
    mixed = (_dot3(cnh_ref[...], cnl_ref[...], ah, al) - _dot3(snh_ref[...], snl_ref[...], bh, bl)) * norm
    y = _dot(mixed.astype(BF16), w_ref[...]) + b_ref[...]
    x1, h2, idx, gates = _post(y, x, mod, lng_ref[...], lnb_ref[...], rwh_ref[...], rwl_ref[...], rb_ref[...])
    x1_ref[...] = x1
    h2_ref[...] = h2
    idx_ref[...] = idx
    gate_ref[...] = gates


def _fourier_layer(x_all, mods, w_out, b_out, lng, lnb, router, n_lat, n_ctx):
    t_all, d = x_all.shape
    n2 = LANE
    n1 = n_lat // n2
    cw = d // FN_GROUPS
    rows = t_all // n2
    xv = x_all.reshape(rows, n2 * d)
    cmat = _dft_mats(cw)
    w_bf = w_out.astype(BF16)
    b2 = b_out.reshape(1, d)
    lng2, lnb2 = lng.reshape(1, d), lnb.reshape(1, d)

    k1 = jnp.arange(n1, dtype=I32)
    pos = jnp.arange(n1, dtype=I32)[None, None, :] * n2 + jnp.arange(n2, dtype=I32)[:, None, None]
    ang = ((k1[None, :, None] * pos) % n_lat).astype(F32) * (2.0 * math.pi / n_lat)
    tc, ts = jnp.cos(ang), jnp.sin(ang)

    mat = _full((cw, cw))
    ur, ui = pl.pallas_call(
        _fourier1_kernel,
        grid=(n2,),
        in_specs=[
            pl.BlockSpec((n1, d), lambda j: (0, j)),
            pl.BlockSpec((1, N_MOD, d), lambda j: (0, 0, 0)),
            mat, mat, mat, mat,
            pl.BlockSpec((1, n1, n1), lambda j: (j, 0, 0)),
            pl.BlockSpec((1, n1, n1), lambda j: (j, 0, 0)),
        ],
        out_specs=(pl.BlockSpec((n1, d), lambda j: (0, j)), pl.BlockSpec((n1, d), lambda j: (0, j))),
        out_shape=(jax.ShapeDtypeStruct((n1, n2 * d), F32), jax.ShapeDtypeStruct((n1, n2 * d), F32)),
        compiler_params=_params("arbitrary"),
        name="fourier1",
    )(xv, mods, *cmat, tc, ts)

    m2 = _dft_mats(n2)
    mat2 = _full((n2, n2))
    norm = 1.0 / math.sqrt(n_lat * cw)
    out_shapes = (
        jax.ShapeDtypeStruct((n1, n2 * d), F32),
        jax.ShapeDtypeStruct((n1, n2 * d), F32),
        jax.ShapeDtypeStruct((n1, n2 * LANE), I32),
        jax.ShapeDtypeStruct((n1, n2 * LANE), F32),
    )
    strided = pl.BlockSpec((n2, d), lambda k: (0, k))
    strided_l = pl.BlockSpec((n2, LANE), lambda k: (0, k))
    outs = pl.pallas_call(
        functools.partial(_fourier2_kernel, norm=norm),
        grid=(n1,),
        in_specs=[
            pl.BlockSpec((1, n2, d), lambda k: (k, 0, 0)),
            pl.BlockSpec((1, n2, d), lambda k: (k, 0, 0)),
            mat2, mat2, mat2, mat2,
            _full((d, d)), _full((1, d)),
            strided,
            pl.BlockSpec((1, N_MOD, d), lambda k: (0, 0, 0)),
            _full((1, d)), _full((1, d)),
            _full((d, LANE)), _full((d, LANE)), _full((1, LANE)),
        ],
        out_specs=(strided, strided, strided_l, strided_l),
        out_shape=out_shapes,
        compiler_params=_params("arbitrary"),
        name="fourier2",
    )(ur.reshape(n1, n2, d), ui.reshape(n1, n2, d), *m2, w_bf, b2, xv, mods, lng2, lnb2, *router)
    lat = (outs[0].reshape(n_lat, d), outs[1].reshape(n_lat, d),
           outs[2].reshape(n_lat, LANE), outs[3].reshape(n_lat, LANE))

    cn = _dft_mats(n_ctx)
    matn = _full((n_ctx, n_ctx))
    shapes = (
        jax.ShapeDtypeStruct((n_ctx, d), F32), jax.ShapeDtypeStruct((n_ctx, d), F32),
        jax.ShapeDtypeStruct((n_ctx, LANE), I32), jax.ShapeDtypeStruct((n_ctx, LANE), F32),
    )
    ctx = pl.pallas_call(
        functools.partial(_fourier_ctx_kernel, norm=1.0 / math.sqrt(n_ctx * cw)),
        grid=(1,),
        in_specs=[
            pl.BlockSpec((n_ctx, d), lambda i: (n_lat // n_ctx, 0)),
            pl.BlockSpec((1, N_MOD, d), lambda i: (1, 0, 0)),
            mat, mat, mat, mat, matn, matn, matn, matn,
            _full((d, d)), _full((1, d)), _full((1, d)), _full((1, d)),
            _full((d, LANE)), _full((d, LANE)), _full((1, LANE)),
        ],
        out_specs=(_full((n_ctx, d)), _full((n_ctx, d)), _full((n_ctx, LANE)), _full((n_ctx, LANE))),
        out_shape=shapes,
        compiler_params=_params("arbitrary"),
        name="fourier_ctx",
    )(x_all, mods, *cmat, *cn, w_bf, b2, lng2, lnb2, *router)
    return tuple(jnp.concatenate([a, b], axis=0) for a, b in zip(lat, ctx))


def _qkv_kernel(x_ref, mod_ref, w_ref, b_ref, cos_ref, sin_ref, *rest, n_qk, rms):
    if rms:
        gain_ref, ind_ref, indt_ref, q_ref, k_ref, v_ref = rest
    else:
        q_ref, k_ref, v_ref = rest
    mod = mod_ref[0]
    h = x_ref[...] * (1.0 + mod[1:2]) + mod[0:1]
    y = _dot(h.astype(BF16), w_ref[...]) + b_ref[...]
    qk = y[:, :n_qk]
    if rms:
        sh, sl = _split(qk * qk)
        ms = _dot(sh, ind_ref[...]) + _dot(sl, ind_ref[...])
        mh, ml = _split(ms)
        msb = _dot(mh, indt_ref[...]) + _dot(ml, indt_ref[...])
        qk = qk * lax.rsqrt(msb + RMS_EPS) * gain_ref[...]
    cos = cos_ref[...]
    sin = sin_ref[...]
    even = (lax.broadcasted_iota(I32, cos.shape, 1) & 1) == 0
    parts = []
    for c in range(n_qk // LANE):
        z = qk[:, c * LANE:(c + 1) * LANE]
        swapped = jnp.where(even, pltpu.roll(z, LANE - 1, 1), pltpu.roll(z, 1, 1))
        parts.append(z * cos + swapped * sin)
    nq = q_ref.shape[1]
    q_ref[...] = (jnp.concatenate(parts[:nq // LANE], axis=1) * Q_SCALE).astype(BF16)
    k_ref[...] = jnp.concatenate(parts[nq // LANE:], axis=1).astype(BF16)
    v_ref[...] = y[:, n_qk:].astype(BF16)


def _qkv(x_all, mods, w_qkv, b_qkv, cos_t, sin_t, n_q, n_kv, n_lat_tiles, q_norm=None, k_norm=None):
    t_all, d = x_all.shape
    nq, nk = n_q * HEAD_DIM, n_kv * HEAD_DIM
    n_qk, n_all = nq + nk, nq + 2 * nk
    rms = q_norm is not None
    tok, mod = _tile_specs(n_lat_tiles)
    in_specs = [tok, mod, _full((d, n_all)), _full((1, n_all)),
                pl.BlockSpec((TM, LANE), lambda t: (t, 0)), pl.BlockSpec((TM, LANE), lambda t: (t, 0))]
    args = [x_all, mods, w_qkv.astype(BF16), b_qkv.reshape(1, n_all), cos_t, sin_t]
    if rms:
        gain = jnp.concatenate([jnp.tile(q_norm, n_q), jnp.tile(k_norm, n_kv)]).reshape(1, n_qk)
        head = np.arange(n_qk) // HEAD_DIM
        ind = np.zeros((n_qk, LANE), np.float32)
        ind[np.arange(n_qk), head] = 1.0 / HEAD_DIM
        indt = np.zeros((LANE, n_qk), np.float32)
        indt[head, np.arange(n_qk)] = 1.0
        in_specs += [_full((1, n_qk)), _full((n_qk, LANE)), _full((LANE, n_qk))]
        args += [gain, jnp.asarray(ind, BF16), jnp.asarray(indt, BF16)]
    return pl.pallas_call(
        functools.partial(_qkv_kernel, n_qk=n_qk, rms=rms),
        grid=(t_all // TM,),
        in_specs=in_specs,
        out_specs=(pl.BlockSpec((TM, nq), lambda t: (t, 0)), pl.BlockSpec((TM, nk), lambda t: (t, 0)),
                   pl.BlockSpec((TM, nk), lambda t: (t, 0))),
        out_shape=(jax.ShapeDtypeStruct((t_all, nq), BF16), jax.ShapeDtypeStruct((t_all, nk), BF16),
                   jax.ShapeDtypeStruct((t_all, nk), BF16)),
        compiler_params=_params("arbitrary"),
        name="qkv",
    )(*args)


def _rope_tables(n_lat, n_ctx):
    rows = n_lat // GRID_W
    row = jnp.repeat(jnp.arange(rows, dtype=F32), GRID_W)
    col = jnp.tile(jnp.arange(GRID_W, dtype=F32), rows)
    n_freq = HEAD_DIM // 4
    inv = ROPE_THETA ** (-jnp.arange(n_freq, dtype=F32) / n_freq)
    ang = jnp.concatenate([row[:, None] * inv, col[:, None] * inv], axis=-1)
    ang = jnp.concatenate([ang, jnp.zeros((n_ctx, HEAD_DIM // 2), F32)], axis=0)
    cos = jnp.tile(jnp.repeat(jnp.cos(ang), 2, axis=1), (1, LANE // HEAD_DIM))
    sin = jnp.tile(jnp.repeat(jnp.sin(ang), 2, axis=1), (1, LANE // HEAD_DIM))
    sign = jnp.where(jnp.arange(LANE) % 2 == 0, -1.0, 1.0).astype(F32)
    return cos, sin * sign


def _kv_layouts(k, v, n_kv):
    t_all = k.shape[0]
    kt = k.reshape(t_all, n_kv, HEAD_DIM).transpose(1, 2, 0)
    vh = v.reshape(t_all, n_kv, HEAD_DIM).transpose(1, 0, 2)
    pad = jnp.zeros((n_kv, t_all, LANE - HEAD_DIM), BF16).at[:, :, 0].set(1.0)
    return kt, jnp.concatenate([vh, pad], axis=-1)


def _flash_kernel(q_ref, kt_ref, v_ref, o_ref, *, grp, tk, n_lat, n_ctx):
    qs = _stack_heads(q_ref[...], grp)
    rows = qs.shape[0]
    is_ctx = pl.program_id(1) == n_lat // FLASH_TQ

    def finish(acc):
        o = acc[:, :HEAD_DIM] / acc[:, HEAD_DIM:HEAD_DIM + 1]
        o_ref[...] = _unstack_heads(o, grp).astype(BF16)

    @pl.when(jnp.logical_not(is_ctx))
    def _():
        def body(j, carry):
            m, acc = carry
            off = pl.multiple_of(j * tk, tk)
            s = _dot(qs, kt_ref[0, :, pl.ds(off, tk)])
            m_new = jnp.maximum(m, jnp.max(s, axis=-1, keepdims=True))
            p = jnp.exp2(s - m_new)
            acc = jnp.exp2(m - m_new) * acc + _dot(p.astype(BF16), v_ref[0, pl.ds(off, tk), :])
            return m_new, acc

        m0 = jnp.full((rows, 1), NEG, F32)
        acc0 = jnp.zeros((rows, LANE), F32)
        finish(lax.fori_loop(0, (n_lat + n_ctx) // tk, body, (m0, acc0))[1])

    @pl.when(is_ctx)
    def _():
        s = _dot(qs, kt_ref[0, :, n_lat:n_lat + n_ctx])
        p = jnp.exp2(s - jnp.max(s, axis=-1, keepdims=True))
        finish(_dot(p.astype(BF16), v_ref[0, n_lat:n_lat + n_ctx, :]))


def _full_attention(q, kt, vx, n_lat, n_ctx):
    t_all, nq = q.shape
    n_kv = kt.shape[0]
    grp = nq // HEAD_DIM // n_kv
    gw = grp * HEAD_DIM
    assert n_ctx == FLASH_TQ and t_all % FLASH_TK == 0
    return pl.pallas_call(
        functools.partial(_flash_kernel, grp=grp, tk=FLASH_TK, n_lat=n_lat, n_ctx=n_ctx),
        grid=(n_kv, t_all // FLASH_TQ),
        in_specs=[
            pl.BlockSpec((FLASH_TQ, gw), lambda h, i: (i, h)),
            pl.BlockSpec((1, HEAD_DIM, t_all), lambda h, i: (h, 0, 0)),
            pl.BlockSpec((1, t_all, LANE), lambda h, i: (h, 0, 0)),
        ],
        out_specs=pl.BlockSpec((FLASH_TQ, gw), lambda h, i: (i, h)),
        out_shape=jax.ShapeDtypeStruct((t_all, nq), BF16),
        compiler_params=_params("arbitrary", "arbitrary"),
        name="flash",
    )(q, kt, vx)


def _window_kernel(q_ref, kp_ref, kc_ref, kn_ref, kx_ref, vp_ref, vc_ref, vn_ref, vx_ref, sink_ref, o_ref,
                   *, grp, nb):
    i = pl.program_id(1)
    qs = _stack_heads(q_ref[...], grp)
    rows = qs.shape[0]
    qi = lax.broadcasted_iota(I32, (rows, Q_BLOCK), 0) & (Q_BLOCK - 1)
    kj = lax.broadcasted_iota(I32, (rows, Q_BLOCK), 1)
    sp = jnp.where(kj >= jnp.where(i > 0, qi, Q_BLOCK), _dot(qs, kp_ref[0]), NEG)
    sc = _dot(qs, kc_ref[0])
    sn = jnp.where(kj <= jnp.where(i < nb - 1, qi, -1), _dot(qs, kn_ref[0]), NEG)
    sx = _dot(qs, kx_ref[0])
    sink = sink_ref[0][:, :1]
    m = jnp.maximum(jnp.maximum(jnp.max(sp, axis=-1, keepdims=True), jnp.max(sc, axis=-1, keepdims=True)),
                    jnp.maximum(jnp.max(sn, axis=-1, keepdims=True), jnp.max(sx, axis=-1, keepdims=True)))
    m = jnp.maximum(m, sink)
    acc = (_dot(jnp.exp2(sp - m).astype(BF16), vp_ref[0]) + _dot(jnp.exp2(sc - m).astype(BF16), vc_ref[0])
           + _dot(jnp.exp2(sn - m).astype(BF16), vn_ref[0]) + _dot(jnp.exp2(sx - m).astype(BF16), vx_ref[0]))
    den = acc[:, HEAD_DIM:HEAD_DIM + 1] + jnp.exp2(sink - m)
    o_ref[...] = _unstack_heads(acc[:, :HEAD_DIM] / den, grp).astype(BF16)


def _window_attention(q, kt, vx, sink, n_lat, n_ctx):
    nq = q.shape[1]
    n_kv = kt.shape[0]
    grp = nq // HEAD_DIM // n_kv
    gw = grp * HEAD_DIM
    nb = n_lat // Q_BLOCK
    cb = n_lat // n_ctx
    sink_rows = jnp.broadcast_to((sink.reshape(n_kv, grp) * LOG2E)[:, :, None, None],
                                 (n_kv, grp, Q_BLOCK, LANE)).reshape(n_kv, grp * Q_BLOCK, LANE)
    kspec = lambda f: pl.BlockSpec((1, HEAD_DIM, Q_BLOCK), lambda h, i: (h, 0, f(i)))
    vspec = lambda f: pl.BlockSpec((1, Q_BLOCK, LANE), lambda h, i: (h, f(i), 0))
    prev = lambda i: jnp.maximum(i - 1, 0)
    cur = lambda i: i
    nxt = lambda i: jnp.minimum(i + 1, nb - 1)
    return pl.pallas_call(
        functools.partial(_window_kernel, grp=grp, nb=nb),
        grid=(n_kv, nb),
        in_specs=[
            pl.BlockSpec((Q_BLOCK, gw), lambda h, i: (i, h)),
            kspec(prev), kspec(cur), kspec(nxt),
            pl.BlockSpec((1, HEAD_DIM, n_ctx), lambda h, i: (h, 0, cb)),
            vspec(prev), vspec(cur), vspec(nxt),
            pl.BlockSpec((1, n_ctx, LANE), lambda h, i: (h, cb, 0)),
            pl.BlockSpec((1, grp * Q_BLOCK, LANE), lambda h, i: (h, 0, 0)),
        ],
        out_specs=pl.BlockSpec((Q_BLOCK, gw), lambda h, i: (i, h)),
        out_shape=jax.ShapeDtypeStruct((n_lat, nq), BF16),
        compiler_params=_params("arbitrary", "arbitrary"),
        name="window",
    )(q, kt, kt, kt, kt, vx, vx, vx, vx, sink_rows)


def _gmlp_kernel(x_ref, mod_ref, win_ref, bin_ref, vg_ref, vb_ref, ws_ref, bs_ref, wout_ref, bout_ref,
                 lng_ref, lnb_ref, rwh_ref, rwl_ref, rb_ref, x1_ref, h2_ref, idx_ref, gate_ref):
    mod = mod_ref[0]
    x = x_ref[...]
    h = x * (1.0 + mod[1:2]) + mod[0:1]
    z = _dot(h.astype(BF16), win_ref[...]) + bin_ref[...]
    z = 0.5 * z * (1.0 + lax.erf(z * (2.0 ** -0.5)))
    half = z.shape[1] // 2
    u = z[:, :half]
    v = _layer_norm(z[:, half:], vg_ref[...], vb_ref[...]).astype(BF16)
    cw = half // GM_GROUPS
    chunks = []
    for c in range(x.shape[0] // GM_CHUNK):
        vc = v[c * GM_CHUNK:(c + 1) * GM_CHUNK]
        chunks.append(jnp.concatenate(
            [_dot(ws_ref[g], vc[:, g * cw:(g + 1) * cw]) + bs_ref[g] for g in range(GM_GROUPS)], axis=1))
    gated = u * jnp.concatenate(chunks, axis=0)
    y = _dot(gated.astype(BF16), wout_ref[...]) + bout_ref[...]
    x1, h2, idx, gates = _post(y, x, mod, lng_ref[...], lnb_ref[...], rwh_ref[...], rwl_ref[...], rb_ref[...])
    x1_ref[...] = x1
    h2_ref[...] = h2
    idx_ref[...] = idx
    gate_ref[...] = gates


def _gmlp_layer(x_all, mods, w_in, b_in, vg, vb, w_s, b_s, w_out, b_out, lng, lnb, router, n_lat_tiles):
    t_all, d = x_all.shape
    dffn = w_in.shape[1]
    half = dffn // 2
    cw = half // GM_GROUPS
    tok, mod = _tile_specs(n_lat_tiles)
    shapes, specs = _post_out(t_all)
    bs_full = jnp.broadcast_to(b_s[:, :, None], (GM_GROUPS, GM_CHUNK, cw))
    return pl.pallas_call(
        _gmlp_kernel,
        grid=(t_all // TM,),
        in_specs=[
            tok, mod, _full((d, dffn)), _full((1, dffn)), _full((1, half)), _full((1, half)),
            _full((GM_GROUPS, GM_CHUNK, GM_CHUNK)), _full((GM_GROUPS, GM_CHUNK, cw)),
            _full((half, d)), _full((1, d)), _full((1, d)), _full((1, d)),
            _full((d, LANE)), _full((d, LANE)), _full((1, LANE)),
        ],
        out_specs=specs,
        out_shape=shapes,
        compiler_params=_params("arbitrary"),
        name="gmlp",
    )(x_all, mods, w_in.astype(BF16), b_in.reshape(1, dffn), vg.reshape(1, half), vb.reshape(1, half),
      w_s.astype(BF16), bs_full, w_out.astype(BF16), b_out.reshape(1, d), lng.reshape(1, d), lnb.reshape(1, d),
      *router)


def _route_meta(top_idx, n_tok):
    e = top_idx[:n_tok, :TOP_K].reshape(-1)
    n_assign = n_tok * TOP_K
    n_blocks = -(-n_assign // MOE_BLOCK) + N_EXPERTS
    onehot = (e[:, None] == jnp.arange(N_EXPERTS, dtype=I32)[None, :]).astype(I32)
    csum = jnp.cumsum(onehot, axis=0)
    counts = csum[-1]
    padded = (counts + MOE_BLOCK - 1) // MOE_BLOCK * MOE_BLOCK
    ends_pad = jnp.cumsum(padded)
    base = ends_pad - padded
    dest = jnp.sum(onehot * (csum - 1 + base[None, :]), axis=1)
    slot_tok = jnp.zeros((n_blocks * MOE_BLOCK,), I32).at[dest].set(jnp.arange(n_assign, dtype=I32) // TOP_K)
    block_expert = jnp.minimum(
        jnp.searchsorted(ends_pad, jnp.arange(n_blocks, dtype=I32) * MOE_BLOCK, side='right'), N_EXPERTS - 1)
    return block_expert.astype(I32), slot_tok, dest.astype(I32), n_blocks


def _gather_rows(idx_ref, base, n, src_hbm, dst, sem):
    for r in range(n):
        pltpu.make_async_copy(src_hbm.at[pl.ds(idx_ref[base + r], 1), :], dst.at[pl.ds(r, 1), :], sem).start()


def _expert_kernel(be_ref, tok_ref, h_hbm, wg_ref, wl_ref, bg_ref, bl_ref, wd_ref, bd_ref, y_ref, xbuf, sem,
                   *, n_blocks):
    b = pl.program_id(0)
    slot = b % 2

    @pl.when(b == 0)
    def _():
        _gather_rows(tok_ref, 0, MOE_BLOCK, h_hbm, xbuf.at[0], sem.at[0])

    @pl.when(b + 1 < n_blocks)
    def _():
        _gather_rows(tok_ref, (b + 1) * MOE_BLOCK, MOE_BLOCK, h_hbm, xbuf.at[1 - slot], sem.at[1 - slot])

    pltpu.make_async_copy(h_hbm.at[pl.ds(0, MOE_BLOCK), :], xbuf.at[slot], sem.at[slot]).wait()
    x = xbuf[slot].astype(BF16)
    glu = jnp.minimum(_dot(x, wg_ref[0]) + bg_ref[0], SWIGLU_LIMIT)
    lin = jnp.clip(_dot(x, wl_ref[0]) + bl_ref[0], -SWIGLU_LIMIT, SWIGLU_LIMIT)
    act = glu * (1.0 / (1.0 + jnp.exp(-SWIGLU_ALPHA * glu))) * (lin + 1.0)
    y_ref[...] = _dot(act.astype(BF16), wd_ref[0]) + bd_ref[0]


def _experts(h2, block_expert, slot_tok, n_blocks, wg, wl, bg, bl, wd, bd):
    d = h2.shape[1]
    ff = wg.shape[2]
    wspec = lambda r, c: pl.BlockSpec((1, r, c), lambda b, be, tok: (be[b], 0, 0))
    grid_spec = pltpu.PrefetchScalarGridSpec(
        num_scalar_prefetch=2,
        grid=(n_blocks,),
        in_specs=[pl.BlockSpec(memory_space=pl.ANY), wspec(d, ff), wspec(d, ff), wspec(1, ff), wspec(1, ff),
                  wspec(ff, d), wspec(1, d)],
        out_specs=pl.BlockSpec((MOE_BLOCK, d), lambda b, be, tok: (b, 0)),
        scratch_shapes=[pltpu.VMEM((2, MOE_BLOCK, d), F32), pltpu.SemaphoreType.DMA((2,))],
    )
    return pl.pallas_call(
        functools.partial(_expert_kernel, n_blocks=n_blocks),
        grid_spec=grid_spec,
        out_shape=jax.ShapeDtypeStruct((n_blocks * MOE_BLOCK, d), F32),
        compiler_params=_params("arbitrary"),
        name="experts",
    )(block_expert, slot_tok, h2, wg, wl, bg, bl, wd, bd)


COMB_TM = 128


def _combine_kernel(pos_ref, y_hbm, gate_ref, x_ref, mod_ref, lng_ref, lnb_ref, o_ref, ybuf, sem, *, n_tiles):
    t = pl.program_id(0)
    slot = t % 2
    n = COMB_TM * TOP_K

    @pl.when(t == 0)
    def _():
        _gather_rows(pos_ref, 0, n, y_hbm, ybuf.at[0], sem.at[0])

    @pl.when(t + 1 < n_tiles)
    def _():
        _gather_rows(pos_ref, (t + 1) * n, n, y_hbm, ybuf.at[1 - slot], sem.at[1 - slot])

    pltpu.make_async_copy(y_hbm.at[pl.ds(0, n), :], ybuf.at[slot], sem.at[slot]).wait()
    gates = gate_ref[...]
    f = jnp.zeros(x_ref.shape, F32)
    for k in range(TOP_K):
        f = f + gates[:, k:k + 1] * ybuf[slot, pl.ds(k * COMB_TM, COMB_TM), :]
    o_ref[...] = _layer_norm(ALPHA * x_ref[...] + mod_ref[0][5:6] * f, lng_ref[...], lnb_ref[...])


def _combine(yb, dest, gates, x1, mods, lng, lnb, n_tok, n_lat):
    d = x1.shape[1]
    n_tiles = n_tok // COMB_TM
    n_lat_tiles = n_lat // COMB_TM
    pos = dest.reshape(n_tiles, COMB_TM, TOP_K).transpose(0, 2, 1).reshape(-1)
    grid_spec = pltpu.PrefetchScalarGridSpec(
        num_scalar_prefetch=1,
        grid=(n_tiles,),
        in_specs=[
            pl.BlockSpec(memory_space=pl.ANY),
            pl.BlockSpec((COMB_TM, LANE), lambda t, pos: (t, 0)),
            pl.BlockSpec((COMB_TM, d), lambda t, pos: (t, 0)),
            pl.BlockSpec((1, N_MOD, d), lambda t, pos: (jnp.where(t >= n_lat_tiles, 1, 0), 0, 0)),
            pl.BlockSpec((1, d), lambda t, pos: (0, 0)),
            pl.BlockSpec((1, d), lambda t, pos: (0, 0)),
        ],
        out_specs=pl.BlockSpec((COMB_TM, d), lambda t, pos: (t, 0)),
        scratch_shapes=[pltpu.VMEM((2, COMB_TM * TOP_K, d), F32), pltpu.SemaphoreType.DMA((2,))],
    )
    return pl.pallas_call(
        functools.partial(_combine_kernel, n_tiles=n_tiles),
        grid_spec=grid_spec,
        out_shape=jax.ShapeDtypeStruct((n_tok, d), F32),
        compiler_params=_params("arbitrary"),
        name="combine",
    )(pos, yb, gates, x1, mods, lng.reshape(1, d), lnb.reshape(1, d))


def _moe_layer(x1, h2, top_idx, gates, mods, lng, lnb, w_gate_up, b_gate_up, w_down, b_down, n_tok, n_lat):
    block_expert, slot_tok, dest, n_blocks = _route_meta(top_idx, n_tok)
    wg = w_gate_up[:, :, 0::2].astype(BF16)
    wl = w_gate_up[:, :, 1::2].astype(BF16)
    bg = b_gate_up[:, None, 0::2]
    bl = b_gate_up[:, None, 1::2]
    yb = _experts(h2, block_expert, slot_tok, n_blocks, wg, wl, bg, bl, w_down.astype(BF16), b_down[:, None, :])
    return _combine(yb, dest, gates, x1, mods, lng, lnb, n_tok, n_lat)


def kernel(x, c, ctx, c_ctx, ada_w, ada_b, ln_mix_g, ln_mix_b, ln_ffn_g, ln_ffn_b, fn_w_out, fn_b_out, fa_w_qkv, fa_b_qkv, fa_q_norm, fa_k_norm, fa_w_out, fa_b_out, gm_w_in, gm_b_in, gm_v_norm_g, gm_v_norm_b, gm_w_s, gm_b_s, gm_w_out, gm_b_out, wa_w_qkv, wa_b_qkv, wa_sink, wa_w_out, wa_b_out, router_w, router_b, exp_w_gate_up, exp_b_gate_up, exp_w_down, exp_b_down):
    bsz, n_lat, d = x.shape
    n_ctx = ctx.shape[1]
    assert bsz == 1 and d == D_MODEL and n_lat == LANE * LANE and n_lat % n_ctx == 0 and n_ctx % TM == 0
    t_all = n_lat + n_ctx
    n_lat_tiles = n_lat // TM
    x_all = jnp.concatenate([x[0], ctx[0]], axis=0)
    mods_all = _ada(c, c_ctx, ada_w, ada_b)
    cos_t, sin_t = _rope_tables(n_lat, n_ctx)

    for i in range(DEPTH):
        kind, j = i % 4, i // 4
        last = i == DEPTH - 1
        n_tok = n_lat if last else t_all
        mods = mods_all[i]
        router = _router_operands(router_w[i], router_b[i])
        lng, lnb = ln_mix_g[i], ln_mix_b[i]
        if kind == 0:
            x1, h2, idx, gates = _fourier_layer(x_all, mods, fn_w_out[j], fn_b_out[j], lng, lnb, router,
                                                n_lat, n_ctx)
        elif kind == 1:
            q, k, v = _qkv(x_all, mods, fa_w_qkv[j], fa_b_qkv[j], cos_t, sin_t, FA_Q_HEADS, FA_KV_HEADS,
                           n_lat_tiles, fa_q_norm[j], fa_k_norm[j])
            kt, vx = _kv_layouts(k, v, FA_KV_HEADS)
            o = _full_attention(q, kt, vx, n_lat, n_ctx)
            x1, h2, idx, gates = _proj_post(o, fa_w_out[j], fa_b_out[j], x_all, mods, lng, lnb, router,
                                            n_tok, n_lat_tiles)
        elif kind == 2:
            x1, h2, idx, gates = _gmlp_layer(x_all, mods, gm_w_in[j], gm_b_in[j], gm_v_norm_g[j], gm_v_norm_b[j],
                                             gm_w_s[j], gm_b_s[j], gm_w_out[j], gm_b_out[j], lng, lnb, router,
                                             n_lat_tiles)
        else:
            q, k, v = _qkv(x_all, mods, wa_w_qkv[j], wa_b_qkv[j], cos_t, sin_t, WA_Q_HEADS, WA_KV_HEADS,
                           n_lat_tiles)
            kt, vx = _kv_layouts(k, v, WA_KV_HEADS)
            o = _window_attention(q, kt, vx, wa_sink[j], n_lat, n_ctx)
            x1, h2, idx, gates = _proj_post(o, wa_w_out[j], wa_b_out[j], x_all, mods, lng, lnb, router,
                                            n_tok, n_lat_tiles)
        x_all = _moe_layer(x1, h2, idx, gates, mods, ln_ffn_g[i], ln_ffn_b[i], exp_w_gate_up[i],
                           exp_b_gate_up[i], exp_w_down[i], exp_b_down[i], n_tok, n_lat)
    return x_all[None]
```

```python
import functools
import math

import numpy as np
import jax
import jax.numpy as jnp
from jax import lax
from jax.experimental import pallas as pl
from jax.experimental.pallas import tpu as pltpu

F32, BF16, I32 = jnp.float32, jnp.bfloat16, jnp.int32

D_MODEL = 1024
DEPTH = 4
GRID_W = 64
N_MOD = 6
FN_GROUPS = 4
HEAD_DIM = 64
FA_Q_HEADS, FA_KV_HEADS = 16, 4
WA_Q_HEADS, WA_KV_HEADS = 16, 2
WINDOW = 128
Q_BLOCK = 128
ROPE_THETA = 10000.0
GM_CHUNK = 128
GM_GROUPS = 8
N_EXPERTS = 32
TOP_K = 4
SWIGLU_LIMIT = 7.0
SWIGLU_ALPHA = 1.702
MOE_BLOCK = 128
LN_EPS = 1e-5
RMS_EPS = 1e-6
NEG = -1e30
ALPHA = (2 * DEPTH) ** 0.25
LOG2E = math.log2(math.e)
Q_SCALE = HEAD_DIM ** -0.5 * LOG2E

LANE = 128
TM = 256
FLASH_TQ = 256
FLASH_TK = 1280
VMEM_LIMIT = 56 * 2 ** 20


def _params(*sem):
    return pltpu.CompilerParams(dimension_semantics=sem, vmem_limit_bytes=VMEM_LIMIT)


def _dot(a, b):
    return jnp.dot(a, b, preferred_element_type=F32)


def _split(a):
    hi = a.astype(BF16)
    lo = (a - hi.astype(F32)).astype(BF16)
    return hi, lo


def _dot3(a_hi, a_lo, b_hi, b_lo):
    return _dot(a_hi, b_hi) + (_dot(a_hi, b_lo) + _dot(a_lo, b_hi))


def _layer_norm(x, g, b):
    mu = jnp.mean(x, axis=-1, keepdims=True)
    xc = x - mu
    var = jnp.mean(xc * xc, axis=-1, keepdims=True)
    return xc * lax.rsqrt(var + LN_EPS) * g + b


def _top4(logits):
    lane = lax.broadcasted_iota(I32, logits.shape, 1).astype(F32)
    cur = logits
    vals, idxs = [], []
    for _ in range(TOP_K):
        m = jnp.max(cur, axis=-1, keepdims=True)
        i = jnp.min(jnp.where(cur == m, lane, float(LANE)), axis=-1, keepdims=True)
        vals.append(m)
        idxs.append(i)
        cur = jnp.where(lane == i, -jnp.inf, cur)
    exps = [jnp.exp(v - vals[0]) for v in vals]
    inv = 1.0 / (exps[0] + exps[1] + exps[2] + exps[3])
    idx_out = jnp.zeros_like(logits)
    gate_out = jnp.zeros_like(logits)
    for k in range(TOP_K):
        idx_out = jnp.where(lane == float(k), idxs[k], idx_out)
        gate_out = jnp.where(lane == float(k), exps[k] * inv, gate_out)
    return idx_out.astype(I32), gate_out


def _post(y, x, mod, lng, lnb, rw_hi, rw_lo, rb):
    x1 = _layer_norm(ALPHA * x + mod[2:3] * y, lng, lnb)
    h2 = x1 * (1.0 + mod[4:5]) + mod[3:4]
    hh, hl = _split(h2)
    logits = _dot3(hh, hl, rw_hi, rw_lo) + rb
    idx, gates = _top4(logits)
    return x1, h2, idx, gates


def _stack_heads(qb, grp):
    return jnp.concatenate([qb[:, g * HEAD_DIM:(g + 1) * HEAD_DIM] for g in range(grp)], axis=0)


def _unstack_heads(o, grp):
    t = o.shape[0] // grp
    return jnp.concatenate([o[g * t:(g + 1) * t] for g in range(grp)], axis=1)


def _ada_kernel(cs_ref, w_ref, b_ref, o_ref):
    cs = cs_ref[...]
    s = cs * (1.0 / (1.0 + jnp.exp(-cs)))
    sh, sl = _split(s)
    wh, wl = _split(w_ref[0])
    o_ref[0] = _dot3(sh, sl, wh, wl) + b_ref[0]


def _ada(c, c_ctx, ada_w, ada_b):
    d = c.shape[-1]
    nm = ada_w.shape[-1]
    tn = nm // 4
    cs = jnp.zeros((8, d), F32).at[0].set(c[0]).at[1].set(c_ctx)
    out = pl.pallas_call(
        _ada_kernel,
        grid=(DEPTH, nm // tn),
        in_specs=[
            pl.BlockSpec((8, d), lambda i, j: (0, 0)),
            pl.BlockSpec((1, d, tn), lambda i, j: (i, 0, j)),
            pl.BlockSpec((1, 1, tn), lambda i, j: (i, 0, j)),
        ],
        out_specs=pl.BlockSpec((1, 8, tn), lambda i, j: (i, 0, j)),
        out_shape=jax.ShapeDtypeStruct((DEPTH, 8, nm), F32),
        compiler_params=_params("arbitrary", "arbitrary"),
        name="ada",
    )(cs, ada_w, ada_b.reshape(DEPTH, 1, nm))
    return out[:, :2].reshape(DEPTH, 2, N_MOD, d)


def _tile_specs(n_lat_tiles):
    tok = pl.BlockSpec((TM, D_MODEL), lambda t: (t, 0))
    mod = pl.BlockSpec((1, N_MOD, D_MODEL), lambda t: (jnp.where(t >= n_lat_tiles, 1, 0), 0, 0))
    return tok, mod


def _full(shape):
    nd = len(shape)
    return pl.BlockSpec(shape, lambda *_: (0,) * nd)


def _post_out(n_rows):
    shapes = (
        jax.ShapeDtypeStruct((n_rows, D_MODEL), F32),
        jax.ShapeDtypeStruct((n_rows, D_MODEL), F32),
        jax.ShapeDtypeStruct((n_rows, LANE), I32),
        jax.ShapeDtypeStruct((n_rows, LANE), F32),
    )
    specs = (
        pl.BlockSpec((TM, D_MODEL), lambda t: (t, 0)),
        pl.BlockSpec((TM, D_MODEL), lambda t: (t, 0)),
        pl.BlockSpec((TM, LANE), lambda t: (t, 0)),
        pl.BlockSpec((TM, LANE), lambda t: (t, 0)),
    )
    return shapes, specs


def _router_operands(router_w, router_b):
    rw = jnp.zeros((D_MODEL, LANE), F32).at[:, :N_EXPERTS].set(router_w)
    rw_hi = rw.astype(BF16)
    rw_lo = (rw - rw_hi.astype(F32)).astype(BF16)
    rb = jnp.full((1, LANE), NEG, F32).at[0, :N_EXPERTS].set(router_b)
    return rw_hi, rw_lo, rb


def _proj_post_kernel(a_ref, w_ref, b_ref, x_ref, mod_ref, lng_ref, lnb_ref, rwh_ref, rwl_ref, rb_ref,
                      x1_ref, h2_ref, idx_ref, gate_ref):
    y = _dot(a_ref[...], w_ref[...]) + b_ref[...]
    x1, h2, idx, gates = _post(y, x_ref[...], mod_ref[0], lng_ref[...], lnb_ref[...],
                               rwh_ref[...], rwl_ref[...], rb_ref[...])
    x1_ref[...] = x1
    h2_ref[...] = h2
    idx_ref[...] = idx
    gate_ref[...] = gates


def _proj_post(a, w_out, b_out, x, mods, lng, lnb, router, n_rows, n_lat_tiles):
    k = a.shape[1]
    tok, mod = _tile_specs(n_lat_tiles)
    shapes, specs = _post_out(n_rows)
    return pl.pallas_call(
        _proj_post_kernel,
        grid=(n_rows // TM,),
        in_specs=[
            pl.BlockSpec((TM, k), lambda t: (t, 0)),
            _full((k, D_MODEL)), _full((1, D_MODEL)),
            tok, mod, _full((1, D_MODEL)), _full((1, D_MODEL)),
            _full((D_MODEL, LANE)), _full((D_MODEL, LANE)), _full((1, LANE)),
        ],
        out_specs=specs,
        out_shape=shapes,
        compiler_params=_params("arbitrary"),
        name="proj_post",
    )(a, w_out.astype(BF16), b_out.reshape(1, -1), x, mods, lng.reshape(1, -1), lnb.reshape(1, -1), *router)


def _dft_mats(n):
    jk = np.outer(np.arange(n), np.arange(n)) % n
    ang = 2.0 * np.pi * jk / n
    out = []
    for m in (np.cos(ang), np.sin(ang)):
        m32 = jnp.asarray(m, F32)
        hi = m32.astype(BF16)
        out += [hi, (m32 - hi.astype(F32)).astype(BF16)]
    return out


def _channel_dft(h, cc, sc):
    cw = cc[0].shape[0]
    a_parts, b_parts = [], []
    for g in range(h.shape[1] // cw):
        hh, hl = _split(h[:, g * cw:(g + 1) * cw])
        a_parts.append(_dot3(hh, hl, cc[0][...], cc[1][...]))
        b_parts.append(_dot3(hh, hl, sc[0][...], sc[1][...]))
    return jnp.concatenate(a_parts, axis=1), jnp.concatenate(b_parts, axis=1)


def _fourier1_kernel(x_ref, mod_ref, cch_ref, ccl_ref, sch_ref, scl_ref, tc_ref, ts_ref, ur_ref, ui_ref):
    mod = mod_ref[0]
    h = x_ref[...] * (1.0 + mod[1:2]) + mod[0:1]
    a, b = _channel_dft(h, (cch_ref, ccl_ref), (sch_ref, scl_ref))
    tch, tcl = _split(tc_ref[0])
    tsh, tsl = _split(ts_ref[0])
    ah, al = _split(a)
    bh, bl = _split(b)
    ur_ref[...] = _dot3(tch, tcl, ah, al) - _dot3(tsh, tsl, bh, bl)
    ui_ref[...] = -(_dot3(tch, tcl, bh, bl) + _dot3(tsh, tsl, ah, al))


def _fourier2_kernel(ur_ref, ui_ref, c2h_ref, c2l_ref, s2h_ref, s2l_ref, w_ref, b_ref, x_ref, mod_ref,
                     lng_ref, lnb_ref, rwh_ref, rwl_ref, rb_ref, x1_ref, h2_ref, idx_ref, gate_ref, *, norm):
    urh, url = _split(ur_ref[0])
    uih, uil = _split(ui_ref[0])
    mixed = (_dot3(c2h_ref[...], c2l_ref[...], urh, url) + _dot3(s2h_ref[...], s2l_ref[...], uih, uil)) * norm
    y = _dot(mixed.astype(BF16), w_ref[...]) + b_ref[...]
    x1, h2, idx, gates = _post(y, x_ref[...], mod_ref[0], lng_ref[...], lnb_ref[...],
                               rwh_ref[...], rwl_ref[...], rb_ref[...])
    x1_ref[...] = x1
    h2_ref[...] = h2
    idx_ref[...] = idx
    gate_ref[...] = gates


def _fourier_ctx_kernel(x_ref, mod_ref, cch_ref, ccl_ref, sch_ref, scl_ref, cnh_ref, cnl_ref, snh_ref, snl_ref,
                        w_ref, b_ref, lng_ref, lnb_ref, rwh_ref, rwl_ref, rb_ref,
                        x1_ref, h2_ref, idx_ref, gate_ref, *, norm):
    mod = mod_ref[0]
    x = x_ref[...]
    h = x * (1.0 + mod[1:2]) + mod[0:1]
    a, b = _channel_dft(h, (cch_ref, ccl_ref), (sch_ref, scl_ref))
    ah, al = _split(a)
    bh, bl = _split(b)
    mixed = (_dot3(cnh_ref[...], cnl_ref[...], ah, al) - _dot3(snh_ref[...], snl_ref[...], bh, bl)) * norm
    y = _dot(mixed.astype(BF16), w_ref[...]) + b_ref[...]
    x1, h2, idx, gates = _post(y, x, mod, lng_ref[...], lnb_ref[...], rwh_ref[...], rwl_ref[...], rb_ref[...])
    x1_ref[...] = x1
    h2_ref[...] = h2
    idx_ref[...] = idx
    gate_ref[...] = gates


def _fourier_layer(x_all, mods, w_out, b_out, lng, lnb, router, n_lat, n_ctx):
    t_all, d = x_all.shape
    n2 = LANE
    n1 = n_lat // n2
    cw = d // FN_GROUPS
    rows = t_all // n2
    xv = x_all.reshape(rows, n2 * d)
    cmat = _dft_mats(cw)
    w_bf = w_out.astype(BF16)
    b2 = b_out.reshape(1, d)
    lng2, lnb2 = lng.reshape(1, d), lnb.reshape(1, d)

    k1 = jnp.arange(n1, dtype=I32)
    pos = jnp.arange(n1, dtype=I32)[None, None, :] * n2 + jnp.arange(n2, dtype=I32)[:, None, None]
    ang = ((k1[None, :, None] * pos) % n_lat).astype(F32) * (2.0 * math.pi / n_lat)
    tc, ts = jnp.cos(ang), jnp.sin(ang)

    mat = _full((cw, cw))
    ur, ui = pl.pallas_call(
        _fourier1_kernel,
        grid=(n2,),
        in_specs=[
            pl.BlockSpec((n1, d), lambda j: (0, j)),
            pl.BlockSpec((1, N_MOD, d), lambda j: (0, 0, 0)),
            mat, mat, mat, mat,
            pl.BlockSpec((1, n1, n1), lambda j: (j, 0, 0)),
            pl.BlockSpec((1, n1, n1), lambda j: (j, 0, 0)),
        ],
        out_specs=(pl.BlockSpec((n1, d), lambda j: (0, j)), pl.BlockSpec((n1, d), lambda j: (0, j))),
        out_shape=(jax.ShapeDtypeStruct((n1, n2 * d), F32), jax.ShapeDtypeStruct((n1, n2 * d), F32)),
        compiler_params=_params("arbitrary"),
        name="fourier1",
    )(xv, mods, *cmat, tc, ts)

    m2 = _dft_mats(n2)
    mat2 = _full((n2, n2))
    norm = 1.0 / math.sqrt(n_lat * cw)
    out_shapes = (
        jax.ShapeDtypeStruct((n1, n2 * d), F32),
        jax.ShapeDtypeStruct((n1, n2 * d), F32),
        jax.ShapeDtypeStruct((n1, n2 * LANE), I32),
        jax.ShapeDtypeStruct((n1, n2 * LANE), F32),
    )
    strided = pl.BlockSpec((n2, d), lambda k: (0, k))
    strided_l = pl.BlockSpec((n2, LANE), lambda k: (0, k))
    outs = pl.pallas_call(
        functools.partial(_fourier2_kernel, norm=norm),
        grid=(n1,),
        in_specs=[
            pl.BlockSpec((1, n2, d), lambda k: (k, 0, 0)),
            pl.BlockSpec((1, n2, d), lambda k: (k, 0, 0)),
            mat2, mat2, mat2, mat2,
            _full((d, d)), _full((1, d)),
            strided,
            pl.BlockSpec((1, N_MOD, d), lambda k: (0, 0, 0)),
            _full((1, d)), _full((1, d)),
            _full((d, LANE)), _full((d, LANE)), _full((1, LANE)),
        ],
        out_specs=(strided, strided, strided_l, strided_l),
        out_shape=out_shapes,
        compiler_params=_params("arbitrary"),
        name="fourier2",
    )(ur.reshape(n1, n2, d), ui.reshape(n1, n2, d), *m2, w_bf, b2, xv, mods, lng2, lnb2, *router)
    lat = (outs[0].reshape(n_lat, d), outs[1].reshape(n_lat, d),
           outs[2].reshape(n_lat, LANE), outs[3].reshape(n_lat, LANE))

    cn = _dft_mats(n_ctx)
    matn = _full((n_ctx, n_ctx))
    shapes = (
        jax.ShapeDtypeStruct((n_ctx, d), F32), jax.ShapeDtypeStruct((n_ctx, d), F32),
        jax.ShapeDtypeStruct((n_ctx, LANE), I32), jax.ShapeDtypeStruct((n_ctx, LANE), F32),
    )
    ctx = pl.pallas_call(
        functools.partial(_fourier_ctx_kernel, norm=1.0 / math.sqrt(n_ctx * cw)),
        grid=(1,),
        in_specs=[
            pl.BlockSpec((n_ctx, d), lambda i: (n_lat // n_ctx, 0)),
            pl.BlockSpec((1, N_MOD, d), lambda i: (1, 0, 0)),
            mat, mat, mat, mat, matn, matn, matn, matn,
            _full((d, d)), _full((1, d)), _full((1, d)), _full((1, d)),
            _full((d, LANE)), _full((d, LANE)), _full((1, LANE)),
        ],
        out_specs=(_full((n_ctx, d)), _full((n_ctx, d)), _full((n_ctx, LANE)), _full((n_ctx, LANE))),
        out_shape=shapes,
        compiler_params=_params("arbitrary"),
        name="fourier_ctx",
    )(x_all, mods, *cmat, *cn, w_bf, b2, lng2, lnb2, *router)
    return tuple(jnp.concatenate([a, b], axis=0) for a, b in zip(lat, ctx))


def _qkv_kernel(x_ref, mod_ref, w_ref, b_ref, cos_ref, sin_ref, *rest, n_qk, rms):
    if rms:
        gain_ref, ind_ref, indt_ref, q_ref, k_ref, v_ref = rest
    else:
        q_ref, k_ref, v_ref = rest
    mod = mod_ref[0]
    h = x_ref[...] * (1.0 + mod[1:2]) + mod[0:1]
    y = _dot(h.astype(BF16), w_ref[...]) + b_ref[...]
    qk = y[:, :n_qk]
    if rms:
        sh, sl = _split(qk * qk)
        ms = _dot(sh, ind_ref[...]) + _dot(sl, ind_ref[...])
        mh, ml = _split(ms)
        msb = _dot(mh, indt_ref[...]) + _dot(ml, indt_ref[...])
        qk = qk * lax.rsqrt(msb + RMS_EPS) * gain_ref[...]
    cos = cos_ref[...]
    sin = sin_ref[...]
    even = (lax.broadcasted_iota(I32, cos.shape, 1) & 1) == 0
    parts = []
    for c in range(n_qk // LANE):
        z = qk[:, c * LANE:(c + 1) * LANE]
        swapped = jnp.where(even, pltpu.roll(z, LANE - 1, 1), pltpu.roll(z, 1, 1))
        parts.append(z * cos + swapped * sin)
    nq = q_ref.shape[1]
    q_ref[...] = (jnp.concatenate(parts[:nq // LANE], axis=1) * Q_SCALE).astype(BF16)
    k_ref[...] = jnp.concatenate(parts[nq // LANE:], axis=1).astype(BF16)
    v_ref[...] = y[:, n_qk:].astype(BF16)


def _qkv(x_all, mods, w_qkv, b_qkv, cos_t, sin_t, n_q, n_kv, n_lat_tiles, q_norm=None, k_norm=None):
    t_all, d = x_all.shape
    nq, nk = n_q * HEAD_DIM, n_kv * HEAD_DIM
    n_qk, n_all = nq + nk, nq + 2 * nk
    rms = q_norm is not None
    tok, mod = _tile_specs(n_lat_tiles)
    in_specs = [tok, mod, _full((d, n_all)), _full((1, n_all)),
                pl.BlockSpec((TM, LANE), lambda t: (t, 0)), pl.BlockSpec((TM, LANE), lambda t: (t, 0))]
    args = [x_all, mods, w_qkv.astype(BF16), b_qkv.reshape(1, n_all), cos_t, sin_t]
    if rms:
        gain = jnp.concatenate([jnp.tile(q_norm, n_q), jnp.tile(k_norm, n_kv)]).reshape(1, n_qk)
        head = np.arange(n_qk) // HEAD_DIM
        ind = np.zeros((n_qk, LANE), np.float32)
        ind[np.arange(n_qk), head] = 1.0 / HEAD_DIM
        indt = np.zeros((LANE, n_qk), np.float32)
        indt[head, np.arange(n_qk)] = 1.0
        in_specs += [_full((1, n_qk)), _full((n_qk, LANE)), _full((LANE, n_qk))]
        args += [gain, jnp.asarray(ind, BF16), jnp.asarray(indt, BF16)]
    return pl.pallas_call(
        functools.partial(_qkv_kernel, n_qk=n_qk, rms=rms),
        grid=(t_all // TM,),
        in_specs=in_specs,
        out_specs=(pl.BlockSpec((TM, nq), lambda t: (t, 0)), pl.BlockSpec((TM, nk), lambda t: (t, 0)),
                   pl.BlockSpec((TM, nk), lambda t: (t, 0))),
        out_shape=(jax.ShapeDtypeStruct((t_all, nq), BF16), jax.ShapeDtypeStruct((t_all, nk), BF16),
                   jax.ShapeDtypeStruct((t_all, nk), BF16)),
        compiler_params=_params("arbitrary"),
        name="qkv",
    )(*args)


def _rope_tables(n_lat, n_ctx):
    rows = n_lat // GRID_W
    row = jnp.repeat(jnp.arange(rows, dtype=F32), GRID_W)
    col = jnp.tile(jnp.arange(GRID_W, dtype=F32), rows)
    n_freq = HEAD_DIM // 4
    inv = ROPE_THETA ** (-jnp.arange(n_freq, dtype=F32) / n_freq)
    ang = jnp.concatenate([row[:, None] * inv, col[:, None] * inv], axis=-1)
    ang = jnp.concatenate([ang, jnp.zeros((n_ctx, HEAD_DIM // 2), F32)], axis=0)
    cos = jnp.tile(jnp.repeat(jnp.cos(ang), 2, axis=1), (1, LANE // HEAD_DIM))
    sin = jnp.tile(jnp.repeat(jnp.sin(ang), 2, axis=1), (1, LANE // HEAD_DIM))
    sign = jnp.where(jnp.arange(LANE) % 2 == 0, -1.0, 1.0).astype(F32)
    return cos, sin * sign


def _kv_layouts(k, v, n_kv):
    t_all = k.shape[0]
    kt = k.reshape(t_all, n_kv, HEAD_DIM).transpose(1, 2, 0)
    vh = v.reshape(t_all, n_kv, HEAD_DIM).transpose(1, 0, 2)
    pad = jnp.zeros((n_kv, t_all, LANE - HEAD_DIM), BF16).at[:, :, 0].set(1.0)
    return kt, jnp.concatenate([vh, pad], axis=-1)


def _flash_kernel(q_ref, kt_ref, v_ref, o_ref, *, grp, tk, n_lat, n_ctx):
    qs = _stack_heads(q_ref[...], grp)
    rows = qs.shape[0]
    is_ctx = pl.program_id(1) == n_lat // FLASH_TQ

    def finish(acc):
        o = acc[:, :HEAD_DIM] / acc[:, HEAD_DIM:HEAD_DIM + 1]
        o_ref[...] = _unstack_heads(o, grp).astype(BF16)

    @pl.when(jnp.logical_not(is_ctx))
    def _():
        def body(j, carry):
            m, acc = carry
            off = pl.multiple_of(j * tk, tk)
            s = _dot(qs, kt_ref[0, :, pl.ds(off, tk)])
            m_new = jnp.maximum(m, jnp.max(s, axis=-1, keepdims=True))
            p = jnp.exp2(s - m_new)
            acc = jnp.exp2(m - m_new) * acc + _dot(p.astype(BF16), v_ref[0, pl.ds(off, tk), :])
            return m_new, acc

        m0 = jnp.full((rows, 1), NEG, F32)
        acc0 = jnp.zeros((rows, LANE), F32)
        finish(lax.fori_loop(0, (n_lat + n_ctx) // tk, body, (m0, acc0))[1])

    @pl.when(is_ctx)
    def _():
        s = _dot(qs, kt_ref[0, :, n_lat:n_lat + n_ctx])
        p = jnp.exp2(s - jnp.max(s, axis=-1, keepdims=True))
        finish(_dot(p.astype(BF16), v_ref[0, n_lat:n_lat + n_ctx, :]))


def _full_attention(q, kt, vx, n_lat, n_ctx):
    t_all, nq = q.shape
    n_kv = kt.shape[0]
    grp = nq // HEAD_DIM // n_kv
    gw = grp * HEAD_DIM
    assert n_ctx == FLASH_TQ and t_all % FLASH_TK == 0
    return pl.pallas_call(
        functools.partial(_flash_kernel, grp=grp, tk=FLASH_TK, n_lat=n_lat, n_ctx=n_ctx),
        grid=(n_kv, t_all // FLASH_TQ),
        in_specs=[
            pl.BlockSpec((FLASH_TQ, gw), lambda h, i: (i, h)),
            pl.BlockSpec((1, HEAD_DIM, t_all), lambda h, i: (h, 0, 0)),
            pl.BlockSpec((1, t_all, LANE), lambda h, i: (h, 0, 0)),
        ],
        out_specs=pl.BlockSpec((FLASH_TQ, gw), lambda h, i: (i, h)),
        out_shape=jax.ShapeDtypeStruct((t_all, nq), BF16),
        compiler_params=_params("arbitrary", "arbitrary"),
        name="flash",
    )(q, kt, vx)


def _window_kernel(q_ref, kp_ref, kc_ref, kn_ref, kx_ref, vp_ref, vc_ref, vn_ref, vx_ref, sink_ref, o_ref,
                   *, grp, nb):
    i = pl.program_id(1)
    qs = _stack_heads(q_ref[...], grp)
    rows = qs.shape[0]
    qi = lax.broadcasted_iota(I32, (rows, Q_BLOCK), 0) & (Q_BLOCK - 1)
    kj = lax.broadcasted_iota(I32, (rows, Q_BLOCK), 1)
    sp = jnp.where(kj >= jnp.where(i > 0, qi, Q_BLOCK), _dot(qs, kp_ref[0]), NEG)
    sc = _dot(qs, kc_ref[0])
    sn = jnp.where(kj <= jnp.where(i < nb - 1, qi, -1), _dot(qs, kn_ref[0]), NEG)
    sx = _dot(qs, kx_ref[0])
    sink = sink_ref[0][:, :1]
    m = jnp.maximum(jnp.maximum(jnp.max(sp, axis=-1, keepdims=True), jnp.max(sc, axis=-1, keepdims=True)),
                    jnp.maximum(jnp.max(sn, axis=-1, keepdims=True), jnp.max(sx, axis=-1, keepdims=True)))
    m = jnp.maximum(m, sink)
    acc = (_dot(jnp.exp2(sp - m).astype(BF16), vp_ref[0]) + _dot(jnp.exp2(sc - m).astype(BF16), vc_ref[0])
           + _dot(jnp.exp2(sn - m).astype(BF16), vn_ref[0]) + _dot(jnp.exp2(sx - m).astype(BF16), vx_ref[0]))
    den = acc[:, HEAD_DIM:HEAD_DIM + 1] + jnp.exp2(sink - m)
    o_ref[...] = _unstack_heads(acc[:, :HEAD_DIM] / den, grp).astype(BF16)


def _window_attention(q, kt, vx, sink, n_lat, n_ctx):
    nq = q.shape[1]
    n_kv = kt.shape[0]
    grp = nq // HEAD_DIM // n_kv
    gw = grp * HEAD_DIM
    nb = n_lat // Q_BLOCK
    cb = n_lat // n_ctx
    sink_rows = jnp.broadcast_to((sink.reshape(n_kv, grp) * LOG2E)[:, :, None, None],
                                 (n_kv, grp, Q_BLOCK, LANE)).reshape(n_kv, grp * Q_BLOCK, LANE)
    kspec = lambda f: pl.BlockSpec((1, HEAD_DIM, Q_BLOCK), lambda h, i: (h, 0, f(i)))
    vspec = lambda f: pl.BlockSpec((1, Q_BLOCK, LANE), lambda h, i: (h, f(i), 0))
    prev = lambda i: jnp.maximum(i - 1, 0)
    cur = lambda i: i
    nxt = lambda i: jnp.minimum(i + 1, nb - 1)
    return pl.pallas_call(
        functools.partial(_window_kernel, grp=grp, nb=nb),
        grid=(n_kv, nb),
        in_specs=[
            pl.BlockSpec((Q_BLOCK, gw), lambda h, i: (i, h)),
            kspec(prev), kspec(cur), kspec(nxt),
            pl.BlockSpec((1, HEAD_DIM, n_ctx), lambda h, i: (h, 0, cb)),
            vspec(prev), vspec(cur), vspec(nxt),
            pl.BlockSpec((1, n_ctx, LANE), lambda h, i: (h, cb, 0)),
            pl.BlockSpec((1, grp * Q_BLOCK, LANE), lambda h, i: (h, 0, 0)),
        ],
        out_specs=pl.BlockSpec((Q_BLOCK, gw), lambda h, i: (i, h)),
        out_shape=jax.ShapeDtypeStruct((n_lat, nq), BF16),
        compiler_params=_params("arbitrary", "arbitrary"),
        name="window",
    )(q, kt, kt, kt, kt, vx, vx, vx, vx, sink_rows)


def _gmlp_kernel(x_ref, mod_ref, win_ref, bin_ref, vg_ref, vb_ref, ws_ref, bs_ref, wout_ref, bout_ref,
                 lng_ref, lnb_ref, rwh_ref, rwl_ref, rb_ref, x1_ref, h2_ref, idx_ref, gate_ref):
    mod = mod_ref[0]
    x = x_ref[...]
    h = x * (1.0 + mod[1:2]) + mod[0:1]
    z = _dot(h.astype(BF16), win_ref[...]) + bin_ref[...]
    z = 0.5 * z * (1.0 + lax.erf(z * (2.0 ** -0.5)))
    half = z.shape[1] // 2
    u = z[:, :half]
    v = _layer_norm(z[:, half:], vg_ref[...], vb_ref[...]).astype(BF16)
    cw = half // GM_GROUPS
    chunks = []
    for c in range(x.shape[0] // GM_CHUNK):
        vc = v[c * GM_CHUNK:(c + 1) * GM_CHUNK]
        chunks.append(jnp.concatenate(
            [_dot(ws_ref[g], vc[:, g * cw:(g + 1) * cw]) + bs_ref[g] for g in range(GM_GROUPS)], axis=1))
    gated = u * jnp.concatenate(chunks, axis=0)
    y = _dot(gated.astype(BF16), wout_ref[...]) + bout_ref[...]
    x1, h2, idx, gates = _post(y, x, mod, lng_ref[...], lnb_ref[...], rwh_ref[...], rwl_ref[...], rb_ref[...])
    x1_ref[...] = x1
    h2_ref[...] = h2
    idx_ref[...] = idx
    gate_ref[...] = gates


def _gmlp_layer(x_all, mods, w_in, b_in, vg, vb, w_s, b_s, w_out, b_out, lng, lnb, router, n_lat_tiles):
    t_all, d = x_all.shape
    dffn = w_in.shape[1]
    half = dffn // 2
    cw = half // GM_GROUPS
    tok, mod = _tile_specs(n_lat_tiles)
    shapes, specs = _post_out(t_all)
    bs_full = jnp.broadcast_to(b_s[:, :, None], (GM_GROUPS, GM_CHUNK, cw))
    return pl.pallas_call(
        _gmlp_kernel,
        grid=(t_all // TM,),
        in_specs=[
            tok, mod, _full((d, dffn)), _full((1, dffn)), _full((1, half)), _full((1, half)),
            _full((GM_GROUPS, GM_CHUNK, GM_CHUNK)), _full((GM_GROUPS, GM_CHUNK, cw)),
            _full((half, d)), _full((1, d)), _full((1, d)), _full((1, d)),
            _full((d, LANE)), _full((d, LANE)), _full((1, LANE)),
        ],
        out_specs=specs,
        out_shape=shapes,
        compiler_params=_params("arbitrary"),
        name="gmlp",
    )(x_all, mods, w_in.astype(BF16), b_in.reshape(1, dffn), vg.reshape(1, half), vb.reshape(1, half),
      w_s.astype(BF16), bs_full, w_out.astype(BF16), b_out.reshape(1, d), lng.reshape(1, d), lnb.reshape(1, d),
      *router)


COMB_TM = 128


def _route_meta(top_idx, n_tok):
    n_assign = n_tok * TOP_K
    n_blocks = -(-n_assign // MOE_BLOCK) + N_EXPERTS
    ids = jnp.arange(N_EXPERTS, dtype=I32)
    counts = jnp.sum((top_idx[:n_tok, :TOP_K, None] == ids).astype(I32), axis=(0, 1))
    padded = (counts + MOE_BLOCK - 1) // MOE_BLOCK * MOE_BLOCK
    ends_pad = jnp.cumsum(padded)
    base = ends_pad - padded
    starts = jnp.arange(n_blocks, dtype=I32) * MOE_BLOCK
    block_expert = jnp.minimum(jnp.sum((ends_pad[None, :] <= starts[:, None]).astype(I32), axis=1), N_EXPERTS - 1)
    last_blk = jnp.where(padded > 0, ends_pad - MOE_BLOCK, -1)
    tail = ends_pad[-1] + ids * MOE_BLOCK
    tail = jnp.where(tail < n_blocks * MOE_BLOCK, tail, -1)
    base_b = jnp.broadcast_to(base.astype(F32)[:, None], (N_EXPERTS, LANE))
    return block_expert.astype(I32), base_b, jnp.concatenate([last_blk, tail]).astype(I32), n_blocks


def _rank_kernel(idx_ref, base_ref, upper_ref, dest_ref, run_ref):
    @pl.when(pl.program_id(0) == 0)
    def _():
        run_ref[...] = jnp.zeros(run_ref.shape, F32)

    eid = lax.broadcasted_iota(I32, (N_EXPERTS, COMB_TM), 0)
    onehots = [(eid == idx_ref[k:k + 1, :]).astype(F32) for k in range(TOP_K)]
    cnt = onehots[0] + onehots[1] + onehots[2] + onehots[3]
    before = _dot(cnt.astype(BF16), upper_ref[...])
    slot = base_ref[...] + run_ref[...] + before
    for k in range(TOP_K):
        dest_ref[0, k:k + 1, :] = jnp.sum(onehots[k] * slot, axis=0, keepdims=True).astype(I32)
    run_ref[...] = run_ref[...] + jnp.sum(cnt, axis=1, keepdims=True)


def _rank(top_idx, base_b, n_tok):
    n_tiles = n_tok // COMB_TM
    idx_t = top_idx[:n_tok, :TOP_K].T
    upper = jnp.asarray(np.triu(np.ones((COMB_TM, COMB_TM), np.float32), 1), BF16)
    return pl.pallas_call(
        _rank_kernel,
        grid=(n_tiles,),
        in_specs=[pl.BlockSpec((TOP_K, COMB_TM), lambda t: (0, t)), _full((N_EXPERTS, LANE)),
                  _full((COMB_TM, COMB_TM))],
        out_specs=pl.BlockSpec((1, TOP_K, COMB_TM), lambda t: (t, 0, 0)),
        out_shape=jax.ShapeDtypeStruct((n_tiles, TOP_K, COMB_TM), I32),
        scratch_shapes=[pltpu.VMEM((N_EXPERTS, LANE), F32)],
        compiler_params=_params("arbitrary"),
        name="rank",
    )(idx_t, base_b, upper)


def _dispatch_kernel(dest_ref, zrow_ref, h_ref, xs_hbm, hbuf, zbuf, sem, zsem, *, n_tiles):
    t = pl.program_id(0)
    slot = t % 2
    n_zero = 2 * N_EXPERTS

    def zero_copy(j):
        row = pl.multiple_of(zrow_ref[j], MOE_BLOCK)
        return pltpu.make_async_copy(zbuf, xs_hbm.at[pl.ds(row, MOE_BLOCK), :], zsem.at[0])

    def wait_rows(s):
        for _ in range(TOP_K):
            pltpu.make_async_copy(hbuf.at[s], xs_hbm.at[pl.ds(0, COMB_TM), :], sem.at[s]).wait()

    @pl.when(t == 0)
    def _():
        zbuf[...] = jnp.zeros(zbuf.shape, F32)
        for j in range(n_zero):
            @pl.when(zrow_ref[j] >= 0)
            def _():
                zero_copy(j).start()
        for j in range(n_zero):
            @pl.when(zrow_ref[j] >= 0)
            def _():
                zero_copy(j).wait()

    @pl.when(t >= 2)
    def _():
        wait_rows(slot)

    hbuf[slot] = h_ref[...]
    for k in range(TOP_K):
        for r in range(COMB_TM):
            row = dest_ref[(t * TOP_K + k) * COMB_TM + r]
            pltpu.make_async_copy(hbuf.at[slot, pl.ds(r, 1), :], xs_hbm.at[pl.ds(row, 1), :], sem.at[slot]).start()

    @pl.when(t == n_tiles - 1)
    def _():
        wait_rows(slot)
        if n_tiles > 1:
            wait_rows(1 - slot)


def _dispatch(h2, dest, zrows, n_tok, n_blocks):
    d = h2.shape[1]
    n_tiles = n_tok // COMB_TM
    grid_spec = pltpu.PrefetchScalarGridSpec(
        num_scalar_prefetch=2,
        grid=(n_tiles,),
        in_specs=[pl.BlockSpec((COMB_TM, d), lambda t, dest, zr: (t, 0))],
        out_specs=pl.BlockSpec(memory_space=pl.ANY),
        scratch_shapes=[pltpu.VMEM((2, COMB_TM, d), F32), pltpu.VMEM((MOE_BLOCK, d), F32),
                        pltpu.SemaphoreType.DMA((2,)), pltpu.SemaphoreType.DMA((1,))],
    )
    return pl.pallas_call(
        functools.partial(_dispatch_kernel, n_tiles=n_tiles),
        grid_spec=grid_spec,
        out_shape=jax.ShapeDtypeStruct((n_blocks * MOE_BLOCK, d), F32),
        compiler_params=_params("arbitrary"),
        name="dispatch",
    )(dest.reshape(-1), zrows, h2)


EXP_CHUNK = 512


def _expert_kernel(be_ref, x_ref, wgu_ref, bgu_ref, wd_ref, bd_ref, sel_e_ref, sel_o_ref, y_ref,
                   wg_s, wl_s, wd_s, bg_s, bl_s):
    b = pl.program_id(0)
    ff = wg_s.shape[1]
    half = EXP_CHUNK // 2

    @pl.when(jnp.logical_or(b == 0, be_ref[b] != be_ref[jnp.maximum(b - 1, 0)]))
    def _():
        for c in range(2 * ff // EXP_CHUNK):
            w = wgu_ref[0, :, c * EXP_CHUNK:(c + 1) * EXP_CHUNK].astype(BF16)
            wg_s[:, c * half:(c + 1) * half] = _dot(w, sel_e_ref[...]).astype(BF16)
            wl_s[:, c * half:(c + 1) * half] = _dot(w, sel_o_ref[...]).astype(BF16)
            bh, bl = _split(jnp.broadcast_to(bgu_ref[0, :, c * EXP_CHUNK:(c + 1) * EXP_CHUNK], (8, EXP_CHUNK)))
            bg_s[:, c * half:(c + 1) * half] = _dot(bh, sel_e_ref[...]) + _dot(bl, sel_e_ref[...])
            bl_s[:, c * half:(c + 1) * half] = _dot(bh, sel_o_ref[...]) + _dot(bl, sel_o_ref[...])
        wd_s[...] = wd_ref[0].astype(BF16)

    x = x_ref[...].astype(BF16)
    glu = jnp.minimum(_dot(x, wg_s[...]) + bg_s[0:1, :], SWIGLU_LIMIT)
    lin = jnp.clip(_dot(x, wl_s[...]) + bl_s[0:1, :], -SWIGLU_LIMIT, SWIGLU_LIMIT)
    act = glu * (1.0 / (1.0 + jnp.exp(-SWIGLU_ALPHA * glu))) * (lin + 1.0)
    y_ref[...] = _dot(act.astype(BF16), wd_s[...]) + bd_ref[0]


def _experts(xs, block_expert, n_blocks, w_gate_up, b_gate_up, w_down, b_down):
    d = xs.shape[1]
    ff = w_down.shape[1]
    sel = np.zeros((2, EXP_CHUNK, EXP_CHUNK // 2), np.float32)
    sel[0, 2 * np.arange(EXP_CHUNK // 2), np.arange(EXP_CHUNK // 2)] = 1.0
    sel[1, 2 * np.arange(EXP_CHUNK // 2) + 1, np.arange(EXP_CHUNK // 2)] = 1.0
    wspec = lambda r, c: pl.BlockSpec((1, r, c), lambda b, be: (be[b], 0, 0))
    sspec = pl.BlockSpec((EXP_CHUNK, EXP_CHUNK // 2), lambda b, be: (0, 0))
    grid_spec = pltpu.PrefetchScalarGridSpec(
        num_scalar_prefetch=1,
        grid=(n_blocks,),
        in_specs=[pl.BlockSpec((MOE_BLOCK, d), lambda b, be: (b, 0)), wspec(d, 2 * ff), wspec(1, 2 * ff),
                  wspec(ff, d), wspec(1, d), sspec, sspec],
        out_specs=pl.BlockSpec((MOE_BLOCK, d), lambda b, be: (b, 0)),
        scratch_shapes=[pltpu.VMEM((d, ff), BF16), pltpu.VMEM((d, ff), BF16), pltpu.VMEM((ff, d), BF16),
                        pltpu.VMEM((8, ff), F32), pltpu.VMEM((8, ff), F32)],
    )
    return pl.pallas_call(
        _expert_kernel,
        grid_spec=grid_spec,
        out_shape=jax.ShapeDtypeStruct((n_blocks * MOE_BLOCK, d), F32),
        compiler_params=_params("arbitrary"),
        name="experts",
    )(block_expert, xs, w_gate_up, b_gate_up[:, None, :], w_down, b_down[:, None, :],
      jnp.asarray(sel[0], BF16), jnp.asarray(sel[1], BF16))


def _gather_rows(idx_ref, base, n, src_hbm, dst, sem):
    for r in range(n):
        pltpu.make_async_copy(src_hbm.at[pl.ds(idx_ref[base + r], 1), :], dst.at[pl.ds(r, 1), :], sem).start()


def _combine_kernel(pos_ref, y_hbm, gate_ref, x_ref, mod_ref, lng_ref, lnb_ref, o_ref, ybuf, sem, *, n_tiles):
    t = pl.program_id(0)
    slot = t % 2
    n = COMB_TM * TOP_K

    @pl.when(t == 0)
    def _():
        _gather_rows(pos_ref, 0, n, y_hbm, ybuf.at[0], sem.at[0])

    @pl.when(t + 1 < n_tiles)
    def _():
        _gather_rows(pos_ref, (t + 1) * n, n, y_hbm, ybuf.at[1 - slot], sem.at[1 - slot])

    pltpu.make_async_copy(y_hbm.at[pl.ds(0, n), :], ybuf.at[slot], sem.at[slot]).wait()
    gates = gate_ref[...]
    f = jnp.zeros(x_ref.shape, F32)
    for k in range(TOP_K):
        f = f + gates[:, k:k + 1] * ybuf[slot, pl.ds(k * COMB_TM, COMB_TM), :]
    o_ref[...] = _layer_norm(ALPHA * x_ref[...] + mod_ref[0][5:6] * f, lng_ref[...], lnb_ref[...])


def _combine(yb, dest, gates, x1, mods, lng, lnb, n_tok, n_lat):
    d = x1.shape[1]
    n_tiles = n_tok // COMB_TM
    n_lat_tiles = n_lat // COMB_TM
    pos = dest.reshape(-1)
    grid_spec = pltpu.PrefetchScalarGridSpec(
        num_scalar_prefetch=1,
        grid=(n_tiles,),
        in_specs=[
            pl.BlockSpec(memory_space=pl.ANY),
            pl.BlockSpec((COMB_TM, LANE), lambda t, pos: (t, 0)),
            pl.BlockSpec((COMB_TM, d), lambda t, pos: (t, 0)),
            pl.BlockSpec((1, N_MOD, d), lambda t, pos: (jnp.where(t >= n_lat_tiles, 1, 0), 0, 0)),
            pl.BlockSpec((1, d), lambda t, pos: (0, 0)),
            pl.BlockSpec((1, d), lambda t, pos: (0, 0)),
        ],
        out_specs=pl.BlockSpec((COMB_TM, d), lambda t, pos: (t, 0)),
        scratch_shapes=[pltpu.VMEM((2, COMB_TM * TOP_K, d), F32), pltpu.SemaphoreType.DMA((2,))],
    )
    return pl.pallas_call(
        functools.partial(_combine_kernel, n_tiles=n_tiles),
        grid_spec=grid_spec,
        out_shape=jax.ShapeDtypeStruct((n_tok, d), F32),
        compiler_params=_params("arbitrary"),
        name="combine",
    )(pos, yb, gates, x1, mods, lng.reshape(1, d), lnb.reshape(1, d))


def _moe_layer(x1, h2, top_idx, gates, mods, lng, lnb, w_gate_up, b_gate_up, w_down, b_down, n_tok, n_lat):
    block_expert, base_b, zrows, n_blocks = _route_meta(top_idx, n_tok)
    dest = _rank(top_idx, base_b, n_tok)
    xs = _dispatch(h2, dest, zrows, n_tok, n_blocks)
    yb = _experts(xs, block_expert, n_blocks, w_gate_up, b_gate_up, w_down, b_down)
    return _combine(yb, dest, gates, x1, mods, lng, lnb, n_tok, n_lat)


def kernel(x, c, ctx, c_ctx, ada_w, ada_b, ln_mix_g, ln_mix_b, ln_ffn_g, ln_ffn_b, fn_w_out, fn_b_out, fa_w_qkv, fa_b_qkv, fa_q_norm, fa_k_norm, fa_w_out, fa_b_out, gm_w_in, gm_b_in, gm_v_norm_g, gm_v_norm_b, gm_w_s, gm_b_s, gm_w_out, gm_b_out, wa_w_qkv, wa_b_qkv, wa_sink, wa_w_out, wa_b_out, router_w, router_b, exp_w_gate_up, exp_b_gate_up, exp_w_down, exp_b_down):
    bsz, n_lat, d = x.shape
    n_ctx = ctx.shape[1]
    assert bsz == 1 and d == D_MODEL and n_lat == LANE * LANE and n_lat % n_ctx == 0 and n_ctx % TM == 0
    t_all = n_lat + n_ctx
    n_lat_tiles = n_lat // TM
    x_all = jnp.concatenate([x[0], ctx[0]], axis=0)
    mods_all = _ada(c, c_ctx, ada_w, ada_b)
    cos_t, sin_t = _rope_tables(n_lat, n_ctx)

    for i in range(DEPTH):
        kind, j = i % 4, i // 4
        last = i == DEPTH - 1
        n_tok = n_lat if last else t_all
        mods = mods_all[i]
        router = _router_operands(router_w[i], router_b[i])
        lng, lnb = ln_mix_g[i], ln_mix_b[i]
        if kind == 0:
            x1, h2, idx, gates = _fourier_layer(x_all, mods, fn_w_out[j], fn_b_out[j], lng, lnb, router,
                                                n_lat, n_ctx)
        elif kind == 1:
            q, k, v = _qkv(x_all, mods, fa_w_qkv[j], fa_b_qkv[j], cos_t, sin_t, FA_Q_HEADS, FA_KV_HEADS,
                           n_lat_tiles, fa_q_norm[j], fa_k_norm[j])
            kt, vx = _kv_layouts(k, v, FA_KV_HEADS)
            o = _full_attention(q, kt, vx, n_lat, n_ctx)
            x1, h2, idx, gates = _proj_post(o, fa_w_out[j], fa_b_out[j], x_all, mods, lng, lnb, router,
                                            n_tok, n_lat_tiles)
        elif kind == 2:
            x1, h2, idx, gates = _gmlp_layer(x_all, mods, gm_w_in[j], gm_b_in[j], gm_v_norm_g[j], gm_v_norm_b[j],
                                             gm_w_s[j], gm_b_s[j], gm_w_out[j], gm_b_out[j], lng, lnb, router,
                                             n_lat_tiles)
        else:
            q, k, v = _qkv(x_all, mods, wa_w_qkv[j], wa_b_qkv[j], cos_t, sin_t, WA_Q_HEADS, WA_KV_HEADS,
                           n_lat_tiles)
            kt, vx = _kv_layouts(k, v, WA_KV_HEADS)
            o = _window_attention(q, kt, vx, wa_sink[j], n_lat, n_ctx)
            x1, h2, idx, gates = _proj_post(o, wa_w_out[j], wa_b_out[j], x_all, mods, lng, lnb, router,
                                            n_tok, n_lat_tiles)
        x_all = _moe_layer(x1, h2, idx, gates, mods, ln_ffn_g[i], ln_ffn_b[i], exp_w_gate_up[i],
                           exp_b_gate_up[i], exp_w_down[i], exp_b_down[i], n_tok, n_lat)
    return x_all[None]
```

```python
import functools
import math

import numpy as np
import jax
import jax.numpy as jnp
from jax import lax
from jax.experimental import pallas as pl
from jax.experimental.pallas import tpu as pltpu

F32, BF16, I32 = jnp.float32, jnp.bfloat16, jnp.int32

D_MODEL = 1024
DEPTH = 4
GRID_W = 64
N_MOD = 6
FN_GROUPS = 4
HEAD_DIM = 64
FA_Q_HEADS, FA_KV_HEADS = 16, 4
WA_Q_HEADS, WA_KV_HEADS = 16, 2
WINDOW = 128
Q_BLOCK = 128
ROPE_THETA = 10000.0
GM_CHUNK = 128
GM_GROUPS = 8
N_EXPERTS = 32
TOP_K = 4
SWIGLU_LIMIT = 7.0
SWIGLU_ALPHA = 1.702
MOE_BLOCK = 256
LN_EPS = 1e-5
RMS_EPS = 1e-6
NEG = -1e30
ALPHA = (2 * DEPTH) ** 0.25
LOG2E = math.log2(math.e)
Q_SCALE = HEAD_DIM ** -0.5 * LOG2E

LANE = 128
MXU_WIDTH = 256
TM = 256
FLASH_TQ = 256
FLASH_TK = 1280
VMEM_LIMIT = 56 * 2 ** 20


def _params(*sem):
    return pltpu.CompilerParams(dimension_semantics=sem, vmem_limit_bytes=VMEM_LIMIT)


def _dot(a, b):
    return jnp.dot(a, b, preferred_element_type=F32)


def _split(a):
    hi = a.astype(BF16)
    lo = (a - hi.astype(F32)).astype(BF16)
    return hi, lo


def _dot3(a_hi, a_lo, b_hi, b_lo):
    return _dot(a_hi, b_hi) + (_dot(a_hi, b_lo) + _dot(a_lo, b_hi))


def _layer_norm(x, g, b):
    mu = jnp.mean(x, axis=-1, keepdims=True)
    xc = x - mu
    var = jnp.mean(xc * xc, axis=-1, keepdims=True)
    return xc * lax.rsqrt(var + LN_EPS) * g + b


def _top4(logits):
    lane = lax.broadcasted_iota(I32, logits.shape, 1).astype(F32)
    cur = logits
    vals, idxs = [], []
    for _ in range(TOP_K):
        m = jnp.max(cur, axis=-1, keepdims=True)
        i = jnp.min(jnp.where(cur == m, lane, float(LANE)), axis=-1, keepdims=True)
        vals.append(m)
        idxs.append(i)
        cur = jnp.where(lane == i, -jnp.inf, cur)
    exps = [jnp.exp(v - vals[0]) for v in vals]
    inv = 1.0 / (exps[0] + exps[1] + exps[2] + exps[3])
    idx_out = jnp.zeros_like(logits)
    gate_out = jnp.zeros_like(logits)
    for k in range(TOP_K):
        idx_out = jnp.where(lane == float(k), idxs[k], idx_out)
        gate_out = jnp.where(lane == float(k), exps[k] * inv, gate_out)
    return idx_out.astype(I32), gate_out


def _post(y, x, mod, lng, lnb, rw_hi, rw_lo, rb):
    x1 = _layer_norm(ALPHA * x + mod[2:3] * y, lng, lnb)
    h2 = x1 * (1.0 + mod[4:5]) + mod[3:4]
    hh, hl = _split(h2)
    logits = _dot3(hh, hl, rw_hi, rw_lo) + rb
    idx, gates = _top4(logits)
    return x1, h2, idx, gates


def _stack_heads(qb, grp):
    return jnp.concatenate([qb[:, g * HEAD_DIM:(g + 1) * HEAD_DIM] for g in range(grp)], axis=0)


def _unstack_heads(o, grp):
    t = o.shape[0] // grp
    return jnp.concatenate([o[g * t:(g + 1) * t] for g in range(grp)], axis=1)


def _ada_kernel(cs_ref, w_ref, b_ref, o_ref):
    cs = cs_ref[...]
    s = cs * (1.0 / (1.0 + jnp.exp(-cs)))
    sh, sl = _split(s)
    wh, wl = _split(w_ref[0])
    o_ref[0] = _dot3(sh, sl, wh, wl) + b_ref[0]


def _ada(c, c_ctx, ada_w, ada_b):
    d = c.shape[-1]
    nm = ada_w.shape[-1]
    tn = nm // 4
    cs = jnp.zeros((8, d), F32).at[0].set(c[0]).at[1].set(c_ctx)
    out = pl.pallas_call(
        _ada_kernel,
        grid=(DEPTH, nm // tn),
        in_specs=[
            pl.BlockSpec((8, d), lambda i, j: (0, 0)),
            pl.BlockSpec((1, d, tn), lambda i, j: (i, 0, j)),
            pl.BlockSpec((1, 1, tn), lambda i, j: (i, 0, j)),
        ],
        out_specs=pl.BlockSpec((1, 8, tn), lambda i, j: (i, 0, j)),
        out_shape=jax.ShapeDtypeStruct((DEPTH, 8, nm), F32),
        compiler_params=_params("arbitrary", "arbitrary"),
        name="ada",
    )(cs, ada_w, ada_b.reshape(DEPTH, 1, nm))
    return out[:, :2].reshape(DEPTH, 2, N_MOD, d)


def _tile_specs(n_lat_tiles):
    tok = pl.BlockSpec((TM, D_MODEL), lambda t: (t, 0))
    mod = pl.BlockSpec((1, N_MOD, D_MODEL), lambda t: (jnp.where(t >= n_lat_tiles, 1, 0), 0, 0))
    return tok, mod


def _full(shape):
    nd = len(shape)
    return pl.BlockSpec(shape, lambda *_: (0,) * nd)


def _post_out(n_rows):
    shapes = (
        jax.ShapeDtypeStruct((n_rows, D_MODEL), F32),
        jax.ShapeDtypeStruct((n_rows, D_MODEL), F32),
        jax.ShapeDtypeStruct((n_rows, LANE), I32),
        jax.ShapeDtypeStruct((n_rows, LANE), F32),
    )
    specs = (
        pl.BlockSpec((TM, D_MODEL), lambda t: (t, 0)),
        pl.BlockSpec((TM, D_MODEL), lambda t: (t, 0)),
        pl.BlockSpec((TM, LANE), lambda t: (t, 0)),
        pl.BlockSpec((TM, LANE), lambda t: (t, 0)),
    )
    return shapes, specs


def _router_operands(router_w, router_b):
    rw = jnp.zeros((D_MODEL, LANE), F32).at[:, :N_EXPERTS].set(router_w)
    rw_hi = rw.astype(BF16)
    rw_lo = (rw - rw_hi.astype(F32)).astype(BF16)
    rb = jnp.full((1, LANE), NEG, F32).at[0, :N_EXPERTS].set(router_b)
    return rw_hi, rw_lo, rb


def _proj_post_kernel(a_ref, w_ref, b_ref, x_ref, mod_ref, lng_ref, lnb_ref, rwh_ref, rwl_ref, rb_ref,
                      x1_ref, h2_ref, idx_ref, gate_ref):
    y = _dot(a_ref[...], w_ref[...]) + b_ref[...]
    x1, h2, idx, gates = _post(y, x_ref[...], mod_ref[0], lng_ref[...], lnb_ref[...],
                               rwh_ref[...], rwl_ref[...], rb_ref[...])
    x1_ref[...] = x1
    h2_ref[...] = h2
    idx_ref[...] = idx
    gate_ref[...] = gates


def _proj_post(a, w_out, b_out, x, mods, lng, lnb, router, n_rows, n_lat_tiles):
    k = a.shape[1]
    tok, mod = _tile_specs(n_lat_tiles)
    shapes, specs = _post_out(n_rows)
    return pl.pallas_call(
        _proj_post_kernel,
        grid=(n_rows // TM,),
        in_specs=[
            pl.BlockSpec((TM, k), lambda t: (t, 0)),
            _full((k, D_MODEL)), _full((1, D_MODEL)),
            tok, mod, _full((1, D_MODEL)), _full((1, D_MODEL)),
            _full((D_MODEL, LANE)), _full((D_MODEL, LANE)), _full((1, LANE)),
        ],
        out_specs=specs,
        out_shape=shapes,
        compiler_params=_params("arbitrary"),
        name="proj_post",
    )(a, w_out.astype(BF16), b_out.reshape(1, -1), x, mods, lng.reshape(1, -1), lnb.reshape(1, -1), *router)


def _dft_mats(n):
    jk = np.outer(np.arange(n), np.arange(n)) % n
    ang = 2.0 * np.pi * jk / n
    out = []
    for m in (np.cos(ang), np.sin(ang)):
        m32 = jnp.asarray(m, F32)
        hi = m32.astype(BF16)
        out += [hi, (m32 - hi.astype(F32)).astype(BF16)]
    return out


def _channel_dft(h, cc, sc):
    cw = cc[0].shape[0]
    a_parts, b_parts = [], []
    for g in range(h.shape[1] // cw):
        hh, hl = _split(h[:, g * cw:(g + 1) * cw])
        a_parts.append(_dot3(hh, hl, cc[0][...], cc[1][...]))
        b_parts.append(_dot3(hh, hl, sc[0][...], sc[1][...]))
    return jnp.concatenate(a_parts, axis=1), jnp.concatenate(b_parts, axis=1)


def _fourier1_kernel(x_ref, mod_ref, cch_ref, ccl_ref, sch_ref, scl_ref, tc_ref, ts_ref, ur_ref, ui_ref):
    mod = mod_ref[0]
    h = x_ref[...] * (1.0 + mod[1:2]) + mod[0:1]
    a, b = _channel_dft(h, (cch_ref, ccl_ref), (sch_ref, scl_ref))
    tch, tcl = _split(tc_ref[0])
    tsh, tsl = _split(ts_ref[0])
    ah, al = _split(a)
    bh, bl = _split(b)
    ur_ref[...] = _dot3(tch, tcl, ah, al) - _dot3(tsh, tsl, bh, bl)
    ui_ref[...] = -(_dot3(tch, tcl, bh, bl) + _dot3(tsh, tsl, ah, al))


def _fourier2_kernel(ur_ref, ui_ref, c2h_ref, c2l_ref, s2h_ref, s2l_ref, w_ref, b_ref, x_ref, mod_ref,
                     lng_ref, lnb_ref, rwh_ref, rwl_ref, rb_ref, x1_ref, h2_ref, idx_ref, gate_ref, *, norm):
    urh, url = _split(ur_ref[0])
    uih, uil = _split(ui_ref[0])
    mixed = (_dot3(c2h_ref[...], c2l_ref[...], urh, url) + _dot3(s2h_ref[...], s2l_ref[...], uih, uil)) * norm
    y = _dot(mixed.astype(BF16), w_ref[...]) + b_ref[...]
    x1, h2, idx, gates = _post(y, x_ref[...], mod_ref[0], lng_ref[...], lnb_ref[...],
                               rwh_ref[...], rwl_ref[...], rb_ref[...])
    x1_ref[...] = x1
    h2_ref[...] = h2
    idx_ref[...] = idx
    gate_ref[...] = gates


def _fourier_ctx_kernel(x_ref, mod_ref, cch_ref, ccl_ref, sch_ref, scl_ref, cnh_ref, cnl_ref, snh_ref, snl_ref,
                        w_ref, b_ref, lng_ref, lnb_ref, rwh_ref, rwl_ref, rb_ref,
                        x1_ref, h2_ref, idx_ref, gate_ref, *, norm):
    mod = mod_ref[0]
    x = x_ref[...]
    h = x * (1.0 + mod[1:2]) + mod[0:1]
    a, b = _channel_dft(h, (cch_ref, ccl_ref), (sch_ref, scl_ref))
    ah, al = _split(a)
    bh, bl = _split(b)
    mixed = (_dot3(cnh_ref[...], cnl_ref[...], ah, al) - _dot3(snh_ref[...], snl_ref[...], bh, bl)) * norm
    y = _dot(mixed.astype(BF16), w_ref[...]) + b_ref[...]
    x1, h2, idx, gates = _post(y, x, mod, lng_ref[...], lnb_ref[...], rwh_ref[...], rwl_ref[...], rb_ref[...])
    x1_ref[...] = x1
    h2_ref[...] = h2
    idx_ref[...] = idx
    gate_ref[...] = gates


def _fourier_layer(x_all, mods, w_out, b_out, lng, lnb, router, n_lat, n_ctx):
    t_all, d = x_all.shape
    n2 = LANE
    n1 = n_lat // n2
    cw = d // FN_GROUPS
    rows = t_all // n2
    xv = x_all.reshape(rows, n2 * d)
    cmat = _dft_mats(cw)
    w_bf = w_out.astype(BF16)
    b2 = b_out.reshape(1, d)
    lng2, lnb2 = lng.reshape(1, d), lnb.reshape(1, d)

    k1 = jnp.arange(n1, dtype=I32)
    pos = jnp.arange(n1, dtype=I32)[None, None, :] * n2 + jnp.arange(n2, dtype=I32)[:, None, None]
    ang = ((k1[None, :, None] * pos) % n_lat).astype(F32) * (2.0 * math.pi / n_lat)
    tc, ts = jnp.cos(ang), jnp.sin(ang)

    mat = _full((cw, cw))
    ur, ui = pl.pallas_call(
        _fourier1_kernel,
        grid=(n2,),
        in_specs=[
            pl.BlockSpec((n1, d), lambda j: (0, j)),
            pl.BlockSpec((1, N_MOD, d), lambda j: (0, 0, 0)),
            mat, mat, mat, mat,
            pl.BlockSpec((1, n1, n1), lambda j: (j, 0, 0)),
            pl.BlockSpec((1, n1, n1), lambda j: (j, 0, 0)),
        ],
        out_specs=(pl.BlockSpec((n1, d), lambda j: (0, j)), pl.BlockSpec((n1, d), lambda j: (0, j))),
        out_shape=(jax.ShapeDtypeStruct((n1, n2 * d), F32), jax.ShapeDtypeStruct((n1, n2 * d), F32)),
        compiler_params=_params("arbitrary"),
        name="fourier1",
    )(xv, mods, *cmat, tc, ts)

    m2 = _dft_mats(n2)
    mat2 = _full((n2, n2))
    norm = 1.0 / math.sqrt(n_lat * cw)
    out_shapes = (
        jax.ShapeDtypeStruct((n1, n2 * d), F32),
        jax.ShapeDtypeStruct((n1, n2 * d), F32),
        jax.ShapeDtypeStruct((n1, n2 * LANE), I32),
        jax.ShapeDtypeStruct((n1, n2 * LANE), F32),
    )
    strided = pl.BlockSpec((n2, d), lambda k: (0, k))
    strided_l = pl.BlockSpec((n2, LANE), lambda k: (0, k))
    outs = pl.pallas_call(
        functools.partial(_fourier2_kernel, norm=norm),
        grid=(n1,),
        in_specs=[
            pl.BlockSpec((1, n2, d), lambda k: (k, 0, 0)),
            pl.BlockSpec((1, n2, d), lambda k: (k, 0, 0)),
            mat2, mat2, mat2, mat2,
            _full((d, d)), _full((1, d)),
            strided,
            pl.BlockSpec((1, N_MOD, d), lambda k: (0, 0, 0)),
            _full((1, d)), _full((1, d)),
            _full((d, LANE)), _full((d, LANE)), _full((1, LANE)),
        ],
        out_specs=(strided, strided, strided_l, strided_l),
        out_shape=out_shapes,
        compiler_params=_params("arbitrary"),
        name="fourier2",
    )(ur.reshape(n1, n2, d), ui.reshape(n1, n2, d), *m2, w_bf, b2, xv, mods, lng2, lnb2, *router)
    lat = (outs[0].reshape(n_lat, d), outs[1].reshape(n_lat, d),
           outs[2].reshape(n_lat, LANE), outs[3].reshape(n_lat, LANE))

    cn = _dft_mats(n_ctx)
    matn = _full((n_ctx, n_ctx))
    shapes = (
        jax.ShapeDtypeStruct((n_ctx, d), F32), jax.ShapeDtypeStruct((n_ctx, d), F32),
        jax.ShapeDtypeStruct((n_ctx, LANE), I32), jax.ShapeDtypeStruct((n_ctx, LANE), F32),
    )
    ctx = pl.pallas_call(
        functools.partial(_fourier_ctx_kernel, norm=1.0 / math.sqrt(n_ctx * cw)),
        grid=(1,),
        in_specs=[
            pl.BlockSpec((n_ctx, d), lambda i: (n_lat // n_ctx, 0)),
            pl.BlockSpec((1, N_MOD, d), lambda i: (1, 0, 0)),
            mat, mat, mat, mat, matn, matn, matn, matn,
            _full((d, d)), _full((1, d)), _full((1, d)), _full((1, d)),
            _full((d, LANE)), _full((d, LANE)), _full((1, LANE)),
        ],
        out_specs=(_full((n_ctx, d)), _full((n_ctx, d)), _full((n_ctx, LANE)), _full((n_ctx, LANE))),
        out_shape=shapes,
        compiler_params=_params("arbitrary"),
        name="fourier_ctx",
    )(x_all, mods, *cmat, *cn, w_bf, b2, lng2, lnb2, *router)
    return tuple(jnp.concatenate([a, b], axis=0) for a, b in zip(lat, ctx))


def _qkv_kernel(x_ref, mod_ref, w_ref, b_ref, cos_ref, sin_ref, *rest, n_qk, rms):
    if rms:
        gain_ref, ind_ref, indt_ref, q_ref, k_ref, v_ref = rest
    else:
        q_ref, k_ref, v_ref = rest
    mod = mod_ref[0]
    h = x_ref[...] * (1.0 + mod[1:2]) + mod[0:1]
    y = _dot(h.astype(BF16), w_ref[...]) + b_ref[...]
    qk = y[:, :n_qk]
    if rms:
        sh, sl = _split(qk * qk)
        ms = _dot(sh, ind_ref[...]) + _dot(sl, ind_ref[...])
        mh, ml = _split(ms)
        msb = _dot(mh, indt_ref[...]) + _dot(ml, indt_ref[...])
        qk = qk * lax.rsqrt(msb + RMS_EPS) * gain_ref[...]
    cos = cos_ref[...]
    sin = sin_ref[...]
    even = (lax.broadcasted_iota(I32, cos.shape, 1) & 1) == 0
    parts = []
    for c in range(n_qk // LANE):
        z = qk[:, c * LANE:(c + 1) * LANE]
        swapped = jnp.where(even, pltpu.roll(z, LANE - 1, 1), pltpu.roll(z, 1, 1))
        parts.append(z * cos + swapped * sin)
    nq = q_ref.shape[1]
    q_ref[...] = (jnp.concatenate(parts[:nq // LANE], axis=1) * Q_SCALE).astype(BF16)
    k_ref[...] = jnp.concatenate(parts[nq // LANE:], axis=1).astype(BF16)
    v_ref[...] = y[:, n_qk:].astype(BF16)


def _qkv(x_all, mods, w_qkv, b_qkv, cos_t, sin_t, n_q, n_kv, n_lat_tiles, q_norm=None, k_norm=None):
    t_all, d = x_all.shape
    nq, nk = n_q * HEAD_DIM, n_kv * HEAD_DIM
    n_qk, n_all = nq + nk, nq + 2 * nk
    rms = q_norm is not None
    tok, mod = _tile_specs(n_lat_tiles)
    in_specs = [tok, mod, _full((d, n_all)), _full((1, n_all)),
                pl.BlockSpec((TM, LANE), lambda t: (t, 0)), pl.BlockSpec((TM, LANE), lambda t: (t, 0))]
    args = [x_all, mods, w_qkv.astype(BF16), b_qkv.reshape(1, n_all), cos_t, sin_t]
    if rms:
        gain = jnp.concatenate([jnp.tile(q_norm, n_q), jnp.tile(k_norm, n_kv)]).reshape(1, n_qk)
        head = np.arange(n_qk) // HEAD_DIM
        ind = np.zeros((n_qk, LANE), np.float32)
        ind[np.arange(n_qk), head] = 1.0 / HEAD_DIM
        indt = np.zeros((LANE, n_qk), np.float32)
        indt[head, np.arange(n_qk)] = 1.0
        in_specs += [_full((1, n_qk)), _full((n_qk, LANE)), _full((LANE, n_qk))]
        args += [gain, jnp.asarray(ind, BF16), jnp.asarray(indt, BF16)]
    return pl.pallas_call(
        functools.partial(_qkv_kernel, n_qk=n_qk, rms=rms),
        grid=(t_all // TM,),
        in_specs=in_specs,
        out_specs=(pl.BlockSpec((TM, nq), lambda t: (t, 0)), pl.BlockSpec((TM, nk), lambda t: (t, 0)),
                   pl.BlockSpec((TM, nk), lambda t: (t, 0))),
        out_shape=(jax.ShapeDtypeStruct((t_all, nq), BF16), jax.ShapeDtypeStruct((t_all, nk), BF16),
                   jax.ShapeDtypeStruct((t_all, nk), BF16)),
        compiler_params=_params("arbitrary"),
        name="qkv",
    )(*args)


def _rope_tables(n_lat, n_ctx):
    rows = n_lat // GRID_W
    row = jnp.repeat(jnp.arange(rows, dtype=F32), GRID_W)
    col = jnp.tile(jnp.arange(GRID_W, dtype=F32), rows)
    n_freq = HEAD_DIM // 4
    inv = ROPE_THETA ** (-jnp.arange(n_freq, dtype=F32) / n_freq)
    ang = jnp.concatenate([row[:, None] * inv, col[:, None] * inv], axis=-1)
    ang = jnp.concatenate([ang, jnp.zeros((n_ctx, HEAD_DIM // 2), F32)], axis=0)
    cos = jnp.tile(jnp.repeat(jnp.cos(ang), 2, axis=1), (1, LANE // HEAD_DIM))
    sin = jnp.tile(jnp.repeat(jnp.sin(ang), 2, axis=1), (1, LANE // HEAD_DIM))
    sign = jnp.where(jnp.arange(LANE) % 2 == 0, -1.0, 1.0).astype(F32)
    return cos, sin * sign


def _kv_layouts(k, v, n_kv):
    t_all = k.shape[0]
    kt = k.reshape(t_all, n_kv, HEAD_DIM).transpose(1, 2, 0)
    vh = v.reshape(t_all, n_kv, HEAD_DIM).transpose(1, 0, 2)
    pad = jnp.zeros((n_kv, t_all, LANE - HEAD_DIM), BF16).at[:, :, 0].set(1.0)
    return kt, jnp.concatenate([vh, pad], axis=-1)


def _flash_kernel(qt_ref, k_ref, vt_ref, o_ref, s_even, s_odd, *, tk, n_lat, n_ctx):
    qt = qt_ref[0]
    cols = qt.shape[1]
    is_ctx = pl.program_id(1) == n_lat // FLASH_TQ
    n_chunks = (n_lat + n_ctx) // tk

    def finish(acc):
        o_ref[0] = (acc[:HEAD_DIM] / acc[HEAD_DIM:HEAD_DIM + 1]).astype(BF16)

    groups = [slice(c * MXU_WIDTH, (c + 1) * MXU_WIDTH) for c in range(cols // MXU_WIDTH)]

    def scores(j, buf, g):
        off = pl.multiple_of(j * tk, tk)
        s = _dot(k_ref[0, pl.ds(off, tk), :], qt[:, g])
        buf[:, g] = s
        return jnp.max(s, axis=0, keepdims=True)

    def accumulate(j, buf, g, m, acc, mc):
        off = pl.multiple_of(j * tk, tk)
        m_new = jnp.maximum(m, mc)
        p = jnp.exp2(buf[:, g] - m_new)
        acc = jnp.exp2(m - m_new) * acc + _dot(vt_ref[0, :, pl.ds(off, tk)], p.astype(BF16))
        return m_new, acc

    @pl.when(jnp.logical_not(is_ctx))
    def _():
        def step(j, cur, nxt, carry):
            out = []
            for g, (m, acc, mc) in zip(groups, carry):
                mc_next = scores(j + 1, nxt, g)
                out.append(accumulate(j, cur, g, m, acc, mc) + (mc_next,))
            return out

        def body(jj, carry):
            carry = step(2 * jj, s_even, s_odd, carry)
            return step(2 * jj + 1, s_odd, s_even, carry)

        init = [(jnp.full((1, MXU_WIDTH), NEG, F32), jnp.zeros((LANE, MXU_WIDTH), F32), scores(0, s_even, g))
                for g in groups]
        carry = lax.fori_loop(0, (n_chunks - 1) // 2, body, init)
        finish(jnp.concatenate([accumulate(n_chunks - 1, s_even, g, m, acc, mc)[1]
                                for g, (m, acc, mc) in zip(groups, carry)], axis=1))

    @pl.when(is_ctx)
    def _():
        s = _dot(k_ref[0, n_lat:n_lat + n_ctx, :], qt)
        p = jnp.exp2(s - jnp.max(s, axis=0, keepdims=True))
        finish(_dot(vt_ref[0, :, n_lat:n_lat + n_ctx], p.astype(BF16)))


def _full_attention(q, k, v, n_lat, n_ctx):
    t_all, nq = q.shape
    n_kv = k.shape[1] // HEAD_DIM
    grp = nq // HEAD_DIM // n_kv
    nt = t_all // FLASH_TQ
    cols = grp * FLASH_TQ
    assert n_ctx == FLASH_TQ and t_all % FLASH_TK == 0 and (t_all // FLASH_TK) % 2 == 1
    qt = q.reshape(nt, FLASH_TQ, n_kv, grp, HEAD_DIM).transpose(2, 4, 0, 3, 1).reshape(n_kv, HEAD_DIM, nt * cols)
    kh = k.reshape(t_all, n_kv, HEAD_DIM).transpose(1, 0, 2)
    vt = v.reshape(t_all, n_kv, HEAD_DIM).transpose(1, 2, 0)
    pad = jnp.zeros((n_kv, LANE - HEAD_DIM, t_all), BF16).at[:, 0, :].set(1.0)
    vt = jnp.concatenate([vt, pad], axis=1)
    ot = pl.pallas_call(
        functools.partial(_flash_kernel, tk=FLASH_TK, n_lat=n_lat, n_ctx=n_ctx),
        grid=(n_kv, nt),
        in_specs=[
            pl.BlockSpec((1, HEAD_DIM, cols), lambda h, i: (h, 0, i)),
            pl.BlockSpec((1, t_all, HEAD_DIM), lambda h, i: (h, 0, 0)),
            pl.BlockSpec((1, LANE, t_all), lambda h, i: (h, 0, 0)),
        ],
        out_specs=pl.BlockSpec((1, HEAD_DIM, cols), lambda h, i: (h, 0, i)),
        out_shape=jax.ShapeDtypeStruct((n_kv, HEAD_DIM, nt * cols), BF16),
        scratch_shapes=[pltpu.VMEM((FLASH_TK, cols), F32), pltpu.VMEM((FLASH_TK, cols), F32)],
        compiler_params=_params("arbitrary", "arbitrary"),
        name="flash",
    )(qt, kh, vt)
    return ot.reshape(n_kv, HEAD_DIM, nt, grp, FLASH_TQ).transpose(2, 4, 0, 3, 1).reshape(t_all, nq)


def _window_kernel(q_ref, kp_ref, kc_ref, kn_ref, kx_ref, vp_ref, vc_ref, vn_ref, vx_ref, sink_ref, o_ref,
                   *, grp, nb):
    i = pl.program_id(1)
    qs = _stack_heads(q_ref[...], grp)
    rows = qs.shape[0]
    qi = lax.broadcasted_iota(I32, (rows, Q_BLOCK), 0) & (Q_BLOCK - 1)
    kj = lax.broadcasted_iota(I32, (rows, Q_BLOCK), 1)
    sp = jnp.where(kj >= jnp.where(i > 0, qi, Q_BLOCK), _dot(qs, kp_ref[0]), NEG)
    sc = _dot(qs, kc_ref[0])
    sn = jnp.where(kj <= jnp.where(i < nb - 1, qi, -1), _dot(qs, kn_ref[0]), NEG)
    sx = _dot(qs, kx_ref[0])
    sink = sink_ref[0][:, :1]
    m = jnp.maximum(jnp.maximum(jnp.max(sp, axis=-1, keepdims=True), jnp.max(sc, axis=-1, keepdims=True)),
                    jnp.maximum(jnp.max(sn, axis=-1, keepdims=True), jnp.max(sx, axis=-1, keepdims=True)))
    m = jnp.maximum(m, sink)
    acc = (_dot(jnp.exp2(sp - m).astype(BF16), vp_ref[0]) + _dot(jnp.exp2(sc - m).astype(BF16), vc_ref[0])
           + _dot(jnp.exp2(sn - m).astype(BF16), vn_ref[0]) + _dot(jnp.exp2(sx - m).astype(BF16), vx_ref[0]))
    den = acc[:, HEAD_DIM:HEAD_DIM + 1] + jnp.exp2(sink - m)
    o_ref[...] = _unstack_heads(acc[:, :HEAD_DIM] / den, grp).astype(BF16)


def _window_attention(q, kt, vx, sink, n_lat, n_ctx):
    nq = q.shape[1]
    n_kv = kt.shape[0]
    grp = nq // HEAD_DIM // n_kv
    gw = grp * HEAD_DIM
    nb = n_lat // Q_BLOCK
    cb = n_lat // n_ctx
    sink_rows = jnp.broadcast_to((sink.reshape(n_kv, grp) * LOG2E)[:, :, None, None],
                                 (n_kv, grp, Q_BLOCK, LANE)).reshape(n_kv, grp * Q_BLOCK, LANE)
    kspec = lambda f: pl.BlockSpec((1, HEAD_DIM, Q_BLOCK), lambda h, i: (h, 0, f(i)))
    vspec = lambda f: pl.BlockSpec((1, Q_BLOCK, LANE), lambda h, i: (h, f(i), 0))
    prev = lambda i: jnp.maximum(i - 1, 0)
    cur = lambda i: i
    nxt = lambda i: jnp.minimum(i + 1, nb - 1)
    return pl.pallas_call(
        functools.partial(_window_kernel, grp=grp, nb=nb),
        grid=(n_kv, nb),
        in_specs=[
            pl.BlockSpec((Q_BLOCK, gw), lambda h, i: (i, h)),
            kspec(prev), kspec(cur), kspec(nxt),
            pl.BlockSpec((1, HEAD_DIM, n_ctx), lambda h, i: (h, 0, cb)),
            vspec(prev), vspec(cur), vspec(nxt),
            pl.BlockSpec((1, n_ctx, LANE), lambda h, i: (h, cb, 0)),
            pl.BlockSpec((1, grp * Q_BLOCK, LANE), lambda h, i: (h, 0, 0)),
        ],
        out_specs=pl.BlockSpec((Q_BLOCK, gw), lambda h, i: (i, h)),
        out_shape=jax.ShapeDtypeStruct((n_lat, nq), BF16),
        compiler_params=_params("arbitrary", "arbitrary"),
        name="window",
    )(q, kt, kt, kt, kt, vx, vx, vx, vx, sink_rows)


def _gmlp_kernel(x_ref, mod_ref, win_ref, bin_ref, vg_ref, vb_ref, ws_ref, bs_ref, wout_ref, bout_ref,
                 lng_ref, lnb_ref, rwh_ref, rwl_ref, rb_ref, x1_ref, h2_ref, idx_ref, gate_ref):
    mod = mod_ref[0]
    x = x_ref[...]
    h = x * (1.0 + mod[1:2]) + mod[0:1]
    z = _dot(h.astype(BF16), win_ref[...]) + bin_ref[...]
    z = 0.5 * z * (1.0 + lax.erf(z * (2.0 ** -0.5)))
    half = z.shape[1] // 2
    u = z[:, :half]
    v = _layer_norm(z[:, half:], vg_ref[...], vb_ref[...]).astype(BF16)
    cw = half // GM_GROUPS
    chunks = []
    for c in range(x.shape[0] // GM_CHUNK):
        vc = v[c * GM_CHUNK:(c + 1) * GM_CHUNK]
        chunks.append(jnp.concatenate(
            [_dot(ws_ref[g], vc[:, g * cw:(g + 1) * cw]) + bs_ref[g] for g in range(GM_GROUPS)], axis=1))
    gated = u * jnp.concatenate(chunks, axis=0)
    y = _dot(gated.astype(BF16), wout_ref[...]) + bout_ref[...]
    x1, h2, idx, gates = _post(y, x, mod, lng_ref[...], lnb_ref[...], rwh_ref[...], rwl_ref[...], rb_ref[...])
    x1_ref[...] = x1
    h2_ref[...] = h2
    idx_ref[...] = idx
    gate_ref[...] = gates


def _gmlp_layer(x_all, mods, w_in, b_in, vg, vb, w_s, b_s, w_out, b_out, lng, lnb, router, n_lat_tiles):
    t_all, d = x_all.shape
    dffn = w_in.shape[1]
    half = dffn // 2
    cw = half // GM_GROUPS
    tok, mod = _tile_specs(n_lat_tiles)
    shapes, specs = _post_out(t_all)
    bs_full = jnp.broadcast_to(b_s[:, :, None], (GM_GROUPS, GM_CHUNK, cw))
    return pl.pallas_call(
        _gmlp_kernel,
        grid=(t_all // TM,),
        in_specs=[
            tok, mod, _full((d, dffn)), _full((1, dffn)), _full((1, half)), _full((1, half)),
            _full((GM_GROUPS, GM_CHUNK, GM_CHUNK)), _full((GM_GROUPS, GM_CHUNK, cw)),
            _full((half, d)), _full((1, d)), _full((1, d)), _full((1, d)),
            _full((d, LANE)), _full((d, LANE)), _full((1, LANE)),
        ],
        out_specs=specs,
        out_shape=shapes,
        compiler_params=_params("arbitrary"),
        name="gmlp",
    )(x_all, mods, w_in.astype(BF16), b_in.reshape(1, dffn), vg.reshape(1, half), vb.reshape(1, half),
      w_s.astype(BF16), bs_full, w_out.astype(BF16), b_out.reshape(1, d), lng.reshape(1, d), lnb.reshape(1, d),
      *router)


COMB_TM = 128


def _route_meta(top_idx, n_tok):
    n_assign = n_tok * TOP_K
    n_blocks = -(-n_assign // MOE_BLOCK) + N_EXPERTS
    ids = jnp.arange(N_EXPERTS, dtype=I32)
    counts = jnp.sum((top_idx[:n_tok, :TOP_K, None] == ids).astype(I32), axis=(0, 1))
    padded = (counts + MOE_BLOCK - 1) // MOE_BLOCK * MOE_BLOCK
    ends_pad = jnp.cumsum(padded)
    base = ends_pad - padded
    starts = jnp.arange(n_blocks, dtype=I32) * MOE_BLOCK
    block_expert = jnp.minimum(jnp.sum((ends_pad[None, :] <= starts[:, None]).astype(I32), axis=1), N_EXPERTS - 1)
    last_blk = jnp.where(padded > 0, ends_pad - MOE_BLOCK, -1)
    tail = ends_pad[-1] + ids * MOE_BLOCK
    tail = jnp.where(tail < n_blocks * MOE_BLOCK, tail, -1)
    base_b = jnp.broadcast_to(base.astype(F32)[:, None], (N_EXPERTS, LANE))
    return block_expert.astype(I32), base_b, jnp.concatenate([last_blk, tail]).astype(I32), n_blocks


def _rank_kernel(idx_ref, base_ref, upper_ref, dest_ref, run_ref):
    @pl.when(pl.program_id(0) == 0)
    def _():
        run_ref[...] = jnp.zeros(run_ref.shape, F32)

    eid = lax.broadcasted_iota(I32, (N_EXPERTS, COMB_TM), 0)
    onehots = [(eid == idx_ref[k:k + 1, :]).astype(F32) for k in range(TOP_K)]
    cnt = onehots[0] + onehots[1] + onehots[2] + onehots[3]
    before = _dot(cnt.astype(BF16), upper_ref[...])
    slot = base_ref[...] + run_ref[...] + before
    for k in range(TOP_K):
        dest_ref[0, k:k + 1, :] = jnp.sum(onehots[k] * slot, axis=0, keepdims=True).astype(I32)
    run_ref[...] = run_ref[...] + jnp.sum(cnt, axis=1, keepdims=True)


def _rank(top_idx, base_b, n_tok):
    n_tiles = n_tok // COMB_TM
    idx_t = top_idx[:n_tok, :TOP_K].T
    upper = jnp.asarray(np.triu(np.ones((COMB_TM, COMB_TM), np.float32), 1), BF16)
    return pl.pallas_call(
        _rank_kernel,
        grid=(n_tiles,),
        in_specs=[pl.BlockSpec((TOP_K, COMB_TM), lambda t: (0, t)), _full((N_EXPERTS, LANE)),
                  _full((COMB_TM, COMB_TM))],
        out_specs=pl.BlockSpec((1, TOP_K, COMB_TM), lambda t: (t, 0, 0)),
        out_shape=jax.ShapeDtypeStruct((n_tiles, TOP_K, COMB_TM), I32),
        scratch_shapes=[pltpu.VMEM((N_EXPERTS, LANE), F32)],
        compiler_params=_params("arbitrary"),
        name="rank",
    )(idx_t, base_b, upper)


def _dispatch_kernel(dest_ref, zrow_ref, h_ref, xs_hbm, hbuf, zbuf, sem, zsem, *, n_tiles):
    t = pl.program_id(0)
    slot = t % 2
    n_zero = 2 * N_EXPERTS

    def zero_copy(j):
        row = pl.multiple_of(zrow_ref[j], MOE_BLOCK)
        return pltpu.make_async_copy(zbuf, xs_hbm.at[pl.ds(row, MOE_BLOCK), :], zsem.at[0])

    def wait_rows(s):
        for _ in range(TOP_K):
            pltpu.make_async_copy(hbuf.at[s], xs_hbm.at[pl.ds(0, COMB_TM), :], sem.at[s]).wait()

    @pl.when(t == 0)
    def _():
        zbuf[...] = jnp.zeros(zbuf.shape, F32)
        for j in range(n_zero):
            @pl.when(zrow_ref[j] >= 0)
            def _():
                zero_copy(j).start()
        for j in range(n_zero):
            @pl.when(zrow_ref[j] >= 0)
            def _():
                zero_copy(j).wait()

    @pl.when(t >= 2)
    def _():
        wait_rows(slot)

    hbuf[slot] = h_ref[...]
    for k in range(TOP_K):
        for r in range(COMB_TM):
            row = dest_ref[(t * TOP_K + k) * COMB_TM + r]
            pltpu.make_async_copy(hbuf.at[slot, pl.ds(r, 1), :], xs_hbm.at[pl.ds(row, 1), :], sem.at[slot]).start()

    @pl.when(t == n_tiles - 1)
    def _():
        wait_rows(slot)
        if n_tiles > 1:
            wait_rows(1 - slot)


def _dispatch(h2, dest, zrows, n_tok, n_blocks):
    d = h2.shape[1]
    n_tiles = n_tok // COMB_TM
    grid_spec = pltpu.PrefetchScalarGridSpec(
        num_scalar_prefetch=2,
        grid=(n_tiles,),
        in_specs=[pl.BlockSpec((COMB_TM, d), lambda t, dest, zr: (t, 0))],
        out_specs=pl.BlockSpec(memory_space=pl.ANY),
        scratch_shapes=[pltpu.VMEM((2, COMB_TM, d), F32), pltpu.VMEM((MOE_BLOCK, d), F32),
                        pltpu.SemaphoreType.DMA((2,)), pltpu.SemaphoreType.DMA((1,))],
    )
    return pl.pallas_call(
        functools.partial(_dispatch_kernel, n_tiles=n_tiles),
        grid_spec=grid_spec,
        out_shape=jax.ShapeDtypeStruct((n_blocks * MOE_BLOCK, d), F32),
        compiler_params=_params("arbitrary"),
        name="dispatch",
    )(dest.reshape(-1), zrows, h2)


EXP_CHUNK = 512


def _expert_kernel(be_ref, x_ref, wgu_ref, bgu_ref, wd_ref, bd_ref, sel_e_ref, sel_o_ref, y_ref,
                   wg_s, wl_s, wd_s, bg_s, bl_s):
    b = pl.program_id(0)
    ff = wg_s.shape[1]
    half = EXP_CHUNK // 2

    @pl.when(jnp.logical_or(b == 0, be_ref[b] != be_ref[jnp.maximum(b - 1, 0)]))
    def _():
        for c in range(2 * ff // EXP_CHUNK):
            w = wgu_ref[0, 0, :, c * EXP_CHUNK:(c + 1) * EXP_CHUNK].astype(BF16)
            wg_s[:, c * half:(c + 1) * half] = _dot(w, sel_e_ref[...]).astype(BF16)
            wl_s[:, c * half:(c + 1) * half] = _dot(w, sel_o_ref[...]).astype(BF16)
            bh, bl = _split(jnp.broadcast_to(bgu_ref[0, 0, :, c * EXP_CHUNK:(c + 1) * EXP_CHUNK], (8, EXP_CHUNK)))
            bg_s[:, c * half:(c + 1) * half] = _dot(bh, sel_e_ref[...]) + _dot(bl, sel_e_ref[...])
            bl_s[:, c * half:(c + 1) * half] = _dot(bh, sel_o_ref[...]) + _dot(bl, sel_o_ref[...])
        wd_s[...] = wd_ref[0, 0].astype(BF16)

    x = x_ref[...].astype(BF16)
    glu = jnp.minimum(_dot(x, wg_s[...]) + bg_s[0:1, :], SWIGLU_LIMIT)
    lin = jnp.clip(_dot(x, wl_s[...]) + bl_s[0:1, :], -SWIGLU_LIMIT, SWIGLU_LIMIT)
    act = glu * (1.0 / (1.0 + jnp.exp(-SWIGLU_ALPHA * glu))) * (lin + 1.0)
    y_ref[...] = _dot(act.astype(BF16), wd_s[...]) + bd_ref[0, 0]


def _experts(xs, block_expert, n_blocks, w_gate_up, b_gate_up, w_down, b_down, layer):
    d = xs.shape[1]
    ff = w_down.shape[2]
    sel = np.zeros((2, EXP_CHUNK, EXP_CHUNK // 2), np.float32)
    sel[0, 2 * np.arange(EXP_CHUNK // 2), np.arange(EXP_CHUNK // 2)] = 1.0
    sel[1, 2 * np.arange(EXP_CHUNK // 2) + 1, np.arange(EXP_CHUNK // 2)] = 1.0
    wspec = lambda r, c: pl.BlockSpec((1, 1, r, c), lambda b, be: (layer, be[b], 0, 0))
    sspec = pl.BlockSpec((EXP_CHUNK, EXP_CHUNK // 2), lambda b, be: (0, 0))
    grid_spec = pltpu.PrefetchScalarGridSpec(
        num_scalar_prefetch=1,
        grid=(n_blocks,),
        in_specs=[pl.BlockSpec((MOE_BLOCK, d), lambda b, be: (b, 0)), wspec(d, 2 * ff), wspec(1, 2 * ff),
                  wspec(ff, d), wspec(1, d), sspec, sspec],
        out_specs=pl.BlockSpec((MOE_BLOCK, d), lambda b, be: (b, 0)),
        scratch_shapes=[pltpu.VMEM((d, ff), BF16), pltpu.VMEM((d, ff), BF16), pltpu.VMEM((ff, d), BF16),
                        pltpu.VMEM((8, ff), F32), pltpu.VMEM((8, ff), F32)],
    )
    return pl.pallas_call(
        _expert_kernel,
        grid_spec=grid_spec,
        out_shape=jax.ShapeDtypeStruct((n_blocks * MOE_BLOCK, d), F32),
        compiler_params=_params("arbitrary"),
        name="experts",
    )(block_expert, xs, w_gate_up, b_gate_up[:, :, None, :], w_down, b_down[:, :, None, :],
      jnp.asarray(sel[0], BF16), jnp.asarray(sel[1], BF16))


def _gather_rows(idx_ref, base, n, src_hbm, dst, sem):
    for r in range(n):
        pltpu.make_async_copy(src_hbm.at[pl.ds(idx_ref[base + r], 1), :], dst.at[pl.ds(r, 1), :], sem).start()


def _combine_kernel(pos_ref, y_hbm, gate_ref, x_ref, mod_ref, lng_ref, lnb_ref, o_ref, ybuf, sem, *, n_tiles):
    t = pl.program_id(0)
    slot = t % 2
    n = COMB_TM * TOP_K

    @pl.when(t == 0)
    def _():
        _gather_rows(pos_ref, 0, n, y_hbm, ybuf.at[0], sem.at[0])

    @pl.when(t + 1 < n_tiles)
    def _():
        _gather_rows(pos_ref, (t + 1) * n, n, y_hbm, ybuf.at[1 - slot], sem.at[1 - slot])

    pltpu.make_async_copy(y_hbm.at[pl.ds(0, n), :], ybuf.at[slot], sem.at[slot]).wait()
    gates = gate_ref[...]
    f = jnp.zeros(x_ref.shape, F32)
    for k in range(TOP_K):
        f = f + gates[:, k:k + 1] * ybuf[slot, pl.ds(k * COMB_TM, COMB_TM), :]
    o_ref[...] = _layer_norm(ALPHA * x_ref[...] + mod_ref[0][5:6] * f, lng_ref[...], lnb_ref[...])


def _combine(yb, dest, gates, x1, mods, lng, lnb, n_tok, n_lat):
    d = x1.shape[1]
    n_tiles = n_tok // COMB_TM
    n_lat_tiles = n_lat // COMB_TM
    pos = dest.reshape(-1)
    grid_spec = pltpu.PrefetchScalarGridSpec(
        num_scalar_prefetch=1,
        grid=(n_tiles,),
        in_specs=[
            pl.BlockSpec(memory_space=pl.ANY),
            pl.BlockSpec((COMB_TM, LANE), lambda t, pos: (t, 0)),
            pl.BlockSpec((COMB_TM, d), lambda t, pos: (t, 0)),
            pl.BlockSpec((1, N_MOD, d), lambda t, pos: (jnp.where(t >= n_lat_tiles, 1, 0), 0, 0)),
            pl.BlockSpec((1, d), lambda t, pos: (0, 0)),
            pl.BlockSpec((1, d), lambda t, pos: (0, 0)),
        ],
        out_specs=pl.BlockSpec((COMB_TM, d), lambda t, pos: (t, 0)),
        scratch_shapes=[pltpu.VMEM((2, COMB_TM * TOP_K, d), F32), pltpu.SemaphoreType.DMA((2,))],
    )
    return pl.pallas_call(
        functools.partial(_combine_kernel, n_tiles=n_tiles),
        grid_spec=grid_spec,
        out_shape=jax.ShapeDtypeStruct((n_tok, d), F32),
        compiler_params=_params("arbitrary"),
        name="combine",
    )(pos, yb, gates, x1, mods, lng.reshape(1, d), lnb.reshape(1, d))


def _moe_layer(x1, h2, top_idx, gates, mods, lng, lnb, w_gate_up, b_gate_up, w_down, b_down, layer, n_tok, n_lat):
    block_expert, base_b, zrows, n_blocks = _route_meta(top_idx, n_tok)
    dest = _rank(top_idx, base_b, n_tok)
    xs = _dispatch(h2, dest, zrows, n_tok, n_blocks)
    yb = _experts(xs, block_expert, n_blocks, w_gate_up, b_gate_up, w_down, b_down, layer)
    return _combine(yb, dest, gates, x1, mods, lng, lnb, n_tok, n_lat)


def kernel(x, c, ctx, c_ctx, ada_w, ada_b, ln_mix_g, ln_mix_b, ln_ffn_g, ln_ffn_b, fn_w_out, fn_b_out, fa_w_qkv, fa_b_qkv, fa_q_norm, fa_k_norm, fa_w_out, fa_b_out, gm_w_in, gm_b_in, gm_v_norm_g, gm_v_norm_b, gm_w_s, gm_b_s, gm_w_out, gm_b_out, wa_w_qkv, wa_b_qkv, wa_sink, wa_w_out, wa_b_out, router_w, router_b, exp_w_gate_up, exp_b_gate_up, exp_w_down, exp_b_down):
    bsz, n_lat, d = x.shape
    n_ctx = ctx.shape[1]
    assert bsz == 1 and d == D_MODEL and n_lat == LANE * LANE and n_lat % n_ctx == 0 and n_ctx % TM == 0
    t_all = n_lat + n_ctx
    n_lat_tiles = n_lat // TM
    x_all = jnp.concatenate([x[0], ctx[0]], axis=0)
    mods_all = _ada(c, c_ctx, ada_w, ada_b)
    cos_t, sin_t = _rope_tables(n_lat, n_ctx)

    for i in range(DEPTH):
        kind, j = i % 4, i // 4
        last = i == DEPTH - 1
        n_tok = n_lat if last else t_all
        mods = mods_all[i]
        router = _router_operands(router_w[i], router_b[i])
        lng, lnb = ln_mix_g[i], ln_mix_b[i]
        if kind == 0:
            x1, h2, idx, gates = _fourier_layer(x_all, mods, fn_w_out[j], fn_b_out[j], lng, lnb, router,
                                                n_lat, n_ctx)
        elif kind == 1:
            q, k, v = _qkv(x_all, mods, fa_w_qkv[j], fa_b_qkv[j], cos_t, sin_t, FA_Q_HEADS, FA_KV_HEADS,
                           n_lat_tiles, fa_q_norm[j], fa_k_norm[j])
            o = _full_attention(q, k, v, n_lat, n_ctx)
            x1, h2, idx, gates = _proj_post(o, fa_w_out[j], fa_b_out[j], x_all, mods, lng, lnb, router,
                                            n_tok, n_lat_tiles)
        elif kind == 2:
            x1, h2, idx, gates = _gmlp_layer(x_all, mods, gm_w_in[j], gm_b_in[j], gm_v_norm_g[j], gm_v_norm_b[j],
                                             gm_w_s[j], gm_b_s[j], gm_w_out[j], gm_b_out[j], lng, lnb, router,
                                             n_lat_tiles)
        else:
            q, k, v = _qkv(x_all, mods, wa_w_qkv[j], wa_b_qkv[j], cos_t, sin_t, WA_Q_HEADS, WA_KV_HEADS,
                           n_lat_tiles)
            kt, vx = _kv_layouts(k, v, WA_KV_HEADS)
            o = _window_attention(q, kt, vx, wa_sink[j], n_lat, n_ctx)
            x1, h2, idx, gates = _proj_post(o, wa_w_out[j], wa_b_out[j], x_all, mods, lng, lnb, router,
                                            n_tok, n_lat_tiles)
        x_all = _moe_layer(x1, h2, idx, gates, mods, ln_ffn_g[i], ln_ffn_b[i], exp_w_gate_up,
                           exp_b_gate_up, exp_w_down, exp_b_down, i, n_tok, n_lat)
    return x_all[None]
```

```python
import functools
import math

import numpy as np
import jax
import jax.numpy as jnp
from jax import lax
from jax.experimental import pallas as pl
from jax.experimental.pallas import tpu as pltpu

F32, BF16, I32 = jnp.float32, jnp.bfloat16, jnp.int32

D_MODEL = 1024
DEPTH = 4
GRID_W = 64
N_MOD = 6
FN_GROUPS = 4
HEAD_DIM = 64
FA_Q_HEADS, FA_KV_HEADS = 16, 4
WA_Q_HEADS, WA_KV_HEADS = 16, 2
WINDOW = 128
Q_BLOCK = 128
ROPE_THETA = 10000.0
GM_CHUNK = 128
GM_GROUPS = 8
N_EXPERTS = 32
TOP_K = 4
SWIGLU_LIMIT = 7.0
SWIGLU_ALPHA = 1.702
MOE_BLOCK = 256
LN_EPS = 1e-5
RMS_EPS = 1e-6
NEG = -1e30
ALPHA = (2 * DEPTH) ** 0.25
LOG2E = math.log2(math.e)
Q_SCALE = HEAD_DIM ** -0.5 * LOG2E

LANE = 128
MXU_WIDTH = 256
TM = 256
FLASH_TQ = 256
FLASH_TK = 1280
FLASH_PAIRS = 3
VMEM_LIMIT = 56 * 2 ** 20


def _params(*sem):
    return pltpu.CompilerParams(dimension_semantics=sem, vmem_limit_bytes=VMEM_LIMIT)


def _dot(a, b):
    return jnp.dot(a, b, preferred_element_type=F32)


def _split(a):
    hi = a.astype(BF16)
    lo = (a - hi.astype(F32)).astype(BF16)
    return hi, lo


def _dot3(a_hi, a_lo, b_hi, b_lo):
    return _dot(a_hi, b_hi) + (_dot(a_hi, b_lo) + _dot(a_lo, b_hi))


def _layer_norm(x, g, b):
    mu = jnp.mean(x, axis=-1, keepdims=True)
    xc = x - mu
    var = jnp.mean(xc * xc, axis=-1, keepdims=True)
    return xc * lax.rsqrt(var + LN_EPS) * g + b


def _top4(logits):
    lane = lax.broadcasted_iota(I32, logits.shape, 1).astype(F32)
    cur = logits
    vals, idxs = [], []
    for _ in range(TOP_K):
        m = jnp.max(cur, axis=-1, keepdims=True)
        i = jnp.min(jnp.where(cur == m, lane, float(LANE)), axis=-1, keepdims=True)
        vals.append(m)
        idxs.append(i)
        cur = jnp.where(lane == i, -jnp.inf, cur)
    exps = [jnp.exp(v - vals[0]) for v in vals]
    inv = 1.0 / (exps[0] + exps[1] + exps[2] + exps[3])
    idx_out = jnp.zeros_like(logits)
    gate_out = jnp.zeros_like(logits)
    for k in range(TOP_K):
        idx_out = jnp.where(lane == float(k), idxs[k], idx_out)
        gate_out = jnp.where(lane == float(k), exps[k] * inv, gate_out)
    return idx_out.astype(I32), gate_out


def _post(y, x, mod, lng, lnb, rw_hi, rw_lo, rb):
    x1 = _layer_norm(ALPHA * x + mod[2:3] * y, lng, lnb)
    h2 = x1 * (1.0 + mod[4:5]) + mod[3:4]
    hh, hl = _split(h2)
    logits = _dot3(hh, hl, rw_hi, rw_lo) + rb
    idx, gates = _top4(logits)
    return x1, h2, idx, gates


def _stack_heads(qb, grp):
    return jnp.concatenate([qb[:, g * HEAD_DIM:(g + 1) * HEAD_DIM] for g in range(grp)], axis=0)


def _unstack_heads(o, grp):
    t = o.shape[0] // grp
    return jnp.concatenate([o[g * t:(g + 1) * t] for g in range(grp)], axis=1)


def _ada_kernel(cs_ref, w_ref, b_ref, o_ref):
    cs = cs_ref[...]
    s = cs * (1.0 / (1.0 + jnp.exp(-cs)))
    sh, sl = _split(s)
    wh, wl = _split(w_ref[0])
    o_ref[0] = _dot3(sh, sl, wh, wl) + b_ref[0]


def _ada(c, c_ctx, ada_w, ada_b):
    d = c.shape[-1]
    nm = ada_w.shape[-1]
    tn = nm // 4
    cs = jnp.zeros((8, d), F32).at[0].set(c[0]).at[1].set(c_ctx)
    out = pl.pallas_call(
        _ada_kernel,
        grid=(DEPTH, nm // tn),
        in_specs=[
            pl.BlockSpec((8, d), lambda i, j: (0, 0)),
            pl.BlockSpec((1, d, tn), lambda i, j: (i, 0, j)),
            pl.BlockSpec((1, 1, tn), lambda i, j: (i, 0, j)),
        ],
        out_specs=pl.BlockSpec((1, 8, tn), lambda i, j: (i, 0, j)),
        out_shape=jax.ShapeDtypeStruct((DEPTH, 8, nm), F32),
        compiler_params=_params("arbitrary", "arbitrary"),
        name="ada",
    )(cs, ada_w, ada_b.reshape(DEPTH, 1, nm))
    return out[:, :2].reshape(DEPTH, 2, N_MOD, d)


def _tile_specs(n_lat_tiles):
    tok = pl.BlockSpec((TM, D_MODEL), lambda t: (t, 0))
    mod = pl.BlockSpec((1, N_MOD, D_MODEL), lambda t: (jnp.where(t >= n_lat_tiles, 1, 0), 0, 0))
    return tok, mod


def _full(shape):
    nd = len(shape)
    return pl.BlockSpec(shape, lambda *_: (0,) * nd)


def _post_out(n_rows):
    shapes = (
        jax.ShapeDtypeStruct((n_rows, D_MODEL), F32),
        jax.ShapeDtypeStruct((n_rows, D_MODEL), F32),
        jax.ShapeDtypeStruct((n_rows, LANE), I32),
        jax.ShapeDtypeStruct((n_rows, LANE), F32),
    )
    specs = (
        pl.BlockSpec((TM, D_MODEL), lambda t: (t, 0)),
        pl.BlockSpec((TM, D_MODEL), lambda t: (t, 0)),
        pl.BlockSpec((TM, LANE), lambda t: (t, 0)),
        pl.BlockSpec((TM, LANE), lambda t: (t, 0)),
    )
    return shapes, specs


def _router_operands(router_w, router_b):
    rw = jnp.zeros((D_MODEL, LANE), F32).at[:, :N_EXPERTS].set(router_w)
    rw_hi = rw.astype(BF16)
    rw_lo = (rw - rw_hi.astype(F32)).astype(BF16)
    rb = jnp.full((1, LANE), NEG, F32).at[0, :N_EXPERTS].set(router_b)
    return rw_hi, rw_lo, rb


def _proj_post_kernel(a_ref, w_ref, b_ref, x_ref, mod_ref, lng_ref, lnb_ref, rwh_ref, rwl_ref, rb_ref,
                      x1_ref, h2_ref, idx_ref, gate_ref):
    y = _dot(a_ref[...], w_ref[...]) + b_ref[...]
    x1, h2, idx, gates = _post(y, x_ref[...], mod_ref[0], lng_ref[...], lnb_ref[...],
                               rwh_ref[...], rwl_ref[...], rb_ref[...])
    x1_ref[...] = x1
    h2_ref[...] = h2
    idx_ref[...] = idx
    gate_ref[...] = gates


def _proj_post(a, w_out, b_out, x, mods, lng, lnb, router, n_rows, n_lat_tiles):
    k = a.shape[1]
    tok, mod = _tile_specs(n_lat_tiles)
    shapes, specs = _post_out(n_rows)
    return pl.pallas_call(
        _proj_post_kernel,
        grid=(n_rows // TM,),
        in_specs=[
            pl.BlockSpec((TM, k), lambda t: (t, 0)),
            _full((k, D_MODEL)), _full((1, D_MODEL)),
            tok, mod, _full((1, D_MODEL)), _full((1, D_MODEL)),
            _full((D_MODEL, LANE)), _full((D_MODEL, LANE)), _full((1, LANE)),
        ],
        out_specs=specs,
        out_shape=shapes,
        compiler_params=_params("arbitrary"),
        name="proj_post",
    )(a, w_out.astype(BF16), b_out.reshape(1, -1), x, mods, lng.reshape(1, -1), lnb.reshape(1, -1), *router)


def _dft_mats(n):
    jk = np.outer(np.arange(n), np.arange(n)) % n
    ang = 2.0 * np.pi * jk / n
    out = []
    for m in (np.cos(ang), np.sin(ang)):
        m32 = jnp.asarray(m, F32)
        hi = m32.astype(BF16)
        out += [hi, (m32 - hi.astype(F32)).astype(BF16)]
    return out


def _channel_dft(h, cc, sc):
    cw = cc[0].shape[0]
    a_parts, b_parts = [], []
    for g in range(h.shape[1] // cw):
        hh, hl = _split(h[:, g * cw:(g + 1) * cw])
        a_parts.append(_dot3(hh, hl, cc[0][...], cc[1][...]))
        b_parts.append(_dot3(hh, hl, sc[0][...], sc[1][...]))
    return jnp.concatenate(a_parts, axis=1), jnp.concatenate(b_parts, axis=1)


def _fourier1_kernel(x_ref, mod_ref, cch_ref, ccl_ref, sch_ref, scl_ref, tc_ref, ts_ref, ur_ref, ui_ref):
    mod = mod_ref[0]
    h = x_ref[...] * (1.0 + mod[1:2]) + mod[0:1]
    a, b = _channel_dft(h, (cch_ref, ccl_ref), (sch_ref, scl_ref))
    tch, tcl = _split(tc_ref[0])
    tsh, tsl = _split(ts_ref[0])
    ah, al = _split(a)
    bh, bl = _split(b)
    ur_ref[...] = _dot3(tch, tcl, ah, al) - _dot3(tsh, tsl, bh, bl)
    ui_ref[...] = -(_dot3(tch, tcl, bh, bl) + _dot3(tsh, tsl, ah, al))


def _fourier2_kernel(ur_ref, ui_ref, c2h_ref, c2l_ref, s2h_ref, s2l_ref, w_ref, b_ref, x_ref, mod_ref,
                     lng_ref, lnb_ref, rwh_ref, rwl_ref, rb_ref, x1_ref, h2_ref, idx_ref, gate_ref, *, norm):
    urh, url = _split(ur_ref[0])
    uih, uil = _split(ui_ref[0])
    mixed = (_dot3(c2h_ref[...], c2l_ref[...], urh, url) + _dot3(s2h_ref[...], s2l_ref[...], uih, uil)) * norm
    y = _dot(mixed.astype(BF16), w_ref[...]) + b_ref[...]
    x1, h2, idx, gates = _post(y, x_ref[...], mod_ref[0], lng_ref[...], lnb_ref[...],
                               rwh_ref[...], rwl_ref[...], rb_ref[...])
    x1_ref[...] = x1
    h2_ref[...] = h2
    idx_ref[...] = idx
    gate_ref[...] = gates


def _fourier_ctx_kernel(x_ref, mod_ref, cch_ref, ccl_ref, sch_ref, scl_ref, cnh_ref, cnl_ref, snh_ref, snl_ref,
                        w_ref, b_ref, lng_ref, lnb_ref, rwh_ref, rwl_ref, rb_ref,
                        x1_ref, h2_ref, idx_ref, gate_ref, *, norm):
    mod = mod_ref[0]
    x = x_ref[...]
    h = x * (1.0 + mod[1:2]) + mod[0:1]
    a, b = _channel_dft(h, (cch_ref, ccl_ref), (sch_ref, scl_ref))
    ah, al = _split(a)
    bh, bl = _split(b)
    mixed = (_dot3(cnh_ref[...], cnl_ref[...], ah, al) - _dot3(snh_ref[...], snl_ref[...], bh, bl)) * norm
    y = _dot(mixed.astype(BF16), w_ref[...]) + b_ref[...]
    x1, h2, idx, gates = _post(y, x, mod, lng_ref[...], lnb_ref[...], rwh_ref[...], rwl_ref[...], rb_ref[...])
    x1_ref[...] = x1
    h2_ref[...] = h2
    idx_ref[...] = idx
    gate_ref[...] = gates


def _fourier_layer(x_all, mods, w_out, b_out, lng, lnb, router, n_lat, n_ctx):
    t_all, d = x_all.shape
    n2 = LANE
    n1 = n_lat // n2
    cw = d // FN_GROUPS
    rows = t_all // n2
    xv = x_all.reshape(rows, n2 * d)
    cmat = _dft_mats(cw)
    w_bf = w_out.astype(BF16)
    b2 = b_out.reshape(1, d)
    lng2, lnb2 = lng.reshape(1, d), lnb.reshape(1, d)

    k1 = jnp.arange(n1, dtype=I32)
    pos = jnp.arange(n1, dtype=I32)[None, None, :] * n2 + jnp.arange(n2, dtype=I32)[:, None, None]
    ang = ((k1[None, :, None] * pos) % n_lat).astype(F32) * (2.0 * math.pi / n_lat)
    tc, ts = jnp.cos(ang), jnp.sin(ang)

    mat = _full((cw, cw))
    ur, ui = pl.pallas_call(
        _fourier1_kernel,
        grid=(n2,),
        in_specs=[
            pl.BlockSpec((n1, d), lambda j: (0, j)),
            pl.BlockSpec((1, N_MOD, d), lambda j: (0, 0, 0)),
            mat, mat, mat, mat,
            pl.BlockSpec((1, n1, n1), lambda j: (j, 0, 0)),
            pl.BlockSpec((1, n1, n1), lambda j: (j, 0, 0)),
        ],
        out_specs=(pl.BlockSpec((n1, d), lambda j: (0, j)), pl.BlockSpec((n1, d), lambda j: (0, j))),
        out_shape=(jax.ShapeDtypeStruct((n1, n2 * d), F32), jax.ShapeDtypeStruct((n1, n2 * d), F32)),
        compiler_params=_params("arbitrary"),
        name="fourier1",
    )(xv, mods, *cmat, tc, ts)

    m2 = _dft_mats(n2)
    mat2 = _full((n2, n2))
    norm = 1.0 / math.sqrt(n_lat * cw)
    out_shapes = (
        jax.ShapeDtypeStruct((n1, n2 * d), F32),
        jax.ShapeDtypeStruct((n1, n2 * d), F32),
        jax.ShapeDtypeStruct((n1, n2 * LANE), I32),
        jax.ShapeDtypeStruct((n1, n2 * LANE), F32),
    )
    strided = pl.BlockSpec((n2, d), lambda k: (0, k))
    strided_l = pl.BlockSpec((n2, LANE), lambda k: (0, k))
    outs = pl.pallas_call(
        functools.partial(_fourier2_kernel, norm=norm),
        grid=(n1,),
        in_specs=[
            pl.BlockSpec((1, n2, d), lambda k: (k, 0, 0)),
            pl.BlockSpec((1, n2, d), lambda k: (k, 0, 0)),
            mat2, mat2, mat2, mat2,
            _full((d, d)), _full((1, d)),
            strided,
            pl.BlockSpec((1, N_MOD, d), lambda k: (0, 0, 0)),
            _full((1, d)), _full((1, d)),
            _full((d, LANE)), _full((d, LANE)), _full((1, LANE)),
        ],
        out_specs=(strided, strided, strided_l, strided_l),
        out_shape=out_shapes,
        compiler_params=_params("arbitrary"),
        name="fourier2",
    )(ur.reshape(n1, n2, d), ui.reshape(n1, n2, d), *m2, w_bf, b2, xv, mods, lng2, lnb2, *router)
    lat = (outs[0].reshape(n_lat, d), outs[1].reshape(n_lat, d),
           outs[2].reshape(n_lat, LANE), outs[3].reshape(n_lat, LANE))

    cn = _dft_mats(n_ctx)
    matn = _full((n_ctx, n_ctx))
    shapes = (
        jax.ShapeDtypeStruct((n_ctx, d), F32), jax.ShapeDtypeStruct((n_ctx, d), F32),
        jax.ShapeDtypeStruct((n_ctx, LANE), I32), jax.ShapeDtypeStruct((n_ctx, LANE), F32),
    )
    ctx = pl.pallas_call(
        functools.partial(_fourier_ctx_kernel, norm=1.0 / math.sqrt(n_ctx * cw)),
        grid=(1,),
        in_specs=[
            pl.BlockSpec((n_ctx, d), lambda i: (n_lat // n_ctx, 0)),
            pl.BlockSpec((1, N_MOD, d), lambda i: (1, 0, 0)),
            mat, mat, mat, mat, matn, matn, matn, matn,
            _full((d, d)), _full((1, d)), _full((1, d)), _full((1, d)),
            _full((d, LANE)), _full((d, LANE)), _full((1, LANE)),
        ],
        out_specs=(_full((n_ctx, d)), _full((n_ctx, d)), _full((n_ctx, LANE)), _full((n_ctx, LANE))),
        out_shape=shapes,
        compiler_params=_params("arbitrary"),
        name="fourier_ctx",
    )(x_all, mods, *cmat, *cn, w_bf, b2, lng2, lnb2, *router)
    return tuple(jnp.concatenate([a, b], axis=0) for a, b in zip(lat, ctx))


def _qkv_kernel(x_ref, mod_ref, w_ref, b_ref, cos_ref, sin_ref, *rest, n_qk, rms):
    if rms:
        gain_ref, ind_ref, indt_ref, q_ref, k_ref, v_ref = rest
    else:
        q_ref, k_ref, v_ref = rest
    mod = mod_ref[0]
    h = x_ref[...] * (1.0 + mod[1:2]) + mod[0:1]
    y = _dot(h.astype(BF16), w_ref[...]) + b_ref[...]
    qk = y[:, :n_qk]
    if rms:
        sh, sl = _split(qk * qk)
        ms = _dot(sh, ind_ref[...]) + _dot(sl, ind_ref[...])
        mh, ml = _split(ms)
        msb = _dot(mh, indt_ref[...]) + _dot(ml, indt_ref[...])
        qk = qk * lax.rsqrt(msb + RMS_EPS) * gain_ref[...]
    cos = cos_ref[...]
    sin = sin_ref[...]
    even = (lax.broadcasted_iota(I32, cos.shape, 1) & 1) == 0
    parts = []
    for c in range(n_qk // LANE):
        z = qk[:, c * LANE:(c + 1) * LANE]
        swapped = jnp.where(even, pltpu.roll(z, LANE - 1, 1), pltpu.roll(z, 1, 1))
        parts.append(z * cos + swapped * sin)
    nq = q_ref.shape[1]
    q_ref[...] = (jnp.concatenate(parts[:nq // LANE], axis=1) * Q_SCALE).astype(BF16)
    k_ref[...] = jnp.concatenate(parts[nq // LANE:], axis=1).astype(BF16)
    v_ref[...] = y[:, n_qk:].astype(BF16)


def _qkv(x_all, mods, w_qkv, b_qkv, cos_t, sin_t, n_q, n_kv, n_lat_tiles, q_norm=None, k_norm=None):
    t_all, d = x_all.shape
    nq, nk = n_q * HEAD_DIM, n_kv * HEAD_DIM
    n_qk, n_all = nq + nk, nq + 2 * nk
    rms = q_norm is not None
    tok, mod = _tile_specs(n_lat_tiles)
    in_specs = [tok, mod, _full((d, n_all)), _full((1, n_all)),
                pl.BlockSpec((TM, LANE), lambda t: (t, 0)), pl.BlockSpec((TM, LANE), lambda t: (t, 0))]
    args = [x_all, mods, w_qkv.astype(BF16), b_qkv.reshape(1, n_all), cos_t, sin_t]
    if rms:
        gain = jnp.concatenate([jnp.tile(q_norm, n_q), jnp.tile(k_norm, n_kv)]).reshape(1, n_qk)
        head = np.arange(n_qk) // HEAD_DIM
        ind = np.zeros((n_qk, LANE), np.float32)
        ind[np.arange(n_qk), head] = 1.0 / HEAD_DIM
        indt = np.zeros((LANE, n_qk), np.float32)
        indt[head, np.arange(n_qk)] = 1.0
        in_specs += [_full((1, n_qk)), _full((n_qk, LANE)), _full((LANE, n_qk))]
        args += [gain, jnp.asarray(ind, BF16), jnp.asarray(indt, BF16)]
    return pl.pallas_call(
        functools.partial(_qkv_kernel, n_qk=n_qk, rms=rms),
        grid=(t_all // TM,),
        in_specs=in_specs,
        out_specs=(pl.BlockSpec((TM, nq), lambda t: (t, 0)), pl.BlockSpec((TM, nk), lambda t: (t, 0)),
                   pl.BlockSpec((TM, nk), lambda t: (t, 0))),
        out_shape=(jax.ShapeDtypeStruct((t_all, nq), BF16), jax.ShapeDtypeStruct((t_all, nk), BF16),
                   jax.ShapeDtypeStruct((t_all, nk), BF16)),
        compiler_params=_params("arbitrary"),
        name="qkv",
    )(*args)


def _rope_tables(n_lat, n_ctx):
    rows = n_lat // GRID_W
    row = jnp.repeat(jnp.arange(rows, dtype=F32), GRID_W)
    col = jnp.tile(jnp.arange(GRID_W, dtype=F32), rows)
    n_freq = HEAD_DIM // 4
    inv = ROPE_THETA ** (-jnp.arange(n_freq, dtype=F32) / n_freq)
    ang = jnp.concatenate([row[:, None] * inv, col[:, None] * inv], axis=-1)
    ang = jnp.concatenate([ang, jnp.zeros((n_ctx, HEAD_DIM // 2), F32)], axis=0)
    cos = jnp.tile(jnp.repeat(jnp.cos(ang), 2, axis=1), (1, LANE // HEAD_DIM))
    sin = jnp.tile(jnp.repeat(jnp.sin(ang), 2, axis=1), (1, LANE // HEAD_DIM))
    sign = jnp.where(jnp.arange(LANE) % 2 == 0, -1.0, 1.0).astype(F32)
    return cos, sin * sign


def _kv_layouts(k, v, n_kv):
    t_all = k.shape[0]
    kt = k.reshape(t_all, n_kv, HEAD_DIM).transpose(1, 2, 0)
    vh = v.reshape(t_all, n_kv, HEAD_DIM).transpose(1, 0, 2)
    pad = jnp.zeros((n_kv, t_all, LANE - HEAD_DIM), BF16).at[:, :, 0].set(1.0)
    return kt, jnp.concatenate([vh, pad], axis=-1)


def _flash_kernel(q_ref, k_ref, vt_ref, o_ref, s_even, s_odd, *, grp, tk, n_lat, n_ctx):
    tq = q_ref.shape[0]
    q_t = q_ref[...].astype(F32).T
    qt = jnp.concatenate([q_t[g * HEAD_DIM:(g + 1) * HEAD_DIM] for g in range(grp)], axis=1).astype(BF16)
    cols = qt.shape[1]
    is_ctx = pl.program_id(1) == n_lat // FLASH_TQ
    n_chunks = (n_lat + n_ctx) // tk

    def finish(acc):
        o = acc[:HEAD_DIM] / acc[HEAD_DIM:HEAD_DIM + 1]
        o_t = jnp.concatenate([o[:, g * tq:(g + 1) * tq] for g in range(grp)], axis=0)
        o_ref[...] = o_t.T.astype(BF16)

    groups = [slice(c * MXU_WIDTH, (c + 1) * MXU_WIDTH) for c in range(cols // MXU_WIDTH)]

    def scores(j, buf, g):
        off = pl.multiple_of(j * tk, tk)
        s = _dot(k_ref[0, pl.ds(off, tk), :], qt[:, g])
        buf[:, g] = s
        return jnp.max(s, axis=0, keepdims=True)

    def accumulate(j, buf, g, m, acc, mc):
        off = pl.multiple_of(j * tk, tk)
        m_new = jnp.maximum(m, mc)
        p = jnp.exp2(buf[:, g] - m_new)
        acc = jnp.exp2(m - m_new) * acc + _dot(vt_ref[0, :, pl.ds(off, tk)], p.astype(BF16))
        return m_new, acc

    @pl.when(jnp.logical_not(is_ctx))
    def _():
        def step(j, cur, nxt, carry):
            out = []
            for g, (m, acc, mc) in zip(groups, carry):
                mc_next = scores(j + 1, nxt, g)
                out.append(accumulate(j, cur, g, m, acc, mc) + (mc_next,))
            return out

        def body(jj, carry):
            for pair in range(FLASH_PAIRS):
                j = 2 * (FLASH_PAIRS * jj + pair)
                carry = step(j, s_even, s_odd, carry)
                carry = step(j + 1, s_odd, s_even, carry)
            return carry

        init = [(jnp.full((1, MXU_WIDTH), NEG, F32), jnp.zeros((LANE, MXU_WIDTH), F32), scores(0, s_even, g))
                for g in groups]
        carry = lax.fori_loop(0, (n_chunks - 1) // (2 * FLASH_PAIRS), body, init)
        finish(jnp.concatenate([accumulate(n_chunks - 1, s_even, g, m, acc, mc)[1]
                                for g, (m, acc, mc) in zip(groups, carry)], axis=1))

    @pl.when(is_ctx)
    def _():
        s = _dot(k_ref[0, n_lat:n_lat + n_ctx, :], qt)
        p = jnp.exp2(s - jnp.max(s, axis=0, keepdims=True))
        finish(_dot(vt_ref[0, :, n_lat:n_lat + n_ctx], p.astype(BF16)))


def _full_attention(q, k, v, n_lat, n_ctx):
    t_all, nq = q.shape
    n_kv = k.shape[1] // HEAD_DIM
    grp = nq // HEAD_DIM // n_kv
    nt = t_all // FLASH_TQ
    cols = grp * FLASH_TQ
    assert n_ctx == FLASH_TQ and t_all % FLASH_TK == 0 and (t_all // FLASH_TK - 1) % (2 * FLASH_PAIRS) == 0
    kh = k.reshape(t_all, n_kv, HEAD_DIM).transpose(1, 0, 2)
    vt = v.reshape(t_all, n_kv, HEAD_DIM).transpose(1, 2, 0)
    pad = jnp.zeros((n_kv, LANE - HEAD_DIM, t_all), BF16).at[:, 0, :].set(1.0)
    vt = jnp.concatenate([vt, pad], axis=1)
    return pl.pallas_call(
        functools.partial(_flash_kernel, grp=grp, tk=FLASH_TK, n_lat=n_lat, n_ctx=n_ctx),
        grid=(n_kv, nt),
        in_specs=[
            pl.BlockSpec((FLASH_TQ, grp * HEAD_DIM), lambda h, i: (i, h)),
            pl.BlockSpec((1, t_all, HEAD_DIM), lambda h, i: (h, 0, 0)),
            pl.BlockSpec((1, LANE, t_all), lambda h, i: (h, 0, 0)),
        ],
        out_specs=pl.BlockSpec((FLASH_TQ, grp * HEAD_DIM), lambda h, i: (i, h)),
        out_shape=jax.ShapeDtypeStruct((t_all, nq), BF16),
        scratch_shapes=[pltpu.VMEM((FLASH_TK, cols), F32), pltpu.VMEM((FLASH_TK, cols), F32)],
        compiler_params=_params("arbitrary", "arbitrary"),
        name="flash",
    )(q, kh, vt)


def _window_kernel(q_ref, kp_ref, kc_ref, kn_ref, kx_ref, vp_ref, vc_ref, vn_ref, vx_ref, sink_ref, o_ref,
                   *, grp, nb):
    i = pl.program_id(1)
    qs = _stack_heads(q_ref[...], grp)
    rows = qs.shape[0]
    qi = lax.broadcasted_iota(I32, (rows, Q_BLOCK), 0) & (Q_BLOCK - 1)
    kj = lax.broadcasted_iota(I32, (rows, Q_BLOCK), 1)
    s = _dot(qs, jnp.concatenate([kp_ref[0], kc_ref[0], kn_ref[0], kx_ref[0]], axis=1))
    sp = jnp.where(kj >= jnp.where(i > 0, qi, Q_BLOCK), s[:, :Q_BLOCK], NEG)
    sn = jnp.where(kj <= jnp.where(i < nb - 1, qi, -1), s[:, 2 * Q_BLOCK:3 * Q_BLOCK], NEG)
    s = jnp.concatenate([sp, s[:, Q_BLOCK:2 * Q_BLOCK], sn, s[:, 3 * Q_BLOCK:]], axis=1)
    sink = sink_ref[0][:, :1]
    m = jnp.maximum(jnp.max(s, axis=-1, keepdims=True), sink)
    acc = _dot(jnp.exp2(s - m).astype(BF16),
               jnp.concatenate([vp_ref[0], vc_ref[0], vn_ref[0], vx_ref[0]], axis=0))
    den = acc[:, HEAD_DIM:HEAD_DIM + 1] + jnp.exp2(sink - m)
    o_ref[...] = _unstack_heads(acc[:, :HEAD_DIM] / den, grp).astype(BF16)


def _window_attention(q, kt, vx, sink, n_lat, n_ctx):
    nq = q.shape[1]
    n_kv = kt.shape[0]
    grp = nq // HEAD_DIM // n_kv
    gw = grp * HEAD_DIM
    nb = n_lat // Q_BLOCK
    cb = n_lat // n_ctx
    sink_rows = jnp.broadcast_to((sink.reshape(n_kv, grp) * LOG2E)[:, :, None, None],
                                 (n_kv, grp, Q_BLOCK, LANE)).reshape(n_kv, grp * Q_BLOCK, LANE)
    kspec = lambda f: pl.BlockSpec((1, HEAD_DIM, Q_BLOCK), lambda h, i: (h, 0, f(i)))
    vspec = lambda f: pl.BlockSpec((1, Q_BLOCK, LANE), lambda h, i: (h, f(i), 0))
    prev = lambda i: jnp.maximum(i - 1, 0)
    cur = lambda i: i
    nxt = lambda i: jnp.minimum(i + 1, nb - 1)
    return pl.pallas_call(
        functools.partial(_window_kernel, grp=grp, nb=nb),
        grid=(n_kv, nb),
        in_specs=[
            pl.BlockSpec((Q_BLOCK, gw), lambda h, i: (i, h)),
            kspec(prev), kspec(cur), kspec(nxt),
            pl.BlockSpec((1, HEAD_DIM, n_ctx), lambda h, i: (h, 0, cb)),
            vspec(prev), vspec(cur), vspec(nxt),
            pl.BlockSpec((1, n_ctx, LANE), lambda h, i: (h, cb, 0)),
            pl.BlockSpec((1, grp * Q_BLOCK, LANE), lambda h, i: (h, 0, 0)),
        ],
        out_specs=pl.BlockSpec((Q_BLOCK, gw), lambda h, i: (i, h)),
        out_shape=jax.ShapeDtypeStruct((n_lat, nq), BF16),
        compiler_params=_params("arbitrary", "arbitrary"),
        name="window",
    )(q, kt, kt, kt, kt, vx, vx, vx, vx, sink_rows)


def _gmlp_kernel(x_ref, mod_ref, win_ref, bin_ref, vg_ref, vb_ref, ws_ref, bs_ref, wout_ref, bout_ref,
                 lng_ref, lnb_ref, rwh_ref, rwl_ref, rb_ref, x1_ref, h2_ref, idx_ref, gate_ref):
    mod = mod_ref[0]
    x = x_ref[...]
    h = x * (1.0 + mod[1:2]) + mod[0:1]
    z = _dot(h.astype(BF16), win_ref[...]) + bin_ref[...]
    z = 0.5 * z * (1.0 + lax.erf(z * (2.0 ** -0.5)))
    half = z.shape[1] // 2
    u = z[:, :half]
    v = _layer_norm(z[:, half:], vg_ref[...], vb_ref[...]).astype(BF16)
    cw = half // GM_GROUPS
    chunks = []
    for c in range(x.shape[0] // GM_CHUNK):
        vc = v[c * GM_CHUNK:(c + 1) * GM_CHUNK]
        chunks.append(jnp.concatenate(
            [_dot(ws_ref[g], vc[:, g * cw:(g + 1) * cw]) + bs_ref[g] for g in range(GM_GROUPS)], axis=1))
    gated = u * jnp.concatenate(chunks, axis=0)
    y = _dot(gated.astype(BF16), wout_ref[...]) + bout_ref[...]
    x1, h2, idx, gates = _post(y, x, mod, lng_ref[...], lnb_ref[...], rwh_ref[...], rwl_ref[...], rb_ref[...])
    x1_ref[...] = x1
    h2_ref[...] = h2
    idx_ref[...] = idx
    gate_ref[...] = gates


def _gmlp_layer(x_all, mods, w_in, b_in, vg, vb, w_s, b_s, w_out, b_out, lng, lnb, router, n_lat_tiles):
    t_all, d = x_all.shape
    dffn = w_in.shape[1]
    half = dffn // 2
    cw = half // GM_GROUPS
    tok, mod = _tile_specs(n_lat_tiles)
    shapes, specs = _post_out(t_all)
    bs_full = jnp.broadcast_to(b_s[:, :, None], (GM_GROUPS, GM_CHUNK, cw))
    return pl.pallas_call(
        _gmlp_kernel,
        grid=(t_all // TM,),
        in_specs=[
            tok, mod, _full((d, dffn)), _full((1, dffn)), _full((1, half)), _full((1, half)),
            _full((GM_GROUPS, GM_CHUNK, GM_CHUNK)), _full((GM_GROUPS, GM_CHUNK, cw)),
            _full((half, d)), _full((1, d)), _full((1, d)), _full((1, d)),
            _full((d, LANE)), _full((d, LANE)), _full((1, LANE)),
        ],
        out_specs=specs,
        out_shape=shapes,
        compiler_params=_params("arbitrary"),
        name="gmlp",
    )(x_all, mods, w_in.astype(BF16), b_in.reshape(1, dffn), vg.reshape(1, half), vb.reshape(1, half),
      w_s.astype(BF16), bs_full, w_out.astype(BF16), b_out.reshape(1, d), lng.reshape(1, d), lnb.reshape(1, d),
      *router)


COMB_TM = 128


def _route_meta(top_idx, n_tok):
    n_assign = n_tok * TOP_K
    n_blocks = -(-n_assign // MOE_BLOCK) + N_EXPERTS
    ids = jnp.arange(N_EXPERTS, dtype=I32)
    counts = jnp.sum((top_idx[:n_tok, :TOP_K, None] == ids).astype(I32), axis=(0, 1))
    padded = (counts + MOE_BLOCK - 1) // MOE_BLOCK * MOE_BLOCK
    ends_pad = jnp.cumsum(padded)
    base = ends_pad - padded
    starts = jnp.arange(n_blocks, dtype=I32) * MOE_BLOCK
    block_expert = jnp.minimum(jnp.sum((ends_pad[None, :] <= starts[:, None]).astype(I32), axis=1), N_EXPERTS - 1)
    last_blk = jnp.where(padded > 0, ends_pad - MOE_BLOCK, -1)
    tail = ends_pad[-1] + ids * MOE_BLOCK
    tail = jnp.where(tail < n_blocks * MOE_BLOCK, tail, -1)
    base_b = jnp.broadcast_to(base.astype(F32)[:, None], (N_EXPERTS, LANE))
    return block_expert.astype(I32), base_b, jnp.concatenate([last_blk, tail]).astype(I32), n_blocks


def _rank_kernel(idx_ref, base_ref, upper_ref, dest_ref, run_ref):
    @pl.when(pl.program_id(0) == 0)
    def _():
        run_ref[...] = jnp.zeros(run_ref.shape, F32)

    eid = lax.broadcasted_iota(I32, (N_EXPERTS, COMB_TM), 0)
    onehots = [(eid == idx_ref[k:k + 1, :]).astype(F32) for k in range(TOP_K)]
    cnt = onehots[0] + onehots[1] + onehots[2] + onehots[3]
    before = _dot(cnt.astype(BF16), upper_ref[...])
    slot = base_ref[...] + run_ref[...] + before
    for k in range(TOP_K):
        dest_ref[0, k:k + 1, :] = jnp.sum(onehots[k] * slot, axis=0, keepdims=True).astype(I32)
    run_ref[...] = run_ref[...] + jnp.sum(cnt, axis=1, keepdims=True)


def _rank(top_idx, base_b, n_tok):
    n_tiles = n_tok // COMB_TM
    idx_t = top_idx[:n_tok, :TOP_K].T
    upper = jnp.asarray(np.triu(np.ones((COMB_TM, COMB_TM), np.float32), 1), BF16)
    return pl.pallas_call(
        _rank_kernel,
        grid=(n_tiles,),
        in_specs=[pl.BlockSpec((TOP_K, COMB_TM), lambda t: (0, t)), _full((N_EXPERTS, LANE)),
                  _full((COMB_TM, COMB_TM))],
        out_specs=pl.BlockSpec((1, TOP_K, COMB_TM), lambda t: (t, 0, 0)),
        out_shape=jax.ShapeDtypeStruct((n_tiles, TOP_K, COMB_TM), I32),
        scratch_shapes=[pltpu.VMEM((N_EXPERTS, LANE), F32)],
        compiler_params=_params("arbitrary"),
        name="rank",
    )(idx_t, base_b, upper)


def _dispatch_kernel(dest_ref, zrow_ref, h_ref, xs_hbm, hbuf, zbuf, sem, zsem, *, n_tiles):
    t = pl.program_id(0)
    slot = t % 2
    n_zero = 2 * N_EXPERTS

    def zero_copy(j):
        row = pl.multiple_of(zrow_ref[j], MOE_BLOCK)
        return pltpu.make_async_copy(zbuf, xs_hbm.at[pl.ds(row, MOE_BLOCK), :], zsem.at[0])

    def wait_rows(s):
        for _ in range(TOP_K):
            pltpu.make_async_copy(hbuf.at[s], xs_hbm.at[pl.ds(0, COMB_TM), :], sem.at[s]).wait()

    @pl.when(t == 0)
    def _():
        zbuf[...] = jnp.zeros(zbuf.shape, F32)
        for j in range(n_zero):
            @pl.when(zrow_ref[j] >= 0)
            def _():
                zero_copy(j).start()
        for j in range(n_zero):
            @pl.when(zrow_ref[j] >= 0)
            def _():
                zero_copy(j).wait()

    @pl.when(t >= 2)
    def _():
        wait_rows(slot)

    hbuf[slot] = h_ref[...]
    for k in range(TOP_K):
        for r in range(COMB_TM):
            row = dest_ref[(t * TOP_K + k) * COMB_TM + r]
            pltpu.make_async_copy(hbuf.at[slot, pl.ds(r, 1), :], xs_hbm.at[pl.ds(row, 1), :],
                                  sem.at[slot]).start(priority=r % 2)

    @pl.when(t == n_tiles - 1)
    def _():
        wait_rows(slot)
        if n_tiles > 1:
            wait_rows(1 - slot)


def _dispatch(h2, dest, zrows, n_tok, n_blocks):
    d = h2.shape[1]
    n_tiles = n_tok // COMB_TM
    grid_spec = pltpu.PrefetchScalarGridSpec(
        num_scalar_prefetch=2,
        grid=(n_tiles,),
        in_specs=[pl.BlockSpec((COMB_TM, d), lambda t, dest, zr: (t, 0))],
        out_specs=pl.BlockSpec(memory_space=pl.ANY),
        scratch_shapes=[pltpu.VMEM((2, COMB_TM, d), F32), pltpu.VMEM((MOE_BLOCK, d), F32),
                        pltpu.SemaphoreType.DMA((2,)), pltpu.SemaphoreType.DMA((1,))],
    )
    return pl.pallas_call(
        functools.partial(_dispatch_kernel, n_tiles=n_tiles),
        grid_spec=grid_spec,
        out_shape=jax.ShapeDtypeStruct((n_blocks * MOE_BLOCK, d), F32),
        compiler_params=_params("arbitrary"),
        name="dispatch",
    )(dest.reshape(-1), zrows, h2)


EXP_CHUNK = 512


def _expert_kernel(be_ref, x_ref, wgu_ref, bgu_ref, wd_ref, bd_ref, sel_e_ref, sel_o_ref, y_ref,
                   wg_s, wl_s, wd_s, bg_s, bl_s):
    b = pl.program_id(0)
    ff = wg_s.shape[1]
    half = EXP_CHUNK // 2

    @pl.when(jnp.logical_or(b == 0, be_ref[b] != be_ref[jnp.maximum(b - 1, 0)]))
    def _():
        for c in range(2 * ff // EXP_CHUNK):
            w = wgu_ref[0, 0, :, c * EXP_CHUNK:(c + 1) * EXP_CHUNK].astype(BF16)
            wg_s[:, c * half:(c + 1) * half] = _dot(w, sel_e_ref[...]).astype(BF16)
            wl_s[:, c * half:(c + 1) * half] = _dot(w, sel_o_ref[...]).astype(BF16)
            bh, bl = _split(jnp.broadcast_to(bgu_ref[0, 0, :, c * EXP_CHUNK:(c + 1) * EXP_CHUNK], (8, EXP_CHUNK)))
            bg_s[:, c * half:(c + 1) * half] = _dot(bh, sel_e_ref[...]) + _dot(bl, sel_e_ref[...])
            bl_s[:, c * half:(c + 1) * half] = _dot(bh, sel_o_ref[...]) + _dot(bl, sel_o_ref[...])
        wd_s[...] = wd_ref[0, 0].astype(BF16)

    x = x_ref[...].astype(BF16)
    glu = jnp.minimum(_dot(x, wg_s[...]) + bg_s[0:1, :], SWIGLU_LIMIT)
    lin = jnp.clip(_dot(x, wl_s[...]) + bl_s[0:1, :], -SWIGLU_LIMIT, SWIGLU_LIMIT)
    act = glu * (1.0 / (1.0 + jnp.exp(-SWIGLU_ALPHA * glu))) * (lin + 1.0)
    y_ref[...] = _dot(act.astype(BF16), wd_s[...]) + bd_ref[0, 0]


def _experts(xs, block_expert, n_blocks, w_gate_up, b_gate_up, w_down, b_down, layer):
    d = xs.shape[1]
    ff = w_down.shape[2]
    sel = np.zeros((2, EXP_CHUNK, EXP_CHUNK // 2), np.float32)
    sel[0, 2 * np.arange(EXP_CHUNK // 2), np.arange(EXP_CHUNK // 2)] = 1.0
    sel[1, 2 * np.arange(EXP_CHUNK // 2) + 1, np.arange(EXP_CHUNK // 2)] = 1.0
    wspec = lambda r, c: pl.BlockSpec((1, 1, r, c), lambda b, be: (layer, be[b], 0, 0))
    sspec = pl.BlockSpec((EXP_CHUNK, EXP_CHUNK // 2), lambda b, be: (0, 0))
    grid_spec = pltpu.PrefetchScalarGridSpec(
        num_scalar_prefetch=1,
        grid=(n_blocks,),
        in_specs=[pl.BlockSpec((MOE_BLOCK, d), lambda b, be: (b, 0)), wspec(d, 2 * ff), wspec(1, 2 * ff),
                  wspec(ff, d), wspec(1, d), sspec, sspec],
        out_specs=pl.BlockSpec((MOE_BLOCK, d), lambda b, be: (b, 0)),
        scratch_shapes=[pltpu.VMEM((d, ff), BF16), pltpu.VMEM((d, ff), BF16), pltpu.VMEM((ff, d), BF16),
                        pltpu.VMEM((8, ff), F32), pltpu.VMEM((8, ff), F32)],
    )
    return pl.pallas_call(
        _expert_kernel,
        grid_spec=grid_spec,
        out_shape=jax.ShapeDtypeStruct((n_blocks * MOE_BLOCK, d), F32),
        compiler_params=_params("arbitrary"),
        name="experts",
    )(block_expert, xs, w_gate_up, b_gate_up[:, :, None, :], w_down, b_down[:, :, None, :],
      jnp.asarray(sel[0], BF16), jnp.asarray(sel[1], BF16))


def _gather_rows(idx_ref, base, n, src_hbm, dst, sem):
    for r in range(n):
        pltpu.make_async_copy(src_hbm.at[pl.ds(idx_ref[base + r], 1), :], dst.at[pl.ds(r, 1), :],
                              sem).start(priority=r % 2)


def _combine_kernel(pos_ref, y_hbm, gate_ref, x_ref, mod_ref, lng_ref, lnb_ref, o_ref, ybuf, sem, *, n_tiles):
    t = pl.program_id(0)
    slot = t % 2
    n = COMB_TM * TOP_K

    @pl.when(t == 0)
    def _():
        _gather_rows(pos_ref, 0, n, y_hbm, ybuf.at[0], sem.at[0])

    @pl.when(t + 1 < n_tiles)
    def _():
        _gather_rows(pos_ref, (t + 1) * n, n, y_hbm, ybuf.at[1 - slot], sem.at[1 - slot])

    pltpu.make_async_copy(y_hbm.at[pl.ds(0, n), :], ybuf.at[slot], sem.at[slot]).wait()
    gates = gate_ref[...]
    f = jnp.zeros(x_ref.shape, F32)
    for k in range(TOP_K):
        f = f + gates[:, k:k + 1] * ybuf[slot, pl.ds(k * COMB_TM, COMB_TM), :]
    o_ref[...] = _layer_norm(ALPHA * x_ref[...] + mod_ref[0][5:6] * f, lng_ref[...], lnb_ref[...])


def _combine(yb, dest, gates, x1, mods, lng, lnb, n_tok, n_lat):
    d = x1.shape[1]
    n_tiles = n_tok // COMB_TM
    n_lat_tiles = n_lat // COMB_TM
    pos = dest.reshape(-1)
    grid_spec = pltpu.PrefetchScalarGridSpec(
        num_scalar_prefetch=1,
        grid=(n_tiles,),
        in_specs=[
            pl.BlockSpec(memory_space=pl.ANY),
            pl.BlockSpec((COMB_TM, LANE), lambda t, pos: (t, 0)),
            pl.BlockSpec((COMB_TM, d), lambda t, pos: (t, 0)),
            pl.BlockSpec((1, N_MOD, d), lambda t, pos: (jnp.where(t >= n_lat_tiles, 1, 0), 0, 0)),
            pl.BlockSpec((1, d), lambda t, pos: (0, 0)),
            pl.BlockSpec((1, d), lambda t, pos: (0, 0)),
        ],
        out_specs=pl.BlockSpec((COMB_TM, d), lambda t, pos: (t, 0)),
        scratch_shapes=[pltpu.VMEM((2, COMB_TM * TOP_K, d), F32), pltpu.SemaphoreType.DMA((2,))],
    )
    return pl.pallas_call(
        functools.partial(_combine_kernel, n_tiles=n_tiles),
        grid_spec=grid_spec,
        out_shape=jax.ShapeDtypeStruct((n_tok, d), F32),
        compiler_params=_params("arbitrary"),
        name="combine",
    )(pos, yb, gates, x1, mods, lng.reshape(1, d), lnb.reshape(1, d))


def _moe_layer(x1, h2, top_idx, gates, mods, lng, lnb, w_gate_up, b_gate_up, w_down, b_down, layer, n_tok, n_lat):
    block_expert, base_b, zrows, n_blocks = _route_meta(top_idx, n_tok)
    dest = _rank(top_idx, base_b, n_tok)
    xs = _dispatch(h2, dest, zrows, n_tok, n_blocks)
    yb = _experts(xs, block_expert, n_blocks, w_gate_up, b_gate_up, w_down, b_down, layer)
    return _combine(yb, dest, gates, x1, mods, lng, lnb, n_tok, n_lat)


def kernel(x, c, ctx, c_ctx, ada_w, ada_b, ln_mix_g, ln_mix_b, ln_ffn_g, ln_ffn_b, fn_w_out, fn_b_out, fa_w_qkv, fa_b_qkv, fa_q_norm, fa_k_norm, fa_w_out, fa_b_out, gm_w_in, gm_b_in, gm_v_norm_g, gm_v_norm_b, gm_w_s, gm_b_s, gm_w_out, gm_b_out, wa_w_qkv, wa_b_qkv, wa_sink, wa_w_out, wa_b_out, router_w, router_b, exp_w_gate_up, exp_b_gate_up, exp_w_down, exp_b_down):
    bsz, n_lat, d = x.shape
    n_ctx = ctx.shape[1]
    assert bsz == 1 and d == D_MODEL and n_lat == LANE * LANE and n_lat % n_ctx == 0 and n_ctx % TM == 0
    t_all = n_lat + n_ctx
    n_lat_tiles = n_lat // TM
    x_all = jnp.concatenate([x[0], ctx[0]], axis=0)
    mods_all = _ada(c, c_ctx, ada_w, ada_b)
    cos_t, sin_t = _rope_tables(n_lat, n_ctx)

    for i in range(DEPTH):
        kind, j = i % 4, i // 4
        last = i == DEPTH - 1
        n_tok = n_lat if last else t_all
        mods = mods_all[i]
        router = _router_operands(router_w[i], router_b[i])
        lng, lnb = ln_mix_g[i], ln_mix_b[i]
        if kind == 0:
            x1, h2, idx, gates = _fourier_layer(x_all, mods, fn_w_out[j], fn_b_out[j], lng, lnb, router,
                                                n_lat, n_ctx)
        elif kind == 1:
            q, k, v = _qkv(x_all, mods, fa_w_qkv[j], fa_b_qkv[j], cos_t, sin_t, FA_Q_HEADS, FA_KV_HEADS,
                           n_lat_tiles, fa_q_norm[j], fa_k_norm[j])
            o = _full_attention(q, k, v, n_lat, n_ctx)
            x1, h2, idx, gates = _proj_post(o, fa_w_out[j], fa_b_out[j], x_all, mods, lng, lnb, router,
                                            n_tok, n_lat_tiles)
        elif kind == 2:
            x1, h2, idx, gates = _gmlp_layer(x_all, mods, gm_w_in[j], gm_b_in[j], gm_v_norm_g[j], gm_v_norm_b[j],
                                             gm_w_s[j], gm_b_s[j], gm_w_out[j], gm_b_out[j], lng, lnb, router,
                                             n_lat_tiles)
        else:
            q, k, v = _qkv(x_all, mods, wa_w_qkv[j], wa_b_qkv[j], cos_t, sin_t, WA_Q_HEADS, WA_KV_HEADS,
                           n_lat_tiles)
            kt, vx = _kv_layouts(k, v, WA_KV_HEADS)
            o = _window_attention(q, kt, vx, wa_sink[j], n_lat, n_ctx)
            x1, h2, idx, gates = _proj_post(o, wa_w_out[j], wa_b_out[j], x_all, mods, lng, lnb, router,
                                            n_tok, n_lat_tiles)
        x_all = _moe_layer(x1, h2, idx, gates, mods, ln_ffn_g[i], ln_ffn_b[i], exp_w_gate_up,
                           exp_b_gate_up, exp_w_down, exp_b_down, i, n_tok, n_lat)
    return x_all[None]
```

```python
import functools
import math

import numpy as np
import jax
import jax.numpy as jnp
from jax import lax
from jax.experimental import pallas as pl
from jax.experimental.pallas import tpu as pltpu

F32, BF16, I32 = jnp.float32, jnp.bfloat16, jnp.int32

D_MODEL = 1024
DEPTH = 4
GRID_W = 64
N_MOD = 6
FN_GROUPS = 4
HEAD_DIM = 64
FA_Q_HEADS, FA_KV_HEADS = 16, 4
WA_Q_HEADS, WA_KV_HEADS = 16, 2
WINDOW = 128
Q_BLOCK = 128
ROPE_THETA = 10000.0
GM_CHUNK = 128
GM_GROUPS = 8
N_EXPERTS = 32
TOP_K = 4
SWIGLU_LIMIT = 7.0
SWIGLU_ALPHA = 1.702
MOE_BLOCK = 256
LN_EPS = 1e-5
RMS_EPS = 1e-6
NEG = -1e30
ALPHA = (2 * DEPTH) ** 0.25
LOG2E = math.log2(math.e)
Q_SCALE = HEAD_DIM ** -0.5 * LOG2E

LANE = 128
ROW_TILE = D_MODEL // LANE
MXU_WIDTH = 256
TM = 256
FLASH_TQ = 256
FLASH_TK = 1280
FLASH_PAIRS = 3
VMEM_LIMIT = 56 * 2 ** 20


def _params(*sem):
    return pltpu.CompilerParams(dimension_semantics=sem, vmem_limit_bytes=VMEM_LIMIT)


def _dot(a, b):
    return jnp.dot(a, b, preferred_element_type=F32)


def _split(a):
    hi = a.astype(BF16)
    lo = (a - hi.astype(F32)).astype(BF16)
    return hi, lo


def _dot3(a_hi, a_lo, b_hi, b_lo):
    return _dot(a_hi, b_hi) + (_dot(a_hi, b_lo) + _dot(a_lo, b_hi))


def _layer_norm(x, g, b):
    mu = jnp.mean(x, axis=-1, keepdims=True)
    xc = x - mu
    var = jnp.mean(xc * xc, axis=-1, keepdims=True)
    return xc * lax.rsqrt(var + LN_EPS) * g + b


def _top4(logits):
    lane = lax.broadcasted_iota(I32, logits.shape, 1).astype(F32)
    cur = logits
    vals, idxs = [], []
    for _ in range(TOP_K):
        m = jnp.max(cur, axis=-1, keepdims=True)
        i = jnp.min(jnp.where(cur == m, lane, float(LANE)), axis=-1, keepdims=True)
        vals.append(m)
        idxs.append(i)
        cur = jnp.where(lane == i, -jnp.inf, cur)
    exps = [jnp.exp(v - vals[0]) for v in vals]
    inv = 1.0 / (exps[0] + exps[1] + exps[2] + exps[3])
    idx_out = jnp.zeros_like(logits)
    gate_out = jnp.zeros_like(logits)
    for k in range(TOP_K):
        idx_out = jnp.where(lane == float(k), idxs[k], idx_out)
        gate_out = jnp.where(lane == float(k), exps[k] * inv, gate_out)
    return idx_out.astype(I32), gate_out


def _post(y, x, mod, lng, lnb, rw_hi, rw_lo, rb):
    x1 = _layer_norm(ALPHA * x + mod[2:3] * y, lng, lnb)
    h2 = x1 * (1.0 + mod[4:5]) + mod[3:4]
    hh, hl = _split(h2)
    logits = _dot3(hh, hl, rw_hi, rw_lo) + rb
    idx, gates = _top4(logits)
    return x1, h2, idx, gates


def _stack_heads(qb, grp):
    return jnp.concatenate([qb[:, g * HEAD_DIM:(g + 1) * HEAD_DIM] for g in range(grp)], axis=0)


def _unstack_heads(o, grp):
    t = o.shape[0] // grp
    return jnp.concatenate([o[g * t:(g + 1) * t] for g in range(grp)], axis=1)


def _to_row_tiles(ref, x):
    n = x.shape[0]
    for s in range(ROW_TILE):
        ref[pl.ds(s, n, stride=ROW_TILE), :] = x[:, s * LANE:(s + 1) * LANE]


def _from_row_tiles(ref, start, n):
    return jnp.concatenate([ref[pl.ds(start * ROW_TILE + s, n, stride=ROW_TILE), :] for s in range(ROW_TILE)],
                           axis=1)


def _ada_kernel(cs_ref, w_ref, b_ref, o_ref):
    cs = cs_ref[...]
    s = cs * (1.0 / (1.0 + jnp.exp(-cs)))
    sh, sl = _split(s)
    wh, wl = _split(w_ref[0])
    o_ref[0] = _dot3(sh, sl, wh, wl) + b_ref[0]


def _ada(c, c_ctx, ada_w, ada_b):
    d = c.shape[-1]
    nm = ada_w.shape[-1]
    tn = nm // 4
    cs = jnp.zeros((8, d), F32).at[0].set(c[0]).at[1].set(c_ctx)
    out = pl.pallas_call(
        _ada_kernel,
        grid=(DEPTH, nm // tn),
        in_specs=[
            pl.BlockSpec((8, d), lambda i, j: (0, 0)),
            pl.BlockSpec((1, d, tn), lambda i, j: (i, 0, j)),
            pl.BlockSpec((1, 1, tn), lambda i, j: (i, 0, j)),
        ],
        out_specs=pl.BlockSpec((1, 8, tn), lambda i, j: (i, 0, j)),
        out_shape=jax.ShapeDtypeStruct((DEPTH, 8, nm), F32),
        compiler_params=_params("arbitrary", "arbitrary"),
        name="ada",
    )(cs, ada_w, ada_b.reshape(DEPTH, 1, nm))
    return out[:, :2].reshape(DEPTH, 2, N_MOD, d)


def _tile_specs(n_lat_tiles):
    tok = pl.BlockSpec((TM, D_MODEL), lambda t: (t, 0))
    mod = pl.BlockSpec((1, N_MOD, D_MODEL), lambda t: (jnp.where(t >= n_lat_tiles, 1, 0), 0, 0))
    return tok, mod


def _full(shape):
    nd = len(shape)
    return pl.BlockSpec(shape, lambda *_: (0,) * nd)


def _post_out(n_rows):
    shapes = (
        jax.ShapeDtypeStruct((n_rows, D_MODEL), F32),
        jax.ShapeDtypeStruct((n_rows, D_MODEL), F32),
        jax.ShapeDtypeStruct((n_rows, LANE), I32),
        jax.ShapeDtypeStruct((n_rows, LANE), F32),
    )
    specs = (
        pl.BlockSpec((TM, D_MODEL), lambda t: (t, 0)),
        pl.BlockSpec((TM, D_MODEL), lambda t: (t, 0)),
        pl.BlockSpec((TM, LANE), lambda t: (t, 0)),
        pl.BlockSpec((TM, LANE), lambda t: (t, 0)),
    )
    return shapes, specs


def _router_operands(router_w, router_b):
    rw = jnp.zeros((D_MODEL, LANE), F32).at[:, :N_EXPERTS].set(router_w)
    rw_hi = rw.astype(BF16)
    rw_lo = (rw - rw_hi.astype(F32)).astype(BF16)
    rb = jnp.full((1, LANE), NEG, F32).at[0, :N_EXPERTS].set(router_b)
    return rw_hi, rw_lo, rb


def _proj_post_kernel(a_ref, w_ref, b_ref, x_ref, mod_ref, lng_ref, lnb_ref, rwh_ref, rwl_ref, rb_ref,
                      x1_ref, h2_ref, idx_ref, gate_ref):
    y = _dot(a_ref[...], w_ref[...]) + b_ref[...]
    x1, h2, idx, gates = _post(y, x_ref[...], mod_ref[0], lng_ref[...], lnb_ref[...],
                               rwh_ref[...], rwl_ref[...], rb_ref[...])
    x1_ref[...] = x1
    h2_ref[...] = h2
    idx_ref[...] = idx
    gate_ref[...] = gates


def _proj_post(a, w_out, b_out, x, mods, lng, lnb, router, n_rows, n_lat_tiles):
    k = a.shape[1]
    tok, mod = _tile_specs(n_lat_tiles)
    shapes, specs = _post_out(n_rows)
    return pl.pallas_call(
        _proj_post_kernel,
        grid=(n_rows // TM,),
        in_specs=[
            pl.BlockSpec((TM, k), lambda t: (t, 0)),
            _full((k, D_MODEL)), _full((1, D_MODEL)),
            tok, mod, _full((1, D_MODEL)), _full((1, D_MODEL)),
            _full((D_MODEL, LANE)), _full((D_MODEL, LANE)), _full((1, LANE)),
        ],
        out_specs=specs,
        out_shape=shapes,
        compiler_params=_params("arbitrary"),
        name="proj_post",
    )(a, w_out.astype(BF16), b_out.reshape(1, -1), x, mods, lng.reshape(1, -1), lnb.reshape(1, -1), *router)


def _dft_mats(n):
    jk = np.outer(np.arange(n), np.arange(n)) % n
    ang = 2.0 * np.pi * jk / n
    out = []
    for m in (np.cos(ang), np.sin(ang)):
        m32 = jnp.asarray(m, F32)
        hi = m32.astype(BF16)
        out += [hi, (m32 - hi.astype(F32)).astype(BF16)]
    return out


def _channel_dft(h, cc, sc):
    cw = cc[0].shape[0]
    a_parts, b_parts = [], []
    for g in range(h.shape[1] // cw):
        hh, hl = _split(h[:, g * cw:(g + 1) * cw])
        a_parts.append(_dot3(hh, hl, cc[0][...], cc[1][...]))
        b_parts.append(_dot3(hh, hl, sc[0][...], sc[1][...]))
    return jnp.concatenate(a_parts, axis=1), jnp.concatenate(b_parts, axis=1)


def _fourier1_kernel(x_ref, mod_ref, cch_ref, ccl_ref, sch_ref, scl_ref, tc_ref, ts_ref, ur_ref, ui_ref):
    mod = mod_ref[0]
    h = x_ref[...] * (1.0 + mod[1:2]) + mod[0:1]
    a, b = _channel_dft(h, (cch_ref, ccl_ref), (sch_ref, scl_ref))
    tch, tcl = _split(tc_ref[0])
    tsh, tsl = _split(ts_ref[0])
    ah, al = _split(a)
    bh, bl = _split(b)
    ur_ref[...] = _dot3(tch, tcl, ah, al) - _dot3(tsh, tsl, bh, bl)
    ui_ref[...] = -(_dot3(tch, tcl, bh, bl) + _dot3(tsh, tsl, ah, al))


def _fourier2_kernel(ur_ref, ui_ref, c2h_ref, c2l_ref, s2h_ref, s2l_ref, w_ref, b_ref, x_ref, mod_ref,
                     lng_ref, lnb_ref, rwh_ref, rwl_ref, rb_ref, x1_ref, h2_ref, idx_ref, gate_ref, *, norm):
    urh, url = _split(ur_ref[0])
    uih, uil = _split(ui_ref[0])
    mixed = (_dot3(c2h_ref[...], c2l_ref[...], urh, url) + _dot3(s2h_ref[...], s2l_ref[...], uih, uil)) * norm
    y = _dot(mixed.astype(BF16), w_ref[...]) + b_ref[...]
    x1, h2, idx, gates = _post(y, x_ref[...], mod_ref[0], lng_ref[...], lnb_ref[...],
                               rwh_ref[...], rwl_ref[...], rb_ref[...])
    x1_ref[...] = x1
    h2_ref[...] = h2
    idx_ref[...] = idx
    gate_ref[...] = gates


def _fourier_ctx_kernel(x_ref, mod_ref, cch_ref, ccl_ref, sch_ref, scl_ref, cnh_ref, cnl_ref, snh_ref, snl_ref,
                        w_ref, b_ref, lng_ref, lnb_ref, rwh_ref, rwl_ref, rb_ref,
                        x1_ref, h2_ref, idx_ref, gate_ref, *, norm):
    mod = mod_ref[0]
    x = x_ref[...]
    h = x * (1.0 + mod[1:2]) + mod[0:1]
    a, b = _channel_dft(h, (cch_ref, ccl_ref), (sch_ref, scl_ref))
    ah, al = _split(a)
    bh, bl = _split(b)
    mixed = (_dot3(cnh_ref[...], cnl_ref[...], ah, al) - _dot3(snh_ref[...], snl_ref[...], bh, bl)) * norm
    y = _dot(mixed.astype(BF16), w_ref[...]) + b_ref[...]
    x1, h2, idx, gates = _post(y, x, mod, lng_ref[...], lnb_ref[...], rwh_ref[...], rwl_ref[...], rb_ref[...])
    x1_ref[...] = x1
    h2_ref[...] = h2
    idx_ref[...] = idx
    gate_ref[...] = gates


def _fourier_layer(x_all, mods, w_out, b_out, lng, lnb, router, n_lat, n_ctx):
    t_all, d = x_all.shape
    n2 = LANE
    n1 = n_lat // n2
    cw = d // FN_GROUPS
    rows = t_all // n2
    xv = x_all.reshape(rows, n2 * d)
    cmat = _dft_mats(cw)
    w_bf = w_out.astype(BF16)
    b2 = b_out.reshape(1, d)
    lng2, lnb2 = lng.reshape(1, d), lnb.reshape(1, d)

    k1 = jnp.arange(n1, dtype=I32)
    pos = jnp.arange(n1, dtype=I32)[None, None, :] * n2 + jnp.arange(n2, dtype=I32)[:, None, None]
    ang = ((k1[None, :, None] * pos) % n_lat).astype(F32) * (2.0 * math.pi / n_lat)
    tc, ts = jnp.cos(ang), jnp.sin(ang)

    mat = _full((cw, cw))
    ur, ui = pl.pallas_call(
        _fourier1_kernel,
        grid=(n2,),
        in_specs=[
            pl.BlockSpec((n1, d), lambda j: (0, j)),
            pl.BlockSpec((1, N_MOD, d), lambda j: (0, 0, 0)),
            mat, mat, mat, mat,
            pl.BlockSpec((1, n1, n1), lambda j: (j, 0, 0)),
            pl.BlockSpec((1, n1, n1), lambda j: (j, 0, 0)),
        ],
        out_specs=(pl.BlockSpec((n1, d), lambda j: (0, j)), pl.BlockSpec((n1, d), lambda j: (0, j))),
        out_shape=(jax.ShapeDtypeStruct((n1, n2 * d), F32), jax.ShapeDtypeStruct((n1, n2 * d), F32)),
        compiler_params=_params("arbitrary"),
        name="fourier1",
    )(xv, mods, *cmat, tc, ts)

    m2 = _dft_mats(n2)
    mat2 = _full((n2, n2))
    norm = 1.0 / math.sqrt(n_lat * cw)
    out_shapes = (
        jax.ShapeDtypeStruct((n1, n2 * d), F32),
        jax.ShapeDtypeStruct((n1, n2 * d), F32),
        jax.ShapeDtypeStruct((n1, n2 * LANE), I32),
        jax.ShapeDtypeStruct((n1, n2 * LANE), F32),
    )
    strided = pl.BlockSpec((n2, d), lambda k: (0, k))
    strided_l = pl.BlockSpec((n2, LANE), lambda k: (0, k))
    outs = pl.pallas_call(
        functools.partial(_fourier2_kernel, norm=norm),
        grid=(n1,),
        in_specs=[
            pl.BlockSpec((1, n2, d), lambda k: (k, 0, 0)),
            pl.BlockSpec((1, n2, d), lambda k: (k, 0, 0)),
            mat2, mat2, mat2, mat2,
            _full((d, d)), _full((1, d)),
            strided,
            pl.BlockSpec((1, N_MOD, d), lambda k: (0, 0, 0)),
            _full((1, d)), _full((1, d)),
            _full((d, LANE)), _full((d, LANE)), _full((1, LANE)),
        ],
        out_specs=(strided, strided, strided_l, strided_l),
        out_shape=out_shapes,
        compiler_params=_params("arbitrary"),
        name="fourier2",
    )(ur.reshape(n1, n2, d), ui.reshape(n1, n2, d), *m2, w_bf, b2, xv, mods, lng2, lnb2, *router)
    lat = (outs[0].reshape(n_lat, d), outs[1].reshape(n_lat, d),
           outs[2].reshape(n_lat, LANE), outs[3].reshape(n_lat, LANE))

    cn = _dft_mats(n_ctx)
    matn = _full((n_ctx, n_ctx))
    shapes = (
        jax.ShapeDtypeStruct((n_ctx, d), F32), jax.ShapeDtypeStruct((n_ctx, d), F32),
        jax.ShapeDtypeStruct((n_ctx, LANE), I32), jax.ShapeDtypeStruct((n_ctx, LANE), F32),
    )
    ctx = pl.pallas_call(
        functools.partial(_fourier_ctx_kernel, norm=1.0 / math.sqrt(n_ctx * cw)),
        grid=(1,),
        in_specs=[
            pl.BlockSpec((n_ctx, d), lambda i: (n_lat // n_ctx, 0)),
            pl.BlockSpec((1, N_MOD, d), lambda i: (1, 0, 0)),
            mat, mat, mat, mat, matn, matn, matn, matn,
            _full((d, d)), _full((1, d)), _full((1, d)), _full((1, d)),
            _full((d, LANE)), _full((d, LANE)), _full((1, LANE)),
        ],
        out_specs=(_full((n_ctx, d)), _full((n_ctx, d)), _full((n_ctx, LANE)), _full((n_ctx, LANE))),
        out_shape=shapes,
        compiler_params=_params("arbitrary"),
        name="fourier_ctx",
    )(x_all, mods, *cmat, *cn, w_bf, b2, lng2, lnb2, *router)
    return tuple(jnp.concatenate([a, b], axis=0) for a, b in zip(lat, ctx))


def _qkv_kernel(x_ref, mod_ref, w_ref, b_ref, cos_ref, sin_ref, *rest, n_qk, rms):
    if rms:
        gain_ref, ind_ref, indt_ref, q_ref, k_ref, v_ref = rest
    else:
        q_ref, k_ref, v_ref = rest
    mod = mod_ref[0]
    h = x_ref[...] * (1.0 + mod[1:2]) + mod[0:1]
    y = _dot(h.astype(BF16), w_ref[...]) + b_ref[...]
    qk = y[:, :n_qk]
    if rms:
        sh, sl = _split(qk * qk)
        ms = _dot(sh, ind_ref[...]) + _dot(sl, ind_ref[...])
        mh, ml = _split(ms)
        msb = _dot(mh, indt_ref[...]) + _dot(ml, indt_ref[...])
        qk = qk * lax.rsqrt(msb + RMS_EPS) * gain_ref[...]
    cos = cos_ref[...]
    sin = sin_ref[...]
    even = (lax.broadcasted_iota(I32, cos.shape, 1) & 1) == 0
    parts = []
    for c in range(n_qk // LANE):
        z = qk[:, c * LANE:(c + 1) * LANE]
        swapped = jnp.where(even, pltpu.roll(z, LANE - 1, 1), pltpu.roll(z, 1, 1))
        parts.append(z * cos + swapped * sin)
    nq = q_ref.shape[1]
    q_ref[...] = (jnp.concatenate(parts[:nq // LANE], axis=1) * Q_SCALE).astype(BF16)
    k_ref[...] = jnp.concatenate(parts[nq // LANE:], axis=1).astype(BF16)
    v_ref[...] = y[:, n_qk:].astype(BF16)


def _qkv(x_all, mods, w_qkv, b_qkv, cos_t, sin_t, n_q, n_kv, n_lat_tiles, q_norm=None, k_norm=None):
    t_all, d = x_all.shape
    nq, nk = n_q * HEAD_DIM, n_kv * HEAD_DIM
    n_qk, n_all = nq + nk, nq + 2 * nk
    rms = q_norm is not None
    tok, mod = _tile_specs(n_lat_tiles)
    in_specs = [tok, mod, _full((d, n_all)), _full((1, n_all)),
                pl.BlockSpec((TM, LANE), lambda t: (t, 0)), pl.BlockSpec((TM, LANE), lambda t: (t, 0))]
    args = [x_all, mods, w_qkv.astype(BF16), b_qkv.reshape(1, n_all), cos_t, sin_t]
    if rms:
        gain = jnp.concatenate([jnp.tile(q_norm, n_q), jnp.tile(k_norm, n_kv)]).reshape(1, n_qk)
        head = np.arange(n_qk) // HEAD_DIM
        ind = np.zeros((n_qk, LANE), np.float32)
        ind[np.arange(n_qk), head] = 1.0 / HEAD_DIM
        indt = np.zeros((LANE, n_qk), np.float32)
        indt[head, np.arange(n_qk)] = 1.0
        in_specs += [_full((1, n_qk)), _full((n_qk, LANE)), _full((LANE, n_qk))]
        args += [gain, jnp.asarray(ind, BF16), jnp.asarray(indt, BF16)]
    return pl.pallas_call(
        functools.partial(_qkv_kernel, n_qk=n_qk, rms=rms),
        grid=(t_all // TM,),
        in_specs=in_specs,
        out_specs=(pl.BlockSpec((TM, nq), lambda t: (t, 0)), pl.BlockSpec((TM, nk), lambda t: (t, 0)),
                   pl.BlockSpec((TM, nk), lambda t: (t, 0))),
        out_shape=(jax.ShapeDtypeStruct((t_all, nq), BF16), jax.ShapeDtypeStruct((t_all, nk), BF16),
                   jax.ShapeDtypeStruct((t_all, nk), BF16)),
        compiler_params=_params("arbitrary"),
        name="qkv",
    )(*args)


def _rope_tables(n_lat, n_ctx):
    rows = n_lat // GRID_W
    row = jnp.repeat(jnp.arange(rows, dtype=F32), GRID_W)
    col = jnp.tile(jnp.arange(GRID_W, dtype=F32), rows)
    n_freq = HEAD_DIM // 4
    inv = ROPE_THETA ** (-jnp.arange(n_freq, dtype=F32) / n_freq)
    ang = jnp.concatenate([row[:, None] * inv, col[:, None] * inv], axis=-1)
    ang = jnp.concatenate([ang, jnp.zeros((n_ctx, HEAD_DIM // 2), F32)], axis=0)
    cos = jnp.tile(jnp.repeat(jnp.cos(ang), 2, axis=1), (1, LANE // HEAD_DIM))
    sin = jnp.tile(jnp.repeat(jnp.sin(ang), 2, axis=1), (1, LANE // HEAD_DIM))
    sign = jnp.where(jnp.arange(LANE) % 2 == 0, -1.0, 1.0).astype(F32)
    return cos, sin * sign


def _kv_layouts(k, v, n_kv):
    t_all = k.shape[0]
    kt = k.reshape(t_all, n_kv, HEAD_DIM).transpose(1, 2, 0)
    vh = v.reshape(t_all, n_kv, HEAD_DIM).transpose(1, 0, 2)
    pad = jnp.zeros((n_kv, t_all, LANE - HEAD_DIM), BF16).at[:, :, 0].set(1.0)
    return kt, jnp.concatenate([vh, pad], axis=-1)


def _flash_kernel(q_ref, k_ref, vt_ref, o_ref, s_even, s_odd, *, grp, tk, n_lat, n_ctx):
    tq = q_ref.shape[0]
    q_t = q_ref[...].astype(F32).T
    qt = jnp.concatenate([q_t[g * HEAD_DIM:(g + 1) * HEAD_DIM] for g in range(grp)], axis=1).astype(BF16)
    cols = qt.shape[1]
    is_ctx = pl.program_id(1) == n_lat // FLASH_TQ
    n_chunks = (n_lat + n_ctx) // tk

    def finish(acc):
        o = acc[:HEAD_DIM] / acc[HEAD_DIM:HEAD_DIM + 1]
        o_t = jnp.concatenate([o[:, g * tq:(g + 1) * tq] for g in range(grp)], axis=0)
        o_ref[...] = o_t.T.astype(BF16)

    groups = [slice(c * MXU_WIDTH, (c + 1) * MXU_WIDTH) for c in range(cols // MXU_WIDTH)]

    def scores(j, buf, g):
        off = pl.multiple_of(j * tk, tk)
        s = _dot(k_ref[0, pl.ds(off, tk), :], qt[:, g])
        buf[:, g] = s
        return jnp.max(s, axis=0, keepdims=True)

    def accumulate(j, buf, g, m, acc, mc):
        off = pl.multiple_of(j * tk, tk)
        m_new = jnp.maximum(m, mc)
        p = jnp.exp2(buf[:, g] - m_new)
        acc = jnp.exp2(m - m_new) * acc + _dot(vt_ref[0, :, pl.ds(off, tk)], p.astype(BF16))
        return m_new, acc

    @pl.when(jnp.logical_not(is_ctx))
    def _():
        def step(j, cur, nxt, carry):
            out = []
            for g, (m, acc, mc) in zip(groups, carry):
                mc_next = scores(j + 1, nxt, g)
                out.append(accumulate(j, cur, g, m, acc, mc) + (mc_next,))
            return out

        def body(jj, carry):
            for pair in range(FLASH_PAIRS):
                j = 2 * (FLASH_PAIRS * jj + pair)
                carry = step(j, s_even, s_odd, carry)
                carry = step(j + 1, s_odd, s_even, carry)
            return carry

        init = [(jnp.full((1, MXU_WIDTH), NEG, F32), jnp.zeros((LANE, MXU_WIDTH), F32), scores(0, s_even, g))
                for g in groups]
        carry = lax.fori_loop(0, (n_chunks - 1) // (2 * FLASH_PAIRS), body, init)
        finish(jnp.concatenate([accumulate(n_chunks - 1, s_even, g, m, acc, mc)[1]
                                for g, (m, acc, mc) in zip(groups, carry)], axis=1))

    @pl.when(is_ctx)
    def _():
        s = _dot(k_ref[0, n_lat:n_lat + n_ctx, :], qt)
        p = jnp.exp2(s - jnp.max(s, axis=0, keepdims=True))
        finish(_dot(vt_ref[0, :, n_lat:n_lat + n_ctx], p.astype(BF16)))


def _full_attention(q, k, v, n_lat, n_ctx):
    t_all, nq = q.shape
    n_kv = k.shape[1] // HEAD_DIM
    grp = nq // HEAD_DIM // n_kv
    nt = t_all // FLASH_TQ
    cols = grp * FLASH_TQ
    assert n_ctx == FLASH_TQ and t_all % FLASH_TK == 0 and (t_all // FLASH_TK - 1) % (2 * FLASH_PAIRS) == 0
    kh = k.reshape(t_all, n_kv, HEAD_DIM).transpose(1, 0, 2)
    vt = v.reshape(t_all, n_kv, HEAD_DIM).transpose(1, 2, 0)
    pad = jnp.zeros((n_kv, LANE - HEAD_DIM, t_all), BF16).at[:, 0, :].set(1.0)
    vt = jnp.concatenate([vt, pad], axis=1)
    return pl.pallas_call(
        functools.partial(_flash_kernel, grp=grp, tk=FLASH_TK, n_lat=n_lat, n_ctx=n_ctx),
        grid=(n_kv, nt),
        in_specs=[
            pl.BlockSpec((FLASH_TQ, grp * HEAD_DIM), lambda h, i: (i, h)),
            pl.BlockSpec((1, t_all, HEAD_DIM), lambda h, i: (h, 0, 0)),
            pl.BlockSpec((1, LANE, t_all), lambda h, i: (h, 0, 0)),
        ],
        out_specs=pl.BlockSpec((FLASH_TQ, grp * HEAD_DIM), lambda h, i: (i, h)),
        out_shape=jax.ShapeDtypeStruct((t_all, nq), BF16),
        scratch_shapes=[pltpu.VMEM((FLASH_TK, cols), F32), pltpu.VMEM((FLASH_TK, cols), F32)],
        compiler_params=_params("arbitrary", "arbitrary"),
        name="flash",
    )(q, kh, vt)


def _window_kernel(q_ref, kp_ref, kc_ref, kn_ref, kx_ref, vp_ref, vc_ref, vn_ref, vx_ref, sink_ref, o_ref,
                   *, grp, nb):
    i = pl.program_id(1)
    qs = _stack_heads(q_ref[...], grp)
    rows = qs.shape[0]
    qi = lax.broadcasted_iota(I32, (rows, Q_BLOCK), 0) & (Q_BLOCK - 1)
    kj = lax.broadcasted_iota(I32, (rows, Q_BLOCK), 1)
    sp = jnp.where(kj >= jnp.where(i > 0, qi, Q_BLOCK), _dot(qs, kp_ref[0]), NEG)
    sc = _dot(qs, kc_ref[0])
    sn = jnp.where(kj <= jnp.where(i < nb - 1, qi, -1), _dot(qs, kn_ref[0]), NEG)
    sx = _dot(qs, kx_ref[0])
    sink = sink_ref[0][:, :1]
    m = jnp.maximum(jnp.maximum(jnp.max(sp, axis=-1, keepdims=True), jnp.max(sc, axis=-1, keepdims=True)),
                    jnp.maximum(jnp.max(sn, axis=-1, keepdims=True), jnp.max(sx, axis=-1, keepdims=True)))
    m = jnp.maximum(m, sink)
    acc = (_dot(jnp.exp2(sp - m).astype(BF16), vp_ref[0]) + _dot(jnp.exp2(sc - m).astype(BF16), vc_ref[0])
           + _dot(jnp.exp2(sn - m).astype(BF16), vn_ref[0]) + _dot(jnp.exp2(sx - m).astype(BF16), vx_ref[0]))
    den = acc[:, HEAD_DIM:HEAD_DIM + 1] + jnp.exp2(sink - m)
    o_ref[...] = _unstack_heads(acc[:, :HEAD_DIM] / den, grp).astype(BF16)


def _window_attention(q, kt, vx, sink, n_lat, n_ctx):
    nq = q.shape[1]
    n_kv = kt.shape[0]
    grp = nq // HEAD_DIM // n_kv
    gw = grp * HEAD_DIM
    nb = n_lat // Q_BLOCK
    cb = n_lat // n_ctx
    sink_rows = jnp.broadcast_to((sink.reshape(n_kv, grp) * LOG2E)[:, :, None, None],
                                 (n_kv, grp, Q_BLOCK, LANE)).reshape(n_kv, grp * Q_BLOCK, LANE)
    kspec = lambda f: pl.BlockSpec((1, HEAD_DIM, Q_BLOCK), lambda h, i: (h, 0, f(i)))
    vspec = lambda f: pl.BlockSpec((1, Q_BLOCK, LANE), lambda h, i: (h, f(i), 0))
    prev = lambda i: jnp.maximum(i - 1, 0)
    cur = lambda i: i
    nxt = lambda i: jnp.minimum(i + 1, nb - 1)
    return pl.pallas_call(
        functools.partial(_window_kernel, grp=grp, nb=nb),
        grid=(n_kv, nb),
        in_specs=[
            pl.BlockSpec((Q_BLOCK, gw), lambda h, i: (i, h)),
            kspec(prev), kspec(cur), kspec(nxt),
            pl.BlockSpec((1, HEAD_DIM, n_ctx), lambda h, i: (h, 0, cb)),
            vspec(prev), vspec(cur), vspec(nxt),
            pl.BlockSpec((1, n_ctx, LANE), lambda h, i: (h, cb, 0)),
            pl.BlockSpec((1, grp * Q_BLOCK, LANE), lambda h, i: (h, 0, 0)),
        ],
        out_specs=pl.BlockSpec((Q_BLOCK, gw), lambda h, i: (i, h)),
        out_shape=jax.ShapeDtypeStruct((n_lat, nq), BF16),
        compiler_params=_params("arbitrary", "arbitrary"),
        name="window",
    )(q, kt, kt, kt, kt, vx, vx, vx, vx, sink_rows)


def _gmlp_kernel(x_ref, mod_ref, win_ref, bin_ref, vg_ref, vb_ref, ws_ref, bs_ref, wout_ref, bout_ref,
                 lng_ref, lnb_ref, rwh_ref, rwl_ref, rb_ref, x1_ref, h2_ref, idx_ref, gate_ref):
    mod = mod_ref[0]
    x = x_ref[...]
    h = x * (1.0 + mod[1:2]) + mod[0:1]
    z = _dot(h.astype(BF16), win_ref[...]) + bin_ref[...]
    z = 0.5 * z * (1.0 + lax.erf(z * (2.0 ** -0.5)))
    half = z.shape[1] // 2
    u = z[:, :half]
    v = _layer_norm(z[:, half:], vg_ref[...], vb_ref[...]).astype(BF16)
    cw = half // GM_GROUPS
    chunks = []
    for c in range(x.shape[0] // GM_CHUNK):
        vc = v[c * GM_CHUNK:(c + 1) * GM_CHUNK]
        chunks.append(jnp.concatenate(
            [_dot(ws_ref[g], vc[:, g * cw:(g + 1) * cw]) + bs_ref[g] for g in range(GM_GROUPS)], axis=1))
    gated = u * jnp.concatenate(chunks, axis=0)
    y = _dot(gated.astype(BF16), wout_ref[...]) + bout_ref[...]
    x1, h2, idx, gates = _post(y, x, mod, lng_ref[...], lnb_ref[...], rwh_ref[...], rwl_ref[...], rb_ref[...])
    x1_ref[...] = x1
    h2_ref[...] = h2
    idx_ref[...] = idx
    gate_ref[...] = gates


def _gmlp_layer(x_all, mods, w_in, b_in, vg, vb, w_s, b_s, w_out, b_out, lng, lnb, router, n_lat_tiles):
    t_all, d = x_all.shape
    dffn = w_in.shape[1]
    half = dffn // 2
    cw = half // GM_GROUPS
    tok, mod = _tile_specs(n_lat_tiles)
    shapes, specs = _post_out(t_all)
    bs_full = jnp.broadcast_to(b_s[:, :, None], (GM_GROUPS, GM_CHUNK, cw))
    return pl.pallas_call(
        _gmlp_kernel,
        grid=(t_all // TM,),
        in_specs=[
            tok, mod, _full((d, dffn)), _full((1, dffn)), _full((1, half)), _full((1, half)),
            _full((GM_GROUPS, GM_CHUNK, GM_CHUNK)), _full((GM_GROUPS, GM_CHUNK, cw)),
            _full((half, d)), _full((1, d)), _full((1, d)), _full((1, d)),
            _full((d, LANE)), _full((d, LANE)), _full((1, LANE)),
        ],
        out_specs=specs,
        out_shape=shapes,
        compiler_params=_params("arbitrary"),
        name="gmlp",
    )(x_all, mods, w_in.astype(BF16), b_in.reshape(1, dffn), vg.reshape(1, half), vb.reshape(1, half),
      w_s.astype(BF16), bs_full, w_out.astype(BF16), b_out.reshape(1, d), lng.reshape(1, d), lnb.reshape(1, d),
      *router)


COMB_TM = 128


def _route_meta(top_idx, n_tok):
    n_assign = n_tok * TOP_K
    n_blocks = -(-n_assign // MOE_BLOCK) + N_EXPERTS
    ids = jnp.arange(N_EXPERTS, dtype=I32)
    counts = jnp.sum((top_idx[:n_tok, :TOP_K, None] == ids).astype(I32), axis=(0, 1))
    padded = (counts + MOE_BLOCK - 1) // MOE_BLOCK * MOE_BLOCK
    ends_pad = jnp.cumsum(padded)
    base = ends_pad - padded
    starts = jnp.arange(n_blocks, dtype=I32) * MOE_BLOCK
    block_expert = jnp.minimum(jnp.sum((ends_pad[None, :] <= starts[:, None]).astype(I32), axis=1), N_EXPERTS - 1)
    last_blk = jnp.where(padded > 0, ends_pad - MOE_BLOCK, -1)
    tail = ends_pad[-1] + ids * MOE_BLOCK
    tail = jnp.where(tail < n_blocks * MOE_BLOCK, tail, -1)
    base_b = jnp.broadcast_to(base.astype(F32)[:, None], (N_EXPERTS, LANE))
    return block_expert.astype(I32), base_b, jnp.concatenate([last_blk, tail]).astype(I32), n_blocks


def _rank_kernel(idx_ref, base_ref, upper_ref, dest_ref, run_ref):
    @pl.when(pl.program_id(0) == 0)
    def _():
        run_ref[...] = jnp.zeros(run_ref.shape, F32)

    eid = lax.broadcasted_iota(I32, (N_EXPERTS, COMB_TM), 0)
    onehots = [(eid == idx_ref[k:k + 1, :]).astype(F32) for k in range(TOP_K)]
    cnt = onehots[0] + onehots[1] + onehots[2] + onehots[3]
    before = _dot(cnt.astype(BF16), upper_ref[...])
    slot = base_ref[...] + run_ref[...] + before
    for k in range(TOP_K):
        dest_ref[0, k:k + 1, :] = (jnp.sum(onehots[k] * slot, axis=0, keepdims=True) * ROW_TILE).astype(I32)
    run_ref[...] = run_ref[...] + jnp.sum(cnt, axis=1, keepdims=True)


def _rank(top_idx, base_b, n_tok):
    n_tiles = n_tok // COMB_TM
    idx_t = top_idx[:n_tok, :TOP_K].T
    upper = jnp.asarray(np.triu(np.ones((COMB_TM, COMB_TM), np.float32), 1), BF16)
    return pl.pallas_call(
        _rank_kernel,
        grid=(n_tiles,),
        in_specs=[pl.BlockSpec((TOP_K, COMB_TM), lambda t: (0, t)), _full((N_EXPERTS, LANE)),
                  _full((COMB_TM, COMB_TM))],
        out_specs=pl.BlockSpec((1, TOP_K, COMB_TM), lambda t: (t, 0, 0)),
        out_shape=jax.ShapeDtypeStruct((n_tiles, TOP_K, COMB_TM), I32),
        scratch_shapes=[pltpu.VMEM((N_EXPERTS, LANE), F32)],
        compiler_params=_params("arbitrary"),
        name="rank",
    )(idx_t, base_b, upper)


def _dispatch_kernel(dest_ref, zrow_ref, h_ref, xs_hbm, hbuf, zbuf, sem, zsem, *, n_tiles):
    t = pl.program_id(0)
    slot = t % 2
    n_zero = 2 * N_EXPERTS

    def zero_copy(j):
        row = pl.multiple_of(zrow_ref[j] * ROW_TILE, MOE_BLOCK * ROW_TILE)
        return pltpu.make_async_copy(zbuf, xs_hbm.at[pl.ds(row, MOE_BLOCK * ROW_TILE), :], zsem.at[0])

    def wait_rows(s):
        for _ in range(TOP_K):
            pltpu.make_async_copy(hbuf.at[s], xs_hbm.at[pl.ds(0, COMB_TM * ROW_TILE), :], sem.at[s]).wait()

    @pl.when(t == 0)
    def _():
        zbuf[...] = jnp.zeros(zbuf.shape, F32)
        for j in range(n_zero):
            @pl.when(zrow_ref[j] >= 0)
            def _():
                zero_copy(j).start()
        for j in range(n_zero):
            @pl.when(zrow_ref[j] >= 0)
            def _():
                zero_copy(j).wait()

    @pl.when(t >= 2)
    def _():
        wait_rows(slot)

    _to_row_tiles(hbuf.at[slot], h_ref[...])
    for k in range(TOP_K):
        for r in range(COMB_TM):
            row = pl.multiple_of(dest_ref[(t * TOP_K + k) * COMB_TM + r], ROW_TILE)
            pltpu.make_async_copy(hbuf.at[slot, pl.ds(r * ROW_TILE, ROW_TILE), :],
                                  xs_hbm.at[pl.ds(row, ROW_TILE), :], sem.at[slot]).start(priority=r % 2)

    @pl.when(t == n_tiles - 1)
    def _():
        wait_rows(slot)
        if n_tiles > 1:
            wait_rows(1 - slot)


def _dispatch(h2, dest, zrows, n_tok, n_blocks):
    d = h2.shape[1]
    n_tiles = n_tok // COMB_TM
    grid_spec = pltpu.PrefetchScalarGridSpec(
        num_scalar_prefetch=2,
        grid=(n_tiles,),
        in_specs=[pl.BlockSpec((COMB_TM, d), lambda t, dest, zr: (t, 0))],
        out_specs=pl.BlockSpec(memory_space=pl.ANY),
        scratch_shapes=[pltpu.VMEM((2, COMB_TM * ROW_TILE, LANE), F32),
                        pltpu.VMEM((MOE_BLOCK * ROW_TILE, LANE), F32),
                        pltpu.SemaphoreType.DMA((2,)), pltpu.SemaphoreType.DMA((1,))],
    )
    return pl.pallas_call(
        functools.partial(_dispatch_kernel, n_tiles=n_tiles),
        grid_spec=grid_spec,
        out_shape=jax.ShapeDtypeStruct((n_blocks * MOE_BLOCK * ROW_TILE, LANE), F32),
        compiler_params=_params("arbitrary"),
        name="dispatch",
    )(dest.reshape(-1), zrows, h2)


EXP_CHUNK = 512


def _expert_kernel(be_ref, x_ref, wgu_ref, bgu_ref, wd_ref, bd_ref, sel_e_ref, sel_o_ref, y_ref,
                   wg_s, wl_s, wd_s, bg_s, bl_s):
    b = pl.program_id(0)
    ff = wg_s.shape[1]
    half = EXP_CHUNK // 2

    @pl.when(jnp.logical_or(b == 0, be_ref[b] != be_ref[jnp.maximum(b - 1, 0)]))
    def _():
        for c in range(2 * ff // EXP_CHUNK):
            w = wgu_ref[0, 0, :, c * EXP_CHUNK:(c + 1) * EXP_CHUNK].astype(BF16)
            wg_s[:, c * half:(c + 1) * half] = _dot(w, sel_e_ref[...]).astype(BF16)
            wl_s[:, c * half:(c + 1) * half] = _dot(w, sel_o_ref[...]).astype(BF16)
            bh, bl = _split(jnp.broadcast_to(bgu_ref[0, 0, :, c * EXP_CHUNK:(c + 1) * EXP_CHUNK], (8, EXP_CHUNK)))
            bg_s[:, c * half:(c + 1) * half] = _dot(bh, sel_e_ref[...]) + _dot(bl, sel_e_ref[...])
            bl_s[:, c * half:(c + 1) * half] = _dot(bh, sel_o_ref[...]) + _dot(bl, sel_o_ref[...])
        wd_s[...] = wd_ref[0, 0].astype(BF16)

    x = _from_row_tiles(x_ref, 0, MOE_BLOCK).astype(BF16)
    glu = jnp.minimum(_dot(x, wg_s[...]) + bg_s[0:1, :], SWIGLU_LIMIT)
    lin = jnp.clip(_dot(x, wl_s[...]) + bl_s[0:1, :], -SWIGLU_LIMIT, SWIGLU_LIMIT)
    act = glu * (1.0 / (1.0 + jnp.exp(-SWIGLU_ALPHA * glu))) * (lin + 1.0)
    _to_row_tiles(y_ref, _dot(act.astype(BF16), wd_s[...]) + bd_ref[0, 0])


def _experts(xs, block_expert, n_blocks, w_gate_up, b_gate_up, w_down, b_down, layer):
    d = w_down.shape[3]
    ff = w_down.shape[2]
    sel = np.zeros((2, EXP_CHUNK, EXP_CHUNK // 2), np.float32)
    sel[0, 2 * np.arange(EXP_CHUNK // 2), np.arange(EXP_CHUNK // 2)] = 1.0
    sel[1, 2 * np.arange(EXP_CHUNK // 2) + 1, np.arange(EXP_CHUNK // 2)] = 1.0
    wspec = lambda r, c: pl.BlockSpec((1, 1, r, c), lambda b, be: (layer, be[b], 0, 0))
    sspec = pl.BlockSpec((EXP_CHUNK, EXP_CHUNK // 2), lambda b, be: (0, 0))
    grid_spec = pltpu.PrefetchScalarGridSpec(
        num_scalar_prefetch=1,
        grid=(n_blocks,),
        in_specs=[pl.BlockSpec((MOE_BLOCK * ROW_TILE, LANE), lambda b, be: (b, 0)), wspec(d, 2 * ff),
                  wspec(1, 2 * ff), wspec(ff, d), wspec(1, d), sspec, sspec],
        out_specs=pl.BlockSpec((MOE_BLOCK * ROW_TILE, LANE), lambda b, be: (b, 0)),
        scratch_shapes=[pltpu.VMEM((d, ff), BF16), pltpu.VMEM((d, ff), BF16), pltpu.VMEM((ff, d), BF16),
                        pltpu.VMEM((8, ff), F32), pltpu.VMEM((8, ff), F32)],
    )
    return pl.pallas_call(
        _expert_kernel,
        grid_spec=grid_spec,
        out_shape=jax.ShapeDtypeStruct((n_blocks * MOE_BLOCK * ROW_TILE, LANE), F32),
        compiler_params=_params("arbitrary"),
        name="experts",
    )(block_expert, xs, w_gate_up, b_gate_up[:, :, None, :], w_down, b_down[:, :, None, :],
      jnp.asarray(sel[0], BF16), jnp.asarray(sel[1], BF16))


def _gather_rows(idx_ref, base, n, src_hbm, dst, sem):
    for r in range(n):
        row = pl.multiple_of(idx_ref[base + r], ROW_TILE)
        pltpu.make_async_copy(src_hbm.at[pl.ds(row, ROW_TILE), :], dst.at[pl.ds(r * ROW_TILE, ROW_TILE), :],
                              sem).start(priority=r % 2)


def _combine_kernel(pos_ref, y_hbm, gate_ref, x_ref, mod_ref, lng_ref, lnb_ref, o_ref, ybuf, sem, *, n_tiles):
    t = pl.program_id(0)
    slot = t % 2
    n = COMB_TM * TOP_K

    @pl.when(t == 0)
    def _():
        _gather_rows(pos_ref, 0, n, y_hbm, ybuf.at[0], sem.at[0])

    @pl.when(t + 1 < n_tiles)
    def _():
        _gather_rows(pos_ref, (t + 1) * n, n, y_hbm, ybuf.at[1 - slot], sem.at[1 - slot])

    pltpu.make_async_copy(y_hbm.at[pl.ds(0, n * ROW_TILE), :], ybuf.at[slot], sem.at[slot]).wait()
    gates = gate_ref[...]
    f = jnp.zeros(x_ref.shape, F32)
    for k in range(TOP_K):
        f = f + gates[:, k:k + 1] * _from_row_tiles(ybuf.at[slot], k * COMB_TM, COMB_TM)
    o_ref[...] = _layer_norm(ALPHA * x_ref[...] + mod_ref[0][5:6] * f, lng_ref[...], lnb_ref[...])


def _combine(yb, dest, gates, x1, mods, lng, lnb, n_tok, n_lat):
    d = x1.shape[1]
    n_tiles = n_tok // COMB_TM
    n_lat_tiles = n_lat // COMB_TM
    pos = dest.reshape(-1)
    grid_spec = pltpu.PrefetchScalarGridSpec(
        num_scalar_prefetch=1,
        grid=(n_tiles,),
        in_specs=[
            pl.BlockSpec(memory_space=pl.ANY),
            pl.BlockSpec((COMB_TM, LANE), lambda t, pos: (t, 0)),
            pl.BlockSpec((COMB_TM, d), lambda t, pos: (t, 0)),
            pl.BlockSpec((1, N_MOD, d), lambda t, pos: (jnp.where(t >= n_lat_tiles, 1, 0), 0, 0)),
            pl.BlockSpec((1, d), lambda t, pos: (0, 0)),
            pl.BlockSpec((1, d), lambda t, pos: (0, 0)),
        ],
        out_specs=pl.BlockSpec((COMB_TM, d), lambda t, pos: (t, 0)),
        scratch_shapes=[pltpu.VMEM((2, COMB_TM * TOP_K * ROW_TILE, LANE), F32), pltpu.SemaphoreType.DMA((2,))],
    )
    return pl.pallas_call(
        functools.partial(_combine_kernel, n_tiles=n_tiles),
        grid_spec=grid_spec,
        out_shape=jax.ShapeDtypeStruct((n_tok, d), F32),
        compiler_params=_params("arbitrary"),
        name="combine",
    )(pos, yb, gates, x1, mods, lng.reshape(1, d), lnb.reshape(1, d))


def _moe_layer(x1, h2, top_idx, gates, mods, lng, lnb, w_gate_up, b_gate_up, w_down, b_down, layer, n_tok, n_lat):
    block_expert, base_b, zrows, n_blocks = _route_meta(top_idx, n_tok)
    dest = _rank(top_idx, base_b, n_tok)
    xs = _dispatch(h2, dest, zrows, n_tok, n_blocks)
    yb = _experts(xs, block_expert, n_blocks, w_gate_up, b_gate_up, w_down, b_down, layer)
    return _combine(yb, dest, gates, x1, mods, lng, lnb, n_tok, n_lat)


def kernel(x, c, ctx, c_ctx, ada_w, ada_b, ln_mix_g, ln_mix_b, ln_ffn_g, ln_ffn_b, fn_w_out, fn_b_out, fa_w_qkv, fa_b_qkv, fa_q_norm, fa_k_norm, fa_w_out, fa_b_out, gm_w_in, gm_b_in, gm_v_norm_g, gm_v_norm_b, gm_w_s, gm_b_s, gm_w_out, gm_b_out, wa_w_qkv, wa_b_qkv, wa_sink, wa_w_out, wa_b_out, router_w, router_b, exp_w_gate_up, exp_b_gate_up, exp_w_down, exp_b_down):
    bsz, n_lat, d = x.shape
    n_ctx = ctx.shape[1]
    assert bsz == 1 and d == D_MODEL and n_lat == LANE * LANE and n_lat % n_ctx == 0 and n_ctx % TM == 0
    t_all = n_lat + n_ctx
    n_lat_tiles = n_lat // TM
    x_all = jnp.concatenate([x[0], ctx[0]], axis=0)
    mods_all = _ada(c, c_ctx, ada_w, ada_b)
    cos_t, sin_t = _rope_tables(n_lat, n_ctx)

    for i in range(DEPTH):
        kind, j = i % 4, i // 4
        last = i == DEPTH - 1
        n_tok = n_lat if last else t_all
        mods = mods_all[i]
        router = _router_operands(router_w[i], router_b[i])
        lng, lnb = ln_mix_g[i], ln_mix_b[i]
        if kind == 0:
            x1, h2, idx, gates = _fourier_layer(x_all, mods, fn_w_out[j], fn_b_out[j], lng, lnb, router,
                                                n_lat, n_ctx)
        elif kind == 1:
            q, k, v = _qkv(x_all, mods, fa_w_qkv[j], fa_b_qkv[j], cos_t, sin_t, FA_Q_HEADS, FA_KV_HEADS,
                           n_lat_tiles, fa_q_norm[j], fa_k_norm[j])
            o = _full_attention(q, k, v, n_lat, n_ctx)
            x1, h2, idx, gates = _proj_post(o, fa_w_out[j], fa_b_out[j], x_all, mods, lng, lnb, router,
                                            n_tok, n_lat_tiles)
        elif kind == 2:
            x1, h2, idx, gates = _gmlp_layer(x_all, mods, gm_w_in[j], gm_b_in[j], gm_v_norm_g[j], gm_v_norm_b[j],
                                             gm_w_s[j], gm_b_s[j], gm_w_out[j], gm_b_out[j], lng, lnb, router,
                                             n_lat_tiles)
        else:
            q, k, v = _qkv(x_all, mods, wa_w_qkv[j], wa_b_qkv[j], cos_t, sin_t, WA_Q_HEADS, WA_KV_HEADS,
                           n_lat_tiles)
            kt, vx = _kv_layouts(k, v, WA_KV_HEADS)
            o = _window_attention(q, kt, vx, wa_sink[j], n_lat, n_ctx)
            x1, h2, idx, gates = _proj_post(o, wa_w_out[j], wa_b_out[j], x_all, mods, lng, lnb, router,
                                            n_tok, n_lat_tiles)
        x_all = _moe_layer(x1, h2, idx, gates, mods, ln_ffn_g[i], ln_ffn_b[i], exp_w_gate_up,
                           exp_b_gate_up, exp_w_down, exp_b_down, i, n_tok, n_lat)
    return x_all[None]
```

```python
import functools
import math

import numpy as np
import jax
import jax.numpy as jnp
from jax import lax
from jax.experimental import pallas as pl
from jax.experimental.pallas import tpu as pltpu

F32, BF16, I32 = jnp.float32, jnp.bfloat16, jnp.int32

D_MODEL = 1024
DEPTH = 4
GRID_W = 64
N_MOD = 6
FN_GROUPS = 4
HEAD_DIM = 64
FA_Q_HEADS, FA_KV_HEADS = 16, 4
WA_Q_HEADS, WA_KV_HEADS = 16, 2
WINDOW = 128
Q_BLOCK = 128
ROPE_THETA = 10000.0
GM_CHUNK = 128
GM_GROUPS = 8
N_EXPERTS = 32
TOP_K = 4
SWIGLU_LIMIT = 7.0
SWIGLU_ALPHA = 1.702
MOE_BLOCK = 256
LN_EPS = 1e-5
RMS_EPS = 1e-6
NEG = -1e30
ALPHA = (2 * DEPTH) ** 0.25
LOG2E = math.log2(math.e)
Q_SCALE = HEAD_DIM ** -0.5 * LOG2E

LANE = 128
ROW_TILE = D_MODEL // LANE
MXU_WIDTH = 256
TM = 256
FLASH_TQ = 256
FLASH_TK = 1280
FLASH_PAIRS = 3
VMEM_LIMIT = 56 * 2 ** 20


def _params(*sem):
    return pltpu.CompilerParams(dimension_semantics=sem, vmem_limit_bytes=VMEM_LIMIT)


def _dot(a, b):
    return jnp.dot(a, b, preferred_element_type=F32)


def _split(a):
    hi = a.astype(BF16)
    lo = (a - hi.astype(F32)).astype(BF16)
    return hi, lo


def _dot3(a_hi, a_lo, b_hi, b_lo):
    return _dot(a_hi, b_hi) + (_dot(a_hi, b_lo) + _dot(a_lo, b_hi))


def _layer_norm(x, g, b):
    mu = jnp.mean(x, axis=-1, keepdims=True)
    xc = x - mu
    var = jnp.mean(xc * xc, axis=-1, keepdims=True)
    return xc * lax.rsqrt(var + LN_EPS) * g + b


def _top4(logits):
    lane = lax.broadcasted_iota(I32, logits.shape, 1).astype(F32)
    cur = logits
    vals, idxs = [], []
    for _ in range(TOP_K):
        m = jnp.max(cur, axis=-1, keepdims=True)
        i = jnp.min(jnp.where(cur == m, lane, float(LANE)), axis=-1, keepdims=True)
        vals.append(m)
        idxs.append(i)
        cur = jnp.where(lane == i, -jnp.inf, cur)
    exps = [jnp.exp(v - vals[0]) for v in vals]
    inv = 1.0 / (exps[0] + exps[1] + exps[2] + exps[3])
    idx_out = jnp.zeros_like(logits)
    gate_out = jnp.zeros_like(logits)
    for k in range(TOP_K):
        idx_out = jnp.where(lane == float(k), idxs[k], idx_out)
        gate_out = jnp.where(lane == float(k), exps[k] * inv, gate_out)
    return idx_out.astype(I32), gate_out


def _post(y, x, mod, lng, lnb, rw_hi, rw_lo, rb):
    x1 = _layer_norm(ALPHA * x + mod[2:3] * y, lng, lnb)
    h2 = x1 * (1.0 + mod[4:5]) + mod[3:4]
    hh, hl = _split(h2)
    logits = _dot3(hh, hl, rw_hi, rw_lo) + rb
    idx, gates = _top4(logits)
    return x1, h2, idx, gates


def _to_row_tiles(ref, x):
    n = x.shape[0]
    for s in range(ROW_TILE):
        ref[pl.ds(s, n, stride=ROW_TILE), :] = x[:, s * LANE:(s + 1) * LANE]


def _from_row_tiles(ref, start, n):
    return jnp.concatenate([ref[pl.ds(start * ROW_TILE + s, n, stride=ROW_TILE), :] for s in range(ROW_TILE)],
                           axis=1)


def _ada_kernel(cs_ref, w_ref, b_ref, o_ref):
    cs = cs_ref[...]
    s = cs * (1.0 / (1.0 + jnp.exp(-cs)))
    sh, sl = _split(s)
    wh, wl = _split(w_ref[0])
    o_ref[0] = _dot3(sh, sl, wh, wl) + b_ref[0]


def _ada(c, c_ctx, ada_w, ada_b):
    d = c.shape[-1]
    nm = ada_w.shape[-1]
    tn = nm // 4
    cs = jnp.zeros((8, d), F32).at[0].set(c[0]).at[1].set(c_ctx)
    out = pl.pallas_call(
        _ada_kernel,
        grid=(DEPTH, nm // tn),
        in_specs=[
            pl.BlockSpec((8, d), lambda i, j: (0, 0)),
            pl.BlockSpec((1, d, tn), lambda i, j: (i, 0, j)),
            pl.BlockSpec((1, 1, tn), lambda i, j: (i, 0, j)),
        ],
        out_specs=pl.BlockSpec((1, 8, tn), lambda i, j: (i, 0, j)),
        out_shape=jax.ShapeDtypeStruct((DEPTH, 8, nm), F32),
        compiler_params=_params("arbitrary", "arbitrary"),
        name="ada",
    )(cs, ada_w, ada_b.reshape(DEPTH, 1, nm))
    return out[:, :2].reshape(DEPTH, 2, N_MOD, d)


def _tile_specs(n_lat_tiles):
    tok = pl.BlockSpec((TM, D_MODEL), lambda t: (t, 0))
    mod = pl.BlockSpec((1, N_MOD, D_MODEL), lambda t: (jnp.where(t >= n_lat_tiles, 1, 0), 0, 0))
    return tok, mod


def _full(shape):
    nd = len(shape)
    return pl.BlockSpec(shape, lambda *_: (0,) * nd)


def _post_out(n_rows):
    shapes = (
        jax.ShapeDtypeStruct((n_rows, D_MODEL), F32),
        jax.ShapeDtypeStruct((n_rows, D_MODEL), F32),
        jax.ShapeDtypeStruct((n_rows, LANE), I32),
        jax.ShapeDtypeStruct((n_rows, LANE), F32),
    )
    specs = (
        pl.BlockSpec((TM, D_MODEL), lambda t: (t, 0)),
        pl.BlockSpec((TM, D_MODEL), lambda t: (t, 0)),
        pl.BlockSpec((TM, LANE), lambda t: (t, 0)),
        pl.BlockSpec((TM, LANE), lambda t: (t, 0)),
    )
    return shapes, specs


def _router_operands(router_w, router_b):
    rw = jnp.zeros((D_MODEL, LANE), F32).at[:, :N_EXPERTS].set(router_w)
    rw_hi = rw.astype(BF16)
    rw_lo = (rw - rw_hi.astype(F32)).astype(BF16)
    rb = jnp.full((1, LANE), NEG, F32).at[0, :N_EXPERTS].set(router_b)
    return rw_hi, rw_lo, rb


def _proj_post_kernel(a_ref, w_ref, b_ref, x_ref, mod_ref, lng_ref, lnb_ref, rwh_ref, rwl_ref, rb_ref,
                      x1_ref, h2_ref, idx_ref, gate_ref):
    y = _dot(a_ref[...], w_ref[...]) + b_ref[...]
    x1, h2, idx, gates = _post(y, x_ref[...], mod_ref[0], lng_ref[...], lnb_ref[...],
                               rwh_ref[...], rwl_ref[...], rb_ref[...])
    x1_ref[...] = x1
    h2_ref[...] = h2
    idx_ref[...] = idx
    gate_ref[...] = gates


def _proj_post(a, w_out, b_out, x, mods, lng, lnb, router, n_rows, n_lat_tiles):
    k = a.shape[1]
    tok, mod = _tile_specs(n_lat_tiles)
    shapes, specs = _post_out(n_rows)
    return pl.pallas_call(
        _proj_post_kernel,
        grid=(n_rows // TM,),
        in_specs=[
            pl.BlockSpec((TM, k), lambda t: (t, 0)),
            _full((k, D_MODEL)), _full((1, D_MODEL)),
            tok, mod, _full((1, D_MODEL)), _full((1, D_MODEL)),
            _full((D_MODEL, LANE)), _full((D_MODEL, LANE)), _full((1, LANE)),
        ],
        out_specs=specs,
        out_shape=shapes,
        compiler_params=_params("arbitrary"),
        name="proj_post",
    )(a, w_out.astype(BF16), b_out.reshape(1, -1), x, mods, lng.reshape(1, -1), lnb.reshape(1, -1), *router)


def _dft_mats(n):
    jk = np.outer(np.arange(n), np.arange(n)) % n
    ang = 2.0 * np.pi * jk / n
    out = []
    for m in (np.cos(ang), np.sin(ang)):
        m32 = jnp.asarray(m, F32)
        hi = m32.astype(BF16)
        out += [hi, (m32 - hi.astype(F32)).astype(BF16)]
    return out


def _channel_dft(h, cc, sc):
    cw = cc[0].shape[0]
    a_parts, b_parts = [], []
    for g in range(h.shape[1] // cw):
        hh, hl = _split(h[:, g * cw:(g + 1) * cw])
        a_parts.append(_dot3(hh, hl, cc[0][...], cc[1][...]))
        b_parts.append(_dot3(hh, hl, sc[0][...], sc[1][...]))
    return jnp.concatenate(a_parts, axis=1), jnp.concatenate(b_parts, axis=1)


def _fourier1_kernel(x_ref, mod_ref, cch_ref, ccl_ref, sch_ref, scl_ref, tc_ref, ts_ref, ur_ref, ui_ref):
    mod = mod_ref[0]
    h = x_ref[...] * (1.0 + mod[1:2]) + mod[0:1]
    a, b = _channel_dft(h, (cch_ref, ccl_ref), (sch_ref, scl_ref))
    tch, tcl = _split(tc_ref[0])
    tsh, tsl = _split(ts_ref[0])
    ah, al = _split(a)
    bh, bl = _split(b)
    ur_ref[...] = _dot3(tch, tcl, ah, al) - _dot3(tsh, tsl, bh, bl)
    ui_ref[...] = -(_dot3(tch, tcl, bh, bl) + _dot3(tsh, tsl, ah, al))


def _fourier2_kernel(ur_ref, ui_ref, c2h_ref, c2l_ref, s2h_ref, s2l_ref, w_ref, b_ref, x_ref, mod_ref,
                     lng_ref, lnb_ref, rwh_ref, rwl_ref, rb_ref, x1_ref, h2_ref, idx_ref, gate_ref, *, norm):
    urh, url = _split(ur_ref[0])
    uih, uil = _split(ui_ref[0])
    mixed = (_dot3(c2h_ref[...], c2l_ref[...], urh, url) + _dot3(s2h_ref[...], s2l_ref[...], uih, uil)) * norm
    y = _dot(mixed.astype(BF16), w_ref[...]) + b_ref[...]
    x1, h2, idx, gates = _post(y, x_ref[...], mod_ref[0], lng_ref[...], lnb_ref[...],
                               rwh_ref[...], rwl_ref[...], rb_ref[...])
    x1_ref[...] = x1
    h2_ref[...] = h2
    idx_ref[...] = idx
    gate_ref[...] = gates


def _fourier_ctx_kernel(x_ref, mod_ref, cch_ref, ccl_ref, sch_ref, scl_ref, cnh_ref, cnl_ref, snh_ref, snl_ref,
                        w_ref, b_ref, lng_ref, lnb_ref, rwh_ref, rwl_ref, rb_ref,
                        x1_ref, h2_ref, idx_ref, gate_ref, *, norm):
    mod = mod_ref[0]
    x = x_ref[...]
    h = x * (1.0 + mod[1:2]) + mod[0:1]
    a, b = _channel_dft(h, (cch_ref, ccl_ref), (sch_ref, scl_ref))
    ah, al = _split(a)
    bh, bl = _split(b)
    mixed = (_dot3(cnh_ref[...], cnl_ref[...], ah, al) - _dot3(snh_ref[...], snl_ref[...], bh, bl)) * norm
    y = _dot(mixed.astype(BF16), w_ref[...]) + b_ref[...]
    x1, h2, idx, gates = _post(y, x, mod, lng_ref[...], lnb_ref[...], rwh_ref[...], rwl_ref[...], rb_ref[...])
    x1_ref[...] = x1
    h2_ref[...] = h2
    idx_ref[...] = idx
    gate_ref[...] = gates


def _fourier_layer(x_all, mods, w_out, b_out, lng, lnb, router, n_lat, n_ctx):
    t_all, d = x_all.shape
    n2 = LANE
    n1 = n_lat // n2
    cw = d // FN_GROUPS
    rows = t_all // n2
    xv = x_all.reshape(rows, n2 * d)
    cmat = _dft_mats(cw)
    w_bf = w_out.astype(BF16)
    b2 = b_out.reshape(1, d)
    lng2, lnb2 = lng.reshape(1, d), lnb.reshape(1, d)

    k1 = jnp.arange(n1, dtype=I32)
    pos = jnp.arange(n1, dtype=I32)[None, None, :] * n2 + jnp.arange(n2, dtype=I32)[:, None, None]
    ang = ((k1[None, :, None] * pos) % n_lat).astype(F32) * (2.0 * math.pi / n_lat)
    tc, ts = jnp.cos(ang), jnp.sin(ang)

    mat = _full((cw, cw))
    ur, ui = pl.pallas_call(
        _fourier1_kernel,
        grid=(n2,),
        in_specs=[
            pl.BlockSpec((n1, d), lambda j: (0, j)),
            pl.BlockSpec((1, N_MOD, d), lambda j: (0, 0, 0)),
            mat, mat, mat, mat,
            pl.BlockSpec((1, n1, n1), lambda j: (j, 0, 0)),
            pl.BlockSpec((1, n1, n1), lambda j: (j, 0, 0)),
        ],
        out_specs=(pl.BlockSpec((n1, d), lambda j: (0, j)), pl.BlockSpec((n1, d), lambda j: (0, j))),
        out_shape=(jax.ShapeDtypeStruct((n1, n2 * d), F32), jax.ShapeDtypeStruct((n1, n2 * d), F32)),
        compiler_params=_params("arbitrary"),
        name="fourier1",
    )(xv, mods, *cmat, tc, ts)

    m2 = _dft_mats(n2)
    mat2 = _full((n2, n2))
    norm = 1.0 / math.sqrt(n_lat * cw)
    out_shapes = (
        jax.ShapeDtypeStruct((n1, n2 * d), F32),
        jax.ShapeDtypeStruct((n1, n2 * d), F32),
        jax.ShapeDtypeStruct((n1, n2 * LANE), I32),
        jax.ShapeDtypeStruct((n1, n2 * LANE), F32),
    )
    strided = pl.BlockSpec((n2, d), lambda k: (0, k))
    strided_l = pl.BlockSpec((n2, LANE), lambda k: (0, k))
    outs = pl.pallas_call(
        functools.partial(_fourier2_kernel, norm=norm),
        grid=(n1,),
        in_specs=[
            pl.BlockSpec((1, n2, d), lambda k: (k, 0, 0)),
            pl.BlockSpec((1, n2, d), lambda k: (k, 0, 0)),
            mat2, mat2, mat2, mat2,
            _full((d, d)), _full((1, d)),
            strided,
            pl.BlockSpec((1, N_MOD, d), lambda k: (0, 0, 0)),
            _full((1, d)), _full((1, d)),
            _full((d, LANE)), _full((d, LANE)), _full((1, LANE)),
        ],
        out_specs=(strided, strided, strided_l, strided_l),
        out_shape=out_shapes,
        compiler_params=_params("arbitrary"),
        name="fourier2",
    )(ur.reshape(n1, n2, d), ui.reshape(n1, n2, d), *m2, w_bf, b2, xv, mods, lng2, lnb2, *router)
    lat = (outs[0].reshape(n_lat, d), outs[1].reshape(n_lat, d),
           outs[2].reshape(n_lat, LANE), outs[3].reshape(n_lat, LANE))

    cn = _dft_mats(n_ctx)
    matn = _full((n_ctx, n_ctx))
    shapes = (
        jax.ShapeDtypeStruct((n_ctx, d), F32), jax.ShapeDtypeStruct((n_ctx, d), F32),
        jax.ShapeDtypeStruct((n_ctx, LANE), I32), jax.ShapeDtypeStruct((n_ctx, LANE), F32),
    )
    ctx = pl.pallas_call(
        functools.partial(_fourier_ctx_kernel, norm=1.0 / math.sqrt(n_ctx * cw)),
        grid=(1,),
        in_specs=[
            pl.BlockSpec((n_ctx, d), lambda i: (n_lat // n_ctx, 0)),
            pl.BlockSpec((1, N_MOD, d), lambda i: (1, 0, 0)),
            mat, mat, mat, mat, matn, matn, matn, matn,
            _full((d, d)), _full((1, d)), _full((1, d)), _full((1, d)),
            _full((d, LANE)), _full((d, LANE)), _full((1, LANE)),
        ],
        out_specs=(_full((n_ctx, d)), _full((n_ctx, d)), _full((n_ctx, LANE)), _full((n_ctx, LANE))),
        out_shape=shapes,
        compiler_params=_params("arbitrary"),
        name="fourier_ctx",
    )(x_all, mods, *cmat, *cn, w_bf, b2, lng2, lnb2, *router)
    return tuple(jnp.concatenate([a, b], axis=0) for a, b in zip(lat, ctx))


def _qkv_kernel(x_ref, mod_ref, w_ref, b_ref, cos_ref, sin_ref, *rest, n_qk, rms):
    if rms:
        gain_ref, ind_ref, indt_ref, q_ref, k_ref, v_ref = rest
    else:
        q_ref, k_ref, v_ref = rest
    mod = mod_ref[0]
    h = x_ref[...] * (1.0 + mod[1:2]) + mod[0:1]
    y = _dot(h.astype(BF16), w_ref[...]) + b_ref[...]
    qk = y[:, :n_qk]
    if rms:
        sh, sl = _split(qk * qk)
        ms = _dot(sh, ind_ref[...]) + _dot(sl, ind_ref[...])
        mh, ml = _split(ms)
        msb = _dot(mh, indt_ref[...]) + _dot(ml, indt_ref[...])
        qk = qk * lax.rsqrt(msb + RMS_EPS) * gain_ref[...]
    cos = cos_ref[...]
    sin = sin_ref[...]
    even = (lax.broadcasted_iota(I32, cos.shape, 1) & 1) == 0
    parts = []
    for c in range(n_qk // LANE):
        z = qk[:, c * LANE:(c + 1) * LANE]
        swapped = jnp.where(even, pltpu.roll(z, LANE - 1, 1), pltpu.roll(z, 1, 1))
        parts.append(z * cos + swapped * sin)
    nq = q_ref.shape[1]
    q_ref[...] = (jnp.concatenate(parts[:nq // LANE], axis=1) * Q_SCALE).astype(BF16)
    k_ref[...] = jnp.concatenate(parts[nq // LANE:], axis=1).astype(BF16)
    v_ref[...] = y[:, n_qk:].astype(BF16)


def _qkv(x_all, mods, w_qkv, b_qkv, cos_t, sin_t, n_q, n_kv, n_lat_tiles, q_norm=None, k_norm=None):
    t_all, d = x_all.shape
    nq, nk = n_q * HEAD_DIM, n_kv * HEAD_DIM
    n_qk, n_all = nq + nk, nq + 2 * nk
    rms = q_norm is not None
    tok, mod = _tile_specs(n_lat_tiles)
    in_specs = [tok, mod, _full((d, n_all)), _full((1, n_all)),
                pl.BlockSpec((TM, LANE), lambda t: (t, 0)), pl.BlockSpec((TM, LANE), lambda t: (t, 0))]
    args = [x_all, mods, w_qkv.astype(BF16), b_qkv.reshape(1, n_all), cos_t, sin_t]
    if rms:
        gain = jnp.concatenate([jnp.tile(q_norm, n_q), jnp.tile(k_norm, n_kv)]).reshape(1, n_qk)
        head = np.arange(n_qk) // HEAD_DIM
        ind = np.zeros((n_qk, LANE), np.float32)
        ind[np.arange(n_qk), head] = 1.0 / HEAD_DIM
        indt = np.zeros((LANE, n_qk), np.float32)
        indt[head, np.arange(n_qk)] = 1.0
        in_specs += [_full((1, n_qk)), _full((n_qk, LANE)), _full((LANE, n_qk))]
        args += [gain, jnp.asarray(ind, BF16), jnp.asarray(indt, BF16)]
    return pl.pallas_call(
        functools.partial(_qkv_kernel, n_qk=n_qk, rms=rms),
        grid=(t_all // TM,),
        in_specs=in_specs,
        out_specs=(pl.BlockSpec((TM, nq), lambda t: (t, 0)), pl.BlockSpec((TM, nk), lambda t: (t, 0)),
                   pl.BlockSpec((TM, nk), lambda t: (t, 0))),
        out_shape=(jax.ShapeDtypeStruct((t_all, nq), BF16), jax.ShapeDtypeStruct((t_all, nk), BF16),
                   jax.ShapeDtypeStruct((t_all, nk), BF16)),
        compiler_params=_params("arbitrary"),
        name="qkv",
    )(*args)


def _rope_tables(n_lat, n_ctx):
    rows = n_lat // GRID_W
    row = jnp.repeat(jnp.arange(rows, dtype=F32), GRID_W)
    col = jnp.tile(jnp.arange(GRID_W, dtype=F32), rows)
    n_freq = HEAD_DIM // 4
    inv = ROPE_THETA ** (-jnp.arange(n_freq, dtype=F32) / n_freq)
    ang = jnp.concatenate([row[:, None] * inv, col[:, None] * inv], axis=-1)
    ang = jnp.concatenate([ang, jnp.zeros((n_ctx, HEAD_DIM // 2), F32)], axis=0)
    cos = jnp.tile(jnp.repeat(jnp.cos(ang), 2, axis=1), (1, LANE // HEAD_DIM))
    sin = jnp.tile(jnp.repeat(jnp.sin(ang), 2, axis=1), (1, LANE // HEAD_DIM))
    sign = jnp.where(jnp.arange(LANE) % 2 == 0, -1.0, 1.0).astype(F32)
    return cos, sin * sign


def _kv_layouts(k, v, n_kv):
    t_all = k.shape[0]
    kh = k.reshape(t_all, n_kv, HEAD_DIM).transpose(1, 0, 2)
    vt = v.reshape(t_all, n_kv, HEAD_DIM).transpose(1, 2, 0)
    pad = jnp.zeros((n_kv, LANE - HEAD_DIM, t_all), BF16).at[:, 0, :].set(1.0)
    return kh, jnp.concatenate([vt, pad], axis=1)


def _flash_kernel(q_ref, k_ref, vt_ref, o_ref, s_even, s_odd, *, grp, tk, n_lat, n_ctx):
    tq = q_ref.shape[0]
    q_t = q_ref[...].astype(F32).T
    qt = jnp.concatenate([q_t[g * HEAD_DIM:(g + 1) * HEAD_DIM] for g in range(grp)], axis=1).astype(BF16)
    cols = qt.shape[1]
    is_ctx = pl.program_id(1) == n_lat // FLASH_TQ
    n_chunks = (n_lat + n_ctx) // tk

    def finish(acc):
        o = acc[:HEAD_DIM] / acc[HEAD_DIM:HEAD_DIM + 1]
        o_t = jnp.concatenate([o[:, g * tq:(g + 1) * tq] for g in range(grp)], axis=0)
        o_ref[...] = o_t.T.astype(BF16)

    groups = [slice(c * MXU_WIDTH, (c + 1) * MXU_WIDTH) for c in range(cols // MXU_WIDTH)]

    def scores(j, buf, g):
        off = pl.multiple_of(j * tk, tk)
        s = _dot(k_ref[0, pl.ds(off, tk), :], qt[:, g])
        buf[:, g] = s
        return jnp.max(s, axis=0, keepdims=True)

    def accumulate(j, buf, g, m, acc, mc):
        off = pl.multiple_of(j * tk, tk)
        m_new = jnp.maximum(m, mc)
        p = jnp.exp2(buf[:, g] - m_new)
        acc = jnp.exp2(m - m_new) * acc + _dot(vt_ref[0, :, pl.ds(off, tk)], p.astype(BF16))
        return m_new, acc

    @pl.when(jnp.logical_not(is_ctx))
    def _():
        def step(j, cur, nxt, carry):
            out = []
            for g, (m, acc, mc) in zip(groups, carry):
                mc_next = scores(j + 1, nxt, g)
                out.append(accumulate(j, cur, g, m, acc, mc) + (mc_next,))
            return out

        def body(jj, carry):
            for pair in range(FLASH_PAIRS):
                j = 2 * (FLASH_PAIRS * jj + pair)
                carry = step(j, s_even, s_odd, carry)
                carry = step(j + 1, s_odd, s_even, carry)
            return carry

        init = [(jnp.full((1, MXU_WIDTH), NEG, F32), jnp.zeros((LANE, MXU_WIDTH), F32), scores(0, s_even, g))
                for g in groups]
        carry = lax.fori_loop(0, (n_chunks - 1) // (2 * FLASH_PAIRS), body, init)
        finish(jnp.concatenate([accumulate(n_chunks - 1, s_even, g, m, acc, mc)[1]
                                for g, (m, acc, mc) in zip(groups, carry)], axis=1))

    @pl.when(is_ctx)
    def _():
        s = _dot(k_ref[0, n_lat:n_lat + n_ctx, :], qt)
        p = jnp.exp2(s - jnp.max(s, axis=0, keepdims=True))
        finish(_dot(vt_ref[0, :, n_lat:n_lat + n_ctx], p.astype(BF16)))


def _full_attention(q, k, v, n_lat, n_ctx):
    t_all, nq = q.shape
    n_kv = k.shape[1] // HEAD_DIM
    grp = nq // HEAD_DIM // n_kv
    nt = t_all // FLASH_TQ
    cols = grp * FLASH_TQ
    assert n_ctx == FLASH_TQ and t_all % FLASH_TK == 0 and (t_all // FLASH_TK - 1) % (2 * FLASH_PAIRS) == 0
    kh, vt = _kv_layouts(k, v, n_kv)
    return pl.pallas_call(
        functools.partial(_flash_kernel, grp=grp, tk=FLASH_TK, n_lat=n_lat, n_ctx=n_ctx),
        grid=(n_kv, nt),
        in_specs=[
            pl.BlockSpec((FLASH_TQ, grp * HEAD_DIM), lambda h, i: (i, h)),
            pl.BlockSpec((1, t_all, HEAD_DIM), lambda h, i: (h, 0, 0)),
            pl.BlockSpec((1, LANE, t_all), lambda h, i: (h, 0, 0)),
        ],
        out_specs=pl.BlockSpec((FLASH_TQ, grp * HEAD_DIM), lambda h, i: (i, h)),
        out_shape=jax.ShapeDtypeStruct((t_all, nq), BF16),
        scratch_shapes=[pltpu.VMEM((FLASH_TK, cols), F32), pltpu.VMEM((FLASH_TK, cols), F32)],
        compiler_params=_params("arbitrary", "arbitrary"),
        name="flash",
    )(q, kh, vt)


def _window_kernel(q_ref, kp_ref, kc_ref, kn_ref, kx_ref, vp_ref, vc_ref, vn_ref, vx_ref, sink_ref, o_ref,
                   *, grp, nb):
    i = pl.program_id(1)
    q_t = q_ref[...].astype(F32).T
    qt = jnp.concatenate([q_t[g * HEAD_DIM:(g + 1) * HEAD_DIM] for g in range(grp)], axis=1).astype(BF16)
    cols = qt.shape[1]
    s = _dot(jnp.concatenate([kp_ref[0], kc_ref[0], kn_ref[0], kx_ref[0]], axis=0), qt)
    kj = lax.broadcasted_iota(I32, (Q_BLOCK, cols), 0)
    qi = lax.broadcasted_iota(I32, (Q_BLOCK, cols), 1) & (Q_BLOCK - 1)
    sp = jnp.where(kj >= jnp.where(i > 0, qi, Q_BLOCK), s[:Q_BLOCK], NEG)
    sn = jnp.where(kj <= jnp.where(i < nb - 1, qi, -1), s[2 * Q_BLOCK:3 * Q_BLOCK], NEG)
    s = jnp.concatenate([sp, s[Q_BLOCK:2 * Q_BLOCK], sn, s[3 * Q_BLOCK:]], axis=0)
    sink = sink_ref[0]
    m = jnp.maximum(jnp.max(s, axis=0, keepdims=True), sink)
    p = jnp.exp2(s - m).astype(BF16)
    acc = _dot(jnp.concatenate([vp_ref[0], vc_ref[0], vn_ref[0], vx_ref[0]], axis=1), p)
    o = acc[:HEAD_DIM] / (acc[HEAD_DIM:HEAD_DIM + 1] + jnp.exp2(sink - m))
    o_t = jnp.concatenate([o[:, g * Q_BLOCK:(g + 1) * Q_BLOCK] for g in range(grp)], axis=0)
    o_ref[...] = o_t.T.astype(BF16)


def _window_attention(q, k, v, sink, n_lat, n_ctx):
    nq = q.shape[1]
    n_kv = k.shape[1] // HEAD_DIM
    grp = nq // HEAD_DIM // n_kv
    gw = grp * HEAD_DIM
    nb = n_lat // Q_BLOCK
    cb = n_lat // n_ctx
    kh, vt = _kv_layouts(k, v, n_kv)
    sink_cols = jnp.repeat(sink.reshape(n_kv, 1, grp) * LOG2E, Q_BLOCK, axis=2)
    kspec = lambda f: pl.BlockSpec((1, Q_BLOCK, HEAD_DIM), lambda h, i: (h, f(i), 0))
    vspec = lambda f: pl.BlockSpec((1, LANE, Q_BLOCK), lambda h, i: (h, 0, f(i)))
    prev = lambda i: jnp.maximum(i - 1, 0)
    cur = lambda i: i
    nxt = lambda i: jnp.minimum(i + 1, nb - 1)
    return pl.pallas_call(
        functools.partial(_window_kernel, grp=grp, nb=nb),
        grid=(n_kv, nb),
        in_specs=[
            pl.BlockSpec((Q_BLOCK, gw), lambda h, i: (i, h)),
            kspec(prev), kspec(cur), kspec(nxt),
            pl.BlockSpec((1, n_ctx, HEAD_DIM), lambda h, i: (h, cb, 0)),
            vspec(prev), vspec(cur), vspec(nxt),
            pl.BlockSpec((1, LANE, n_ctx), lambda h, i: (h, 0, cb)),
            pl.BlockSpec((1, 1, grp * Q_BLOCK), lambda h, i: (h, 0, 0)),
        ],
        out_specs=pl.BlockSpec((Q_BLOCK, gw), lambda h, i: (i, h)),
        out_shape=jax.ShapeDtypeStruct((n_lat, nq), BF16),
        compiler_params=_params("arbitrary", "arbitrary"),
        name="window",
    )(q, kh, kh, kh, kh, vt, vt, vt, vt, sink_cols)


def _gmlp_kernel(x_ref, mod_ref, win_ref, bin_ref, vg_ref, vb_ref, ws_ref, bs_ref, wout_ref, bout_ref,
                 lng_ref, lnb_ref, rwh_ref, rwl_ref, rb_ref, x1_ref, h2_ref, idx_ref, gate_ref):
    mod = mod_ref[0]
    x = x_ref[...]
    h = x * (1.0 + mod[1:2]) + mod[0:1]
    z = _dot(h.astype(BF16), win_ref[...]) + bin_ref[...]
    z = 0.5 * z * (1.0 + lax.erf(z * (2.0 ** -0.5)))
    half = z.shape[1] // 2
    u = z[:, :half]
    v = _layer_norm(z[:, half:], vg_ref[...], vb_ref[...]).astype(BF16)
    cw = half // GM_GROUPS
    chunks = []
    for c in range(x.shape[0] // GM_CHUNK):
        vc = v[c * GM_CHUNK:(c + 1) * GM_CHUNK]
        chunks.append(jnp.concatenate(
            [_dot(ws_ref[g], vc[:, g * cw:(g + 1) * cw]) + bs_ref[g] for g in range(GM_GROUPS)], axis=1))
    gated = u * jnp.concatenate(chunks, axis=0)
    y = _dot(gated.astype(BF16), wout_ref[...]) + bout_ref[...]
    x1, h2, idx, gates = _post(y, x, mod, lng_ref[...], lnb_ref[...], rwh_ref[...], rwl_ref[...], rb_ref[...])
    x1_ref[...] = x1
    h2_ref[...] = h2
    idx_ref[...] = idx
    gate_ref[...] = gates


def _gmlp_layer(x_all, mods, w_in, b_in, vg, vb, w_s, b_s, w_out, b_out, lng, lnb, router, n_lat_tiles):
    t_all, d = x_all.shape
    dffn = w_in.shape[1]
    half = dffn // 2
    cw = half // GM_GROUPS
    tok, mod = _tile_specs(n_lat_tiles)
    shapes, specs = _post_out(t_all)
    bs_full = jnp.broadcast_to(b_s[:, :, None], (GM_GROUPS, GM_CHUNK, cw))
    return pl.pallas_call(
        _gmlp_kernel,
        grid=(t_all // TM,),
        in_specs=[
            tok, mod, _full((d, dffn)), _full((1, dffn)), _full((1, half)), _full((1, half)),
            _full((GM_GROUPS, GM_CHUNK, GM_CHUNK)), _full((GM_GROUPS, GM_CHUNK, cw)),
            _full((half, d)), _full((1, d)), _full((1, d)), _full((1, d)),
            _full((d, LANE)), _full((d, LANE)), _full((1, LANE)),
        ],
        out_specs=specs,
        out_shape=shapes,
        compiler_params=_params("arbitrary"),
        name="gmlp",
    )(x_all, mods, w_in.astype(BF16), b_in.reshape(1, dffn), vg.reshape(1, half), vb.reshape(1, half),
      w_s.astype(BF16), bs_full, w_out.astype(BF16), b_out.reshape(1, d), lng.reshape(1, d), lnb.reshape(1, d),
      *router)


COMB_TM = 128


def _route_meta(top_idx, n_tok):
    n_assign = n_tok * TOP_K
    n_blocks = -(-n_assign // MOE_BLOCK) + N_EXPERTS
    ids = jnp.arange(N_EXPERTS, dtype=I32)
    counts = jnp.sum((top_idx[:n_tok, :TOP_K, None] == ids).astype(I32), axis=(0, 1))
    padded = (counts + MOE_BLOCK - 1) // MOE_BLOCK * MOE_BLOCK
    ends_pad = jnp.cumsum(padded)
    base = ends_pad - padded
    starts = jnp.arange(n_blocks, dtype=I32) * MOE_BLOCK
    block_expert = jnp.minimum(jnp.sum((ends_pad[None, :] <= starts[:, None]).astype(I32), axis=1), N_EXPERTS - 1)
    last_blk = jnp.where(padded > 0, ends_pad - MOE_BLOCK, -1)
    tail = ends_pad[-1] + ids * MOE_BLOCK
    tail = jnp.where(tail < n_blocks * MOE_BLOCK, tail, -1)
    base_b = jnp.broadcast_to(base.astype(F32)[:, None], (N_EXPERTS, LANE))
    return block_expert.astype(I32), base_b, jnp.concatenate([last_blk, tail]).astype(I32), n_blocks


def _rank_kernel(idx_ref, base_ref, upper_ref, dest_ref, run_ref):
    @pl.when(pl.program_id(0) == 0)
    def _():
        run_ref[...] = jnp.zeros(run_ref.shape, F32)

    eid = lax.broadcasted_iota(I32, (N_EXPERTS, TM), 0)
    onehots = [(eid == idx_ref[k:k + 1, :]).astype(F32) for k in range(TOP_K)]
    cnt = onehots[0] + onehots[1] + onehots[2] + onehots[3]
    before = _dot(cnt.astype(BF16), upper_ref[...])
    slot = base_ref[:, :1] + run_ref[:, :1] + before
    for k in range(TOP_K):
        rows = (jnp.sum(onehots[k] * slot, axis=0, keepdims=True) * ROW_TILE).astype(I32)
        for j in range(TM // COMB_TM):
            dest_ref[j, k:k + 1, :] = rows[:, j * COMB_TM:(j + 1) * COMB_TM]
    run_ref[...] = run_ref[...] + jnp.sum(cnt, axis=1, keepdims=True)


def _rank(top_idx, base_b, n_tok):
    per_step = TM // COMB_TM
    idx_t = top_idx[:n_tok, :TOP_K].T
    upper = jnp.asarray(np.triu(np.ones((TM, TM), np.float32), 1), BF16)
    return pl.pallas_call(
        _rank_kernel,
        grid=(n_tok // TM,),
        in_specs=[pl.BlockSpec((TOP_K, TM), lambda t: (0, t)), _full((N_EXPERTS, LANE)), _full((TM, TM))],
        out_specs=pl.BlockSpec((per_step, TOP_K, COMB_TM), lambda t: (t, 0, 0)),
        out_shape=jax.ShapeDtypeStruct((n_tok // COMB_TM, TOP_K, COMB_TM), I32),
        scratch_shapes=[pltpu.VMEM((N_EXPERTS, LANE), F32)],
        compiler_params=_params("arbitrary"),
        name="rank",
    )(idx_t, base_b, upper)


def _dispatch_kernel(dest_ref, zrow_ref, h_ref, xs_hbm, hbuf, zbuf, sem, zsem, *, n_tiles):
    t = pl.program_id(0)
    slot = t % 2
    n_zero = 2 * N_EXPERTS

    def zero_copy(j):
        row = pl.multiple_of(zrow_ref[j] * ROW_TILE, MOE_BLOCK * ROW_TILE)
        return pltpu.make_async_copy(zbuf, xs_hbm.at[pl.ds(row, MOE_BLOCK * ROW_TILE), :], zsem.at[0])

    def wait_rows(s):
        for _ in range(TOP_K):
            pltpu.make_async_copy(hbuf.at[s], xs_hbm.at[pl.ds(0, COMB_TM * ROW_TILE), :], sem.at[s]).wait()

    @pl.when(t == 0)
    def _():
        zbuf[...] = jnp.zeros(zbuf.shape, F32)
        for j in range(n_zero):
            @pl.when(zrow_ref[j] >= 0)
            def _():
                zero_copy(j).start()
        for j in range(n_zero):
            @pl.when(zrow_ref[j] >= 0)
            def _():
                zero_copy(j).wait()

    @pl.when(t >= 2)
    def _():
        wait_rows(slot)

    _to_row_tiles(hbuf.at[slot], h_ref[...])
    for k in range(TOP_K):
        for r in range(COMB_TM):
            row = pl.multiple_of(dest_ref[(t * TOP_K + k) * COMB_TM + r], ROW_TILE)
            pltpu.make_async_copy(hbuf.at[slot, pl.ds(r * ROW_TILE, ROW_TILE), :],
                                  xs_hbm.at[pl.ds(row, ROW_TILE), :], sem.at[slot]).start(priority=r % 2)

    @pl.when(t == n_tiles - 1)
    def _():
        wait_rows(slot)
        if n_tiles > 1:
            wait_rows(1 - slot)


def _dispatch(h2, dest, zrows, n_tok, n_blocks):
    d = h2.shape[1]
    n_tiles = n_tok // COMB_TM
    grid_spec = pltpu.PrefetchScalarGridSpec(
        num_scalar_prefetch=2,
        grid=(n_tiles,),
        in_specs=[pl.BlockSpec((COMB_TM, d), lambda t, dest, zr: (t, 0))],
        out_specs=pl.BlockSpec(memory_space=pl.ANY),
        scratch_shapes=[pltpu.VMEM((2, COMB_TM * ROW_TILE, LANE), F32),
                        pltpu.VMEM((MOE_BLOCK * ROW_TILE, LANE), F32),
                        pltpu.SemaphoreType.DMA((2,)), pltpu.SemaphoreType.DMA((1,))],
    )
    return pl.pallas_call(
        functools.partial(_dispatch_kernel, n_tiles=n_tiles),
        grid_spec=grid_spec,
        out_shape=jax.ShapeDtypeStruct((n_blocks * MOE_BLOCK * ROW_TILE, LANE), F32),
        compiler_params=_params("arbitrary"),
        name="dispatch",
    )(dest.reshape(-1), zrows, h2)


EXP_CHUNK = 512


def _expert_kernel(be_ref, x_ref, wgu_ref, bgu_ref, wd_ref, bd_ref, sel_e_ref, sel_o_ref, y_ref,
                   wg_s, wl_s, wd_s, bg_s, bl_s):
    b = pl.program_id(0)
    ff = wg_s.shape[1]
    half = EXP_CHUNK // 2

    @pl.when(jnp.logical_or(b == 0, be_ref[b] != be_ref[jnp.maximum(b - 1, 0)]))
    def _():
        for c in range(2 * ff // EXP_CHUNK):
            w = wgu_ref[0, 0, :, c * EXP_CHUNK:(c + 1) * EXP_CHUNK].astype(BF16)
            wg_s[:, c * half:(c + 1) * half] = _dot(w, sel_e_ref[...]).astype(BF16)
            wl_s[:, c * half:(c + 1) * half] = _dot(w, sel_o_ref[...]).astype(BF16)
            bh, bl = _split(jnp.broadcast_to(bgu_ref[0, 0, :, c * EXP_CHUNK:(c + 1) * EXP_CHUNK], (8, EXP_CHUNK)))
            bg_s[:, c * half:(c + 1) * half] = _dot(bh, sel_e_ref[...]) + _dot(bl, sel_e_ref[...])
            bl_s[:, c * half:(c + 1) * half] = _dot(bh, sel_o_ref[...]) + _dot(bl, sel_o_ref[...])
        wd_s[...] = wd_ref[0, 0].astype(BF16)

    x = _from_row_tiles(x_ref, 0, MOE_BLOCK).astype(BF16)
    glu = jnp.minimum(_dot(x, wg_s[...]) + bg_s[0:1, :], SWIGLU_LIMIT)
    lin = jnp.clip(_dot(x, wl_s[...]) + bl_s[0:1, :], -SWIGLU_LIMIT, SWIGLU_LIMIT)
    act = glu * (1.0 / (1.0 + jnp.exp(-SWIGLU_ALPHA * glu))) * (lin + 1.0)
    _to_row_tiles(y_ref, _dot(act.astype(BF16), wd_s[...]) + bd_ref[0, 0])


def _experts(xs, block_expert, n_blocks, w_gate_up, b_gate_up, w_down, b_down, layer):
    d = w_down.shape[3]
    ff = w_down.shape[2]
    sel = np.zeros((2, EXP_CHUNK, EXP_CHUNK // 2), np.float32)
    sel[0, 2 * np.arange(EXP_CHUNK // 2), np.arange(EXP_CHUNK // 2)] = 1.0
    sel[1, 2 * np.arange(EXP_CHUNK // 2) + 1, np.arange(EXP_CHUNK // 2)] = 1.0
    wspec = lambda r, c: pl.BlockSpec((1, 1, r, c), lambda b, be: (layer, be[b], 0, 0))
    sspec = pl.BlockSpec((EXP_CHUNK, EXP_CHUNK // 2), lambda b, be: (0, 0))
    grid_spec = pltpu.PrefetchScalarGridSpec(
        num_scalar_prefetch=1,
        grid=(n_blocks,),
        in_specs=[pl.BlockSpec((MOE_BLOCK * ROW_TILE, LANE), lambda b, be: (b, 0)), wspec(d, 2 * ff),
                  wspec(1, 2 * ff), wspec(ff, d), wspec(1, d), sspec, sspec],
        out_specs=pl.BlockSpec((MOE_BLOCK * ROW_TILE, LANE), lambda b, be: (b, 0)),
        scratch_shapes=[pltpu.VMEM((d, ff), BF16), pltpu.VMEM((d, ff), BF16), pltpu.VMEM((ff, d), BF16),
                        pltpu.VMEM((8, ff), F32), pltpu.VMEM((8, ff), F32)],
    )
    return pl.pallas_call(
        _expert_kernel,
        grid_spec=grid_spec,
        out_shape=jax.ShapeDtypeStruct((n_blocks * MOE_BLOCK * ROW_TILE, LANE), F32),
        compiler_params=_params("arbitrary"),
        name="experts",
    )(block_expert, xs, w_gate_up, b_gate_up[:, :, None, :], w_down, b_down[:, :, None, :],
      jnp.asarray(sel[0], BF16), jnp.asarray(sel[1], BF16))


def _gather_rows(idx_ref, base, n, src_hbm, dst, sem):
    for r in range(n):
        row = pl.multiple_of(idx_ref[base + r], ROW_TILE)
        pltpu.make_async_copy(src_hbm.at[pl.ds(row, ROW_TILE), :], dst.at[pl.ds(r * ROW_TILE, ROW_TILE), :],
                              sem).start(priority=r % 2)


def _combine_kernel(pos_ref, y_hbm, gate_ref, x_ref, mod_ref, lng_ref, lnb_ref, o_ref, ybuf, sem, *, n_tiles):
    t = pl.program_id(0)
    slot = t % 2
    n = COMB_TM * TOP_K

    @pl.when(t == 0)
    def _():
        _gather_rows(pos_ref, 0, n, y_hbm, ybuf.at[0], sem.at[0])

    @pl.when(t + 1 < n_tiles)
    def _():
        _gather_rows(pos_ref, (t + 1) * n, n, y_hbm, ybuf.at[1 - slot], sem.at[1 - slot])

    pltpu.make_async_copy(y_hbm.at[pl.ds(0, n * ROW_TILE), :], ybuf.at[slot], sem.at[slot]).wait()
    gates = gate_ref[...]
    f = jnp.zeros(x_ref.shape, F32)
    for k in range(TOP_K):
        f = f + gates[:, k:k + 1] * _from_row_tiles(ybuf.at[slot], k * COMB_TM, COMB_TM)
    o_ref[...] = _layer_norm(ALPHA * x_ref[...] + mod_ref[0][5:6] * f, lng_ref[...], lnb_ref[...])


def _combine(yb, dest, gates, x1, mods, lng, lnb, n_tok, n_lat):
    d = x1.shape[1]
    n_tiles = n_tok // COMB_TM
    n_lat_tiles = n_lat // COMB_TM
    pos = dest.reshape(-1)
    grid_spec = pltpu.PrefetchScalarGridSpec(
        num_scalar_prefetch=1,
        grid=(n_tiles,),
        in_specs=[
            pl.BlockSpec(memory_space=pl.ANY),
            pl.BlockSpec((COMB_TM, LANE), lambda t, pos: (t, 0)),
            pl.BlockSpec((COMB_TM, d), lambda t, pos: (t, 0)),
            pl.BlockSpec((1, N_MOD, d), lambda t, pos: (jnp.where(t >= n_lat_tiles, 1, 0), 0, 0)),
            pl.BlockSpec((1, d), lambda t, pos: (0, 0)),
            pl.BlockSpec((1, d), lambda t, pos: (0, 0)),
        ],
        out_specs=pl.BlockSpec((COMB_TM, d), lambda t, pos: (t, 0)),
        scratch_shapes=[pltpu.VMEM((2, COMB_TM * TOP_K * ROW_TILE, LANE), F32), pltpu.SemaphoreType.DMA((2,))],
    )
    return pl.pallas_call(
        functools.partial(_combine_kernel, n_tiles=n_tiles),
        grid_spec=grid_spec,
        out_shape=jax.ShapeDtypeStruct((n_tok, d), F32),
        compiler_params=_params("arbitrary"),
        name="combine",
    )(pos, yb, gates, x1, mods, lng.reshape(1, d), lnb.reshape(1, d))


def _moe_layer(x1, h2, top_idx, gates, mods, lng, lnb, w_gate_up, b_gate_up, w_down, b_down, layer, n_tok, n_lat):
    block_expert, base_b, zrows, n_blocks = _route_meta(top_idx, n_tok)
    dest = _rank(top_idx, base_b, n_tok)
    xs = _dispatch(h2, dest, zrows, n_tok, n_blocks)
    yb = _experts(xs, block_expert, n_blocks, w_gate_up, b_gate_up, w_down, b_down, layer)
    return _combine(yb, dest, gates, x1, mods, lng, lnb, n_tok, n_lat)


def kernel(x, c, ctx, c_ctx, ada_w, ada_b, ln_mix_g, ln_mix_b, ln_ffn_g, ln_ffn_b, fn_w_out, fn_b_out, fa_w_qkv, fa_b_qkv, fa_q_norm, fa_k_norm, fa_w_out, fa_b_out, gm_w_in, gm_b_in, gm_v_norm_g, gm_v_norm_b, gm_w_s, gm_b_s, gm_w_out, gm_b_out, wa_w_qkv, wa_b_qkv, wa_sink, wa_w_out, wa_b_out, router_w, router_b, exp_w_gate_up, exp_b_gate_up, exp_w_down, exp_b_down):
    bsz, n_lat, d = x.shape
    n_ctx = ctx.shape[1]
    assert bsz == 1 and d == D_MODEL and n_lat == LANE * LANE and n_lat % n_ctx == 0 and n_ctx % TM == 0
    t_all = n_lat + n_ctx
    n_lat_tiles = n_lat // TM
    x_all = jnp.concatenate([x[0], ctx[0]], axis=0)
    mods_all = _ada(c, c_ctx, ada_w, ada_b)
    cos_t, sin_t = _rope_tables(n_lat, n_ctx)

    for i in range(DEPTH):
        kind, j = i % 4, i // 4
        last = i == DEPTH - 1
        n_tok = n_lat if last else t_all
        mods = mods_all[i]
        router = _router_operands(router_w[i], router_b[i])
        lng, lnb = ln_mix_g[i], ln_mix_b[i]
        if kind == 0:
            x1, h2, idx, gates = _fourier_layer(x_all, mods, fn_w_out[j], fn_b_out[j], lng, lnb, router,
                                                n_lat, n_ctx)
        elif kind == 1:
            q, k, v = _qkv(x_all, mods, fa_w_qkv[j], fa_b_qkv[j], cos_t, sin_t, FA_Q_HEADS, FA_KV_HEADS,
                           n_lat_tiles, fa_q_norm[j], fa_k_norm[j])
            o = _full_attention(q, k, v, n_lat, n_ctx)
            x1, h2, idx, gates = _proj_post(o, fa_w_out[j], fa_b_out[j], x_all, mods, lng, lnb, router,
                                            n_tok, n_lat_tiles)
        elif kind == 2:
            x1, h2, idx, gates = _gmlp_layer(x_all, mods, gm_w_in[j], gm_b_in[j], gm_v_norm_g[j], gm_v_norm_b[j],
                                             gm_w_s[j], gm_b_s[j], gm_w_out[j], gm_b_out[j], lng, lnb, router,
                                             n_lat_tiles)
        else:
            q, k, v = _qkv(x_all, mods, wa_w_qkv[j], wa_b_qkv[j], cos_t, sin_t, WA_Q_HEADS, WA_KV_HEADS,
                           n_lat_tiles)
            o = _window_attention(q, k, v, wa_sink[j], n_lat, n_ctx)
            x1, h2, idx, gates = _proj_post(o, wa_w_out[j], wa_b_out[j], x_all, mods, lng, lnb, router,
                                            n_tok, n_lat_tiles)
        x_all = _moe_layer(x1, h2, idx, gates, mods, ln_ffn_g[i], ln_ffn_b[i], exp_w_gate_up,
                           exp_b_gate_up, exp_w_down, exp_b_down, i, n_tok, n_lat)
    return x_all[None]
```

```python
import functools
import math

import numpy as np
import jax
import jax.numpy as jnp
from jax import lax
from jax.experimental import pallas as pl
from jax.experimental.pallas import tpu as pltpu

F32, BF16, I32 = jnp.float32, jnp.bfloat16, jnp.int32

D_MODEL = 1024
DEPTH = 4
GRID_W = 64
N_MOD = 6
FN_GROUPS = 4
HEAD_DIM = 64
FA_Q_HEADS, FA_KV_HEADS = 16, 4
WA_Q_HEADS, WA_KV_HEADS = 16, 2
WINDOW = 128
Q_BLOCK = 128
ROPE_THETA = 10000.0
GM_CHUNK = 128
GM_GROUPS = 8
N_EXPERTS = 32
TOP_K = 4
SWIGLU_LIMIT = 7.0
SWIGLU_ALPHA = 1.702
MOE_BLOCK = 256
LN_EPS = 1e-5
RMS_EPS = 1e-6
NEG = -1e30
ALPHA = (2 * DEPTH) ** 0.25
LOG2E = math.log2(math.e)
Q_SCALE = HEAD_DIM ** -0.5 * LOG2E

LANE = 128
ROW_TILE = D_MODEL // LANE
MXU_WIDTH = 256
TM = 256
FLASH_TQ = 256
FLASH_TK = 1280
FLASH_PAIRS = 3
VMEM_LIMIT = 56 * 2 ** 20


def _params(*sem):
    return pltpu.CompilerParams(dimension_semantics=sem, vmem_limit_bytes=VMEM_LIMIT)


def _dot(a, b):
    return jnp.dot(a, b, preferred_element_type=F32)


def _split(a):
    hi = a.astype(BF16)
    lo = (a - hi.astype(F32)).astype(BF16)
    return hi, lo


def _dot3(a_hi, a_lo, b_hi, b_lo):
    return _dot(a_hi, b_hi) + (_dot(a_hi, b_lo) + _dot(a_lo, b_hi))


def _layer_norm(x, g, b):
    mu = jnp.mean(x, axis=-1, keepdims=True)
    xc = x - mu
    var = jnp.mean(xc * xc, axis=-1, keepdims=True)
    return xc * lax.rsqrt(var + LN_EPS) * g + b


def _top4(logits):
    lane = lax.broadcasted_iota(I32, logits.shape, 1).astype(F32)
    cur = logits
    vals, idxs = [], []
    for _ in range(TOP_K):
        m = jnp.max(cur, axis=-1, keepdims=True)
        i = jnp.min(jnp.where(cur == m, lane, float(LANE)), axis=-1, keepdims=True)
        vals.append(m)
        idxs.append(i)
        cur = jnp.where(lane == i, -jnp.inf, cur)
    exps = [jnp.exp(v - vals[0]) for v in vals]
    inv = 1.0 / (exps[0] + exps[1] + exps[2] + exps[3])
    idx_out = jnp.zeros_like(logits)
    gate_out = jnp.zeros_like(logits)
    picked = jnp.zeros_like(logits)
    for k in range(TOP_K):
        idx_out = jnp.where(lane == float(k), idxs[k], idx_out)
        gate_out = jnp.where(lane == float(k), exps[k] * inv, gate_out)
        picked = picked + jnp.where(lane == idxs[k], 1.0, 0.0)
    return idx_out.astype(I32), gate_out, jnp.sum(picked, axis=0, keepdims=True)


def _post(y, x, mod, lng, lnb, rw_hi, rw_lo, rb, out_refs):
    x1_ref, h2_ref, idx_ref, gate_ref, cnt_ref = out_refs
    x1 = _layer_norm(ALPHA * x + mod[2:3] * y, lng, lnb)
    h2 = x1 * (1.0 + mod[4:5]) + mod[3:4]
    hh, hl = _split(h2)
    logits = _dot3(hh, hl, rw_hi, rw_lo) + rb
    idx, gates, cnt = _top4(logits)
    x1_ref[...] = x1
    h2_ref[...] = h2
    idx_ref[...] = idx
    gate_ref[...] = gates

    @pl.when(pl.program_id(0) == 0)
    def _():
        cnt_ref[...] = jnp.zeros(cnt_ref.shape, F32)

    cnt_ref[...] = cnt_ref[...] + cnt


def _to_row_tiles(ref, x):
    n = x.shape[0]
    for s in range(ROW_TILE):
        ref[pl.ds(s, n, stride=ROW_TILE), :] = x[:, s * LANE:(s + 1) * LANE]


def _from_row_tiles(ref, start, n):
    return jnp.concatenate([ref[pl.ds(start * ROW_TILE + s, n, stride=ROW_TILE), :] for s in range(ROW_TILE)],
                           axis=1)


def _ada_kernel(cs_ref, w_ref, b_ref, o_ref):
    cs = cs_ref[...]
    s = cs * (1.0 / (1.0 + jnp.exp(-cs)))
    sh, sl = _split(s)
    wh, wl = _split(w_ref[0])
    o_ref[0] = _dot3(sh, sl, wh, wl) + b_ref[0]


def _ada(c, c_ctx, ada_w, ada_b):
    d = c.shape[-1]
    nm = ada_w.shape[-1]
    tn = nm // 4
    cs = jnp.zeros((8, d), F32).at[0].set(c[0]).at[1].set(c_ctx)
    out = pl.pallas_call(
        _ada_kernel,
        grid=(DEPTH, nm // tn),
        in_specs=[
            pl.BlockSpec((8, d), lambda i, j: (0, 0)),
            pl.BlockSpec((1, d, tn), lambda i, j: (i, 0, j)),
            pl.BlockSpec((1, 1, tn), lambda i, j: (i, 0, j)),
        ],
        out_specs=pl.BlockSpec((1, 8, tn), lambda i, j: (i, 0, j)),
        out_shape=jax.ShapeDtypeStruct((DEPTH, 8, nm), F32),
        compiler_params=_params("arbitrary", "arbitrary"),
        name="ada",
    )(cs, ada_w, ada_b.reshape(DEPTH, 1, nm))
    return out[:, :2].reshape(DEPTH, 2, N_MOD, d)


def _tile_specs(n_lat_tiles):
    tok = pl.BlockSpec((TM, D_MODEL), lambda t: (t, 0))
    mod = pl.BlockSpec((1, N_MOD, D_MODEL), lambda t: (jnp.where(t >= n_lat_tiles, 1, 0), 0, 0))
    return tok, mod


def _full(shape):
    nd = len(shape)
    return pl.BlockSpec(shape, lambda *_: (0,) * nd)


_COUNT_SHAPE = jax.ShapeDtypeStruct((8, LANE), F32)
_COUNT_SPEC = pl.BlockSpec((8, LANE), lambda *_: (0, 0))


def _post_out(n_rows):
    shapes = (
        jax.ShapeDtypeStruct((n_rows, D_MODEL), F32),
        jax.ShapeDtypeStruct((n_rows, D_MODEL), F32),
        jax.ShapeDtypeStruct((n_rows, LANE), I32),
        jax.ShapeDtypeStruct((n_rows, LANE), F32),
        _COUNT_SHAPE,
    )
    specs = (
        pl.BlockSpec((TM, D_MODEL), lambda t: (t, 0)),
        pl.BlockSpec((TM, D_MODEL), lambda t: (t, 0)),
        pl.BlockSpec((TM, LANE), lambda t: (t, 0)),
        pl.BlockSpec((TM, LANE), lambda t: (t, 0)),
        _COUNT_SPEC,
    )
    return shapes, specs


def _router_operands(router_w, router_b):
    rw = jnp.zeros((D_MODEL, LANE), F32).at[:, :N_EXPERTS].set(router_w)
    rw_hi = rw.astype(BF16)
    rw_lo = (rw - rw_hi.astype(F32)).astype(BF16)
    rb = jnp.full((1, LANE), NEG, F32).at[0, :N_EXPERTS].set(router_b)
    return rw_hi, rw_lo, rb


def _proj_post_kernel(a_ref, w_ref, b_ref, x_ref, mod_ref, lng_ref, lnb_ref, rwh_ref, rwl_ref, rb_ref, *out_refs):
    y = _dot(a_ref[...], w_ref[...]) + b_ref[...]
    _post(y, x_ref[...], mod_ref[0], lng_ref[...], lnb_ref[...], rwh_ref[...], rwl_ref[...], rb_ref[...], out_refs)


def _proj_post(a, w_out, b_out, x, mods, lng, lnb, router, n_rows, n_lat_tiles):
    k = a.shape[1]
    tok, mod = _tile_specs(n_lat_tiles)
    shapes, specs = _post_out(n_rows)
    return pl.pallas_call(
        _proj_post_kernel,
        grid=(n_rows // TM,),
        in_specs=[
            pl.BlockSpec((TM, k), lambda t: (t, 0)),
            _full((k, D_MODEL)), _full((1, D_MODEL)),
            tok, mod, _full((1, D_MODEL)), _full((1, D_MODEL)),
            _full((D_MODEL, LANE)), _full((D_MODEL, LANE)), _full((1, LANE)),
        ],
        out_specs=specs,
        out_shape=shapes,
        compiler_params=_params("arbitrary"),
        name="proj_post",
    )(a, w_out.astype(BF16), b_out.reshape(1, -1), x, mods, lng.reshape(1, -1), lnb.reshape(1, -1), *router)


def _dft_mats(n):
    jk = np.outer(np.arange(n), np.arange(n)) % n
    ang = 2.0 * np.pi * jk / n
    out = []
    for m in (np.cos(ang), np.sin(ang)):
        m32 = jnp.asarray(m, F32)
        hi = m32.astype(BF16)
        out += [hi, (m32 - hi.astype(F32)).astype(BF16)]
    return out


def _channel_dft(h, cc, sc):
    cw = cc[0].shape[0]
    a_parts, b_parts = [], []
    for g in range(h.shape[1] // cw):
        hh, hl = _split(h[:, g * cw:(g + 1) * cw])
        a_parts.append(_dot3(hh, hl, cc[0][...], cc[1][...]))
        b_parts.append(_dot3(hh, hl, sc[0][...], sc[1][...]))
    return jnp.concatenate(a_parts, axis=1), jnp.concatenate(b_parts, axis=1)


def _fourier1_kernel(x_ref, mod_ref, cch_ref, ccl_ref, sch_ref, scl_ref, tc_ref, ts_ref, ur_ref, ui_ref):
    mod = mod_ref[0]
    h = x_ref[...] * (1.0 + mod[1:2]) + mod[0:1]
    a, b = _channel_dft(h, (cch_ref, ccl_ref), (sch_ref, scl_ref))
    tch, tcl = _split(tc_ref[0])
    tsh, tsl = _split(ts_ref[0])
    ah, al = _split(a)
    bh, bl = _split(b)
    ur_ref[...] = _dot3(tch, tcl, ah, al) - _dot3(tsh, tsl, bh, bl)
    ui_ref[...] = -(_dot3(tch, tcl, bh, bl) + _dot3(tsh, tsl, ah, al))


def _fourier2_kernel(ur_ref, ui_ref, c2h_ref, c2l_ref, s2h_ref, s2l_ref, w_ref, b_ref, x_ref, mod_ref,
                     lng_ref, lnb_ref, rwh_ref, rwl_ref, rb_ref, *out_refs, norm):
    urh, url = _split(ur_ref[0])
    uih, uil = _split(ui_ref[0])
    mixed = (_dot3(c2h_ref[...], c2l_ref[...], urh, url) + _dot3(s2h_ref[...], s2l_ref[...], uih, uil)) * norm
    y = _dot(mixed.astype(BF16), w_ref[...]) + b_ref[...]
    _post(y, x_ref[...], mod_ref[0], lng_ref[...], lnb_ref[...], rwh_ref[...], rwl_ref[...], rb_ref[...], out_refs)


def _fourier_ctx_kernel(x_ref, mod_ref, cch_ref, ccl_ref, sch_ref, scl_ref, cnh_ref, cnl_ref, snh_ref, snl_ref,
                        w_ref, b_ref, lng_ref, lnb_ref, rwh_ref, rwl_ref, rb_ref, *out_refs, norm):
    mod = mod_ref[0]
    x = x_ref[...]
    h = x * (1.0 + mod[1:2]) + mod[0:1]
    a, b = _channel_dft(h, (cch_ref, ccl_ref), (sch_ref, scl_ref))
    ah, al = _split(a)
    bh, bl = _split(b)
    mixed = (_dot3(cnh_ref[...], cnl_ref[...], ah, al) - _dot3(snh_ref[...], snl_ref[...], bh, bl)) * norm
    y = _dot(mixed.astype(BF16), w_ref[...]) + b_ref[...]
    _post(y, x, mod, lng_ref[...], lnb_ref[...], rwh_ref[...], rwl_ref[...], rb_ref[...], out_refs)


def _fourier_layer(x_all, mods, w_out, b_out, lng, lnb, router, n_lat, n_ctx):
    t_all, d = x_all.shape
    n2 = LANE
    n1 = n_lat // n2
    cw = d // FN_GROUPS
    rows = t_all // n2
    xv = x_all.reshape(rows, n2 * d)
    cmat = _dft_mats(cw)
    w_bf = w_out.astype(BF16)
    b2 = b_out.reshape(1, d)
    lng2, lnb2 = lng.reshape(1, d), lnb.reshape(1, d)

    k1 = jnp.arange(n1, dtype=I32)
    pos = jnp.arange(n1, dtype=I32)[None, None, :] * n2 + jnp.arange(n2, dtype=I32)[:, None, None]
    ang = ((k1[None, :, None] * pos) % n_lat).astype(F32) * (2.0 * math.pi / n_lat)
    tc, ts = jnp.cos(ang), jnp.sin(ang)

    mat = _full((cw, cw))
    ur, ui = pl.pallas_call(
        _fourier1_kernel,
        grid=(n2,),
        in_specs=[
            pl.BlockSpec((n1, d), lambda j: (0, j)),
            pl.BlockSpec((1, N_MOD, d), lambda j: (0, 0, 0)),
            mat, mat, mat, mat,
            pl.BlockSpec((1, n1, n1), lambda j: (j, 0, 0)),
            pl.BlockSpec((1, n1, n1), lambda j: (j, 0, 0)),
        ],
        out_specs=(pl.BlockSpec((n1, d), lambda j: (0, j)), pl.BlockSpec((n1, d), lambda j: (0, j))),
        out_shape=(jax.ShapeDtypeStruct((n1, n2 * d), F32), jax.ShapeDtypeStruct((n1, n2 * d), F32)),
        compiler_params=_params("arbitrary"),
        name="fourier1",
    )(xv, mods, *cmat, tc, ts)

    m2 = _dft_mats(n2)
    mat2 = _full((n2, n2))
    norm = 1.0 / math.sqrt(n_lat * cw)
    out_shapes = (
        jax.ShapeDtypeStruct((n1, n2 * d), F32),
        jax.ShapeDtypeStruct((n1, n2 * d), F32),
        jax.ShapeDtypeStruct((n1, n2 * LANE), I32),
        jax.ShapeDtypeStruct((n1, n2 * LANE), F32),
        _COUNT_SHAPE,
    )
    strided = pl.BlockSpec((n2, d), lambda k: (0, k))
    strided_l = pl.BlockSpec((n2, LANE), lambda k: (0, k))
    outs = pl.pallas_call(
        functools.partial(_fourier2_kernel, norm=norm),
        grid=(n1,),
        in_specs=[
            pl.BlockSpec((1, n2, d), lambda k: (k, 0, 0)),
            pl.BlockSpec((1, n2, d), lambda k: (k, 0, 0)),
            mat2, mat2, mat2, mat2,
            _full((d, d)), _full((1, d)),
            strided,
            pl.BlockSpec((1, N_MOD, d), lambda k: (0, 0, 0)),
            _full((1, d)), _full((1, d)),
            _full((d, LANE)), _full((d, LANE)), _full((1, LANE)),
        ],
        out_specs=(strided, strided, strided_l, strided_l, _COUNT_SPEC),
        out_shape=out_shapes,
        compiler_params=_params("arbitrary"),
        name="fourier2",
    )(ur.reshape(n1, n2, d), ui.reshape(n1, n2, d), *m2, w_bf, b2, xv, mods, lng2, lnb2, *router)
    lat = (outs[0].reshape(n_lat, d), outs[1].reshape(n_lat, d),
           outs[2].reshape(n_lat, LANE), outs[3].reshape(n_lat, LANE))

    cn = _dft_mats(n_ctx)
    matn = _full((n_ctx, n_ctx))
    shapes = (
        jax.ShapeDtypeStruct((n_ctx, d), F32), jax.ShapeDtypeStruct((n_ctx, d), F32),
        jax.ShapeDtypeStruct((n_ctx, LANE), I32), jax.ShapeDtypeStruct((n_ctx, LANE), F32), _COUNT_SHAPE,
    )
    ctx = pl.pallas_call(
        functools.partial(_fourier_ctx_kernel, norm=1.0 / math.sqrt(n_ctx * cw)),
        grid=(1,),
        in_specs=[
            pl.BlockSpec((n_ctx, d), lambda i: (n_lat // n_ctx, 0)),
            pl.BlockSpec((1, N_MOD, d), lambda i: (1, 0, 0)),
            mat, mat, mat, mat, matn, matn, matn, matn,
            _full((d, d)), _full((1, d)), _full((1, d)), _full((1, d)),
            _full((d, LANE)), _full((d, LANE)), _full((1, LANE)),
        ],
        out_specs=(_full((n_ctx, d)), _full((n_ctx, d)), _full((n_ctx, LANE)), _full((n_ctx, LANE)),
                   _COUNT_SPEC),
        out_shape=shapes,
        compiler_params=_params("arbitrary"),
        name="fourier_ctx",
    )(x_all, mods, *cmat, *cn, w_bf, b2, lng2, lnb2, *router)
    return tuple(jnp.concatenate([a, b], axis=0) for a, b in zip(lat, ctx[:4])) + (outs[4] + ctx[4],)


def _qkv_kernel(x_ref, mod_ref, w_ref, b_ref, cos_ref, sin_ref, *rest, n_qk, rms):
    if rms:
        gain_ref, ind_ref, indt_ref, q_ref, k_ref, v_ref = rest
    else:
        q_ref, k_ref, v_ref = rest
    mod = mod_ref[0]
    h = x_ref[...] * (1.0 + mod[1:2]) + mod[0:1]
    y = _dot(h.astype(BF16), w_ref[...]) + b_ref[...]
    qk = y[:, :n_qk]
    if rms:
        sh, sl = _split(qk * qk)
        ms = _dot(sh, ind_ref[...]) + _dot(sl, ind_ref[...])
        mh, ml = _split(ms)
        msb = _dot(mh, indt_ref[...]) + _dot(ml, indt_ref[...])
        qk = qk * lax.rsqrt(msb + RMS_EPS) * gain_ref[...]
    cos = cos_ref[...]
    sin = sin_ref[...]
    even = (lax.broadcasted_iota(I32, cos.shape, 1) & 1) == 0
    parts = []
    for c in range(n_qk // LANE):
        z = qk[:, c * LANE:(c + 1) * LANE]
        swapped = jnp.where(even, pltpu.roll(z, LANE - 1, 1), pltpu.roll(z, 1, 1))
        parts.append(z * cos + swapped * sin)
    nq = q_ref.shape[1]
    q_ref[...] = (jnp.concatenate(parts[:nq // LANE], axis=1) * Q_SCALE).astype(BF16)
    k_ref[...] = jnp.concatenate(parts[nq // LANE:], axis=1).astype(BF16)
    v_ref[...] = y[:, n_qk:].astype(BF16)


def _qkv(x_all, mods, w_qkv, b_qkv, cos_t, sin_t, n_q, n_kv, n_lat_tiles, q_norm=None, k_norm=None):
    t_all, d = x_all.shape
    nq, nk = n_q * HEAD_DIM, n_kv * HEAD_DIM
    n_qk, n_all = nq + nk, nq + 2 * nk
    rms = q_norm is not None
    tok, mod = _tile_specs(n_lat_tiles)
    in_specs = [tok, mod, _full((d, n_all)), _full((1, n_all)),
                pl.BlockSpec((TM, LANE), lambda t: (t, 0)), pl.BlockSpec((TM, LANE), lambda t: (t, 0))]
    args = [x_all, mods, w_qkv.astype(BF16), b_qkv.reshape(1, n_all), cos_t, sin_t]
    if rms:
        gain = jnp.concatenate([jnp.tile(q_norm, n_q), jnp.tile(k_norm, n_kv)]).reshape(1, n_qk)
        head = np.arange(n_qk) // HEAD_DIM
        ind = np.zeros((n_qk, LANE), np.float32)
        ind[np.arange(n_qk), head] = 1.0 / HEAD_DIM
        indt = np.zeros((LANE, n_qk), np.float32)
        indt[head, np.arange(n_qk)] = 1.0
        in_specs += [_full((1, n_qk)), _full((n_qk, LANE)), _full((LANE, n_qk))]
        args += [gain, jnp.asarray(ind, BF16), jnp.asarray(indt, BF16)]
    return pl.pallas_call(
        functools.partial(_qkv_kernel, n_qk=n_qk, rms=rms),
        grid=(t_all // TM,),
        in_specs=in_specs,
        out_specs=(pl.BlockSpec((TM, nq), lambda t: (t, 0)), pl.BlockSpec((TM, nk), lambda t: (t, 0)),
                   pl.BlockSpec((TM, nk), lambda t: (t, 0))),
        out_shape=(jax.ShapeDtypeStruct((t_all, nq), BF16), jax.ShapeDtypeStruct((t_all, nk), BF16),
                   jax.ShapeDtypeStruct((t_all, nk), BF16)),
        compiler_params=_params("arbitrary"),
        name="qkv",
    )(*args)


def _rope_tables(n_lat, n_ctx):
    rows = n_lat // GRID_W
    row = jnp.repeat(jnp.arange(rows, dtype=F32), GRID_W)
    col = jnp.tile(jnp.arange(GRID_W, dtype=F32), rows)
    n_freq = HEAD_DIM // 4
    inv = ROPE_THETA ** (-jnp.arange(n_freq, dtype=F32) / n_freq)
    ang = jnp.concatenate([row[:, None] * inv, col[:, None] * inv], axis=-1)
    ang = jnp.concatenate([ang, jnp.zeros((n_ctx, HEAD_DIM // 2), F32)], axis=0)
    cos = jnp.tile(jnp.repeat(jnp.cos(ang), 2, axis=1), (1, LANE // HEAD_DIM))
    sin = jnp.tile(jnp.repeat(jnp.sin(ang), 2, axis=1), (1, LANE // HEAD_DIM))
    sign = jnp.where(jnp.arange(LANE) % 2 == 0, -1.0, 1.0).astype(F32)
    return cos, sin * sign


def _kv_layouts(k, v, n_kv):
    t_all = k.shape[0]
    kh = k.reshape(t_all, n_kv, HEAD_DIM).transpose(1, 0, 2)
    vt = v.reshape(t_all, n_kv, HEAD_DIM).transpose(1, 2, 0)
    pad = jnp.zeros((n_kv, LANE - HEAD_DIM, t_all), BF16).at[:, 0, :].set(1.0)
    return kh, jnp.concatenate([vt, pad], axis=1)


def _flash_kernel(q_ref, k_ref, vt_ref, o_ref, s_even, s_odd, *, grp, tk, n_lat, n_ctx):
    tq = q_ref.shape[0]
    q_t = q_ref[...].astype(F32).T
    qt = jnp.concatenate([q_t[g * HEAD_DIM:(g + 1) * HEAD_DIM] for g in range(grp)], axis=1).astype(BF16)
    cols = qt.shape[1]
    is_ctx = pl.program_id(1) == n_lat // FLASH_TQ
    n_chunks = (n_lat + n_ctx) // tk

    def finish(acc):
        o = acc[:HEAD_DIM] / acc[HEAD_DIM:HEAD_DIM + 1]
        o_t = jnp.concatenate([o[:, g * tq:(g + 1) * tq] for g in range(grp)], axis=0)
        o_ref[...] = o_t.T.astype(BF16)

    groups = [slice(c * MXU_WIDTH, (c + 1) * MXU_WIDTH) for c in range(cols // MXU_WIDTH)]

    def scores(j, buf, g):
        off = pl.multiple_of(j * tk, tk)
        s = _dot(k_ref[0, pl.ds(off, tk), :], qt[:, g])
        buf[:, g] = s
        return jnp.max(s, axis=0, keepdims=True)

    def accumulate(j, buf, g, m, acc, mc):
        off = pl.multiple_of(j * tk, tk)
        m_new = jnp.maximum(m, mc)
        p = jnp.exp2(buf[:, g] - m_new)
        acc = jnp.exp2(m - m_new) * acc + _dot(vt_ref[0, :, pl.ds(off, tk)], p.astype(BF16))
        return m_new, acc

    @pl.when(jnp.logical_not(is_ctx))
    def _():
        def step(j, cur, nxt, carry):
            out = []
            for g, (m, acc, mc) in zip(groups, carry):
                mc_next = scores(j + 1, nxt, g)
                out.append(accumulate(j, cur, g, m, acc, mc) + (mc_next,))
            return out

        def body(jj, carry):
            for pair in range(FLASH_PAIRS):
                j = 2 * (FLASH_PAIRS * jj + pair)
                carry = step(j, s_even, s_odd, carry)
                carry = step(j + 1, s_odd, s_even, carry)
            return carry

        init = [(jnp.full((1, MXU_WIDTH), NEG, F32), jnp.zeros((LANE, MXU_WIDTH), F32), scores(0, s_even, g))
                for g in groups]
        carry = lax.fori_loop(0, (n_chunks - 1) // (2 * FLASH_PAIRS), body, init)
        finish(jnp.concatenate([accumulate(n_chunks - 1, s_even, g, m, acc, mc)[1]
                                for g, (m, acc, mc) in zip(groups, carry)], axis=1))

    @pl.when(is_ctx)
    def _():
        s = _dot(k_ref[0, n_lat:n_lat + n_ctx, :], qt)
        p = jnp.exp2(s - jnp.max(s, axis=0, keepdims=True))
        finish(_dot(vt_ref[0, :, n_lat:n_lat + n_ctx], p.astype(BF16)))


def _full_attention(q, k, v, n_lat, n_ctx):
    t_all, nq = q.shape
    n_kv = k.shape[1] // HEAD_DIM
    grp = nq // HEAD_DIM // n_kv
    nt = t_all // FLASH_TQ
    cols = grp * FLASH_TQ
    assert n_ctx == FLASH_TQ and t_all % FLASH_TK == 0 and (t_all // FLASH_TK - 1) % (2 * FLASH_PAIRS) == 0
    kh, vt = _kv_layouts(k, v, n_kv)
    return pl.pallas_call(
        functools.partial(_flash_kernel, grp=grp, tk=FLASH_TK, n_lat=n_lat, n_ctx=n_ctx),
        grid=(n_kv, nt),
        in_specs=[
            pl.BlockSpec((FLASH_TQ, grp * HEAD_DIM), lambda h, i: (i, h)),
            pl.BlockSpec((1, t_all, HEAD_DIM), lambda h, i: (h, 0, 0)),
            pl.BlockSpec((1, LANE, t_all), lambda h, i: (h, 0, 0)),
        ],
        out_specs=pl.BlockSpec((FLASH_TQ, grp * HEAD_DIM), lambda h, i: (i, h)),
        out_shape=jax.ShapeDtypeStruct((t_all, nq), BF16),
        scratch_shapes=[pltpu.VMEM((FLASH_TK, cols), F32), pltpu.VMEM((FLASH_TK, cols), F32)],
        compiler_params=_params("arbitrary", "arbitrary"),
        name="flash",
    )(q, kh, vt)


def _window_kernel(q_ref, kp_ref, kc_ref, kn_ref, kx_ref, vp_ref, vc_ref, vn_ref, vx_ref, sink_ref, o_ref,
                   *, grp, nb):
    i = pl.program_id(1)
    q_t = q_ref[...].astype(F32).T
    qt = jnp.concatenate([q_t[g * HEAD_DIM:(g + 1) * HEAD_DIM] for g in range(grp)], axis=1).astype(BF16)
    cols = qt.shape[1]
    s = _dot(jnp.concatenate([kp_ref[0], kc_ref[0], kn_ref[0], kx_ref[0]], axis=0), qt)
    kj = lax.broadcasted_iota(I32, (Q_BLOCK, cols), 0)
    qi = lax.broadcasted_iota(I32, (Q_BLOCK, cols), 1) & (Q_BLOCK - 1)
    sp = jnp.where(kj >= jnp.where(i > 0, qi, Q_BLOCK), s[:Q_BLOCK], NEG)
    sn = jnp.where(kj <= jnp.where(i < nb - 1, qi, -1), s[2 * Q_BLOCK:3 * Q_BLOCK], NEG)
    s = jnp.concatenate([sp, s[Q_BLOCK:2 * Q_BLOCK], sn, s[3 * Q_BLOCK:]], axis=0)
    sink = sink_ref[0]
    m = jnp.maximum(jnp.max(s, axis=0, keepdims=True), sink)
    p = jnp.exp2(s - m).astype(BF16)
    acc = _dot(jnp.concatenate([vp_ref[0], vc_ref[0], vn_ref[0], vx_ref[0]], axis=1), p)
    o = acc[:HEAD_DIM] / (acc[HEAD_DIM:HEAD_DIM + 1] + jnp.exp2(sink - m))
    o_t = jnp.concatenate([o[:, g * Q_BLOCK:(g + 1) * Q_BLOCK] for g in range(grp)], axis=0)
    o_ref[...] = o_t.T.astype(BF16)


def _window_attention(q, k, v, sink, n_lat, n_ctx):
    nq = q.shape[1]
    n_kv = k.shape[1] // HEAD_DIM
    grp = nq // HEAD_DIM // n_kv
    gw = grp * HEAD_DIM
    nb = n_lat // Q_BLOCK
    cb = n_lat // n_ctx
    kh, vt = _kv_layouts(k, v, n_kv)
    sink_cols = jnp.repeat(sink.reshape(n_kv, 1, grp) * LOG2E, Q_BLOCK, axis=2)
    kspec = lambda f: pl.BlockSpec((1, Q_BLOCK, HEAD_DIM), lambda h, i: (h, f(i), 0))
    vspec = lambda f: pl.BlockSpec((1, LANE, Q_BLOCK), lambda h, i: (h, 0, f(i)))
    prev = lambda i: jnp.maximum(i - 1, 0)
    cur = lambda i: i
    nxt = lambda i: jnp.minimum(i + 1, nb - 1)
    return pl.pallas_call(
        functools.partial(_window_kernel, grp=grp, nb=nb),
        grid=(n_kv, nb),
        in_specs=[
            pl.BlockSpec((Q_BLOCK, gw), lambda h, i: (i, h)),
            kspec(prev), kspec(cur), kspec(nxt),
            pl.BlockSpec((1, n_ctx, HEAD_DIM), lambda h, i: (h, cb, 0)),
            vspec(prev), vspec(cur), vspec(nxt),
            pl.BlockSpec((1, LANE, n_ctx), lambda h, i: (h, 0, cb)),
            pl.BlockSpec((1, 1, grp * Q_BLOCK), lambda h, i: (h, 0, 0)),
        ],
        out_specs=pl.BlockSpec((Q_BLOCK, gw), lambda h, i: (i, h)),
        out_shape=jax.ShapeDtypeStruct((n_lat, nq), BF16),
        compiler_params=_params("arbitrary", "arbitrary"),
        name="window",
    )(q, kh, kh, kh, kh, vt, vt, vt, vt, sink_cols)


def _gmlp_kernel(x_ref, mod_ref, win_ref, bin_ref, vg_ref, vb_ref, ws_ref, bs_ref, wout_ref, bout_ref,
                 lng_ref, lnb_ref, rwh_ref, rwl_ref, rb_ref, *out_refs):
    mod = mod_ref[0]
    x = x_ref[...]
    h = x * (1.0 + mod[1:2]) + mod[0:1]
    z = _dot(h.astype(BF16), win_ref[...]) + bin_ref[...]
    z = 0.5 * z * (1.0 + lax.erf(z * (2.0 ** -0.5)))
    half = z.shape[1] // 2
    u = z[:, :half]
    v = _layer_norm(z[:, half:], vg_ref[...], vb_ref[...]).astype(BF16)
    cw = half // GM_GROUPS
    chunks = []
    for c in range(x.shape[0] // GM_CHUNK):
        vc = v[c * GM_CHUNK:(c + 1) * GM_CHUNK]
        chunks.append(jnp.concatenate(
            [_dot(ws_ref[g], vc[:, g * cw:(g + 1) * cw]) + bs_ref[g] for g in range(GM_GROUPS)], axis=1))
    gated = u * jnp.concatenate(chunks, axis=0)
    y = _dot(gated.astype(BF16), wout_ref[...]) + bout_ref[...]
    _post(y, x, mod, lng_ref[...], lnb_ref[...], rwh_ref[...], rwl_ref[...], rb_ref[...], out_refs)


def _gmlp_layer(x_all, mods, w_in, b_in, vg, vb, w_s, b_s, w_out, b_out, lng, lnb, router, n_lat_tiles):
    t_all, d = x_all.shape
    dffn = w_in.shape[1]
    half = dffn // 2
    cw = half // GM_GROUPS
    tok, mod = _tile_specs(n_lat_tiles)
    shapes, specs = _post_out(t_all)
    bs_full = jnp.broadcast_to(b_s[:, :, None], (GM_GROUPS, GM_CHUNK, cw))
    return pl.pallas_call(
        _gmlp_kernel,
        grid=(t_all // TM,),
        in_specs=[
            tok, mod, _full((d, dffn)), _full((1, dffn)), _full((1, half)), _full((1, half)),
            _full((GM_GROUPS, GM_CHUNK, GM_CHUNK)), _full((GM_GROUPS, GM_CHUNK, cw)),
            _full((half, d)), _full((1, d)), _full((1, d)), _full((1, d)),
            _full((d, LANE)), _full((d, LANE)), _full((1, LANE)),
        ],
        out_specs=specs,
        out_shape=shapes,
        compiler_params=_params("arbitrary"),
        name="gmlp",
    )(x_all, mods, w_in.astype(BF16), b_in.reshape(1, dffn), vg.reshape(1, half), vb.reshape(1, half),
      w_s.astype(BF16), bs_full, w_out.astype(BF16), b_out.reshape(1, d), lng.reshape(1, d), lnb.reshape(1, d),
      *router)


COMB_TM = 128


def _route_meta(count_blk, n_tok):
    n_assign = n_tok * TOP_K
    n_blocks = -(-n_assign // MOE_BLOCK) + N_EXPERTS
    ids = jnp.arange(N_EXPERTS, dtype=I32)
    counts = count_blk[0, :N_EXPERTS].astype(I32)
    padded = (counts + MOE_BLOCK - 1) // MOE_BLOCK * MOE_BLOCK
    ends_pad = jnp.cumsum(padded)
    base = ends_pad - padded
    starts = jnp.arange(n_blocks, dtype=I32) * MOE_BLOCK
    block_expert = jnp.minimum(jnp.sum((ends_pad[None, :] <= starts[:, None]).astype(I32), axis=1), N_EXPERTS - 1)
    last_blk = jnp.where(padded > 0, ends_pad - MOE_BLOCK, -1)
    tail = ends_pad[-1] + ids * MOE_BLOCK
    tail = jnp.where(tail < n_blocks * MOE_BLOCK, tail, -1)
    base_b = jnp.broadcast_to(base.astype(F32)[:, None], (N_EXPERTS, LANE))
    return block_expert.astype(I32), base_b, jnp.concatenate([last_blk, tail]).astype(I32), n_blocks


def _rank_kernel(idx_ref, base_ref, upper_ref, dest_ref, run_ref):
    @pl.when(pl.program_id(0) == 0)
    def _():
        run_ref[...] = jnp.zeros(run_ref.shape, F32)

    eid = lax.broadcasted_iota(I32, (N_EXPERTS, TM), 0)
    onehots = [(eid == idx_ref[k:k + 1, :]).astype(F32) for k in range(TOP_K)]
    cnt = onehots[0] + onehots[1] + onehots[2] + onehots[3]
    before = _dot(cnt.astype(BF16), upper_ref[...])
    slot = base_ref[:, :1] + run_ref[:, :1] + before
    for k in range(TOP_K):
        rows = (jnp.sum(onehots[k] * slot, axis=0, keepdims=True) * ROW_TILE).astype(I32)
        for j in range(TM // COMB_TM):
            dest_ref[j, k:k + 1, :] = rows[:, j * COMB_TM:(j + 1) * COMB_TM]
    run_ref[...] = run_ref[...] + jnp.sum(cnt, axis=1, keepdims=True)


def _rank(top_idx, base_b, n_tok):
    per_step = TM // COMB_TM
    idx_t = top_idx[:n_tok, :TOP_K].T
    upper = jnp.asarray(np.triu(np.ones((TM, TM), np.float32), 1), BF16)
    return pl.pallas_call(
        _rank_kernel,
        grid=(n_tok // TM,),
        in_specs=[pl.BlockSpec((TOP_K, TM), lambda t: (0, t)), _full((N_EXPERTS, LANE)), _full((TM, TM))],
        out_specs=pl.BlockSpec((per_step, TOP_K, COMB_TM), lambda t: (t, 0, 0)),
        out_shape=jax.ShapeDtypeStruct((n_tok // COMB_TM, TOP_K, COMB_TM), I32),
        scratch_shapes=[pltpu.VMEM((N_EXPERTS, LANE), F32)],
        compiler_params=_params("arbitrary"),
        name="rank",
    )(idx_t, base_b, upper)


def _dispatch_kernel(dest_ref, zrow_ref, h_ref, xs_hbm, hbuf, zbuf, sem, zsem, *, n_tiles):
    t = pl.program_id(0)
    slot = t % 2
    n_zero = 2 * N_EXPERTS

    def zero_copy(j):
        row = pl.multiple_of(zrow_ref[j] * ROW_TILE, MOE_BLOCK * ROW_TILE)
        return pltpu.make_async_copy(zbuf, xs_hbm.at[pl.ds(row, MOE_BLOCK * ROW_TILE), :], zsem.at[0])

    def wait_rows(s):
        for _ in range(TOP_K):
            pltpu.make_async_copy(hbuf.at[s], xs_hbm.at[pl.ds(0, COMB_TM * ROW_TILE), :], sem.at[s]).wait()

    @pl.when(t == 0)
    def _():
        zbuf[...] = jnp.zeros(zbuf.shape, F32)
        for j in range(n_zero):
            @pl.when(zrow_ref[j] >= 0)
            def _():
                zero_copy(j).start()
        for j in range(n_zero):
            @pl.when(zrow_ref[j] >= 0)
            def _():
                zero_copy(j).wait()

    @pl.when(t >= 2)
    def _():
        wait_rows(slot)

    _to_row_tiles(hbuf.at[slot], h_ref[...])
    for k in range(TOP_K):
        for r in range(COMB_TM):
            row = pl.multiple_of(dest_ref[(t * TOP_K + k) * COMB_TM + r], ROW_TILE)
            pltpu.make_async_copy(hbuf.at[slot, pl.ds(r * ROW_TILE, ROW_TILE), :],
                                  xs_hbm.at[pl.ds(row, ROW_TILE), :], sem.at[slot]).start(priority=r % 2)

    @pl.when(t == n_tiles - 1)
    def _():
        wait_rows(slot)
        if n_tiles > 1:
            wait_rows(1 - slot)


def _dispatch(h2, dest, zrows, n_tok, n_blocks):
    d = h2.shape[1]
    n_tiles = n_tok // COMB_TM
    grid_spec = pltpu.PrefetchScalarGridSpec(
        num_scalar_prefetch=2,
        grid=(n_tiles,),
        in_specs=[pl.BlockSpec((COMB_TM, d), lambda t, dest, zr: (t, 0))],
        out_specs=pl.BlockSpec(memory_space=pl.ANY),
        scratch_shapes=[pltpu.VMEM((2, COMB_TM * ROW_TILE, LANE), F32),
                        pltpu.VMEM((MOE_BLOCK * ROW_TILE, LANE), F32),
                        pltpu.SemaphoreType.DMA((2,)), pltpu.SemaphoreType.DMA((1,))],
    )
    return pl.pallas_call(
        functools.partial(_dispatch_kernel, n_tiles=n_tiles),
        grid_spec=grid_spec,
        out_shape=jax.ShapeDtypeStruct((n_blocks * MOE_BLOCK * ROW_TILE, LANE), F32),
        compiler_params=_params("arbitrary"),
        name="dispatch",
    )(dest.reshape(-1), zrows, h2)


EXP_CHUNK = MXU_WIDTH


def _expert_kernel(be_ref, x_ref, wgu_ref, bgu_ref, wd_ref, bd_ref, sel_ref, y_ref,
                   wg_s, wl_s, wd_s, bg_s, bl_s):
    b = pl.program_id(0)
    ff = wg_s.shape[1]
    half = EXP_CHUNK // 2

    @pl.when(jnp.logical_or(b == 0, be_ref[b] != be_ref[jnp.maximum(b - 1, 0)]))
    def _():
        for c in range(2 * ff // EXP_CHUNK):
            w = _dot(wgu_ref[0, 0, :, c * EXP_CHUNK:(c + 1) * EXP_CHUNK].astype(BF16), sel_ref[...])
            wg_s[:, c * half:(c + 1) * half] = w[:, :half].astype(BF16)
            wl_s[:, c * half:(c + 1) * half] = w[:, half:].astype(BF16)
            bh, bl = _split(jnp.broadcast_to(bgu_ref[0, 0, :, c * EXP_CHUNK:(c + 1) * EXP_CHUNK], (8, EXP_CHUNK)))
            bias = _dot(bh, sel_ref[...]) + _dot(bl, sel_ref[...])
            bg_s[:, c * half:(c + 1) * half] = bias[:, :half]
            bl_s[:, c * half:(c + 1) * half] = bias[:, half:]
        wd_s[...] = wd_ref[0, 0].astype(BF16)

    x = _from_row_tiles(x_ref, 0, MOE_BLOCK).astype(BF16)
    glu = jnp.minimum(_dot(x, wg_s[...]) + bg_s[0:1, :], SWIGLU_LIMIT)
    lin = jnp.clip(_dot(x, wl_s[...]) + bl_s[0:1, :], -SWIGLU_LIMIT, SWIGLU_LIMIT)
    act = glu * (1.0 / (1.0 + jnp.exp(-SWIGLU_ALPHA * glu))) * (lin + 1.0)
    _to_row_tiles(y_ref, _dot(act.astype(BF16), wd_s[...]) + bd_ref[0, 0])


def _experts(xs, block_expert, n_blocks, w_gate_up, b_gate_up, w_down, b_down, layer):
    d = w_down.shape[3]
    ff = w_down.shape[2]
    half = EXP_CHUNK // 2
    sel = np.zeros((EXP_CHUNK, EXP_CHUNK), np.float32)
    sel[2 * np.arange(half), np.arange(half)] = 1.0
    sel[2 * np.arange(half) + 1, half + np.arange(half)] = 1.0
    wspec = lambda r, c: pl.BlockSpec((1, 1, r, c), lambda b, be: (layer, be[b], 0, 0))
    sspec = pl.BlockSpec((EXP_CHUNK, EXP_CHUNK), lambda b, be: (0, 0))
    grid_spec = pltpu.PrefetchScalarGridSpec(
        num_scalar_prefetch=1,
        grid=(n_blocks,),
        in_specs=[pl.BlockSpec((MOE_BLOCK * ROW_TILE, LANE), lambda b, be: (b, 0)), wspec(d, 2 * ff),
                  wspec(1, 2 * ff), wspec(ff, d), wspec(1, d), sspec],
        out_specs=pl.BlockSpec((MOE_BLOCK * ROW_TILE, LANE), lambda b, be: (b, 0)),
        scratch_shapes=[pltpu.VMEM((d, ff), BF16), pltpu.VMEM((d, ff), BF16), pltpu.VMEM((ff, d), BF16),
                        pltpu.VMEM((8, ff), F32), pltpu.VMEM((8, ff), F32)],
    )
    return pl.pallas_call(
        _expert_kernel,
        grid_spec=grid_spec,
        out_shape=jax.ShapeDtypeStruct((n_blocks * MOE_BLOCK * ROW_TILE, LANE), F32),
        compiler_params=_params("arbitrary"),
        name="experts",
    )(block_expert, xs, w_gate_up, b_gate_up[:, :, None, :], w_down, b_down[:, :, None, :],
      jnp.asarray(sel, BF16))


def _gather_rows(idx_ref, base, n, src_hbm, dst, sem):
    for r in range(n):
        row = pl.multiple_of(idx_ref[base + r], ROW_TILE)
        pltpu.make_async_copy(src_hbm.at[pl.ds(row, ROW_TILE), :], dst.at[pl.ds(r * ROW_TILE, ROW_TILE), :],
                              sem).start(priority=r % 2)


def _combine_kernel(pos_ref, y_hbm, gate_ref, x_ref, mod_ref, lng_ref, lnb_ref, o_ref, ybuf, sem, *, n_tiles):
    t = pl.program_id(0)
    slot = t % 2
    n = COMB_TM * TOP_K

    @pl.when(t == 0)
    def _():
        _gather_rows(pos_ref, 0, n, y_hbm, ybuf.at[0], sem.at[0])

    @pl.when(t + 1 < n_tiles)
    def _():
        _gather_rows(pos_ref, (t + 1) * n, n, y_hbm, ybuf.at[1 - slot], sem.at[1 - slot])

    pltpu.make_async_copy(y_hbm.at[pl.ds(0, n * ROW_TILE), :], ybuf.at[slot], sem.at[slot]).wait()
    gates = gate_ref[...]
    f = jnp.zeros(x_ref.shape, F32)
    for k in range(TOP_K):
        f = f + gates[:, k:k + 1] * _from_row_tiles(ybuf.at[slot], k * COMB_TM, COMB_TM)
    o_ref[...] = _layer_norm(ALPHA * x_ref[...] + mod_ref[0][5:6] * f, lng_ref[...], lnb_ref[...])


def _combine(yb, dest, gates, x1, mods, lng, lnb, n_tok, n_lat):
    d = x1.shape[1]
    n_tiles = n_tok // COMB_TM
    n_lat_tiles = n_lat // COMB_TM
    pos = dest.reshape(-1)
    grid_spec = pltpu.PrefetchScalarGridSpec(
        num_scalar_prefetch=1,
        grid=(n_tiles,),
        in_specs=[
            pl.BlockSpec(memory_space=pl.ANY),
            pl.BlockSpec((COMB_TM, LANE), lambda t, pos: (t, 0)),
            pl.BlockSpec((COMB_TM, d), lambda t, pos: (t, 0)),
            pl.BlockSpec((1, N_MOD, d), lambda t, pos: (jnp.where(t >= n_lat_tiles, 1, 0), 0, 0)),
            pl.BlockSpec((1, d), lambda t, pos: (0, 0)),
            pl.BlockSpec((1, d), lambda t, pos: (0, 0)),
        ],
        out_specs=pl.BlockSpec((COMB_TM, d), lambda t, pos: (t, 0)),
        scratch_shapes=[pltpu.VMEM((2, COMB_TM * TOP_K * ROW_TILE, LANE), F32), pltpu.SemaphoreType.DMA((2,))],
    )
    return pl.pallas_call(
        functools.partial(_combine_kernel, n_tiles=n_tiles),
        grid_spec=grid_spec,
        out_shape=jax.ShapeDtypeStruct((n_tok, d), F32),
        compiler_params=_params("arbitrary"),
        name="combine",
    )(pos, yb, gates, x1, mods, lng.reshape(1, d), lnb.reshape(1, d))


def _moe_layer(x1, h2, top_idx, gates, counts, mods, lng, lnb, w_gate_up, b_gate_up, w_down, b_down, layer,
               n_tok, n_lat):
    block_expert, base_b, zrows, n_blocks = _route_meta(counts, n_tok)
    dest = _rank(top_idx, base_b, n_tok)
    xs = _dispatch(h2, dest, zrows, n_tok, n_blocks)
    yb = _experts(xs, block_expert, n_blocks, w_gate_up, b_gate_up, w_down, b_down, layer)
    return _combine(yb, dest, gates, x1, mods, lng, lnb, n_tok, n_lat)


def kernel(x, c, ctx, c_ctx, ada_w, ada_b, ln_mix_g, ln_mix_b, ln_ffn_g, ln_ffn_b, fn_w_out, fn_b_out, fa_w_qkv, fa_b_qkv, fa_q_norm, fa_k_norm, fa_w_out, fa_b_out, gm_w_in, gm_b_in, gm_v_norm_g, gm_v_norm_b, gm_w_s, gm_b_s, gm_w_out, gm_b_out, wa_w_qkv, wa_b_qkv, wa_sink, wa_w_out, wa_b_out, router_w, router_b, exp_w_gate_up, exp_b_gate_up, exp_w_down, exp_b_down):
    bsz, n_lat, d = x.shape
    n_ctx = ctx.shape[1]
    assert bsz == 1 and d == D_MODEL and n_lat == LANE * LANE and n_lat % n_ctx == 0 and n_ctx % TM == 0
    t_all = n_lat + n_ctx
    n_lat_tiles = n_lat // TM
    x_all = jnp.concatenate([x[0], ctx[0]], axis=0)
    mods_all = _ada(c, c_ctx, ada_w, ada_b)
    cos_t, sin_t = _rope_tables(n_lat, n_ctx)

    for i in range(DEPTH):
        kind, j = i % 4, i // 4
        last = i == DEPTH - 1
        n_tok = n_lat if last else t_all
        mods = mods_all[i]
        router = _router_operands(router_w[i], router_b[i])
        lng, lnb = ln_mix_g[i], ln_mix_b[i]
        if kind == 0:
            post = _fourier_layer(x_all, mods, fn_w_out[j], fn_b_out[j], lng, lnb, router, n_lat, n_ctx)
        elif kind == 1:
            q, k, v = _qkv(x_all, mods, fa_w_qkv[j], fa_b_qkv[j], cos_t, sin_t, FA_Q_HEADS, FA_KV_HEADS,
                           n_lat_tiles, fa_q_norm[j], fa_k_norm[j])
            o = _full_attention(q, k, v, n_lat, n_ctx)
            post = _proj_post(o, fa_w_out[j], fa_b_out[j], x_all, mods, lng, lnb, router, n_tok, n_lat_tiles)
        elif kind == 2:
            post = _gmlp_layer(x_all, mods, gm_w_in[j], gm_b_in[j], gm_v_norm_g[j], gm_v_norm_b[j],
                               gm_w_s[j], gm_b_s[j], gm_w_out[j], gm_b_out[j], lng, lnb, router, n_lat_tiles)
        else:
            q, k, v = _qkv(x_all, mods, wa_w_qkv[j], wa_b_qkv[j], cos_t, sin_t, WA_Q_HEADS, WA_KV_HEADS,
                           n_lat_tiles)
            o = _window_attention(q, k, v, wa_sink[j], n_lat, n_ctx)
            post = _proj_post(o, wa_w_out[j], wa_b_out[j], x_all, mods, lng, lnb, router, n_tok, n_lat_tiles)
        x_all = _moe_layer(*post, mods, ln_ffn_g[i], ln_ffn_b[i], exp_w_gate_up, exp_b_gate_up, exp_w_down,
                           exp_b_down, i, n_tok, n_lat)
    return x_all[None]
```

```python
import functools
import math

import numpy as np
import jax
import jax.numpy as jnp
from jax import lax
from jax.experimental import pallas as pl
from jax.experimental.pallas import tpu as pltpu

F32, BF16, I32 = jnp.float32, jnp.bfloat16, jnp.int32

D_MODEL = 1024
DEPTH = 4
GRID_W = 64
N_MOD = 6
FN_GROUPS = 4
HEAD_DIM = 64
FA_Q_HEADS, FA_KV_HEADS = 16, 4
WA_Q_HEADS, WA_KV_HEADS = 16, 2
WINDOW = 128
Q_BLOCK = 128
ROPE_THETA = 10000.0
GM_CHUNK = 128
GM_GROUPS = 8
N_EXPERTS = 32
TOP_K = 4
SWIGLU_LIMIT = 7.0
SWIGLU_ALPHA = 1.702
MOE_BLOCK = 256
LN_EPS = 1e-5
RMS_EPS = 1e-6
NEG = -1e30
ALPHA = (2 * DEPTH) ** 0.25
LOG2E = math.log2(math.e)
Q_SCALE = HEAD_DIM ** -0.5 * LOG2E

LANE = 128
ROW_TILE = D_MODEL // LANE
MXU_WIDTH = 256
TM = 256
FLASH_TQ = 256
FLASH_TK = 1280
FLASH_PAIRS = 3
VMEM_LIMIT = 56 * 2 ** 20


def _params(*sem):
    return pltpu.CompilerParams(dimension_semantics=sem, vmem_limit_bytes=VMEM_LIMIT)


def _dot(a, b):
    return jnp.dot(a, b, preferred_element_type=F32)


def _split(a):
    hi = a.astype(BF16)
    lo = (a - hi.astype(F32)).astype(BF16)
    return hi, lo


def _dot3(a_hi, a_lo, b_hi, b_lo):
    return _dot(a_hi, b_hi) + (_dot(a_hi, b_lo) + _dot(a_lo, b_hi))


def _layer_norm(x, g, b):
    mu = jnp.mean(x, axis=-1, keepdims=True)
    xc = x - mu
    var = jnp.mean(xc * xc, axis=-1, keepdims=True)
    return xc * lax.rsqrt(var + LN_EPS) * g + b


def _top4(logits):
    lane = lax.broadcasted_iota(I32, logits.shape, 1).astype(F32)
    cur = logits
    vals, idxs = [], []
    for _ in range(TOP_K):
        m = jnp.max(cur, axis=-1, keepdims=True)
        i = jnp.min(jnp.where(cur == m, lane, float(LANE)), axis=-1, keepdims=True)
        vals.append(m)
        idxs.append(i)
        cur = jnp.where(lane == i, -jnp.inf, cur)
    exps = [jnp.exp(v - vals[0]) for v in vals]
    inv = 1.0 / (exps[0] + exps[1] + exps[2] + exps[3])
    idx_out = jnp.zeros_like(logits)
    gate_out = jnp.zeros_like(logits)
    picked = jnp.zeros_like(logits)
    for k in range(TOP_K):
        idx_out = jnp.where(lane == float(k), idxs[k], idx_out)
        gate_out = jnp.where(lane == float(k), exps[k] * inv, gate_out)
        picked = picked + jnp.where(lane == idxs[k], 1.0, 0.0)
    return idx_out.astype(I32), gate_out, jnp.sum(picked, axis=0, keepdims=True)


def _post(y, x, mod, lng, lnb, rw_hi, rw_lo, rb, out_refs):
    x1_ref, h2_ref, idx_ref, gate_ref, cnt_ref = out_refs
    x1 = _layer_norm(ALPHA * x + mod[2:3] * y, lng, lnb)
    h2 = x1 * (1.0 + mod[4:5]) + mod[3:4]
    hh, hl = _split(h2)
    logits = _dot3(hh, hl, rw_hi, rw_lo) + rb
    idx, gates, cnt = _top4(logits)
    x1_ref[...] = x1
    h2_ref[...] = h2
    idx_ref[...] = idx
    gate_ref[...] = gates

    @pl.when(pl.program_id(0) == 0)
    def _():
        cnt_ref[...] = jnp.zeros(cnt_ref.shape, F32)

    cnt_ref[...] = cnt_ref[...] + cnt


def _to_row_tiles(ref, x):
    n = x.shape[0]
    for s in range(ROW_TILE):
        ref[pl.ds(s, n, stride=ROW_TILE), :] = x[:, s * LANE:(s + 1) * LANE]


def _from_row_tiles(ref, start, n):
    return jnp.concatenate([ref[pl.ds(start * ROW_TILE + s, n, stride=ROW_TILE), :] for s in range(ROW_TILE)],
                           axis=1)


def _ada_kernel(cs_ref, w_ref, b_ref, o_ref):
    cs = cs_ref[...]
    s = cs * (1.0 / (1.0 + jnp.exp(-cs)))
    sh, sl = _split(s)
    wh, wl = _split(w_ref[0])
    o_ref[0] = _dot3(sh, sl, wh, wl) + b_ref[0]


def _ada(c, c_ctx, ada_w, ada_b):
    d = c.shape[-1]
    nm = ada_w.shape[-1]
    tn = nm // 4
    cs = jnp.zeros((8, d), F32).at[0].set(c[0]).at[1].set(c_ctx)
    out = pl.pallas_call(
        _ada_kernel,
        grid=(DEPTH, nm // tn),
        in_specs=[
            pl.BlockSpec((8, d), lambda i, j: (0, 0)),
            pl.BlockSpec((1, d, tn), lambda i, j: (i, 0, j)),
            pl.BlockSpec((1, 1, tn), lambda i, j: (i, 0, j)),
        ],
        out_specs=pl.BlockSpec((1, 8, tn), lambda i, j: (i, 0, j)),
        out_shape=jax.ShapeDtypeStruct((DEPTH, 8, nm), F32),
        compiler_params=_params("arbitrary", "arbitrary"),
        name="ada",
    )(cs, ada_w, ada_b.reshape(DEPTH, 1, nm))
    return out[:, :2].reshape(DEPTH, 2, N_MOD, d)


def _tile_specs(n_lat_tiles):
    tok = pl.BlockSpec((TM, D_MODEL), lambda t, *_: (t, 0))
    mod = pl.BlockSpec((1, N_MOD, D_MODEL), lambda t, *_: (jnp.where(t >= n_lat_tiles, 1, 0), 0, 0))
    return tok, mod


def _full(shape):
    nd = len(shape)
    return pl.BlockSpec(shape, lambda *_: (0,) * nd)


_COUNT_SHAPE = jax.ShapeDtypeStruct((8, LANE), F32)
_COUNT_SPEC = pl.BlockSpec((8, LANE), lambda *_: (0, 0))


def _post_out(n_rows):
    shapes = (
        jax.ShapeDtypeStruct((n_rows, D_MODEL), F32),
        jax.ShapeDtypeStruct((n_rows, D_MODEL), F32),
        jax.ShapeDtypeStruct((n_rows, LANE), I32),
        jax.ShapeDtypeStruct((n_rows, LANE), F32),
        _COUNT_SHAPE,
    )
    specs = (
        pl.BlockSpec((TM, D_MODEL), lambda t, *_: (t, 0)),
        pl.BlockSpec((TM, D_MODEL), lambda t, *_: (t, 0)),
        pl.BlockSpec((TM, LANE), lambda t, *_: (t, 0)),
        pl.BlockSpec((TM, LANE), lambda t, *_: (t, 0)),
        _COUNT_SPEC,
    )
    return shapes, specs


def _router_operands(router_w, router_b):
    rw = jnp.zeros((D_MODEL, LANE), F32).at[:, :N_EXPERTS].set(router_w)
    rw_hi = rw.astype(BF16)
    rw_lo = (rw - rw_hi.astype(F32)).astype(BF16)
    rb = jnp.full((1, LANE), NEG, F32).at[0, :N_EXPERTS].set(router_b)
    return rw_hi, rw_lo, rb


def _proj_post_kernel(a_ref, w_ref, b_ref, x_ref, mod_ref, lng_ref, lnb_ref, rwh_ref, rwl_ref, rb_ref, *out_refs):
    y = _dot(a_ref[...], w_ref[...]) + b_ref[...]
    _post(y, x_ref[...], mod_ref[0], lng_ref[...], lnb_ref[...], rwh_ref[...], rwl_ref[...], rb_ref[...], out_refs)


def _proj_post(a, w_out, b_out, x, mods, lng, lnb, router, n_rows, n_lat_tiles):
    k = a.shape[1]
    tok, mod = _tile_specs(n_lat_tiles)
    shapes, specs = _post_out(n_rows)
    return pl.pallas_call(
        _proj_post_kernel,
        grid=(n_rows // TM,),
        in_specs=[
            pl.BlockSpec((TM, k), lambda t: (t, 0)),
            _full((k, D_MODEL)), _full((1, D_MODEL)),
            tok, mod, _full((1, D_MODEL)), _full((1, D_MODEL)),
            _full((D_MODEL, LANE)), _full((D_MODEL, LANE)), _full((1, LANE)),
        ],
        out_specs=specs,
        out_shape=shapes,
        compiler_params=_params("arbitrary"),
        name="proj_post",
    )(a, w_out.astype(BF16), b_out.reshape(1, -1), x, mods, lng.reshape(1, -1), lnb.reshape(1, -1), *router)


def _dft_mats(n):
    jk = np.outer(np.arange(n), np.arange(n)) % n
    ang = 2.0 * np.pi * jk / n
    out = []
    for m in (np.cos(ang), np.sin(ang)):
        m32 = jnp.asarray(m, F32)
        hi = m32.astype(BF16)
        out += [hi, (m32 - hi.astype(F32)).astype(BF16)]
    return out


def _channel_dft(h, cc, sc):
    cw = cc[0].shape[0]
    a_parts, b_parts = [], []
    for g in range(h.shape[1] // cw):
        hh, hl = _split(h[:, g * cw:(g + 1) * cw])
        a_parts.append(_dot3(hh, hl, cc[0][...], cc[1][...]))
        b_parts.append(_dot3(hh, hl, sc[0][...], sc[1][...]))
    return jnp.concatenate(a_parts, axis=1), jnp.concatenate(b_parts, axis=1)


def _fourier1_kernel(x_ref, mod_ref, cch_ref, ccl_ref, sch_ref, scl_ref, tc_ref, ts_ref, ur_ref, ui_ref):
    mod = mod_ref[0]
    h = x_ref[...] * (1.0 + mod[1:2]) + mod[0:1]
    a, b = _channel_dft(h, (cch_ref, ccl_ref), (sch_ref, scl_ref))
    tch, tcl = _split(tc_ref[0])
    tsh, tsl = _split(ts_ref[0])
    ah, al = _split(a)
    bh, bl = _split(b)
    ur_ref[...] = _dot3(tch, tcl, ah, al) - _dot3(tsh, tsl, bh, bl)
    ui_ref[...] = -(_dot3(tch, tcl, bh, bl) + _dot3(tsh, tsl, ah, al))


def _fourier2_kernel(ur_ref, ui_ref, c2h_ref, c2l_ref, s2h_ref, s2l_ref, w_ref, b_ref, x_ref, mod_ref,
                     lng_ref, lnb_ref, rwh_ref, rwl_ref, rb_ref, *out_refs, norm):
    urh, url = _split(ur_ref[0])
    uih, uil = _split(ui_ref[0])
    mixed = (_dot3(c2h_ref[...], c2l_ref[...], urh, url) + _dot3(s2h_ref[...], s2l_ref[...], uih, uil)) * norm
    y = _dot(mixed.astype(BF16), w_ref[...]) + b_ref[...]
    _post(y, x_ref[...], mod_ref[0], lng_ref[...], lnb_ref[...], rwh_ref[...], rwl_ref[...], rb_ref[...], out_refs)


def _fourier_ctx_kernel(x_ref, mod_ref, cch_ref, ccl_ref, sch_ref, scl_ref, cnh_ref, cnl_ref, snh_ref, snl_ref,
                        w_ref, b_ref, lng_ref, lnb_ref, rwh_ref, rwl_ref, rb_ref, *out_refs, norm):
    mod = mod_ref[0]
    x = x_ref[...]
    h = x * (1.0 + mod[1:2]) + mod[0:1]
    a, b = _channel_dft(h, (cch_ref, ccl_ref), (sch_ref, scl_ref))
    ah, al = _split(a)
    bh, bl = _split(b)
    mixed = (_dot3(cnh_ref[...], cnl_ref[...], ah, al) - _dot3(snh_ref[...], snl_ref[...], bh, bl)) * norm
    y = _dot(mixed.astype(BF16), w_ref[...]) + b_ref[...]
    _post(y, x, mod, lng_ref[...], lnb_ref[...], rwh_ref[...], rwl_ref[...], rb_ref[...], out_refs)


def _fourier_layer(x_all, mods, w_out, b_out, lng, lnb, router, n_lat, n_ctx):
    t_all, d = x_all.shape
    n2 = LANE
    n1 = n_lat // n2
    cw = d // FN_GROUPS
    rows = t_all // n2
    xv = x_all.reshape(rows, n2 * d)
    cmat = _dft_mats(cw)
    w_bf = w_out.astype(BF16)
    b2 = b_out.reshape(1, d)
    lng2, lnb2 = lng.reshape(1, d), lnb.reshape(1, d)

    k1 = jnp.arange(n1, dtype=I32)
    pos = jnp.arange(n1, dtype=I32)[None, None, :] * n2 + jnp.arange(n2, dtype=I32)[:, None, None]
    ang = ((k1[None, :, None] * pos) % n_lat).astype(F32) * (2.0 * math.pi / n_lat)
    tc, ts = jnp.cos(ang), jnp.sin(ang)

    mat = _full((cw, cw))
    ur, ui = pl.pallas_call(
        _fourier1_kernel,
        grid=(n2,),
        in_specs=[
            pl.BlockSpec((n1, d), lambda j: (0, j)),
            pl.BlockSpec((1, N_MOD, d), lambda j: (0, 0, 0)),
            mat, mat, mat, mat,
            pl.BlockSpec((1, n1, n1), lambda j: (j, 0, 0)),
            pl.BlockSpec((1, n1, n1), lambda j: (j, 0, 0)),
        ],
        out_specs=(pl.BlockSpec((n1, d), lambda j: (0, j)), pl.BlockSpec((n1, d), lambda j: (0, j))),
        out_shape=(jax.ShapeDtypeStruct((n1, n2 * d), F32), jax.ShapeDtypeStruct((n1, n2 * d), F32)),
        compiler_params=_params("arbitrary"),
        name="fourier1",
    )(xv, mods, *cmat, tc, ts)

    m2 = _dft_mats(n2)
    mat2 = _full((n2, n2))
    norm = 1.0 / math.sqrt(n_lat * cw)
    out_shapes = (
        jax.ShapeDtypeStruct((n1, n2 * d), F32),
        jax.ShapeDtypeStruct((n1, n2 * d), F32),
        jax.ShapeDtypeStruct((n1, n2 * LANE), I32),
        jax.ShapeDtypeStruct((n1, n2 * LANE), F32),
        _COUNT_SHAPE,
    )
    strided = pl.BlockSpec((n2, d), lambda k: (0, k))
    strided_l = pl.BlockSpec((n2, LANE), lambda k: (0, k))
    outs = pl.pallas_call(
        functools.partial(_fourier2_kernel, norm=norm),
        grid=(n1,),
        in_specs=[
            pl.BlockSpec((1, n2, d), lambda k: (k, 0, 0)),
            pl.BlockSpec((1, n2, d), lambda k: (k, 0, 0)),
            mat2, mat2, mat2, mat2,
            _full((d, d)), _full((1, d)),
            strided,
            pl.BlockSpec((1, N_MOD, d), lambda k: (0, 0, 0)),
            _full((1, d)), _full((1, d)),
            _full((d, LANE)), _full((d, LANE)), _full((1, LANE)),
        ],
        out_specs=(strided, strided, strided_l, strided_l, _COUNT_SPEC),
        out_shape=out_shapes,
        compiler_params=_params("arbitrary"),
        name="fourier2",
    )(ur.reshape(n1, n2, d), ui.reshape(n1, n2, d), *m2, w_bf, b2, xv, mods, lng2, lnb2, *router)
    lat = (outs[0].reshape(n_lat, d), outs[1].reshape(n_lat, d),
           outs[2].reshape(n_lat, LANE), outs[3].reshape(n_lat, LANE))

    cn = _dft_mats(n_ctx)
    matn = _full((n_ctx, n_ctx))
    shapes = (
        jax.ShapeDtypeStruct((n_ctx, d), F32), jax.ShapeDtypeStruct((n_ctx, d), F32),
        jax.ShapeDtypeStruct((n_ctx, LANE), I32), jax.ShapeDtypeStruct((n_ctx, LANE), F32), _COUNT_SHAPE,
    )
    ctx = pl.pallas_call(
        functools.partial(_fourier_ctx_kernel, norm=1.0 / math.sqrt(n_ctx * cw)),
        grid=(1,),
        in_specs=[
            pl.BlockSpec((n_ctx, d), lambda i: (n_lat // n_ctx, 0)),
            pl.BlockSpec((1, N_MOD, d), lambda i: (1, 0, 0)),
            mat, mat, mat, mat, matn, matn, matn, matn,
            _full((d, d)), _full((1, d)), _full((1, d)), _full((1, d)),
            _full((d, LANE)), _full((d, LANE)), _full((1, LANE)),
        ],
        out_specs=(_full((n_ctx, d)), _full((n_ctx, d)), _full((n_ctx, LANE)), _full((n_ctx, LANE)),
                   _COUNT_SPEC),
        out_shape=shapes,
        compiler_params=_params("arbitrary"),
        name="fourier_ctx",
    )(x_all, mods, *cmat, *cn, w_bf, b2, lng2, lnb2, *router)
    return tuple(jnp.concatenate([a, b], axis=0) for a, b in zip(lat, ctx[:4])) + (outs[4] + ctx[4],)


def _qkv_kernel(pos_ref, *refs, n_qk, rms):
    mod_ref, w_ref, b_ref, cos_ref, sin_ref, *rest = refs[N_COMBINE_REFS:]
    if rms:
        gain_ref, ind_ref, indt_ref, x_ref, q_ref, k_ref, v_ref, ybuf, sem = rest
    else:
        x_ref, q_ref, k_ref, v_ref, ybuf, sem = rest
    x = _combine_tile(pos_ref, *refs[:N_COMBINE_REFS], ybuf, sem)
    x_ref[...] = x
    mod = mod_ref[0]
    h = x * (1.0 + mod[1:2]) + mod[0:1]
    y = _dot(h.astype(BF16), w_ref[...]) + b_ref[...]
    qk = y[:, :n_qk]
    if rms:
        sh, sl = _split(qk * qk)
        ms = _dot(sh, ind_ref[...]) + _dot(sl, ind_ref[...])
        mh, ml = _split(ms)
        msb = _dot(mh, indt_ref[...]) + _dot(ml, indt_ref[...])
        qk = qk * lax.rsqrt(msb + RMS_EPS) * gain_ref[...]
    cos = cos_ref[...]
    sin = sin_ref[...]
    even = (lax.broadcasted_iota(I32, cos.shape, 1) & 1) == 0
    parts = []
    for c in range(n_qk // LANE):
        z = qk[:, c * LANE:(c + 1) * LANE]
        swapped = jnp.where(even, pltpu.roll(z, LANE - 1, 1), pltpu.roll(z, 1, 1))
        parts.append(z * cos + swapped * sin)
    nq = q_ref.shape[1]
    q_ref[...] = (jnp.concatenate(parts[:nq // LANE], axis=1) * Q_SCALE).astype(BF16)
    k_ref[...] = jnp.concatenate(parts[nq // LANE:], axis=1).astype(BF16)
    v_ref[...] = y[:, n_qk:].astype(BF16)


def _qkv(pending, mods, w_qkv, b_qkv, cos_t, sin_t, n_q, n_kv, n_lat, q_norm=None, k_norm=None):
    t_all, d = pending[3].shape
    nq, nk = n_q * HEAD_DIM, n_kv * HEAD_DIM
    n_qk, n_all = nq + nk, nq + 2 * nk
    rms = q_norm is not None
    pos, args, in_specs, scratch = _combine_operands(pending, TM, n_lat)
    row = lambda w: pl.BlockSpec((TM, w), lambda t, *_: (t, 0))
    in_specs += [_tile_specs(n_lat // TM)[1], _full((d, n_all)), _full((1, n_all)), row(LANE), row(LANE)]
    args += [mods, w_qkv.astype(BF16), b_qkv.reshape(1, n_all), cos_t, sin_t]
    if rms:
        gain = jnp.concatenate([jnp.tile(q_norm, n_q), jnp.tile(k_norm, n_kv)]).reshape(1, n_qk)
        head = np.arange(n_qk) // HEAD_DIM
        ind = np.zeros((n_qk, LANE), np.float32)
        ind[np.arange(n_qk), head] = 1.0 / HEAD_DIM
        indt = np.zeros((LANE, n_qk), np.float32)
        indt[head, np.arange(n_qk)] = 1.0
        in_specs += [_full((1, n_qk)), _full((n_qk, LANE)), _full((LANE, n_qk))]
        args += [gain, jnp.asarray(ind, BF16), jnp.asarray(indt, BF16)]
    grid_spec = pltpu.PrefetchScalarGridSpec(
        num_scalar_prefetch=1,
        grid=(t_all // TM,),
        in_specs=in_specs,
        out_specs=(row(d), row(nq), row(nk), row(nk)),
        scratch_shapes=scratch,
    )
    return pl.pallas_call(
        functools.partial(_qkv_kernel, n_qk=n_qk, rms=rms),
        grid_spec=grid_spec,
        out_shape=(jax.ShapeDtypeStruct((t_all, d), F32), jax.ShapeDtypeStruct((t_all, nq), BF16),
                   jax.ShapeDtypeStruct((t_all, nk), BF16), jax.ShapeDtypeStruct((t_all, nk), BF16)),
        compiler_params=_params("arbitrary"),
        name="qkv",
    )(pos, *args)


def _rope_tables(n_lat, n_ctx):
    rows = n_lat // GRID_W
    row = jnp.repeat(jnp.arange(rows, dtype=F32), GRID_W)
    col = jnp.tile(jnp.arange(GRID_W, dtype=F32), rows)
    n_freq = HEAD_DIM // 4
    inv = ROPE_THETA ** (-jnp.arange(n_freq, dtype=F32) / n_freq)
    ang = jnp.concatenate([row[:, None] * inv, col[:, None] * inv], axis=-1)
    ang = jnp.concatenate([ang, jnp.zeros((n_ctx, HEAD_DIM // 2), F32)], axis=0)
    cos = jnp.tile(jnp.repeat(jnp.cos(ang), 2, axis=1), (1, LANE // HEAD_DIM))
    sin = jnp.tile(jnp.repeat(jnp.sin(ang), 2, axis=1), (1, LANE // HEAD_DIM))
    sign = jnp.where(jnp.arange(LANE) % 2 == 0, -1.0, 1.0).astype(F32)
    return cos, sin * sign


def _kv_layouts(k, v, n_kv):
    t_all = k.shape[0]
    kh = k.reshape(t_all, n_kv, HEAD_DIM).transpose(1, 0, 2)
    vt = v.reshape(t_all, n_kv, HEAD_DIM).transpose(1, 2, 0)
    pad = jnp.zeros((n_kv, LANE - HEAD_DIM, t_all), BF16).at[:, 0, :].set(1.0)
    return kh, jnp.concatenate([vt, pad], axis=1)


def _flash_kernel(q_ref, k_ref, vt_ref, o_ref, s_even, s_odd, *, grp, tk, n_lat, n_ctx):
    tq = q_ref.shape[0]
    q_t = q_ref[...].astype(F32).T
    qt = jnp.concatenate([q_t[g * HEAD_DIM:(g + 1) * HEAD_DIM] for g in range(grp)], axis=1).astype(BF16)
    cols = qt.shape[1]
    is_ctx = pl.program_id(1) == n_lat // FLASH_TQ
    n_chunks = (n_lat + n_ctx) // tk

    def finish(acc):
        o = acc[:HEAD_DIM] / acc[HEAD_DIM:HEAD_DIM + 1]
        o_t = jnp.concatenate([o[:, g * tq:(g + 1) * tq] for g in range(grp)], axis=0)
        o_ref[...] = o_t.T.astype(BF16)

    groups = [slice(c * MXU_WIDTH, (c + 1) * MXU_WIDTH) for c in range(cols // MXU_WIDTH)]

    def scores(j, buf, g):
        off = pl.multiple_of(j * tk, tk)
        s = _dot(k_ref[0, pl.ds(off, tk), :], qt[:, g])
        buf[:, g] = s
        return jnp.max(s, axis=0, keepdims=True)

    def accumulate(j, buf, g, m, acc, mc):
        off = pl.multiple_of(j * tk, tk)
        m_new = jnp.maximum(m, mc)
        p = jnp.exp2(buf[:, g] - m_new)
        acc = jnp.exp2(m - m_new) * acc + _dot(vt_ref[0, :, pl.ds(off, tk)], p.astype(BF16))
        return m_new, acc

    @pl.when(jnp.logical_not(is_ctx))
    def _():
        def step(j, cur, nxt, carry):
            out = []
            for g, (m, acc, mc) in zip(groups, carry):
                mc_next = scores(j + 1, nxt, g)
                out.append(accumulate(j, cur, g, m, acc, mc) + (mc_next,))
            return out

        def body(jj, carry):
            for pair in range(FLASH_PAIRS):
                j = 2 * (FLASH_PAIRS * jj + pair)
                carry = step(j, s_even, s_odd, carry)
                carry = step(j + 1, s_odd, s_even, carry)
            return carry

        init = [(jnp.full((1, MXU_WIDTH), NEG, F32), jnp.zeros((LANE, MXU_WIDTH), F32), scores(0, s_even, g))
                for g in groups]
        carry = lax.fori_loop(0, (n_chunks - 1) // (2 * FLASH_PAIRS), body, init)
        finish(jnp.concatenate([accumulate(n_chunks - 1, s_even, g, m, acc, mc)[1]
                                for g, (m, acc, mc) in zip(groups, carry)], axis=1))

    @pl.when(is_ctx)
    def _():
        s = _dot(k_ref[0, n_lat:n_lat + n_ctx, :], qt)
        p = jnp.exp2(s - jnp.max(s, axis=0, keepdims=True))
        finish(_dot(vt_ref[0, :, n_lat:n_lat + n_ctx], p.astype(BF16)))


def _full_attention(q, k, v, n_lat, n_ctx):
    t_all, nq = q.shape
    n_kv = k.shape[1] // HEAD_DIM
    grp = nq // HEAD_DIM // n_kv
    nt = t_all // FLASH_TQ
    cols = grp * FLASH_TQ
    assert n_ctx == FLASH_TQ and t_all % FLASH_TK == 0 and (t_all // FLASH_TK - 1) % (2 * FLASH_PAIRS) == 0
    kh, vt = _kv_layouts(k, v, n_kv)
    return pl.pallas_call(
        functools.partial(_flash_kernel, grp=grp, tk=FLASH_TK, n_lat=n_lat, n_ctx=n_ctx),
        grid=(n_kv, nt),
        in_specs=[
            pl.BlockSpec((FLASH_TQ, grp * HEAD_DIM), lambda h, i: (i, h)),
            pl.BlockSpec((1, t_all, HEAD_DIM), lambda h, i: (h, 0, 0)),
            pl.BlockSpec((1, LANE, t_all), lambda h, i: (h, 0, 0)),
        ],
        out_specs=pl.BlockSpec((FLASH_TQ, grp * HEAD_DIM), lambda h, i: (i, h)),
        out_shape=jax.ShapeDtypeStruct((t_all, nq), BF16),
        scratch_shapes=[pltpu.VMEM((FLASH_TK, cols), F32), pltpu.VMEM((FLASH_TK, cols), F32)],
        compiler_params=_params("arbitrary", "arbitrary"),
        name="flash",
    )(q, kh, vt)


def _window_kernel(q_ref, kp_ref, kc_ref, kn_ref, kx_ref, vp_ref, vc_ref, vn_ref, vx_ref, sink_ref, o_ref,
                   *, grp, nb):
    i = pl.program_id(1)
    q_t = q_ref[...].astype(F32).T
    qt = jnp.concatenate([q_t[g * HEAD_DIM:(g + 1) * HEAD_DIM] for g in range(grp)], axis=1).astype(BF16)
    cols = qt.shape[1]
    s = _dot(jnp.concatenate([kp_ref[0], kc_ref[0], kn_ref[0], kx_ref[0]], axis=0), qt)
    kj = lax.broadcasted_iota(I32, (Q_BLOCK, cols), 0)
    qi = lax.broadcasted_iota(I32, (Q_BLOCK, cols), 1) & (Q_BLOCK - 1)
    sp = jnp.where(kj >= jnp.where(i > 0, qi, Q_BLOCK), s[:Q_BLOCK], NEG)
    sn = jnp.where(kj <= jnp.where(i < nb - 1, qi, -1), s[2 * Q_BLOCK:3 * Q_BLOCK], NEG)
    s = jnp.concatenate([sp, s[Q_BLOCK:2 * Q_BLOCK], sn, s[3 * Q_BLOCK:]], axis=0)
    sink = sink_ref[0]
    m = jnp.maximum(jnp.max(s, axis=0, keepdims=True), sink)
    p = jnp.exp2(s - m).astype(BF16)
    acc = _dot(jnp.concatenate([vp_ref[0], vc_ref[0], vn_ref[0], vx_ref[0]], axis=1), p)
    o = acc[:HEAD_DIM] / (acc[HEAD_DIM:HEAD_DIM + 1] + jnp.exp2(sink - m))
    o_t = jnp.concatenate([o[:, g * Q_BLOCK:(g + 1) * Q_BLOCK] for g in range(grp)], axis=0)
    o_ref[...] = o_t.T.astype(BF16)


def _window_attention(q, k, v, sink, n_lat, n_ctx):
    nq = q.shape[1]
    n_kv = k.shape[1] // HEAD_DIM
    grp = nq // HEAD_DIM // n_kv
    gw = grp * HEAD_DIM
    nb = n_lat // Q_BLOCK
    cb = n_lat // n_ctx
    kh, vt = _kv_layouts(k, v, n_kv)
    sink_cols = jnp.repeat(sink.reshape(n_kv, 1, grp) * LOG2E, Q_BLOCK, axis=2)
    kspec = lambda f: pl.BlockSpec((1, Q_BLOCK, HEAD_DIM), lambda h, i: (h, f(i), 0))
    vspec = lambda f: pl.BlockSpec((1, LANE, Q_BLOCK), lambda h, i: (h, 0, f(i)))
    prev = lambda i: jnp.maximum(i - 1, 0)
    cur = lambda i: i
    nxt = lambda i: jnp.minimum(i + 1, nb - 1)
    return pl.pallas_call(
        functools.partial(_window_kernel, grp=grp, nb=nb),
        grid=(n_kv, nb),
        in_specs=[
            pl.BlockSpec((Q_BLOCK, gw), lambda h, i: (i, h)),
            kspec(prev), kspec(cur), kspec(nxt),
            pl.BlockSpec((1, n_ctx, HEAD_DIM), lambda h, i: (h, cb, 0)),
            vspec(prev), vspec(cur), vspec(nxt),
            pl.BlockSpec((1, LANE, n_ctx), lambda h, i: (h, 0, cb)),
            pl.BlockSpec((1, 1, grp * Q_BLOCK), lambda h, i: (h, 0, 0)),
        ],
        out_specs=pl.BlockSpec((Q_BLOCK, gw), lambda h, i: (i, h)),
        out_shape=jax.ShapeDtypeStruct((n_lat, nq), BF16),
        compiler_params=_params("arbitrary", "arbitrary"),
        name="window",
    )(q, kh, kh, kh, kh, vt, vt, vt, vt, sink_cols)


def _gmlp_kernel(pos_ref, *refs):
    (mod_ref, win_ref, bin_ref, vg_ref, vb_ref, ws_ref, bs_ref, wout_ref, bout_ref,
     lng_ref, lnb_ref, rwh_ref, rwl_ref, rb_ref, *rest) = refs[N_COMBINE_REFS:]
    out_refs, (ybuf, sem) = rest[:-2], rest[-2:]
    mod = mod_ref[0]
    x = _combine_tile(pos_ref, *refs[:N_COMBINE_REFS], ybuf, sem)
    h = x * (1.0 + mod[1:2]) + mod[0:1]
    z = _dot(h.astype(BF16), win_ref[...]) + bin_ref[...]
    z = 0.5 * z * (1.0 + lax.erf(z * (2.0 ** -0.5)))
    half = z.shape[1] // 2
    u = z[:, :half]
    v = _layer_norm(z[:, half:], vg_ref[...], vb_ref[...]).astype(BF16)
    cw = half // GM_GROUPS
    chunks = []
    for c in range(x.shape[0] // GM_CHUNK):
        vc = v[c * GM_CHUNK:(c + 1) * GM_CHUNK]
        chunks.append(jnp.concatenate(
            [_dot(ws_ref[g], vc[:, g * cw:(g + 1) * cw]) + bs_ref[g] for g in range(GM_GROUPS)], axis=1))
    gated = u * jnp.concatenate(chunks, axis=0)
    y = _dot(gated.astype(BF16), wout_ref[...]) + bout_ref[...]
    _post(y, x, mod, lng_ref[...], lnb_ref[...], rwh_ref[...], rwl_ref[...], rb_ref[...], out_refs)


def _gmlp_layer(pending, mods, w_in, b_in, vg, vb, w_s, b_s, w_out, b_out, lng, lnb, router, n_lat):
    t_all, d = pending[3].shape
    dffn = w_in.shape[1]
    half = dffn // 2
    cw = half // GM_GROUPS
    shapes, specs = _post_out(t_all)
    bs_full = jnp.broadcast_to(b_s[:, :, None], (GM_GROUPS, GM_CHUNK, cw))
    pos, args, in_specs, scratch = _combine_operands(pending, TM, n_lat)
    in_specs += [
        _tile_specs(n_lat // TM)[1], _full((d, dffn)), _full((1, dffn)), _full((1, half)), _full((1, half)),
        _full((GM_GROUPS, GM_CHUNK, GM_CHUNK)), _full((GM_GROUPS, GM_CHUNK, cw)),
        _full((half, d)), _full((1, d)), _full((1, d)), _full((1, d)),
        _full((d, LANE)), _full((d, LANE)), _full((1, LANE)),
    ]
    args += [mods, w_in.astype(BF16), b_in.reshape(1, dffn), vg.reshape(1, half), vb.reshape(1, half),
             w_s.astype(BF16), bs_full, w_out.astype(BF16), b_out.reshape(1, d), lng.reshape(1, d),
             lnb.reshape(1, d), *router]
    grid_spec = pltpu.PrefetchScalarGridSpec(
        num_scalar_prefetch=1,
        grid=(t_all // TM,),
        in_specs=in_specs,
        out_specs=specs,
        scratch_shapes=scratch,
    )
    return pl.pallas_call(
        _gmlp_kernel,
        grid_spec=grid_spec,
        out_shape=shapes,
        compiler_params=_params("arbitrary"),
        name="gmlp",
    )(pos, *args)


COMB_TM = 128


def _route_meta(count_blk, n_tok):
    n_assign = n_tok * TOP_K
    n_blocks = -(-n_assign // MOE_BLOCK) + N_EXPERTS
    ids = jnp.arange(N_EXPERTS, dtype=I32)
    counts = count_blk[0, :N_EXPERTS].astype(I32)
    padded = (counts + MOE_BLOCK - 1) // MOE_BLOCK * MOE_BLOCK
    ends_pad = jnp.cumsum(padded)
    base = ends_pad - padded
    starts = jnp.arange(n_blocks, dtype=I32) * MOE_BLOCK
    block_expert = jnp.minimum(jnp.sum((ends_pad[None, :] <= starts[:, None]).astype(I32), axis=1), N_EXPERTS - 1)
    last_blk = jnp.where(padded > 0, ends_pad - MOE_BLOCK, -1)
    tail = ends_pad[-1] + ids * MOE_BLOCK
    tail = jnp.where(tail < n_blocks * MOE_BLOCK, tail, -1)
    base_b = jnp.broadcast_to(base.astype(F32)[:, None], (N_EXPERTS, LANE))
    return block_expert.astype(I32), base_b, jnp.concatenate([last_blk, tail]).astype(I32), n_blocks


def _rank_kernel(idx_ref, base_ref, upper_ref, dest_ref, run_ref):
    @pl.when(pl.program_id(0) == 0)
    def _():
        run_ref[...] = jnp.zeros(run_ref.shape, F32)

    eid = lax.broadcasted_iota(I32, (N_EXPERTS, TM), 0)
    onehots = [(eid == idx_ref[k:k + 1, :]).astype(F32) for k in range(TOP_K)]
    cnt = onehots[0] + onehots[1] + onehots[2] + onehots[3]
    before = _dot(cnt.astype(BF16), upper_ref[...])
    slot = base_ref[:, :1] + run_ref[:, :1] + before
    for k in range(TOP_K):
        rows = (jnp.sum(onehots[k] * slot, axis=0, keepdims=True) * ROW_TILE).astype(I32)
        for j in range(TM // COMB_TM):
            dest_ref[j, k:k + 1, :] = rows[:, j * COMB_TM:(j + 1) * COMB_TM]
    run_ref[...] = run_ref[...] + jnp.sum(cnt, axis=1, keepdims=True)


def _rank(top_idx, base_b, n_tok):
    per_step = TM // COMB_TM
    idx_t = top_idx[:n_tok, :TOP_K].T
    upper = jnp.asarray(np.triu(np.ones((TM, TM), np.float32), 1), BF16)
    return pl.pallas_call(
        _rank_kernel,
        grid=(n_tok // TM,),
        in_specs=[pl.BlockSpec((TOP_K, TM), lambda t: (0, t)), _full((N_EXPERTS, LANE)), _full((TM, TM))],
        out_specs=pl.BlockSpec((per_step, TOP_K, COMB_TM), lambda t: (t, 0, 0)),
        out_shape=jax.ShapeDtypeStruct((n_tok // COMB_TM, TOP_K, COMB_TM), I32),
        scratch_shapes=[pltpu.VMEM((N_EXPERTS, LANE), F32)],
        compiler_params=_params("arbitrary"),
        name="rank",
    )(idx_t, base_b, upper)


def _dispatch_kernel(dest_ref, zrow_ref, h_ref, xs_hbm, hbuf, zbuf, sem, zsem, *, n_tiles):
    t = pl.program_id(0)
    slot = t % 2
    n_zero = 2 * N_EXPERTS

    def zero_copy(j):
        row = pl.multiple_of(zrow_ref[j] * ROW_TILE, MOE_BLOCK * ROW_TILE)
        return pltpu.make_async_copy(zbuf, xs_hbm.at[pl.ds(row, MOE_BLOCK * ROW_TILE), :], zsem.at[0])

    def wait_rows(s):
        for _ in range(TOP_K):
            pltpu.make_async_copy(hbuf.at[s], xs_hbm.at[pl.ds(0, COMB_TM * ROW_TILE), :], sem.at[s]).wait()

    @pl.when(t == 0)
    def _():
        zbuf[...] = jnp.zeros(zbuf.shape, F32)
        for j in range(n_zero):
            @pl.when(zrow_ref[j] >= 0)
            def _():
                zero_copy(j).start()
        for j in range(n_zero):
            @pl.when(zrow_ref[j] >= 0)
            def _():
                zero_copy(j).wait()

    @pl.when(t >= 2)
    def _():
        wait_rows(slot)

    _to_row_tiles(hbuf.at[slot], h_ref[...])
    for k in range(TOP_K):
        for r in range(COMB_TM):
            row = pl.multiple_of(dest_ref[(t * TOP_K + k) * COMB_TM + r], ROW_TILE)
            pltpu.make_async_copy(hbuf.at[slot, pl.ds(r * ROW_TILE, ROW_TILE), :],
                                  xs_hbm.at[pl.ds(row, ROW_TILE), :], sem.at[slot]).start(priority=r % 2)

    @pl.when(t == n_tiles - 1)
    def _():
        wait_rows(slot)
        if n_tiles > 1:
            wait_rows(1 - slot)


def _dispatch(h2, dest, zrows, n_tok, n_blocks):
    d = h2.shape[1]
    n_tiles = n_tok // COMB_TM
    grid_spec = pltpu.PrefetchScalarGridSpec(
        num_scalar_prefetch=2,
        grid=(n_tiles,),
        in_specs=[pl.BlockSpec((COMB_TM, d), lambda t, dest, zr: (t, 0))],
        out_specs=pl.BlockSpec(memory_space=pl.ANY),
        scratch_shapes=[pltpu.VMEM((2, COMB_TM * ROW_TILE, LANE), F32),
                        pltpu.VMEM((MOE_BLOCK * ROW_TILE, LANE), F32),
                        pltpu.SemaphoreType.DMA((2,)), pltpu.SemaphoreType.DMA((1,))],
    )
    return pl.pallas_call(
        functools.partial(_dispatch_kernel, n_tiles=n_tiles),
        grid_spec=grid_spec,
        out_shape=jax.ShapeDtypeStruct((n_blocks * MOE_BLOCK * ROW_TILE, LANE), F32),
        compiler_params=_params("arbitrary"),
        name="dispatch",
    )(dest.reshape(-1), zrows, h2)


EXP_CHUNK = MXU_WIDTH


def _expert_kernel(be_ref, x_ref, wgu_ref, bgu_ref, wd_ref, bd_ref, sel_ref, y_ref,
                   wg_s, wl_s, wd_s, bg_s, bl_s):
    b = pl.program_id(0)
    ff = wg_s.shape[1]
    half = EXP_CHUNK // 2

    @pl.when(jnp.logical_or(b == 0, be_ref[b] != be_ref[jnp.maximum(b - 1, 0)]))
    def _():
        for c in range(2 * ff // EXP_CHUNK):
            w = _dot(wgu_ref[0, 0, :, c * EXP_CHUNK:(c + 1) * EXP_CHUNK].astype(BF16), sel_ref[...])
            wg_s[:, c * half:(c + 1) * half] = w[:, :half].astype(BF16)
            wl_s[:, c * half:(c + 1) * half] = w[:, half:].astype(BF16)
            bh, bl = _split(jnp.broadcast_to(bgu_ref[0, 0, :, c * EXP_CHUNK:(c + 1) * EXP_CHUNK], (8, EXP_CHUNK)))
            bias = _dot(bh, sel_ref[...]) + _dot(bl, sel_ref[...])
            bg_s[:, c * half:(c + 1) * half] = bias[:, :half]
            bl_s[:, c * half:(c + 1) * half] = bias[:, half:]
        wd_s[...] = wd_ref[0, 0].astype(BF16)

    x = _from_row_tiles(x_ref, 0, MOE_BLOCK).astype(BF16)
    glu = jnp.minimum(_dot(x, wg_s[...]) + bg_s[0:1, :], SWIGLU_LIMIT)
    lin = jnp.clip(_dot(x, wl_s[...]) + bl_s[0:1, :], -SWIGLU_LIMIT, SWIGLU_LIMIT)
    act = glu * (1.0 / (1.0 + jnp.exp(-SWIGLU_ALPHA * glu))) * (lin + 1.0)
    _to_row_tiles(y_ref, _dot(act.astype(BF16), wd_s[...]) + bd_ref[0, 0])


def _experts(xs, block_expert, n_blocks, w_gate_up, b_gate_up, w_down, b_down, layer):
    d = w_down.shape[3]
    ff = w_down.shape[2]
    half = EXP_CHUNK // 2
    sel = np.zeros((EXP_CHUNK, EXP_CHUNK), np.float32)
    sel[2 * np.arange(half), np.arange(half)] = 1.0
    sel[2 * np.arange(half) + 1, half + np.arange(half)] = 1.0
    wspec = lambda r, c: pl.BlockSpec((1, 1, r, c), lambda b, be: (layer, be[b], 0, 0))
    sspec = pl.BlockSpec((EXP_CHUNK, EXP_CHUNK), lambda b, be: (0, 0))
    grid_spec = pltpu.PrefetchScalarGridSpec(
        num_scalar_prefetch=1,
        grid=(n_blocks,),
        in_specs=[pl.BlockSpec((MOE_BLOCK * ROW_TILE, LANE), lambda b, be: (b, 0)), wspec(d, 2 * ff),
                  wspec(1, 2 * ff), wspec(ff, d), wspec(1, d), sspec],
        out_specs=pl.BlockSpec((MOE_BLOCK * ROW_TILE, LANE), lambda b, be: (b, 0)),
        scratch_shapes=[pltpu.VMEM((d, ff), BF16), pltpu.VMEM((d, ff), BF16), pltpu.VMEM((ff, d), BF16),
                        pltpu.VMEM((8, ff), F32), pltpu.VMEM((8, ff), F32)],
    )
    return pl.pallas_call(
        _expert_kernel,
        grid_spec=grid_spec,
        out_shape=jax.ShapeDtypeStruct((n_blocks * MOE_BLOCK * ROW_TILE, LANE), F32),
        compiler_params=_params("arbitrary"),
        name="experts",
    )(block_expert, xs, w_gate_up, b_gate_up[:, :, None, :], w_down, b_down[:, :, None, :],
      jnp.asarray(sel, BF16))


def _gather_rows(idx_ref, base, n, src_hbm, dst, sem):
    for r in range(n):
        row = pl.multiple_of(idx_ref[base + r], ROW_TILE)
        pltpu.make_async_copy(src_hbm.at[pl.ds(row, ROW_TILE), :], dst.at[pl.ds(r * ROW_TILE, ROW_TILE), :],
                              sem).start(priority=r % 2)


N_COMBINE_REFS = 6


def _combine_tile(pos_ref, y_hbm, gate_ref, x1_ref, mod_ref, lng_ref, lnb_ref, ybuf, sem):
    t = pl.program_id(0)
    slot = t % 2
    tm, d = x1_ref.shape
    n = tm * TOP_K

    @pl.when(t == 0)
    def _():
        _gather_rows(pos_ref, 0, n, y_hbm, ybuf.at[0], sem.at[0])

    @pl.when(t + 1 < pl.num_programs(0))
    def _():
        _gather_rows(pos_ref, (t + 1) * n, n, y_hbm, ybuf.at[1 - slot], sem.at[1 - slot])

    pltpu.make_async_copy(y_hbm.at[pl.ds(0, n * ROW_TILE), :], ybuf.at[slot], sem.at[slot]).wait()
    gates = gate_ref[...]
    parts = []
    for j in range(tm // COMB_TM):
        f = jnp.zeros((COMB_TM, d), F32)
        for k in range(TOP_K):
            rows = _from_row_tiles(ybuf.at[slot], (j * TOP_K + k) * COMB_TM, COMB_TM)
            f = f + gates[j * COMB_TM:(j + 1) * COMB_TM, k:k + 1] * rows
        parts.append(f)
    f = parts[0] if len(parts) == 1 else jnp.concatenate(parts, axis=0)
    return _layer_norm(ALPHA * x1_ref[...] + mod_ref[0][5:6] * f, lng_ref[...], lnb_ref[...])


def _combine_operands(pending, tm, n_lat):
    yb, dest, gates, x1, mods, lng, lnb = pending
    d = x1.shape[1]
    n_lat_tiles = n_lat // tm
    specs = [
        pl.BlockSpec(memory_space=pl.ANY),
        pl.BlockSpec((tm, LANE), lambda t, *_: (t, 0)),
        pl.BlockSpec((tm, d), lambda t, *_: (t, 0)),
        pl.BlockSpec((1, N_MOD, d), lambda t, *_: (jnp.where(t >= n_lat_tiles, 1, 0), 0, 0)),
        _full((1, d)), _full((1, d)),
    ]
    scratch = [pltpu.VMEM((2, tm * TOP_K * ROW_TILE, LANE), F32), pltpu.SemaphoreType.DMA((2,))]
    return dest.reshape(-1), [yb, gates, x1, mods, lng.reshape(1, d), lnb.reshape(1, d)], specs, scratch


def _combine_kernel(pos_ref, *refs):
    o_ref, ybuf, sem = refs[N_COMBINE_REFS:]
    o_ref[...] = _combine_tile(pos_ref, *refs[:N_COMBINE_REFS], ybuf, sem)


def _combine(pending, n_tok, n_lat):
    d = pending[3].shape[1]
    pos, args, specs, scratch = _combine_operands(pending, COMB_TM, n_lat)
    grid_spec = pltpu.PrefetchScalarGridSpec(
        num_scalar_prefetch=1,
        grid=(n_tok // COMB_TM,),
        in_specs=specs,
        out_specs=pl.BlockSpec((COMB_TM, d), lambda t, *_: (t, 0)),
        scratch_shapes=scratch,
    )
    return pl.pallas_call(
        _combine_kernel,
        grid_spec=grid_spec,
        out_shape=jax.ShapeDtypeStruct((n_tok, d), F32),
        compiler_params=_params("arbitrary"),
        name="combine",
    )(pos, *args)


def _moe_layer(x1, h2, top_idx, gates, counts, mods, lng, lnb, w_gate_up, b_gate_up, w_down, b_down, layer, n_tok):
    block_expert, base_b, zrows, n_blocks = _route_meta(counts, n_tok)
    dest = _rank(top_idx, base_b, n_tok)
    xs = _dispatch(h2, dest, zrows, n_tok, n_blocks)
    yb = _experts(xs, block_expert, n_blocks, w_gate_up, b_gate_up, w_down, b_down, layer)
    return yb, dest, gates, x1, mods, lng, lnb


def kernel(x, c, ctx, c_ctx, ada_w, ada_b, ln_mix_g, ln_mix_b, ln_ffn_g, ln_ffn_b, fn_w_out, fn_b_out, fa_w_qkv, fa_b_qkv, fa_q_norm, fa_k_norm, fa_w_out, fa_b_out, gm_w_in, gm_b_in, gm_v_norm_g, gm_v_norm_b, gm_w_s, gm_b_s, gm_w_out, gm_b_out, wa_w_qkv, wa_b_qkv, wa_sink, wa_w_out, wa_b_out, router_w, router_b, exp_w_gate_up, exp_b_gate_up, exp_w_down, exp_b_down):
    bsz, n_lat, d = x.shape
    n_ctx = ctx.shape[1]
    assert bsz == 1 and d == D_MODEL and n_lat == LANE * LANE and n_lat % n_ctx == 0 and n_ctx % TM == 0
    t_all = n_lat + n_ctx
    n_lat_tiles = n_lat // TM
    x_all = jnp.concatenate([x[0], ctx[0]], axis=0)
    mods_all = _ada(c, c_ctx, ada_w, ada_b)
    cos_t, sin_t = _rope_tables(n_lat, n_ctx)

    for i in range(DEPTH):
        kind, j = i % 4, i // 4
        last = i == DEPTH - 1
        n_tok = n_lat if last else t_all
        mods = mods_all[i]
        router = _router_operands(router_w[i], router_b[i])
        lng, lnb = ln_mix_g[i], ln_mix_b[i]
        if kind == 0:
            post = _fourier_layer(x_all, mods, fn_w_out[j], fn_b_out[j], lng, lnb, router, n_lat, n_ctx)
        elif kind == 1:
            x_all, q, k, v = _qkv(pending, mods, fa_w_qkv[j], fa_b_qkv[j], cos_t, sin_t, FA_Q_HEADS,
                                  FA_KV_HEADS, n_lat, fa_q_norm[j], fa_k_norm[j])
            o = _full_attention(q, k, v, n_lat, n_ctx)
            post = _proj_post(o, fa_w_out[j], fa_b_out[j], x_all, mods, lng, lnb, router, n_tok, n_lat_tiles)
        elif kind == 2:
            post = _gmlp_layer(pending, mods, gm_w_in[j], gm_b_in[j], gm_v_norm_g[j], gm_v_norm_b[j],
                               gm_w_s[j], gm_b_s[j], gm_w_out[j], gm_b_out[j], lng, lnb, router, n_lat)
        else:
            x_all, q, k, v = _qkv(pending, mods, wa_w_qkv[j], wa_b_qkv[j], cos_t, sin_t, WA_Q_HEADS,
                                  WA_KV_HEADS, n_lat)
            o = _window_attention(q, k, v, wa_sink[j], n_lat, n_ctx)
            post = _proj_post(o, wa_w_out[j], wa_b_out[j], x_all, mods, lng, lnb, router, n_tok, n_lat_tiles)
        pending = _moe_layer(*post, mods, ln_ffn_g[i], ln_ffn_b[i], exp_w_gate_up, exp_b_gate_up, exp_w_down,
                             exp_b_down, i, n_tok)
    return _combine(pending, n_lat, n_lat)[None]
```

```python
import functools
import math

import numpy as np
import jax
import jax.numpy as jnp
from jax import lax
from jax.experimental import pallas as pl
from jax.experimental.pallas import tpu as pltpu

F32, BF16, I32 = jnp.float32, jnp.bfloat16, jnp.int32

D_MODEL = 1024
DEPTH = 4
GRID_W = 64
N_MOD = 6
FN_GROUPS = 4
HEAD_DIM = 64
FA_Q_HEADS, FA_KV_HEADS = 16, 4
WA_Q_HEADS, WA_KV_HEADS = 16, 2
WINDOW = 128
Q_BLOCK = 128
ROPE_THETA = 10000.0
GM_CHUNK = 128
GM_GROUPS = 8
N_EXPERTS = 32
TOP_K = 4
SWIGLU_LIMIT = 7.0
SWIGLU_ALPHA = 1.702
MOE_BLOCK = 256
LN_EPS = 1e-5
RMS_EPS = 1e-6
NEG = -1e30
ALPHA = (2 * DEPTH) ** 0.25
LOG2E = math.log2(math.e)
Q_SCALE = HEAD_DIM ** -0.5 * LOG2E

LANE = 128
ROW_TILE = D_MODEL // LANE
MXU_WIDTH = 256
TM = 256
FLASH_TQ = 256
FLASH_TK = 1280
FLASH_PAIRS = 3
VMEM_LIMIT = 56 * 2 ** 20


def _params(*sem):
    return pltpu.CompilerParams(dimension_semantics=sem, vmem_limit_bytes=VMEM_LIMIT)


def _dot(a, b):
    return jnp.dot(a, b, preferred_element_type=F32)


def _split(a):
    hi = a.astype(BF16)
    lo = (a - hi.astype(F32)).astype(BF16)
    return hi, lo


def _dot3(a_hi, a_lo, b_hi, b_lo):
    return _dot(a_hi, b_hi) + (_dot(a_hi, b_lo) + _dot(a_lo, b_hi))


def _layer_norm(x, g, b):
    mu = jnp.mean(x, axis=-1, keepdims=True)
    xc = x - mu
    var = jnp.mean(xc * xc, axis=-1, keepdims=True)
    return xc * lax.rsqrt(var + LN_EPS) * g + b


def _top4(logits):
    lane = lax.broadcasted_iota(I32, logits.shape, 1).astype(F32)
    cur = logits
    vals, idxs = [], []
    for _ in range(TOP_K):
        m = jnp.max(cur, axis=-1, keepdims=True)
        i = jnp.min(jnp.where(cur == m, lane, float(LANE)), axis=-1, keepdims=True)
        vals.append(m)
        idxs.append(i)
        cur = jnp.where(lane == i, -jnp.inf, cur)
    exps = [jnp.exp(v - vals[0]) for v in vals]
    inv = 1.0 / (exps[0] + exps[1] + exps[2] + exps[3])
    idx_out = jnp.zeros_like(logits)
    gate_out = jnp.zeros_like(logits)
    picked = jnp.zeros_like(logits)
    for k in range(TOP_K):
        idx_out = jnp.where(lane == float(k), idxs[k], idx_out)
        gate_out = jnp.where(lane == float(k), exps[k] * inv, gate_out)
        picked = picked + jnp.where(lane == idxs[k], 1.0, 0.0)
    return idx_out.astype(I32), gate_out, jnp.sum(picked, axis=0, keepdims=True)


def _post(y, x, mod, lng, lnb, rw_hi, rw_lo, rb, out_refs):
    x1_ref, h2_ref, idx_ref, gate_ref, cnt_ref = out_refs
    x1 = _layer_norm(ALPHA * x + mod[2:3] * y, lng, lnb)
    h2 = x1 * (1.0 + mod[4:5]) + mod[3:4]
    hh, hl = _split(h2)
    logits = _dot3(hh, hl, rw_hi, rw_lo) + rb
    idx, gates, cnt = _top4(logits)
    x1_ref[...] = x1
    h2_ref[...] = h2
    idx_ref[...] = idx
    gate_ref[...] = gates

    @pl.when(pl.program_id(0) == 0)
    def _():
        cnt_ref[...] = jnp.zeros(cnt_ref.shape, F32)

    cnt_ref[...] = cnt_ref[...] + cnt


def _to_row_tiles(ref, x):
    n = x.shape[0]
    for s in range(ROW_TILE):
        ref[pl.ds(s, n, stride=ROW_TILE), :] = x[:, s * LANE:(s + 1) * LANE]


def _from_row_tiles(ref, start, n):
    return jnp.concatenate([ref[pl.ds(start * ROW_TILE + s, n, stride=ROW_TILE), :] for s in range(ROW_TILE)],
                           axis=1)


def _ada_kernel(cs_ref, w_ref, b_ref, o_ref):
    cs = cs_ref[...]
    s = cs * (1.0 / (1.0 + jnp.exp(-cs)))
    sh, sl = _split(s)
    wh, wl = _split(w_ref[0])
    o_ref[0] = _dot3(sh, sl, wh, wl) + b_ref[0]


def _ada(c, c_ctx, ada_w, ada_b):
    d = c.shape[-1]
    nm = ada_w.shape[-1]
    tn = nm // 4
    cs = jnp.zeros((8, d), F32).at[0].set(c[0]).at[1].set(c_ctx)
    out = pl.pallas_call(
        _ada_kernel,
        grid=(DEPTH, nm // tn),
        in_specs=[
            pl.BlockSpec((8, d), lambda i, j: (0, 0)),
            pl.BlockSpec((1, d, tn), lambda i, j: (i, 0, j)),
            pl.BlockSpec((1, 1, tn), lambda i, j: (i, 0, j)),
        ],
        out_specs=pl.BlockSpec((1, 8, tn), lambda i, j: (i, 0, j)),
        out_shape=jax.ShapeDtypeStruct((DEPTH, 8, nm), F32),
        compiler_params=_params("arbitrary", "arbitrary"),
        name="ada",
    )(cs, ada_w, ada_b.reshape(DEPTH, 1, nm))
    return out[:, :2].reshape(DEPTH, 2, N_MOD, d)


def _tile_specs(n_lat_tiles):
    tok = pl.BlockSpec((TM, D_MODEL), lambda t, *_: (t, 0))
    mod = pl.BlockSpec((1, N_MOD, D_MODEL), lambda t, *_: (jnp.where(t >= n_lat_tiles, 1, 0), 0, 0))
    return tok, mod


def _full(shape):
    nd = len(shape)
    return pl.BlockSpec(shape, lambda *_: (0,) * nd)


_COUNT_SHAPE = jax.ShapeDtypeStruct((8, LANE), F32)
_COUNT_SPEC = pl.BlockSpec((8, LANE), lambda *_: (0, 0))


def _post_out(n_rows):
    shapes = (
        jax.ShapeDtypeStruct((n_rows, D_MODEL), F32),
        jax.ShapeDtypeStruct((n_rows, D_MODEL), F32),
        jax.ShapeDtypeStruct((n_rows, LANE), I32),
        jax.ShapeDtypeStruct((n_rows, LANE), F32),
        _COUNT_SHAPE,
    )
    specs = (
        pl.BlockSpec((TM, D_MODEL), lambda t, *_: (t, 0)),
        pl.BlockSpec((TM, D_MODEL), lambda t, *_: (t, 0)),
        pl.BlockSpec((TM, LANE), lambda t, *_: (t, 0)),
        pl.BlockSpec((TM, LANE), lambda t, *_: (t, 0)),
        _COUNT_SPEC,
    )
    return shapes, specs


def _router_operands(router_w, router_b):
    rw = jnp.zeros((D_MODEL, LANE), F32).at[:, :N_EXPERTS].set(router_w)
    rw_hi = rw.astype(BF16)
    rw_lo = (rw - rw_hi.astype(F32)).astype(BF16)
    rb = jnp.full((1, LANE), NEG, F32).at[0, :N_EXPERTS].set(router_b)
    return rw_hi, rw_lo, rb


def _proj_post_kernel(a_ref, w_ref, b_ref, x_ref, mod_ref, lng_ref, lnb_ref, rwh_ref, rwl_ref, rb_ref, *out_refs):
    y = _dot(a_ref[...], w_ref[...]) + b_ref[...]
    _post(y, x_ref[...], mod_ref[0], lng_ref[...], lnb_ref[...], rwh_ref[...], rwl_ref[...], rb_ref[...], out_refs)


def _proj_post(a, w_out, b_out, x, mods, lng, lnb, router, n_rows, n_lat_tiles):
    k = a.shape[1]
    tok, mod = _tile_specs(n_lat_tiles)
    shapes, specs = _post_out(n_rows)
    return pl.pallas_call(
        _proj_post_kernel,
        grid=(n_rows // TM,),
        in_specs=[
            pl.BlockSpec((TM, k), lambda t: (t, 0)),
            _full((k, D_MODEL)), _full((1, D_MODEL)),
            tok, mod, _full((1, D_MODEL)), _full((1, D_MODEL)),
            _full((D_MODEL, LANE)), _full((D_MODEL, LANE)), _full((1, LANE)),
        ],
        out_specs=specs,
        out_shape=shapes,
        compiler_params=_params("arbitrary"),
        name="proj_post",
    )(a, w_out.astype(BF16), b_out.reshape(1, -1), x, mods, lng.reshape(1, -1), lnb.reshape(1, -1), *router)


def _dft_mats(n):
    jk = np.outer(np.arange(n), np.arange(n)) % n
    ang = 2.0 * np.pi * jk / n
    out = []
    for m in (np.cos(ang), np.sin(ang)):
        m32 = jnp.asarray(m, F32)
        hi = m32.astype(BF16)
        out += [hi, (m32 - hi.astype(F32)).astype(BF16)]
    return out


def _channel_dft(h, cc, sc):
    cw = cc[0].shape[0]
    a_parts, b_parts = [], []
    for g in range(h.shape[1] // cw):
        hh, hl = _split(h[:, g * cw:(g + 1) * cw])
        a_parts.append(_dot3(hh, hl, cc[0][...], cc[1][...]))
        b_parts.append(_dot3(hh, hl, sc[0][...], sc[1][...]))
    return jnp.concatenate(a_parts, axis=1), jnp.concatenate(b_parts, axis=1)


def _fourier1_kernel(x_ref, mod_ref, cch_ref, ccl_ref, sch_ref, scl_ref, tc_ref, ts_ref, ur_ref, ui_ref):
    mod = mod_ref[0]
    h = x_ref[...] * (1.0 + mod[1:2]) + mod[0:1]
    a, b = _channel_dft(h, (cch_ref, ccl_ref), (sch_ref, scl_ref))
    tch, tcl = _split(tc_ref[0])
    tsh, tsl = _split(ts_ref[0])
    ah, al = _split(a)
    bh, bl = _split(b)
    ur_ref[...] = _dot3(tch, tcl, ah, al) - _dot3(tsh, tsl, bh, bl)
    ui_ref[...] = -(_dot3(tch, tcl, bh, bl) + _dot3(tsh, tsl, ah, al))


def _fourier2_kernel(ur_ref, ui_ref, c2h_ref, c2l_ref, s2h_ref, s2l_ref, w_ref, b_ref, x_ref, mod_ref,
                     lng_ref, lnb_ref, rwh_ref, rwl_ref, rb_ref, *out_refs, norm):
    urh, url = _split(ur_ref[0])
    uih, uil = _split(ui_ref[0])
    mixed = (_dot3(c2h_ref[...], c2l_ref[...], urh, url) + _dot3(s2h_ref[...], s2l_ref[...], uih, uil)) * norm
    y = _dot(mixed.astype(BF16), w_ref[...]) + b_ref[...]
    _post(y, x_ref[...], mod_ref[0], lng_ref[...], lnb_ref[...], rwh_ref[...], rwl_ref[...], rb_ref[...], out_refs)


def _fourier_ctx_kernel(x_ref, mod_ref, cch_ref, ccl_ref, sch_ref, scl_ref, cnh_ref, cnl_ref, snh_ref, snl_ref,
                        w_ref, b_ref, lng_ref, lnb_ref, rwh_ref, rwl_ref, rb_ref, *out_refs, norm):
    mod = mod_ref[0]
    x = x_ref[...]
    h = x * (1.0 + mod[1:2]) + mod[0:1]
    a, b = _channel_dft(h, (cch_ref, ccl_ref), (sch_ref, scl_ref))
    ah, al = _split(a)
    bh, bl = _split(b)
    mixed = (_dot3(cnh_ref[...], cnl_ref[...], ah, al) - _dot3(snh_ref[...], snl_ref[...], bh, bl)) * norm
    y = _dot(mixed.astype(BF16), w_ref[...]) + b_ref[...]
    _post(y, x, mod, lng_ref[...], lnb_ref[...], rwh_ref[...], rwl_ref[...], rb_ref[...], out_refs)


def _fourier_layer(x_lat, x_ctx, mods, w_out, b_out, lng, lnb, router):
    n_lat, d = x_lat.shape
    n_ctx = x_ctx.shape[0]
    n2 = LANE
    n1 = n_lat // n2
    cw = d // FN_GROUPS
    xv = x_lat.reshape(n1, n2 * d)
    cmat = _dft_mats(cw)
    w_bf = w_out.astype(BF16)
    b2 = b_out.reshape(1, d)
    lng2, lnb2 = lng.reshape(1, d), lnb.reshape(1, d)

    k1 = jnp.arange(n1, dtype=I32)
    pos = jnp.arange(n1, dtype=I32)[None, None, :] * n2 + jnp.arange(n2, dtype=I32)[:, None, None]
    ang = ((k1[None, :, None] * pos) % n_lat).astype(F32) * (2.0 * math.pi / n_lat)
    tc, ts = jnp.cos(ang), jnp.sin(ang)

    mat = _full((cw, cw))
    ur, ui = pl.pallas_call(
        _fourier1_kernel,
        grid=(n2,),
        in_specs=[
            pl.BlockSpec((n1, d), lambda j: (0, j)),
            pl.BlockSpec((1, N_MOD, d), lambda j: (0, 0, 0)),
            mat, mat, mat, mat,
            pl.BlockSpec((1, n1, n1), lambda j: (j, 0, 0)),
            pl.BlockSpec((1, n1, n1), lambda j: (j, 0, 0)),
        ],
        out_specs=(pl.BlockSpec((n1, d), lambda j: (0, j)), pl.BlockSpec((n1, d), lambda j: (0, j))),
        out_shape=(jax.ShapeDtypeStruct((n1, n2 * d), F32), jax.ShapeDtypeStruct((n1, n2 * d), F32)),
        compiler_params=_params("arbitrary"),
        name="fourier1",
    )(xv, mods, *cmat, tc, ts)

    m2 = _dft_mats(n2)
    mat2 = _full((n2, n2))
    norm = 1.0 / math.sqrt(n_lat * cw)
    out_shapes = (
        jax.ShapeDtypeStruct((n1, n2 * d), F32),
        jax.ShapeDtypeStruct((n1, n2 * d), F32),
        jax.ShapeDtypeStruct((n1, n2 * LANE), I32),
        jax.ShapeDtypeStruct((n1, n2 * LANE), F32),
        _COUNT_SHAPE,
    )
    strided = pl.BlockSpec((n2, d), lambda k: (0, k))
    strided_l = pl.BlockSpec((n2, LANE), lambda k: (0, k))
    outs = pl.pallas_call(
        functools.partial(_fourier2_kernel, norm=norm),
        grid=(n1,),
        in_specs=[
            pl.BlockSpec((1, n2, d), lambda k: (k, 0, 0)),
            pl.BlockSpec((1, n2, d), lambda k: (k, 0, 0)),
            mat2, mat2, mat2, mat2,
            _full((d, d)), _full((1, d)),
            strided,
            pl.BlockSpec((1, N_MOD, d), lambda k: (0, 0, 0)),
            _full((1, d)), _full((1, d)),
            _full((d, LANE)), _full((d, LANE)), _full((1, LANE)),
        ],
        out_specs=(strided, strided, strided_l, strided_l, _COUNT_SPEC),
        out_shape=out_shapes,
        compiler_params=_params("arbitrary"),
        name="fourier2",
    )(ur.reshape(n1, n2, d), ui.reshape(n1, n2, d), *m2, w_bf, b2, xv, mods, lng2, lnb2, *router)
    lat = (outs[0].reshape(n_lat, d), outs[1].reshape(n_lat, d),
           outs[2].reshape(n_lat, LANE), outs[3].reshape(n_lat, LANE))

    cn = _dft_mats(n_ctx)
    matn = _full((n_ctx, n_ctx))
    shapes = (
        jax.ShapeDtypeStruct((n_ctx, d), F32), jax.ShapeDtypeStruct((n_ctx, d), F32),
        jax.ShapeDtypeStruct((n_ctx, LANE), I32), jax.ShapeDtypeStruct((n_ctx, LANE), F32), _COUNT_SHAPE,
    )
    ctx = pl.pallas_call(
        functools.partial(_fourier_ctx_kernel, norm=1.0 / math.sqrt(n_ctx * cw)),
        grid=(1,),
        in_specs=[
            _full((n_ctx, d)),
            pl.BlockSpec((1, N_MOD, d), lambda i: (1, 0, 0)),
            mat, mat, mat, mat, matn, matn, matn, matn,
            _full((d, d)), _full((1, d)), _full((1, d)), _full((1, d)),
            _full((d, LANE)), _full((d, LANE)), _full((1, LANE)),
        ],
        out_specs=(_full((n_ctx, d)), _full((n_ctx, d)), _full((n_ctx, LANE)), _full((n_ctx, LANE)),
                   _COUNT_SPEC),
        out_shape=shapes,
        compiler_params=_params("arbitrary"),
        name="fourier_ctx",
    )(x_ctx, mods, *cmat, *cn, w_bf, b2, lng2, lnb2, *router)
    return tuple(jnp.concatenate([a, b], axis=0) for a, b in zip(lat, ctx[:4])) + (outs[4] + ctx[4],)


def _qkv_kernel(pos_ref, *refs, n_qk, rms):
    mod_ref, w_ref, b_ref, cos_ref, sin_ref, *rest = refs[N_COMBINE_REFS:]
    if rms:
        gain_ref, ind_ref, indt_ref, x_ref, q_ref, k_ref, v_ref, ybuf, sem = rest
    else:
        x_ref, q_ref, k_ref, v_ref, ybuf, sem = rest
    x = _combine_tile(pos_ref, *refs[:N_COMBINE_REFS], ybuf, sem)
    x_ref[...] = x
    mod = mod_ref[0]
    h = x * (1.0 + mod[1:2]) + mod[0:1]
    y = _dot(h.astype(BF16), w_ref[...]) + b_ref[...]
    qk = y[:, :n_qk]
    if rms:
        sh, sl = _split(qk * qk)
        ms = _dot(sh, ind_ref[...]) + _dot(sl, ind_ref[...])
        mh, ml = _split(ms)
        msb = _dot(mh, indt_ref[...]) + _dot(ml, indt_ref[...])
        qk = qk * lax.rsqrt(msb + RMS_EPS) * gain_ref[...]
    cos = cos_ref[...]
    sin = sin_ref[...]
    even = (lax.broadcasted_iota(I32, cos.shape, 1) & 1) == 0
    parts = []
    for c in range(n_qk // LANE):
        z = qk[:, c * LANE:(c + 1) * LANE]
        swapped = jnp.where(even, pltpu.roll(z, LANE - 1, 1), pltpu.roll(z, 1, 1))
        parts.append(z * cos + swapped * sin)
    nq = q_ref.shape[1]
    q_ref[...] = (jnp.concatenate(parts[:nq // LANE], axis=1) * Q_SCALE).astype(BF16)
    k_ref[...] = jnp.concatenate(parts[nq // LANE:], axis=1).astype(BF16)
    v_ref[...] = y[:, n_qk:].astype(BF16)


def _qkv(pending, mods, w_qkv, b_qkv, cos_t, sin_t, n_q, n_kv, n_lat, q_norm=None, k_norm=None):
    t_all, d = pending[3].shape
    nq, nk = n_q * HEAD_DIM, n_kv * HEAD_DIM
    n_qk, n_all = nq + nk, nq + 2 * nk
    rms = q_norm is not None
    pos, args, in_specs, scratch = _combine_operands(pending, TM, n_lat)
    row = lambda w: pl.BlockSpec((TM, w), lambda t, *_: (t, 0))
    in_specs += [_tile_specs(n_lat // TM)[1], _full((d, n_all)), _full((1, n_all)), row(LANE), row(LANE)]
    args += [mods, w_qkv.astype(BF16), b_qkv.reshape(1, n_all), cos_t, sin_t]
    if rms:
        gain = jnp.concatenate([jnp.tile(q_norm, n_q), jnp.tile(k_norm, n_kv)]).reshape(1, n_qk)
        head = np.arange(n_qk) // HEAD_DIM
        ind = np.zeros((n_qk, LANE), np.float32)
        ind[np.arange(n_qk), head] = 1.0 / HEAD_DIM
        indt = np.zeros((LANE, n_qk), np.float32)
        indt[head, np.arange(n_qk)] = 1.0
        in_specs += [_full((1, n_qk)), _full((n_qk, LANE)), _full((LANE, n_qk))]
        args += [gain, jnp.asarray(ind, BF16), jnp.asarray(indt, BF16)]
    grid_spec = pltpu.PrefetchScalarGridSpec(
        num_scalar_prefetch=1,
        grid=(t_all // TM,),
        in_specs=in_specs,
        out_specs=(row(d), row(nq), row(nk), row(nk)),
        scratch_shapes=scratch,
    )
    return pl.pallas_call(
        functools.partial(_qkv_kernel, n_qk=n_qk, rms=rms),
        grid_spec=grid_spec,
        out_shape=(jax.ShapeDtypeStruct((t_all, d), F32), jax.ShapeDtypeStruct((t_all, nq), BF16),
                   jax.ShapeDtypeStruct((t_all, nk), BF16), jax.ShapeDtypeStruct((t_all, nk), BF16)),
        compiler_params=_params("arbitrary"),
        name="qkv",
    )(pos, *args)


def _rope_tables(n_lat, n_ctx):
    rows = n_lat // GRID_W
    row = jnp.repeat(jnp.arange(rows, dtype=F32), GRID_W)
    col = jnp.tile(jnp.arange(GRID_W, dtype=F32), rows)
    n_freq = HEAD_DIM // 4
    inv = ROPE_THETA ** (-jnp.arange(n_freq, dtype=F32) / n_freq)
    ang = jnp.concatenate([row[:, None] * inv, col[:, None] * inv], axis=-1)
    ang = jnp.concatenate([ang, jnp.zeros((n_ctx, HEAD_DIM // 2), F32)], axis=0)
    cos = jnp.tile(jnp.repeat(jnp.cos(ang), 2, axis=1), (1, LANE // HEAD_DIM))
    sin = jnp.tile(jnp.repeat(jnp.sin(ang), 2, axis=1), (1, LANE // HEAD_DIM))
    sign = jnp.where(jnp.arange(LANE) % 2 == 0, -1.0, 1.0).astype(F32)
    return cos, sin * sign


def _kv_layouts(k, v, n_kv):
    t_all = k.shape[0]
    kh = k.reshape(t_all, n_kv, HEAD_DIM).transpose(1, 0, 2)
    vt = v.reshape(t_all, n_kv, HEAD_DIM).transpose(1, 2, 0)
    pad = jnp.zeros((n_kv, LANE - HEAD_DIM, t_all), BF16).at[:, 0, :].set(1.0)
    return kh, jnp.concatenate([vt, pad], axis=1)


def _flash_kernel(q_ref, k_ref, vt_ref, o_ref, s_even, s_odd, *, grp, tk, n_lat, n_ctx):
    tq = q_ref.shape[0]
    q_t = q_ref[...].astype(F32).T
    qt = jnp.concatenate([q_t[g * HEAD_DIM:(g + 1) * HEAD_DIM] for g in range(grp)], axis=1).astype(BF16)
    cols = qt.shape[1]
    is_ctx = pl.program_id(1) == n_lat // FLASH_TQ
    n_chunks = (n_lat + n_ctx) // tk

    def finish(acc):
        o = acc[:HEAD_DIM] / acc[HEAD_DIM:HEAD_DIM + 1]
        o_t = jnp.concatenate([o[:, g * tq:(g + 1) * tq] for g in range(grp)], axis=0)
        o_ref[...] = o_t.T.astype(BF16)

    groups = [slice(c * MXU_WIDTH, (c + 1) * MXU_WIDTH) for c in range(cols // MXU_WIDTH)]

    def scores(j, buf, g):
        off = pl.multiple_of(j * tk, tk)
        s = _dot(k_ref[0, pl.ds(off, tk), :], qt[:, g])
        buf[:, g] = s
        return jnp.max(s, axis=0, keepdims=True)

    def accumulate(j, buf, g, m, acc, mc):
        off = pl.multiple_of(j * tk, tk)
        m_new = jnp.maximum(m, mc)
        p = jnp.exp2(buf[:, g] - m_new)
        acc = jnp.exp2(m - m_new) * acc + _dot(vt_ref[0, :, pl.ds(off, tk)], p.astype(BF16))
        return m_new, acc

    @pl.when(jnp.logical_not(is_ctx))
    def _():
        def step(j, cur, nxt, carry):
            out = []
            for g, (m, acc, mc) in zip(groups, carry):
                mc_next = scores(j + 1, nxt, g)
                out.append(accumulate(j, cur, g, m, acc, mc) + (mc_next,))
            return out

        def body(jj, carry):
            for pair in range(FLASH_PAIRS):
                j = 2 * (FLASH_PAIRS * jj + pair)
                carry = step(j, s_even, s_odd, carry)
                carry = step(j + 1, s_odd, s_even, carry)
            return carry

        init = [(jnp.full((1, MXU_WIDTH), NEG, F32), jnp.zeros((LANE, MXU_WIDTH), F32), scores(0, s_even, g))
                for g in groups]
        carry = lax.fori_loop(0, (n_chunks - 1) // (2 * FLASH_PAIRS), body, init)
        finish(jnp.concatenate([accumulate(n_chunks - 1, s_even, g, m, acc, mc)[1]
                                for g, (m, acc, mc) in zip(groups, carry)], axis=1))

    @pl.when(is_ctx)
    def _():
        s = _dot(k_ref[0, n_lat:n_lat + n_ctx, :], qt)
        p = jnp.exp2(s - jnp.max(s, axis=0, keepdims=True))
        finish(_dot(vt_ref[0, :, n_lat:n_lat + n_ctx], p.astype(BF16)))


def _full_attention(q, k, v, n_lat, n_ctx):
    t_all, nq = q.shape
    n_kv = k.shape[1] // HEAD_DIM
    grp = nq // HEAD_DIM // n_kv
    nt = t_all // FLASH_TQ
    cols = grp * FLASH_TQ
    assert n_ctx == FLASH_TQ and t_all % FLASH_TK == 0 and (t_all // FLASH_TK - 1) % (2 * FLASH_PAIRS) == 0
    kh, vt = _kv_layouts(k, v, n_kv)
    return pl.pallas_call(
        functools.partial(_flash_kernel, grp=grp, tk=FLASH_TK, n_lat=n_lat, n_ctx=n_ctx),
        grid=(n_kv, nt),
        in_specs=[
            pl.BlockSpec((FLASH_TQ, grp * HEAD_DIM), lambda h, i: (i, h)),
            pl.BlockSpec((1, t_all, HEAD_DIM), lambda h, i: (h, 0, 0)),
            pl.BlockSpec((1, LANE, t_all), lambda h, i: (h, 0, 0)),
        ],
        out_specs=pl.BlockSpec((FLASH_TQ, grp * HEAD_DIM), lambda h, i: (i, h)),
        out_shape=jax.ShapeDtypeStruct((t_all, nq), BF16),
        scratch_shapes=[pltpu.VMEM((FLASH_TK, cols), F32), pltpu.VMEM((FLASH_TK, cols), F32)],
        compiler_params=_params("arbitrary", "arbitrary"),
        name="flash",
    )(q, kh, vt)


def _window_kernel(q_ref, kp_ref, kc_ref, kn_ref, kx_ref, vp_ref, vc_ref, vn_ref, vx_ref, sink_ref, o_ref,
                   *, grp, nb):
    i = pl.program_id(1)
    q_t = q_ref[...].astype(F32).T
    qt = jnp.concatenate([q_t[g * HEAD_DIM:(g + 1) * HEAD_DIM] for g in range(grp)], axis=1).astype(BF16)
    cols = qt.shape[1]
    s = _dot(jnp.concatenate([kp_ref[0], kc_ref[0], kn_ref[0], kx_ref[0]], axis=0), qt)
    kj = lax.broadcasted_iota(I32, (Q_BLOCK, cols), 0)
    qi = lax.broadcasted_iota(I32, (Q_BLOCK, cols), 1) & (Q_BLOCK - 1)
    sp = jnp.where(kj >= jnp.where(i > 0, qi, Q_BLOCK), s[:Q_BLOCK], NEG)
    sn = jnp.where(kj <= jnp.where(i < nb - 1, qi, -1), s[2 * Q_BLOCK:3 * Q_BLOCK], NEG)
    s = jnp.concatenate([sp, s[Q_BLOCK:2 * Q_BLOCK], sn, s[3 * Q_BLOCK:]], axis=0)
    sink = sink_ref[0]
    m = jnp.maximum(jnp.max(s, axis=0, keepdims=True), sink)
    p = jnp.exp2(s - m).astype(BF16)
    acc = _dot(jnp.concatenate([vp_ref[0], vc_ref[0], vn_ref[0], vx_ref[0]], axis=1), p)
    o = acc[:HEAD_DIM] / (acc[HEAD_DIM:HEAD_DIM + 1] + jnp.exp2(sink - m))
    o_t = jnp.concatenate([o[:, g * Q_BLOCK:(g + 1) * Q_BLOCK] for g in range(grp)], axis=0)
    o_ref[...] = o_t.T.astype(BF16)


def _window_attention(q, k, v, sink, n_lat, n_ctx):
    nq = q.shape[1]
    n_kv = k.shape[1] // HEAD_DIM
    grp = nq // HEAD_DIM // n_kv
    gw = grp * HEAD_DIM
    nb = n_lat // Q_BLOCK
    cb = n_lat // n_ctx
    kh, vt = _kv_layouts(k, v, n_kv)
    sink_cols = jnp.repeat(sink.reshape(n_kv, 1, grp) * LOG2E, Q_BLOCK, axis=2)
    kspec = lambda f: pl.BlockSpec((1, Q_BLOCK, HEAD_DIM), lambda h, i: (h, f(i), 0))
    vspec = lambda f: pl.BlockSpec((1, LANE, Q_BLOCK), lambda h, i: (h, 0, f(i)))
    prev = lambda i: jnp.maximum(i - 1, 0)
    cur = lambda i: i
    nxt = lambda i: jnp.minimum(i + 1, nb - 1)
    return pl.pallas_call(
        functools.partial(_window_kernel, grp=grp, nb=nb),
        grid=(n_kv, nb),
        in_specs=[
            pl.BlockSpec((Q_BLOCK, gw), lambda h, i: (i, h)),
            kspec(prev), kspec(cur), kspec(nxt),
            pl.BlockSpec((1, n_ctx, HEAD_DIM), lambda h, i: (h, cb, 0)),
            vspec(prev), vspec(cur), vspec(nxt),
            pl.BlockSpec((1, LANE, n_ctx), lambda h, i: (h, 0, cb)),
            pl.BlockSpec((1, 1, grp * Q_BLOCK), lambda h, i: (h, 0, 0)),
        ],
        out_specs=pl.BlockSpec((Q_BLOCK, gw), lambda h, i: (i, h)),
        out_shape=jax.ShapeDtypeStruct((n_lat, nq), BF16),
        compiler_params=_params("arbitrary", "arbitrary"),
        name="window",
    )(q, kh, kh, kh, kh, vt, vt, vt, vt, sink_cols)


def _gmlp_kernel(pos_ref, *refs):
    (mod_ref, win_ref, bin_ref, vg_ref, vb_ref, ws_ref, bs_ref, wout_ref, bout_ref,
     lng_ref, lnb_ref, rwh_ref, rwl_ref, rb_ref, *rest) = refs[N_COMBINE_REFS:]
    out_refs, (ybuf, sem) = rest[:-2], rest[-2:]
    mod = mod_ref[0]
    x = _combine_tile(pos_ref, *refs[:N_COMBINE_REFS], ybuf, sem)
    h = x * (1.0 + mod[1:2]) + mod[0:1]
    z = _dot(h.astype(BF16), win_ref[...]) + bin_ref[...]
    z = 0.5 * z * (1.0 + lax.erf(z * (2.0 ** -0.5)))
    half = z.shape[1] // 2
    u = z[:, :half]
    v = _layer_norm(z[:, half:], vg_ref[...], vb_ref[...]).astype(BF16)
    cw = half // GM_GROUPS
    chunks = []
    for c in range(x.shape[0] // GM_CHUNK):
        vc = v[c * GM_CHUNK:(c + 1) * GM_CHUNK]
        chunks.append(jnp.concatenate(
            [_dot(ws_ref[g], vc[:, g * cw:(g + 1) * cw]) + bs_ref[g] for g in range(GM_GROUPS)], axis=1))
    gated = u * jnp.concatenate(chunks, axis=0)
    y = _dot(gated.astype(BF16), wout_ref[...]) + bout_ref[...]
    _post(y, x, mod, lng_ref[...], lnb_ref[...], rwh_ref[...], rwl_ref[...], rb_ref[...], out_refs)


def _gmlp_layer(pending, mods, w_in, b_in, vg, vb, w_s, b_s, w_out, b_out, lng, lnb, router, n_lat):
    t_all, d = pending[3].shape
    dffn = w_in.shape[1]
    half = dffn // 2
    cw = half // GM_GROUPS
    shapes, specs = _post_out(t_all)
    bs_full = jnp.broadcast_to(b_s[:, :, None], (GM_GROUPS, GM_CHUNK, cw))
    pos, args, in_specs, scratch = _combine_operands(pending, TM, n_lat)
    in_specs += [
        _tile_specs(n_lat // TM)[1], _full((d, dffn)), _full((1, dffn)), _full((1, half)), _full((1, half)),
        _full((GM_GROUPS, GM_CHUNK, GM_CHUNK)), _full((GM_GROUPS, GM_CHUNK, cw)),
        _full((half, d)), _full((1, d)), _full((1, d)), _full((1, d)),
        _full((d, LANE)), _full((d, LANE)), _full((1, LANE)),
    ]
    args += [mods, w_in.astype(BF16), b_in.reshape(1, dffn), vg.reshape(1, half), vb.reshape(1, half),
             w_s.astype(BF16), bs_full, w_out.astype(BF16), b_out.reshape(1, d), lng.reshape(1, d),
             lnb.reshape(1, d), *router]
    grid_spec = pltpu.PrefetchScalarGridSpec(
        num_scalar_prefetch=1,
        grid=(t_all // TM,),
        in_specs=in_specs,
        out_specs=specs,
        scratch_shapes=scratch,
    )
    return pl.pallas_call(
        _gmlp_kernel,
        grid_spec=grid_spec,
        out_shape=shapes,
        compiler_params=_params("arbitrary"),
        name="gmlp",
    )(pos, *args)


COMB_TM = 128


def _route_meta(count_blk, n_tok):
    n_assign = n_tok * TOP_K
    n_blocks = -(-n_assign // MOE_BLOCK) + N_EXPERTS
    ids = jnp.arange(N_EXPERTS, dtype=I32)
    counts = count_blk[0, :N_EXPERTS].astype(I32)
    padded = (counts + MOE_BLOCK - 1) // MOE_BLOCK * MOE_BLOCK
    ends_pad = jnp.cumsum(padded)
    base = ends_pad - padded
    starts = jnp.arange(n_blocks, dtype=I32) * MOE_BLOCK
    block_expert = jnp.minimum(jnp.sum((ends_pad[None, :] <= starts[:, None]).astype(I32), axis=1), N_EXPERTS - 1)
    last_blk = jnp.where(padded > 0, ends_pad - MOE_BLOCK, -1)
    tail = ends_pad[-1] + ids * MOE_BLOCK
    tail = jnp.where(tail < n_blocks * MOE_BLOCK, tail, -1)
    base_b = jnp.broadcast_to(base.astype(F32)[:, None], (N_EXPERTS, LANE))
    return block_expert.astype(I32), base_b, jnp.concatenate([last_blk, tail]).astype(I32), n_blocks


def _rank_kernel(idx_ref, base_ref, upper_ref, dest_ref, run_ref):
    @pl.when(pl.program_id(0) == 0)
    def _():
        run_ref[...] = jnp.zeros(run_ref.shape, F32)

    eid = lax.broadcasted_iota(I32, (N_EXPERTS, TM), 0)
    onehots = [(eid == idx_ref[k:k + 1, :]).astype(F32) for k in range(TOP_K)]
    cnt = onehots[0] + onehots[1] + onehots[2] + onehots[3]
    before = _dot(cnt.astype(BF16), upper_ref[...])
    slot = base_ref[:, :1] + run_ref[:, :1] + before
    for k in range(TOP_K):
        rows = (jnp.sum(onehots[k] * slot, axis=0, keepdims=True) * ROW_TILE).astype(I32)
        for j in range(TM // COMB_TM):
            dest_ref[j, k:k + 1, :] = rows[:, j * COMB_TM:(j + 1) * COMB_TM]
    run_ref[...] = run_ref[...] + jnp.sum(cnt, axis=1, keepdims=True)


def _rank(top_idx, base_b, n_tok):
    per_step = TM // COMB_TM
    idx_t = top_idx[:n_tok, :TOP_K].T
    upper = jnp.asarray(np.triu(np.ones((TM, TM), np.float32), 1), BF16)
    return pl.pallas_call(
        _rank_kernel,
        grid=(n_tok // TM,),
        in_specs=[pl.BlockSpec((TOP_K, TM), lambda t: (0, t)), _full((N_EXPERTS, LANE)), _full((TM, TM))],
        out_specs=pl.BlockSpec((per_step, TOP_K, COMB_TM), lambda t: (t, 0, 0)),
        out_shape=jax.ShapeDtypeStruct((n_tok // COMB_TM, TOP_K, COMB_TM), I32),
        scratch_shapes=[pltpu.VMEM((N_EXPERTS, LANE), F32)],
        compiler_params=_params("arbitrary"),
        name="rank",
    )(idx_t, base_b, upper)


def _dispatch_kernel(dest_ref, zrow_ref, h_ref, xs_hbm, hbuf, zbuf, sem, zsem, *, n_tiles):
    t = pl.program_id(0)
    slot = t % 2
    n_zero = 2 * N_EXPERTS

    def zero_copy(j):
        row = pl.multiple_of(zrow_ref[j] * ROW_TILE, MOE_BLOCK * ROW_TILE)
        return pltpu.make_async_copy(zbuf, xs_hbm.at[pl.ds(row, MOE_BLOCK * ROW_TILE), :], zsem.at[0])

    def wait_rows(s):
        for _ in range(TOP_K):
            pltpu.make_async_copy(hbuf.at[s], xs_hbm.at[pl.ds(0, COMB_TM * ROW_TILE), :], sem.at[s]).wait()

    @pl.when(t == 0)
    def _():
        zbuf[...] = jnp.zeros(zbuf.shape, F32)
        for j in range(n_zero):
            @pl.when(zrow_ref[j] >= 0)
            def _():
                zero_copy(j).start()
        for j in range(n_zero):
            @pl.when(zrow_ref[j] >= 0)
            def _():
                zero_copy(j).wait()

    @pl.when(t >= 2)
    def _():
        wait_rows(slot)

    _to_row_tiles(hbuf.at[slot], h_ref[...])
    for k in range(TOP_K):
        for r in range(COMB_TM):
            row = pl.multiple_of(dest_ref[(t * TOP_K + k) * COMB_TM + r], ROW_TILE)
            pltpu.make_async_copy(hbuf.at[slot, pl.ds(r * ROW_TILE, ROW_TILE), :],
                                  xs_hbm.at[pl.ds(row, ROW_TILE), :], sem.at[slot]).start(priority=r % 2)

    @pl.when(t == n_tiles - 1)
    def _():
        wait_rows(slot)
        if n_tiles > 1:
            wait_rows(1 - slot)


def _dispatch(h2, dest, zrows, n_tok, n_blocks):
    d = h2.shape[1]
    n_tiles = n_tok // COMB_TM
    grid_spec = pltpu.PrefetchScalarGridSpec(
        num_scalar_prefetch=2,
        grid=(n_tiles,),
        in_specs=[pl.BlockSpec((COMB_TM, d), lambda t, dest, zr: (t, 0))],
        out_specs=pl.BlockSpec(memory_space=pl.ANY),
        scratch_shapes=[pltpu.VMEM((2, COMB_TM * ROW_TILE, LANE), F32),
                        pltpu.VMEM((MOE_BLOCK * ROW_TILE, LANE), F32),
                        pltpu.SemaphoreType.DMA((2,)), pltpu.SemaphoreType.DMA((1,))],
    )
    return pl.pallas_call(
        functools.partial(_dispatch_kernel, n_tiles=n_tiles),
        grid_spec=grid_spec,
        out_shape=jax.ShapeDtypeStruct((n_blocks * MOE_BLOCK * ROW_TILE, LANE), F32),
        compiler_params=_params("arbitrary"),
        name="dispatch",
    )(dest.reshape(-1), zrows, h2)


EXP_CHUNK = MXU_WIDTH


def _expert_kernel(be_ref, x_ref, wgu_ref, bgu_ref, wd_ref, bd_ref, sel_ref, y_ref,
                   wg_s, wl_s, wd_s, bg_s, bl_s):
    b = pl.program_id(0)
    ff = wg_s.shape[1]
    half = EXP_CHUNK // 2

    @pl.when(jnp.logical_or(b == 0, be_ref[b] != be_ref[jnp.maximum(b - 1, 0)]))
    def _():
        for c in range(2 * ff // EXP_CHUNK):
            w = _dot(wgu_ref[0, 0, :, c * EXP_CHUNK:(c + 1) * EXP_CHUNK].astype(BF16), sel_ref[...])
            wg_s[:, c * half:(c + 1) * half] = w[:, :half].astype(BF16)
            wl_s[:, c * half:(c + 1) * half] = w[:, half:].astype(BF16)
            bh, bl = _split(jnp.broadcast_to(bgu_ref[0, 0, :, c * EXP_CHUNK:(c + 1) * EXP_CHUNK], (8, EXP_CHUNK)))
            bias = _dot(bh, sel_ref[...]) + _dot(bl, sel_ref[...])
            bg_s[:, c * half:(c + 1) * half] = bias[:, :half]
            bl_s[:, c * half:(c + 1) * half] = bias[:, half:]
        wd_s[...] = wd_ref[0, 0].astype(BF16)

    x = _from_row_tiles(x_ref, 0, MOE_BLOCK).astype(BF16)
    glu = jnp.minimum(_dot(x, wg_s[...]) + bg_s[0:1, :], SWIGLU_LIMIT)
    lin = jnp.clip(_dot(x, wl_s[...]) + bl_s[0:1, :], -SWIGLU_LIMIT, SWIGLU_LIMIT)
    act = glu * (1.0 / (1.0 + jnp.exp(-SWIGLU_ALPHA * glu))) * (lin + 1.0)
    _to_row_tiles(y_ref, _dot(act.astype(BF16), wd_s[...]) + bd_ref[0, 0])


def _experts(xs, block_expert, n_blocks, w_gate_up, b_gate_up, w_down, b_down, layer):
    d = w_down.shape[3]
    ff = w_down.shape[2]
    half = EXP_CHUNK // 2
    sel = np.zeros((EXP_CHUNK, EXP_CHUNK), np.float32)
    sel[2 * np.arange(half), np.arange(half)] = 1.0
    sel[2 * np.arange(half) + 1, half + np.arange(half)] = 1.0
    wspec = lambda r, c: pl.BlockSpec((1, 1, r, c), lambda b, be: (layer, be[b], 0, 0))
    sspec = pl.BlockSpec((EXP_CHUNK, EXP_CHUNK), lambda b, be: (0, 0))
    grid_spec = pltpu.PrefetchScalarGridSpec(
        num_scalar_prefetch=1,
        grid=(n_blocks,),
        in_specs=[pl.BlockSpec((MOE_BLOCK * ROW_TILE, LANE), lambda b, be: (b, 0)), wspec(d, 2 * ff),
                  wspec(1, 2 * ff), wspec(ff, d), wspec(1, d), sspec],
        out_specs=pl.BlockSpec((MOE_BLOCK * ROW_TILE, LANE), lambda b, be: (b, 0)),
        scratch_shapes=[pltpu.VMEM((d, ff), BF16), pltpu.VMEM((d, ff), BF16), pltpu.VMEM((ff, d), BF16),
                        pltpu.VMEM((8, ff), F32), pltpu.VMEM((8, ff), F32)],
    )
    return pl.pallas_call(
        _expert_kernel,
        grid_spec=grid_spec,
        out_shape=jax.ShapeDtypeStruct((n_blocks * MOE_BLOCK * ROW_TILE, LANE), F32),
        compiler_params=_params("arbitrary"),
        name="experts",
    )(block_expert, xs, w_gate_up, b_gate_up[:, :, None, :], w_down, b_down[:, :, None, :],
      jnp.asarray(sel, BF16))


def _gather_rows(idx_ref, base, n, src_hbm, dst, sem):
    for r in range(n):
        row = pl.multiple_of(idx_ref[base + r], ROW_TILE)
        pltpu.make_async_copy(src_hbm.at[pl.ds(row, ROW_TILE), :], dst.at[pl.ds(r * ROW_TILE, ROW_TILE), :],
                              sem).start(priority=r % 2)


N_COMBINE_REFS = 6


def _combine_tile(pos_ref, y_hbm, gate_ref, x1_ref, mod_ref, lng_ref, lnb_ref, ybuf, sem):
    t = pl.program_id(0)
    slot = t % 2
    tm, d = x1_ref.shape
    n = tm * TOP_K

    @pl.when(t == 0)
    def _():
        _gather_rows(pos_ref, 0, n, y_hbm, ybuf.at[0], sem.at[0])

    @pl.when(t + 1 < pl.num_programs(0))
    def _():
        _gather_rows(pos_ref, (t + 1) * n, n, y_hbm, ybuf.at[1 - slot], sem.at[1 - slot])

    pltpu.make_async_copy(y_hbm.at[pl.ds(0, n * ROW_TILE), :], ybuf.at[slot], sem.at[slot]).wait()
    gates = gate_ref[...]
    parts = []
    for j in range(tm // COMB_TM):
        f = jnp.zeros((COMB_TM, d), F32)
        for k in range(TOP_K):
            rows = _from_row_tiles(ybuf.at[slot], (j * TOP_K + k) * COMB_TM, COMB_TM)
            f = f + gates[j * COMB_TM:(j + 1) * COMB_TM, k:k + 1] * rows
        parts.append(f)
    f = parts[0] if len(parts) == 1 else jnp.concatenate(parts, axis=0)
    return _layer_norm(ALPHA * x1_ref[...] + mod_ref[0][5:6] * f, lng_ref[...], lnb_ref[...])


def _combine_operands(pending, tm, n_lat):
    yb, dest, gates, x1, mods, lng, lnb = pending
    d = x1.shape[1]
    n_lat_tiles = n_lat // tm
    specs = [
        pl.BlockSpec(memory_space=pl.ANY),
        pl.BlockSpec((tm, LANE), lambda t, *_: (t, 0)),
        pl.BlockSpec((tm, d), lambda t, *_: (t, 0)),
        pl.BlockSpec((1, N_MOD, d), lambda t, *_: (jnp.where(t >= n_lat_tiles, 1, 0), 0, 0)),
        _full((1, d)), _full((1, d)),
    ]
    scratch = [pltpu.VMEM((2, tm * TOP_K * ROW_TILE, LANE), F32), pltpu.SemaphoreType.DMA((2,))]
    return dest.reshape(-1), [yb, gates, x1, mods, lng.reshape(1, d), lnb.reshape(1, d)], specs, scratch


def _combine_kernel(pos_ref, *refs):
    o_ref, ybuf, sem = refs[N_COMBINE_REFS:]
    o_ref[...] = _combine_tile(pos_ref, *refs[:N_COMBINE_REFS], ybuf, sem)


def _combine(pending, n_tok, n_lat):
    d = pending[3].shape[1]
    pos, args, specs, scratch = _combine_operands(pending, COMB_TM, n_lat)
    grid_spec = pltpu.PrefetchScalarGridSpec(
        num_scalar_prefetch=1,
        grid=(n_tok // COMB_TM,),
        in_specs=specs,
        out_specs=pl.BlockSpec((COMB_TM, d), lambda t, *_: (t, 0)),
        scratch_shapes=scratch,
    )
    return pl.pallas_call(
        _combine_kernel,
        grid_spec=grid_spec,
        out_shape=jax.ShapeDtypeStruct((n_tok, d), F32),
        compiler_params=_params("arbitrary"),
        name="combine",
    )(pos, *args)


def _moe_layer(x1, h2, top_idx, gates, counts, mods, lng, lnb, w_gate_up, b_gate_up, w_down, b_down, layer, n_tok):
    block_expert, base_b, zrows, n_blocks = _route_meta(counts, n_tok)
    dest = _rank(top_idx, base_b, n_tok)
    xs = _dispatch(h2, dest, zrows, n_tok, n_blocks)
    yb = _experts(xs, block_expert, n_blocks, w_gate_up, b_gate_up, w_down, b_down, layer)
    return yb, dest, gates, x1, mods, lng, lnb


def kernel(x, c, ctx, c_ctx, ada_w, ada_b, ln_mix_g, ln_mix_b, ln_ffn_g, ln_ffn_b, fn_w_out, fn_b_out, fa_w_qkv, fa_b_qkv, fa_q_norm, fa_k_norm, fa_w_out, fa_b_out, gm_w_in, gm_b_in, gm_v_norm_g, gm_v_norm_b, gm_w_s, gm_b_s, gm_w_out, gm_b_out, wa_w_qkv, wa_b_qkv, wa_sink, wa_w_out, wa_b_out, router_w, router_b, exp_w_gate_up, exp_b_gate_up, exp_w_down, exp_b_down):
    bsz, n_lat, d = x.shape
    n_ctx = ctx.shape[1]
    assert bsz == 1 and d == D_MODEL and n_lat == LANE * LANE and n_lat % n_ctx == 0 and n_ctx % TM == 0
    t_all = n_lat + n_ctx
    n_lat_tiles = n_lat // TM
    mods_all = _ada(c, c_ctx, ada_w, ada_b)
    cos_t, sin_t = _rope_tables(n_lat, n_ctx)

    for i in range(DEPTH):
        kind, j = i % 4, i // 4
        last = i == DEPTH - 1
        n_tok = n_lat if last else t_all
        mods = mods_all[i]
        router = _router_operands(router_w[i], router_b[i])
        lng, lnb = ln_mix_g[i], ln_mix_b[i]
        if kind == 0:
            post = _fourier_layer(x[0], ctx[0], mods, fn_w_out[j], fn_b_out[j], lng, lnb, router)
        elif kind == 1:
            x_all, q, k, v = _qkv(pending, mods, fa_w_qkv[j], fa_b_qkv[j], cos_t, sin_t, FA_Q_HEADS,
                                  FA_KV_HEADS, n_lat, fa_q_norm[j], fa_k_norm[j])
            o = _full_attention(q, k, v, n_lat, n_ctx)
            post = _proj_post(o, fa_w_out[j], fa_b_out[j], x_all, mods, lng, lnb, router, n_tok, n_lat_tiles)
        elif kind == 2:
            post = _gmlp_layer(pending, mods, gm_w_in[j], gm_b_in[j], gm_v_norm_g[j], gm_v_norm_b[j],
                               gm_w_s[j], gm_b_s[j], gm_w_out[j], gm_b_out[j], lng, lnb, router, n_lat)
        else:
            x_all, q, k, v = _qkv(pending, mods, wa_w_qkv[j], wa_b_qkv[j], cos_t, sin_t, WA_Q_HEADS,
                                  WA_KV_HEADS, n_lat)
            o = _window_attention(q, k, v, wa_sink[j], n_lat, n_ctx)
            post = _proj_post(o, wa_w_out[j], wa_b_out[j], x_all, mods, lng, lnb, router, n_tok, n_lat_tiles)
        pending = _moe_layer(*post, mods, ln_ffn_g[i], ln_ffn_b[i], exp_w_gate_up, exp_b_gate_up, exp_w_down,
                             exp_b_down, i, n_tok)
    return _combine(pending, n_lat, n_lat)[None]
```

```python
import functools
import math

import numpy as np
import jax
import jax.numpy as jnp
from jax import lax
from jax.experimental import pallas as pl
from jax.experimental.pallas import tpu as pltpu

F32, BF16, I32 = jnp.float32, jnp.bfloat16, jnp.int32

D_MODEL = 1024
DEPTH = 4
GRID_W = 64
N_MOD = 6
FN_GROUPS = 4
HEAD_DIM = 64
FA_Q_HEADS, FA_KV_HEADS = 16, 4
WA_Q_HEADS, WA_KV_HEADS = 16, 2
WINDOW = 128
Q_BLOCK = 128
ROPE_THETA = 10000.0
GM_CHUNK = 128
GM_GROUPS = 8
N_EXPERTS = 32
TOP_K = 4
SWIGLU_LIMIT = 7.0
SWIGLU_ALPHA = 1.702
MOE_BLOCK = 256
LN_EPS = 1e-5
RMS_EPS = 1e-6
NEG = -1e30
ALPHA = (2 * DEPTH) ** 0.25
LOG2E = math.log2(math.e)
Q_SCALE = HEAD_DIM ** -0.5 * LOG2E

LANE = 128
SUBLANE = 8
ROW_TILE = D_MODEL // LANE
assert ROW_TILE == SUBLANE
MXU_WIDTH = 256
TM = 256
FLASH_TQ = 256
FLASH_TK = 1280
FLASH_PAIRS = 3
VMEM_LIMIT = 56 * 2 ** 20


def _params(*sem):
    return pltpu.CompilerParams(dimension_semantics=sem, vmem_limit_bytes=VMEM_LIMIT)


def _dot(a, b):
    return jnp.dot(a, b, preferred_element_type=F32)


def _split(a):
    hi = a.astype(BF16)
    lo = (a - hi.astype(F32)).astype(BF16)
    return hi, lo


def _dot3(a_hi, a_lo, b_hi, b_lo):
    return _dot(a_hi, b_hi) + (_dot(a_hi, b_lo) + _dot(a_lo, b_hi))


def _layer_norm(x, g, b):
    mu = jnp.mean(x, axis=-1, keepdims=True)
    xc = x - mu
    var = jnp.mean(xc * xc, axis=-1, keepdims=True)
    return xc * lax.rsqrt(var + LN_EPS) * g + b


def _top4(logits):
    lane = lax.broadcasted_iota(I32, logits.shape, 1).astype(F32)
    cur = logits
    vals, idxs = [], []
    for _ in range(TOP_K):
        m = jnp.max(cur, axis=-1, keepdims=True)
        i = jnp.min(jnp.where(cur == m, lane, float(LANE)), axis=-1, keepdims=True)
        vals.append(m)
        idxs.append(i)
        cur = jnp.where(lane == i, -jnp.inf, cur)
    exps = [jnp.exp(v - vals[0]) for v in vals]
    inv = 1.0 / (exps[0] + exps[1] + exps[2] + exps[3])
    idx_out = jnp.zeros_like(logits)
    gate_out = jnp.zeros_like(logits)
    picked = jnp.zeros_like(logits)
    for k in range(TOP_K):
        idx_out = jnp.where(lane == float(k), idxs[k], idx_out)
        gate_out = jnp.where(lane == float(k), exps[k] * inv, gate_out)
        picked = picked + jnp.where(lane == idxs[k], 1.0, 0.0)
    return idx_out.astype(I32), gate_out, jnp.sum(picked, axis=0, keepdims=True)


def _post(y, x, mod, lng, lnb, rw_hi, rw_lo, rb, out_refs):
    x1_ref, h2_ref, idx_ref, gate_ref, cnt_ref = out_refs
    x1 = _layer_norm(ALPHA * x + mod[2:3] * y, lng, lnb)
    h2 = x1 * (1.0 + mod[4:5]) + mod[3:4]
    hh, hl = _split(h2)
    logits = _dot3(hh, hl, rw_hi, rw_lo) + rb
    idx, gates, cnt = _top4(logits)
    x1_ref[...] = x1
    h2_ref[...] = h2
    idx_ref[...] = idx
    gate_ref[...] = gates

    @pl.when(pl.program_id(0) == 0)
    def _():
        cnt_ref[...] = jnp.zeros(cnt_ref.shape, F32)

    cnt_ref[...] = cnt_ref[...] + cnt


def _to_row_tiles(ref, x):
    n = x.shape[0]
    for s in range(ROW_TILE):
        ref[pl.ds(s, n, stride=ROW_TILE), :] = x[:, s * LANE:(s + 1) * LANE]


def _from_row_tiles(ref, start, n):
    return jnp.concatenate([ref[pl.ds(start * ROW_TILE + s, n, stride=ROW_TILE), :] for s in range(ROW_TILE)],
                           axis=1)


def _ada_kernel(cs_ref, w_ref, b_ref, o_ref):
    cs = cs_ref[...]
    s = cs * (1.0 / (1.0 + jnp.exp(-cs)))
    sh, sl = _split(s)
    wh, wl = _split(w_ref[0])
    o_ref[0] = _dot3(sh, sl, wh, wl) + b_ref[0]


def _ada(c, c_ctx, ada_w, ada_b):
    d = c.shape[-1]
    nm = ada_w.shape[-1]
    tn = nm // 4
    cs = jnp.zeros((SUBLANE, d), F32).at[0].set(c[0]).at[1].set(c_ctx)
    out = pl.pallas_call(
        _ada_kernel,
        grid=(DEPTH, nm // tn),
        in_specs=[
            pl.BlockSpec((SUBLANE, d), lambda i, j: (0, 0)),
            pl.BlockSpec((1, d, tn), lambda i, j: (i, 0, j)),
            pl.BlockSpec((1, 1, tn), lambda i, j: (i, 0, j)),
        ],
        out_specs=pl.BlockSpec((1, SUBLANE, tn), lambda i, j: (i, 0, j)),
        out_shape=jax.ShapeDtypeStruct((DEPTH, SUBLANE, nm), F32),
        compiler_params=_params("arbitrary", "arbitrary"),
        name="ada",
    )(cs, ada_w, ada_b.reshape(DEPTH, 1, nm))
    return out[:, :2].reshape(DEPTH, 2, N_MOD, d)


def _tile_specs(n_lat_tiles):
    tok = pl.BlockSpec((TM, D_MODEL), lambda t, *_: (t, 0))
    mod = pl.BlockSpec((1, N_MOD, D_MODEL), lambda t, *_: (jnp.where(t >= n_lat_tiles, 1, 0), 0, 0))
    return tok, mod


def _full(shape):
    nd = len(shape)
    return pl.BlockSpec(shape, lambda *_: (0,) * nd)


_COUNT_SHAPE = jax.ShapeDtypeStruct((SUBLANE, LANE), F32)
_COUNT_SPEC = pl.BlockSpec((SUBLANE, LANE), lambda *_: (0, 0))


def _post_out(n_rows):
    shapes = (
        jax.ShapeDtypeStruct((n_rows, D_MODEL), F32),
        jax.ShapeDtypeStruct((n_rows, D_MODEL), F32),
        jax.ShapeDtypeStruct((n_rows, LANE), I32),
        jax.ShapeDtypeStruct((n_rows, LANE), F32),
        _COUNT_SHAPE,
    )
    specs = (
        pl.BlockSpec((TM, D_MODEL), lambda t, *_: (t, 0)),
        pl.BlockSpec((TM, D_MODEL), lambda t, *_: (t, 0)),
        pl.BlockSpec((TM, LANE), lambda t, *_: (t, 0)),
        pl.BlockSpec((TM, LANE), lambda t, *_: (t, 0)),
        _COUNT_SPEC,
    )
    return shapes, specs


def _router_operands(router_w, router_b):
    rw = jnp.zeros((D_MODEL, LANE), F32).at[:, :N_EXPERTS].set(router_w)
    rw_hi = rw.astype(BF16)
    rw_lo = (rw - rw_hi.astype(F32)).astype(BF16)
    rb = jnp.full((1, LANE), NEG, F32).at[0, :N_EXPERTS].set(router_b)
    return rw_hi, rw_lo, rb


def _proj_post_kernel(a_ref, w_ref, b_ref, x_ref, mod_ref, lng_ref, lnb_ref, rwh_ref, rwl_ref, rb_ref, *out_refs):
    y = _dot(a_ref[...], w_ref[...]) + b_ref[...]
    _post(y, x_ref[...], mod_ref[0], lng_ref[...], lnb_ref[...], rwh_ref[...], rwl_ref[...], rb_ref[...], out_refs)


def _proj_post(a, w_out, b_out, x, mods, lng, lnb, router, n_rows, n_lat_tiles):
    k = a.shape[1]
    tok, mod = _tile_specs(n_lat_tiles)
    shapes, specs = _post_out(n_rows)
    return pl.pallas_call(
        _proj_post_kernel,
        grid=(n_rows // TM,),
        in_specs=[
            pl.BlockSpec((TM, k), lambda t: (t, 0)),
            _full((k, D_MODEL)), _full((1, D_MODEL)),
            tok, mod, _full((1, D_MODEL)), _full((1, D_MODEL)),
            _full((D_MODEL, LANE)), _full((D_MODEL, LANE)), _full((1, LANE)),
        ],
        out_specs=specs,
        out_shape=shapes,
        compiler_params=_params("arbitrary"),
        name="proj_post",
    )(a, w_out.astype(BF16), b_out.reshape(1, -1), x, mods, lng.reshape(1, -1), lnb.reshape(1, -1), *router)


def _dft_mats(n):
    jk = np.outer(np.arange(n), np.arange(n)) % n
    ang = 2.0 * np.pi * jk / n
    out = []
    for m in (np.cos(ang), np.sin(ang)):
        m32 = jnp.asarray(m, F32)
        hi = m32.astype(BF16)
        out += [hi, (m32 - hi.astype(F32)).astype(BF16)]
    return out


def _channel_dft(h, cc, sc):
    cw = cc[0].shape[0]
    a_parts, b_parts = [], []
    for g in range(h.shape[1] // cw):
        hh, hl = _split(h[:, g * cw:(g + 1) * cw])
        a_parts.append(_dot3(hh, hl, cc[0][...], cc[1][...]))
        b_parts.append(_dot3(hh, hl, sc[0][...], sc[1][...]))
    return jnp.concatenate(a_parts, axis=1), jnp.concatenate(b_parts, axis=1)


def _fourier1_kernel(x_ref, mod_ref, cch_ref, ccl_ref, sch_ref, scl_ref, tc_ref, ts_ref, ur_ref, ui_ref):
    mod = mod_ref[0]
    h = x_ref[...] * (1.0 + mod[1:2]) + mod[0:1]
    a, b = _channel_dft(h, (cch_ref, ccl_ref), (sch_ref, scl_ref))
    tch, tcl = _split(tc_ref[0])
    tsh, tsl = _split(ts_ref[0])
    ah, al = _split(a)
    bh, bl = _split(b)
    ur_ref[...] = _dot3(tch, tcl, ah, al) - _dot3(tsh, tsl, bh, bl)
    ui_ref[...] = -(_dot3(tch, tcl, bh, bl) + _dot3(tsh, tsl, ah, al))


def _fourier2_kernel(ur_ref, ui_ref, c2h_ref, c2l_ref, s2h_ref, s2l_ref, w_ref, b_ref, x_ref, mod_ref,
                     lng_ref, lnb_ref, rwh_ref, rwl_ref, rb_ref, *out_refs, norm):
    urh, url = _split(ur_ref[0])
    uih, uil = _split(ui_ref[0])
    mixed = (_dot3(c2h_ref[...], c2l_ref[...], urh, url) + _dot3(s2h_ref[...], s2l_ref[...], uih, uil)) * norm
    y = _dot(mixed.astype(BF16), w_ref[...]) + b_ref[...]
    _post(y, x_ref[...], mod_ref[0], lng_ref[...], lnb_ref[...], rwh_ref[...], rwl_ref[...], rb_ref[...], out_refs)


def _fourier_ctx_kernel(x_ref, mod_ref, cch_ref, ccl_ref, sch_ref, scl_ref, cnh_ref, cnl_ref, snh_ref, snl_ref,
                        w_ref, b_ref, lng_ref, lnb_ref, rwh_ref, rwl_ref, rb_ref, *out_refs, norm):
    mod = mod_ref[0]
    x = x_ref[...]
    h = x * (1.0 + mod[1:2]) + mod[0:1]
    a, b = _channel_dft(h, (cch_ref, ccl_ref), (sch_ref, scl_ref))
    ah, al = _split(a)
    bh, bl = _split(b)
    mixed = (_dot3(cnh_ref[...], cnl_ref[...], ah, al) - _dot3(snh_ref[...], snl_ref[...], bh, bl)) * norm
    y = _dot(mixed.astype(BF16), w_ref[...]) + b_ref[...]
    _post(y, x, mod, lng_ref[...], lnb_ref[...], rwh_ref[...], rwl_ref[...], rb_ref[...], out_refs)


def _fourier_layer(x_lat, x_ctx, mods, w_out, b_out, lng, lnb, router):
    n_lat, d = x_lat.shape
    n_ctx = x_ctx.shape[0]
    n2 = LANE
    n1 = n_lat // n2
    cw = d // FN_GROUPS
    xv = x_lat.reshape(n1, n2 * d)
    cmat = _dft_mats(cw)
    w_bf = w_out.astype(BF16)
    b2 = b_out.reshape(1, d)
    lng2, lnb2 = lng.reshape(1, d), lnb.reshape(1, d)

    k1 = jnp.arange(n1, dtype=I32)
    pos = jnp.arange(n1, dtype=I32)[None, None, :] * n2 + jnp.arange(n2, dtype=I32)[:, None, None]
    ang = ((k1[None, :, None] * pos) % n_lat).astype(F32) * (2.0 * math.pi / n_lat)
    tc, ts = jnp.cos(ang), jnp.sin(ang)

    mat = _full((cw, cw))
    ur, ui = pl.pallas_call(
        _fourier1_kernel,
        grid=(n2,),
        in_specs=[
            pl.BlockSpec((n1, d), lambda j: (0, j)),
            pl.BlockSpec((1, N_MOD, d), lambda j: (0, 0, 0)),
            mat, mat, mat, mat,
            pl.BlockSpec((1, n1, n1), lambda j: (j, 0, 0)),
            pl.BlockSpec((1, n1, n1), lambda j: (j, 0, 0)),
        ],
        out_specs=(pl.BlockSpec((n1, d), lambda j: (0, j)), pl.BlockSpec((n1, d), lambda j: (0, j))),
        out_shape=(jax.ShapeDtypeStruct((n1, n2 * d), F32), jax.ShapeDtypeStruct((n1, n2 * d), F32)),
        compiler_params=_params("arbitrary"),
        name="fourier1",
    )(xv, mods, *cmat, tc, ts)

    m2 = _dft_mats(n2)
    mat2 = _full((n2, n2))
    norm = 1.0 / math.sqrt(n_lat * cw)
    out_shapes = (
        jax.ShapeDtypeStruct((n1, n2 * d), F32),
        jax.ShapeDtypeStruct((n1, n2 * d), F32),
        jax.ShapeDtypeStruct((n1, n2 * LANE), I32),
        jax.ShapeDtypeStruct((n1, n2 * LANE), F32),
        _COUNT_SHAPE,
    )
    strided = pl.BlockSpec((n2, d), lambda k: (0, k))
    strided_l = pl.BlockSpec((n2, LANE), lambda k: (0, k))
    outs = pl.pallas_call(
        functools.partial(_fourier2_kernel, norm=norm),
        grid=(n1,),
        in_specs=[
            pl.BlockSpec((1, n2, d), lambda k: (k, 0, 0)),
            pl.BlockSpec((1, n2, d), lambda k: (k, 0, 0)),
            mat2, mat2, mat2, mat2,
            _full((d, d)), _full((1, d)),
            strided,
            pl.BlockSpec((1, N_MOD, d), lambda k: (0, 0, 0)),
            _full((1, d)), _full((1, d)),
            _full((d, LANE)), _full((d, LANE)), _full((1, LANE)),
        ],
        out_specs=(strided, strided, strided_l, strided_l, _COUNT_SPEC),
        out_shape=out_shapes,
        compiler_params=_params("arbitrary"),
        name="fourier2",
    )(ur.reshape(n1, n2, d), ui.reshape(n1, n2, d), *m2, w_bf, b2, xv, mods, lng2, lnb2, *router)
    lat = (outs[0].reshape(n_lat, d), outs[1].reshape(n_lat, d),
           outs[2].reshape(n_lat, LANE), outs[3].reshape(n_lat, LANE))

    cn = _dft_mats(n_ctx)
    matn = _full((n_ctx, n_ctx))
    shapes = (
        jax.ShapeDtypeStruct((n_ctx, d), F32), jax.ShapeDtypeStruct((n_ctx, d), F32),
        jax.ShapeDtypeStruct((n_ctx, LANE), I32), jax.ShapeDtypeStruct((n_ctx, LANE), F32), _COUNT_SHAPE,
    )
    ctx = pl.pallas_call(
        functools.partial(_fourier_ctx_kernel, norm=1.0 / math.sqrt(n_ctx * cw)),
        grid=(1,),
        in_specs=[
            _full((n_ctx, d)),
            pl.BlockSpec((1, N_MOD, d), lambda i: (1, 0, 0)),
            mat, mat, mat, mat, matn, matn, matn, matn,
            _full((d, d)), _full((1, d)), _full((1, d)), _full((1, d)),
            _full((d, LANE)), _full((d, LANE)), _full((1, LANE)),
        ],
        out_specs=(_full((n_ctx, d)), _full((n_ctx, d)), _full((n_ctx, LANE)), _full((n_ctx, LANE)),
                   _COUNT_SPEC),
        out_shape=shapes,
        compiler_params=_params("arbitrary"),
        name="fourier_ctx",
    )(x_ctx, mods, *cmat, *cn, w_bf, b2, lng2, lnb2, *router)
    return tuple(jnp.concatenate([a, b], axis=0) for a, b in zip(lat, ctx[:4])) + (outs[4] + ctx[4],)


def _qkv_kernel(pos_ref, *refs, n_qk, rms):
    mod_ref, w_ref, b_ref, cos_ref, sin_ref, *rest = refs[N_COMBINE_REFS:]
    if rms:
        gain_ref, ind_ref, indt_ref, x_ref, q_ref, k_ref, v_ref, ybuf, sem = rest
    else:
        x_ref, q_ref, k_ref, v_ref, ybuf, sem = rest
    x = _combine_tile(pos_ref, *refs[:N_COMBINE_REFS], ybuf, sem)
    x_ref[...] = x
    mod = mod_ref[0]
    h = x * (1.0 + mod[1:2]) + mod[0:1]
    y = _dot(h.astype(BF16), w_ref[...]) + b_ref[...]
    qk = y[:, :n_qk]
    if rms:
        sh, sl = _split(qk * qk)
        ms = _dot(sh, ind_ref[...]) + _dot(sl, ind_ref[...])
        mh, ml = _split(ms)
        msb = _dot(mh, indt_ref[...]) + _dot(ml, indt_ref[...])
        qk = qk * lax.rsqrt(msb + RMS_EPS) * gain_ref[...]
    cos = cos_ref[...]
    sin = sin_ref[...]
    even = (lax.broadcasted_iota(I32, cos.shape, 1) & 1) == 0
    parts = []
    for c in range(n_qk // LANE):
        z = qk[:, c * LANE:(c + 1) * LANE]
        swapped = jnp.where(even, pltpu.roll(z, LANE - 1, 1), pltpu.roll(z, 1, 1))
        parts.append(z * cos + swapped * sin)
    nq = q_ref.shape[1]
    q_ref[...] = (jnp.concatenate(parts[:nq // LANE], axis=1) * Q_SCALE).astype(BF16)
    k_ref[...] = jnp.concatenate(parts[nq // LANE:], axis=1).astype(BF16)
    v_ref[...] = y[:, n_qk:].astype(BF16)


def _qkv(pending, mods, w_qkv, b_qkv, cos_t, sin_t, n_q, n_kv, n_lat, q_norm=None, k_norm=None):
    t_all, d = pending[3].shape
    nq, nk = n_q * HEAD_DIM, n_kv * HEAD_DIM
    n_qk, n_all = nq + nk, nq + 2 * nk
    rms = q_norm is not None
    pos, args, in_specs, scratch = _combine_operands(pending, TM, n_lat)
    row = lambda w: pl.BlockSpec((TM, w), lambda t, *_: (t, 0))
    in_specs += [_tile_specs(n_lat // TM)[1], _full((d, n_all)), _full((1, n_all)), row(LANE), row(LANE)]
    args += [mods, w_qkv.astype(BF16), b_qkv.reshape(1, n_all), cos_t, sin_t]
    if rms:
        gain = jnp.concatenate([jnp.tile(q_norm, n_q), jnp.tile(k_norm, n_kv)]).reshape(1, n_qk)
        head = np.arange(n_qk) // HEAD_DIM
        ind = np.zeros((n_qk, LANE), np.float32)
        ind[np.arange(n_qk), head] = 1.0 / HEAD_DIM
        indt = np.zeros((LANE, n_qk), np.float32)
        indt[head, np.arange(n_qk)] = 1.0
        in_specs += [_full((1, n_qk)), _full((n_qk, LANE)), _full((LANE, n_qk))]
        args += [gain, jnp.asarray(ind, BF16), jnp.asarray(indt, BF16)]
    grid_spec = pltpu.PrefetchScalarGridSpec(
        num_scalar_prefetch=1,
        grid=(t_all // TM,),
        in_specs=in_specs,
        out_specs=(row(d), row(nq), row(nk), row(nk)),
        scratch_shapes=scratch,
    )
    return pl.pallas_call(
        functools.partial(_qkv_kernel, n_qk=n_qk, rms=rms),
        grid_spec=grid_spec,
        out_shape=(jax.ShapeDtypeStruct((t_all, d), F32), jax.ShapeDtypeStruct((t_all, nq), BF16),
                   jax.ShapeDtypeStruct((t_all, nk), BF16), jax.ShapeDtypeStruct((t_all, nk), BF16)),
        compiler_params=_params("arbitrary"),
        name="qkv",
    )(pos, *args)


def _rope_tables(n_lat, n_ctx):
    rows = n_lat // GRID_W
    row = jnp.repeat(jnp.arange(rows, dtype=F32), GRID_W)
    col = jnp.tile(jnp.arange(GRID_W, dtype=F32), rows)
    n_freq = HEAD_DIM // 4
    inv = ROPE_THETA ** (-jnp.arange(n_freq, dtype=F32) / n_freq)
    ang = jnp.concatenate([row[:, None] * inv, col[:, None] * inv], axis=-1)
    ang = jnp.concatenate([ang, jnp.zeros((n_ctx, HEAD_DIM // 2), F32)], axis=0)
    cos = jnp.tile(jnp.repeat(jnp.cos(ang), 2, axis=1), (1, LANE // HEAD_DIM))
    sin = jnp.tile(jnp.repeat(jnp.sin(ang), 2, axis=1), (1, LANE // HEAD_DIM))
    sign = jnp.where(jnp.arange(LANE) % 2 == 0, -1.0, 1.0).astype(F32)
    return cos, sin * sign


def _kv_layouts(k, v, n_kv):
    t_all = k.shape[0]
    kh = k.reshape(t_all, n_kv, HEAD_DIM).transpose(1, 0, 2)
    vt = v.reshape(t_all, n_kv, HEAD_DIM).transpose(1, 2, 0)
    pad = jnp.zeros((n_kv, LANE - HEAD_DIM, t_all), BF16).at[:, 0, :].set(1.0)
    return kh, jnp.concatenate([vt, pad], axis=1)


def _flash_kernel(q_ref, k_ref, vt_ref, o_ref, s_even, s_odd, *, grp, tk, n_lat, n_ctx):
    tq = q_ref.shape[0]
    q_t = q_ref[...].astype(F32).T
    qt = jnp.concatenate([q_t[g * HEAD_DIM:(g + 1) * HEAD_DIM] for g in range(grp)], axis=1).astype(BF16)
    cols = qt.shape[1]
    is_ctx = pl.program_id(1) == n_lat // FLASH_TQ
    n_chunks = (n_lat + n_ctx) // tk

    def finish(acc):
        o = acc[:HEAD_DIM] / acc[HEAD_DIM:HEAD_DIM + 1]
        o_t = jnp.concatenate([o[:, g * tq:(g + 1) * tq] for g in range(grp)], axis=0)
        o_ref[...] = o_t.T.astype(BF16)

    groups = [slice(c * MXU_WIDTH, (c + 1) * MXU_WIDTH) for c in range(cols // MXU_WIDTH)]

    def scores(j, buf, g):
        off = pl.multiple_of(j * tk, tk)
        s = _dot(k_ref[0, pl.ds(off, tk), :], qt[:, g])
        buf[:, g] = s
        return jnp.max(s, axis=0, keepdims=True)

    def accumulate(j, buf, g, m, acc, mc):
        off = pl.multiple_of(j * tk, tk)
        m_new = jnp.maximum(m, mc)
        p = jnp.exp2(buf[:, g] - m_new)
        acc = jnp.exp2(m - m_new) * acc + _dot(vt_ref[0, :, pl.ds(off, tk)], p.astype(BF16))
        return m_new, acc

    @pl.when(jnp.logical_not(is_ctx))
    def _():
        def step(j, cur, nxt, carry):
            out = []
            for g, (m, acc, mc) in zip(groups, carry):
                mc_next = scores(j + 1, nxt, g)
                out.append(accumulate(j, cur, g, m, acc, mc) + (mc_next,))
            return out

        def body(jj, carry):
            for pair in range(FLASH_PAIRS):
                j = 2 * (FLASH_PAIRS * jj + pair)
                carry = step(j, s_even, s_odd, carry)
                carry = step(j + 1, s_odd, s_even, carry)
            return carry

        init = [(jnp.full((1, MXU_WIDTH), NEG, F32), jnp.zeros((LANE, MXU_WIDTH), F32), scores(0, s_even, g))
                for g in groups]
        carry = lax.fori_loop(0, (n_chunks - 1) // (2 * FLASH_PAIRS), body, init)
        finish(jnp.concatenate([accumulate(n_chunks - 1, s_even, g, m, acc, mc)[1]
                                for g, (m, acc, mc) in zip(groups, carry)], axis=1))

    @pl.when(is_ctx)
    def _():
        s = _dot(k_ref[0, n_lat:n_lat + n_ctx, :], qt)
        p = jnp.exp2(s - jnp.max(s, axis=0, keepdims=True))
        finish(_dot(vt_ref[0, :, n_lat:n_lat + n_ctx], p.astype(BF16)))


def _full_attention(q, k, v, n_lat, n_ctx):
    t_all, nq = q.shape
    n_kv = k.shape[1] // HEAD_DIM
    grp = nq // HEAD_DIM // n_kv
    nt = t_all // FLASH_TQ
    cols = grp * FLASH_TQ
    assert n_ctx == FLASH_TQ and t_all % FLASH_TK == 0 and (t_all // FLASH_TK - 1) % (2 * FLASH_PAIRS) == 0
    kh, vt = _kv_layouts(k, v, n_kv)
    return pl.pallas_call(
        functools.partial(_flash_kernel, grp=grp, tk=FLASH_TK, n_lat=n_lat, n_ctx=n_ctx),
        grid=(n_kv, nt),
        in_specs=[
            pl.BlockSpec((FLASH_TQ, grp * HEAD_DIM), lambda h, i: (i, h)),
            pl.BlockSpec((1, t_all, HEAD_DIM), lambda h, i: (h, 0, 0)),
            pl.BlockSpec((1, LANE, t_all), lambda h, i: (h, 0, 0)),
        ],
        out_specs=pl.BlockSpec((FLASH_TQ, grp * HEAD_DIM), lambda h, i: (i, h)),
        out_shape=jax.ShapeDtypeStruct((t_all, nq), BF16),
        scratch_shapes=[pltpu.VMEM((FLASH_TK, cols), F32), pltpu.VMEM((FLASH_TK, cols), F32)],
        compiler_params=_params("arbitrary", "arbitrary"),
        name="flash",
    )(q, kh, vt)


def _window_kernel(q_ref, kp_ref, kc_ref, kn_ref, kx_ref, vp_ref, vc_ref, vn_ref, vx_ref, sink_ref, o_ref,
                   *, grp, nb):
    i = pl.program_id(1)
    q_t = q_ref[...].astype(F32).T
    qt = jnp.concatenate([q_t[g * HEAD_DIM:(g + 1) * HEAD_DIM] for g in range(grp)], axis=1).astype(BF16)
    cols = qt.shape[1]
    s = _dot(jnp.concatenate([kp_ref[0], kc_ref[0], kn_ref[0], kx_ref[0]], axis=0), qt)
    kj = lax.broadcasted_iota(I32, (Q_BLOCK, cols), 0)
    qi = lax.broadcasted_iota(I32, (Q_BLOCK, cols), 1) & (Q_BLOCK - 1)
    sp = jnp.where(kj >= jnp.where(i > 0, qi, Q_BLOCK), s[:Q_BLOCK], NEG)
    sn = jnp.where(kj <= jnp.where(i < nb - 1, qi, -1), s[2 * Q_BLOCK:3 * Q_BLOCK], NEG)
    s = jnp.concatenate([sp, s[Q_BLOCK:2 * Q_BLOCK], sn, s[3 * Q_BLOCK:]], axis=0)
    sink = sink_ref[0]
    m = jnp.maximum(jnp.max(s, axis=0, keepdims=True), sink)
    p = jnp.exp2(s - m).astype(BF16)
    acc = _dot(jnp.concatenate([vp_ref[0], vc_ref[0], vn_ref[0], vx_ref[0]], axis=1), p)
    o = acc[:HEAD_DIM] / (acc[HEAD_DIM:HEAD_DIM + 1] + jnp.exp2(sink - m))
    o_t = jnp.concatenate([o[:, g * Q_BLOCK:(g + 1) * Q_BLOCK] for g in range(grp)], axis=0)
    o_ref[...] = o_t.T.astype(BF16)


def _window_attention(q, k, v, sink, n_lat, n_ctx):
    nq = q.shape[1]
    n_kv = k.shape[1] // HEAD_DIM
    grp = nq // HEAD_DIM // n_kv
    gw = grp * HEAD_DIM
    nb = n_lat // Q_BLOCK
    cb = n_lat // n_ctx
    kh, vt = _kv_layouts(k, v, n_kv)
    sink_cols = jnp.repeat(sink.reshape(n_kv, 1, grp) * LOG2E, Q_BLOCK, axis=2)
    kspec = lambda f: pl.BlockSpec((1, Q_BLOCK, HEAD_DIM), lambda h, i: (h, f(i), 0))
    vspec = lambda f: pl.BlockSpec((1, LANE, Q_BLOCK), lambda h, i: (h, 0, f(i)))
    prev = lambda i: jnp.maximum(i - 1, 0)
    cur = lambda i: i
    nxt = lambda i: jnp.minimum(i + 1, nb - 1)
    return pl.pallas_call(
        functools.partial(_window_kernel, grp=grp, nb=nb),
        grid=(n_kv, nb),
        in_specs=[
            pl.BlockSpec((Q_BLOCK, gw), lambda h, i: (i, h)),
            kspec(prev), kspec(cur), kspec(nxt),
            pl.BlockSpec((1, n_ctx, HEAD_DIM), lambda h, i: (h, cb, 0)),
            vspec(prev), vspec(cur), vspec(nxt),
            pl.BlockSpec((1, LANE, n_ctx), lambda h, i: (h, 0, cb)),
            pl.BlockSpec((1, 1, grp * Q_BLOCK), lambda h, i: (h, 0, 0)),
        ],
        out_specs=pl.BlockSpec((Q_BLOCK, gw), lambda h, i: (i, h)),
        out_shape=jax.ShapeDtypeStruct((n_lat, nq), BF16),
        compiler_params=_params("arbitrary", "arbitrary"),
        name="window",
    )(q, kh, kh, kh, kh, vt, vt, vt, vt, sink_cols)


def _gmlp_kernel(pos_ref, *refs):
    (mod_ref, win_ref, bin_ref, vg_ref, vb_ref, ws_ref, bs_ref, wout_ref, bout_ref,
     lng_ref, lnb_ref, rwh_ref, rwl_ref, rb_ref, *rest) = refs[N_COMBINE_REFS:]
    out_refs, (ybuf, sem) = rest[:-2], rest[-2:]
    mod = mod_ref[0]
    x = _combine_tile(pos_ref, *refs[:N_COMBINE_REFS], ybuf, sem)
    h = x * (1.0 + mod[1:2]) + mod[0:1]
    z = _dot(h.astype(BF16), win_ref[...]) + bin_ref[...]
    z = 0.5 * z * (1.0 + lax.erf(z * (2.0 ** -0.5)))
    half = z.shape[1] // 2
    u = z[:, :half]
    v = _layer_norm(z[:, half:], vg_ref[...], vb_ref[...]).astype(BF16)
    cw = half // GM_GROUPS
    chunks = []
    for c in range(x.shape[0] // GM_CHUNK):
        vc = v[c * GM_CHUNK:(c + 1) * GM_CHUNK]
        chunks.append(jnp.concatenate(
            [_dot(ws_ref[g], vc[:, g * cw:(g + 1) * cw]) + bs_ref[g] for g in range(GM_GROUPS)], axis=1))
    gated = u * jnp.concatenate(chunks, axis=0)
    y = _dot(gated.astype(BF16), wout_ref[...]) + bout_ref[...]
    _post(y, x, mod, lng_ref[...], lnb_ref[...], rwh_ref[...], rwl_ref[...], rb_ref[...], out_refs)


def _gmlp_layer(pending, mods, w_in, b_in, vg, vb, w_s, b_s, w_out, b_out, lng, lnb, router, n_lat):
    t_all, d = pending[3].shape
    dffn = w_in.shape[1]
    half = dffn // 2
    cw = half // GM_GROUPS
    shapes, specs = _post_out(t_all)
    bs_full = jnp.broadcast_to(b_s[:, :, None], (GM_GROUPS, GM_CHUNK, cw))
    pos, args, in_specs, scratch = _combine_operands(pending, TM, n_lat)
    in_specs += [
        _tile_specs(n_lat // TM)[1], _full((d, dffn)), _full((1, dffn)), _full((1, half)), _full((1, half)),
        _full((GM_GROUPS, GM_CHUNK, GM_CHUNK)), _full((GM_GROUPS, GM_CHUNK, cw)),
        _full((half, d)), _full((1, d)), _full((1, d)), _full((1, d)),
        _full((d, LANE)), _full((d, LANE)), _full((1, LANE)),
    ]
    args += [mods, w_in.astype(BF16), b_in.reshape(1, dffn), vg.reshape(1, half), vb.reshape(1, half),
             w_s.astype(BF16), bs_full, w_out.astype(BF16), b_out.reshape(1, d), lng.reshape(1, d),
             lnb.reshape(1, d), *router]
    grid_spec = pltpu.PrefetchScalarGridSpec(
        num_scalar_prefetch=1,
        grid=(t_all // TM,),
        in_specs=in_specs,
        out_specs=specs,
        scratch_shapes=scratch,
    )
    return pl.pallas_call(
        _gmlp_kernel,
        grid_spec=grid_spec,
        out_shape=shapes,
        compiler_params=_params("arbitrary"),
        name="gmlp",
    )(pos, *args)


COMB_TM = 128


def _route_meta(count_blk, n_tok):
    n_assign = n_tok * TOP_K
    n_blocks = -(-n_assign // MOE_BLOCK) + N_EXPERTS
    ids = jnp.arange(N_EXPERTS, dtype=I32)
    counts = count_blk[0, :N_EXPERTS].astype(I32)
    padded = (counts + MOE_BLOCK - 1) // MOE_BLOCK * MOE_BLOCK
    ends_pad = jnp.cumsum(padded)
    base = ends_pad - padded
    starts = jnp.arange(n_blocks, dtype=I32) * MOE_BLOCK
    block_expert = jnp.minimum(jnp.sum((ends_pad[None, :] <= starts[:, None]).astype(I32), axis=1), N_EXPERTS - 1)
    last_blk = jnp.where(padded > 0, ends_pad - MOE_BLOCK, -1)
    tail = ends_pad[-1] + ids * MOE_BLOCK
    tail = jnp.where(tail < n_blocks * MOE_BLOCK, tail, -1)
    base_b = jnp.broadcast_to(base.astype(F32)[:, None], (N_EXPERTS, LANE))
    return block_expert.astype(I32), base_b, jnp.concatenate([last_blk, tail]).astype(I32), n_blocks


def _rank_kernel(idx_ref, base_ref, upper_ref, dest_ref, run_ref):
    @pl.when(pl.program_id(0) == 0)
    def _():
        run_ref[...] = jnp.zeros(run_ref.shape, F32)

    eid = lax.broadcasted_iota(I32, (N_EXPERTS, TM), 0)
    onehots = [(eid == idx_ref[k:k + 1, :]).astype(F32) for k in range(TOP_K)]
    cnt = onehots[0] + onehots[1] + onehots[2] + onehots[3]
    before = _dot(cnt.astype(BF16), upper_ref[...])
    slot = base_ref[:, :1] + run_ref[:, :1] + before
    for k in range(TOP_K):
        rows = (jnp.sum(onehots[k] * slot, axis=0, keepdims=True) * ROW_TILE).astype(I32)
        for j in range(TM // COMB_TM):
            dest_ref[j, k:k + 1, :] = rows[:, j * COMB_TM:(j + 1) * COMB_TM]
    run_ref[...] = run_ref[...] + jnp.sum(cnt, axis=1, keepdims=True)


def _rank(top_idx, base_b, n_tok):
    per_step = TM // COMB_TM
    idx_t = top_idx[:n_tok, :TOP_K].T
    upper = jnp.asarray(np.triu(np.ones((TM, TM), np.float32), 1), BF16)
    return pl.pallas_call(
        _rank_kernel,
        grid=(n_tok // TM,),
        in_specs=[pl.BlockSpec((TOP_K, TM), lambda t: (0, t)), _full((N_EXPERTS, LANE)), _full((TM, TM))],
        out_specs=pl.BlockSpec((per_step, TOP_K, COMB_TM), lambda t: (t, 0, 0)),
        out_shape=jax.ShapeDtypeStruct((n_tok // COMB_TM, TOP_K, COMB_TM), I32),
        scratch_shapes=[pltpu.VMEM((N_EXPERTS, LANE), F32)],
        compiler_params=_params("arbitrary"),
        name="rank",
    )(idx_t, base_b, upper)


def _dispatch_kernel(dest_ref, zrow_ref, h_ref, xs_hbm, hbuf, zbuf, sem, zsem, *, n_tiles):
    t = pl.program_id(0)
    slot = t % 2
    n_zero = 2 * N_EXPERTS

    def zero_copy(j):
        row = pl.multiple_of(zrow_ref[j] * ROW_TILE, MOE_BLOCK * ROW_TILE)
        return pltpu.make_async_copy(zbuf, xs_hbm.at[pl.ds(row, MOE_BLOCK * ROW_TILE), :], zsem.at[0])

    def wait_rows(s):
        for _ in range(TOP_K):
            pltpu.make_async_copy(hbuf.at[s], xs_hbm.at[pl.ds(0, COMB_TM * ROW_TILE), :], sem.at[s]).wait()

    @pl.when(t == 0)
    def _():
        zbuf[...] = jnp.zeros(zbuf.shape, F32)
        for j in range(n_zero):
            @pl.when(zrow_ref[j] >= 0)
            def _():
                zero_copy(j).start()
        for j in range(n_zero):
            @pl.when(zrow_ref[j] >= 0)
            def _():
                zero_copy(j).wait()

    @pl.when(t >= 2)
    def _():
        wait_rows(slot)

    _to_row_tiles(hbuf.at[slot], h_ref[...])
    for k in range(TOP_K):
        for r in range(COMB_TM):
            row = pl.multiple_of(dest_ref[(t * TOP_K + k) * COMB_TM + r], ROW_TILE)
            pltpu.make_async_copy(hbuf.at[slot, pl.ds(r * ROW_TILE, ROW_TILE), :],
                                  xs_hbm.at[pl.ds(row, ROW_TILE), :], sem.at[slot]).start(priority=r % 2)

    @pl.when(t == n_tiles - 1)
    def _():
        wait_rows(slot)
        if n_tiles > 1:
            wait_rows(1 - slot)


def _dispatch(h2, dest, zrows, n_tok, n_blocks):
    d = h2.shape[1]
    n_tiles = n_tok // COMB_TM
    grid_spec = pltpu.PrefetchScalarGridSpec(
        num_scalar_prefetch=2,
        grid=(n_tiles,),
        in_specs=[pl.BlockSpec((COMB_TM, d), lambda t, dest, zr: (t, 0))],
        out_specs=pl.BlockSpec(memory_space=pl.ANY),
        scratch_shapes=[pltpu.VMEM((2, COMB_TM * ROW_TILE, LANE), F32),
                        pltpu.VMEM((MOE_BLOCK * ROW_TILE, LANE), F32),
                        pltpu.SemaphoreType.DMA((2,)), pltpu.SemaphoreType.DMA((1,))],
    )
    return pl.pallas_call(
        functools.partial(_dispatch_kernel, n_tiles=n_tiles),
        grid_spec=grid_spec,
        out_shape=jax.ShapeDtypeStruct((n_blocks * MOE_BLOCK * ROW_TILE, LANE), F32),
        compiler_params=_params("arbitrary"),
        name="dispatch",
    )(dest.reshape(-1), zrows, h2)


EXP_CHUNK = MXU_WIDTH


def _expert_kernel(be_ref, x_ref, wgu_ref, bgu_ref, wd_ref, bd_ref, sel_ref, y_ref,
                   wg_s, wl_s, wd_s, bg_s, bl_s):
    b = pl.program_id(0)
    ff = wg_s.shape[1]
    half = EXP_CHUNK // 2

    @pl.when(jnp.logical_or(b == 0, be_ref[b] != be_ref[jnp.maximum(b - 1, 0)]))
    def _():
        for c in range(2 * ff // EXP_CHUNK):
            w = _dot(wgu_ref[0, 0, :, c * EXP_CHUNK:(c + 1) * EXP_CHUNK].astype(BF16), sel_ref[...])
            wg_s[:, c * half:(c + 1) * half] = w[:, :half].astype(BF16)
            wl_s[:, c * half:(c + 1) * half] = w[:, half:].astype(BF16)
            bh, bl = _split(jnp.broadcast_to(bgu_ref[0, 0, :, c * EXP_CHUNK:(c + 1) * EXP_CHUNK],
                                             (SUBLANE, EXP_CHUNK)))
            bias = _dot(bh, sel_ref[...]) + _dot(bl, sel_ref[...])
            bg_s[:, c * half:(c + 1) * half] = bias[:, :half]
            bl_s[:, c * half:(c + 1) * half] = bias[:, half:]
        wd_s[...] = wd_ref[0, 0].astype(BF16)

    x = _from_row_tiles(x_ref, 0, MOE_BLOCK).astype(BF16)
    glu = jnp.minimum(_dot(x, wg_s[...]) + bg_s[0:1, :], SWIGLU_LIMIT)
    lin = jnp.clip(_dot(x, wl_s[...]) + bl_s[0:1, :], -SWIGLU_LIMIT, SWIGLU_LIMIT)
    act = glu * (1.0 / (1.0 + jnp.exp(-SWIGLU_ALPHA * glu))) * (lin + 1.0)
    _to_row_tiles(y_ref, _dot(act.astype(BF16), wd_s[...]) + bd_ref[0, 0])


def _experts(xs, block_expert, n_blocks, w_gate_up, b_gate_up, w_down, b_down, layer):
    d = w_down.shape[3]
    ff = w_down.shape[2]
    half = EXP_CHUNK // 2
    sel = np.zeros((EXP_CHUNK, EXP_CHUNK), np.float32)
    sel[2 * np.arange(half), np.arange(half)] = 1.0
    sel[2 * np.arange(half) + 1, half + np.arange(half)] = 1.0
    wspec = lambda r, c: pl.BlockSpec((1, 1, r, c), lambda b, be: (layer, be[b], 0, 0))
    sspec = pl.BlockSpec((EXP_CHUNK, EXP_CHUNK), lambda b, be: (0, 0))
    grid_spec = pltpu.PrefetchScalarGridSpec(
        num_scalar_prefetch=1,
        grid=(n_blocks,),
        in_specs=[pl.BlockSpec((MOE_BLOCK * ROW_TILE, LANE), lambda b, be: (b, 0)), wspec(d, 2 * ff),
                  wspec(1, 2 * ff), wspec(ff, d), wspec(1, d), sspec],
        out_specs=pl.BlockSpec((MOE_BLOCK * ROW_TILE, LANE), lambda b, be: (b, 0)),
        scratch_shapes=[pltpu.VMEM((d, ff), BF16), pltpu.VMEM((d, ff), BF16), pltpu.VMEM((ff, d), BF16),
                        pltpu.VMEM((SUBLANE, ff), F32), pltpu.VMEM((SUBLANE, ff), F32)],
    )
    return pl.pallas_call(
        _expert_kernel,
        grid_spec=grid_spec,
        out_shape=jax.ShapeDtypeStruct((n_blocks * MOE_BLOCK * ROW_TILE, LANE), F32),
        compiler_params=_params("arbitrary"),
        name="experts",
    )(block_expert, xs, w_gate_up, b_gate_up[:, :, None, :], w_down, b_down[:, :, None, :],
      jnp.asarray(sel, BF16))


def _gather_rows(idx_ref, base, n, src_hbm, dst, sem):
    for r in range(n):
        row = pl.multiple_of(idx_ref[base + r], ROW_TILE)
        pltpu.make_async_copy(src_hbm.at[pl.ds(row, ROW_TILE), :], dst.at[pl.ds(r * ROW_TILE, ROW_TILE), :],
                              sem).start(priority=r % 2)


N_COMBINE_REFS = 6


def _combine_tile(pos_ref, y_hbm, gate_ref, x1_ref, mod_ref, lng_ref, lnb_ref, ybuf, sem):
    t = pl.program_id(0)
    slot = t % 2
    tm, d = x1_ref.shape
    n = tm * TOP_K

    @pl.when(t == 0)
    def _():
        _gather_rows(pos_ref, 0, n, y_hbm, ybuf.at[0], sem.at[0])

    @pl.when(t + 1 < pl.num_programs(0))
    def _():
        _gather_rows(pos_ref, (t + 1) * n, n, y_hbm, ybuf.at[1 - slot], sem.at[1 - slot])

    pltpu.make_async_copy(y_hbm.at[pl.ds(0, n * ROW_TILE), :], ybuf.at[slot], sem.at[slot]).wait()
    gates = gate_ref[...]
    parts = []
    for j in range(tm // COMB_TM):
        f = jnp.zeros((COMB_TM, d), F32)
        for k in range(TOP_K):
            rows = _from_row_tiles(ybuf.at[slot], (j * TOP_K + k) * COMB_TM, COMB_TM)
            f = f + gates[j * COMB_TM:(j + 1) * COMB_TM, k:k + 1] * rows
        parts.append(f)
    f = parts[0] if len(parts) == 1 else jnp.concatenate(parts, axis=0)
    return _layer_norm(ALPHA * x1_ref[...] + mod_ref[0][5:6] * f, lng_ref[...], lnb_ref[...])


def _combine_operands(pending, tm, n_lat):
    yb, dest, gates, x1, mods, lng, lnb = pending
    d = x1.shape[1]
    n_lat_tiles = n_lat // tm
    specs = [
        pl.BlockSpec(memory_space=pl.ANY),
        pl.BlockSpec((tm, LANE), lambda t, *_: (t, 0)),
        pl.BlockSpec((tm, d), lambda t, *_: (t, 0)),
        pl.BlockSpec((1, N_MOD, d), lambda t, *_: (jnp.where(t >= n_lat_tiles, 1, 0), 0, 0)),
        _full((1, d)), _full((1, d)),
    ]
    scratch = [pltpu.VMEM((2, tm * TOP_K * ROW_TILE, LANE), F32), pltpu.SemaphoreType.DMA((2,))]
    return dest.reshape(-1), [yb, gates, x1, mods, lng.reshape(1, d), lnb.reshape(1, d)], specs, scratch


def _combine_kernel(pos_ref, *refs):
    o_ref, ybuf, sem = refs[N_COMBINE_REFS:]
    o_ref[...] = _combine_tile(pos_ref, *refs[:N_COMBINE_REFS], ybuf, sem)


def _combine(pending, n_tok, n_lat):
    d = pending[3].shape[1]
    pos, args, specs, scratch = _combine_operands(pending, COMB_TM, n_lat)
    grid_spec = pltpu.PrefetchScalarGridSpec(
        num_scalar_prefetch=1,
        grid=(n_tok // COMB_TM,),
        in_specs=specs,
        out_specs=pl.BlockSpec((COMB_TM, d), lambda t, *_: (t, 0)),
        scratch_shapes=scratch,
    )
    return pl.pallas_call(
        _combine_kernel,
        grid_spec=grid_spec,
        out_shape=jax.ShapeDtypeStruct((n_tok, d), F32),
        compiler_params=_params("arbitrary"),
        name="combine",
    )(pos, *args)


def _moe_layer(x1, h2, top_idx, gates, counts, mods, lng, lnb, w_gate_up, b_gate_up, w_down, b_down, layer, n_tok):
    block_expert, base_b, zrows, n_blocks = _route_meta(counts, n_tok)
    dest = _rank(top_idx, base_b, n_tok)
    xs = _dispatch(h2, dest, zrows, n_tok, n_blocks)
    yb = _experts(xs, block_expert, n_blocks, w_gate_up, b_gate_up, w_down, b_down, layer)
    return yb, dest, gates, x1, mods, lng, lnb


def kernel(x, c, ctx, c_ctx, ada_w, ada_b, ln_mix_g, ln_mix_b, ln_ffn_g, ln_ffn_b, fn_w_out, fn_b_out, fa_w_qkv, fa_b_qkv, fa_q_norm, fa_k_norm, fa_w_out, fa_b_out, gm_w_in, gm_b_in, gm_v_norm_g, gm_v_norm_b, gm_w_s, gm_b_s, gm_w_out, gm_b_out, wa_w_qkv, wa_b_qkv, wa_sink, wa_w_out, wa_b_out, router_w, router_b, exp_w_gate_up, exp_b_gate_up, exp_w_down, exp_b_down):
    bsz, n_lat, d = x.shape
    n_ctx = ctx.shape[1]
    assert bsz == 1 and d == D_MODEL and n_lat == LANE * LANE and n_lat % n_ctx == 0 and n_ctx % TM == 0
    t_all = n_lat + n_ctx
    n_lat_tiles = n_lat // TM
    mods_all = _ada(c, c_ctx, ada_w, ada_b)
    cos_t, sin_t = _rope_tables(n_lat, n_ctx)

    for i in range(DEPTH):
        kind, j = i % 4, i // 4
        last = i == DEPTH - 1
        n_tok = n_lat if last else t_all
        mods = mods_all[i]
        router = _router_operands(router_w[i], router_b[i])
        lng, lnb = ln_mix_g[i], ln_mix_b[i]
        if kind == 0:
            post = _fourier_layer(x[0], ctx[0], mods, fn_w_out[j], fn_b_out[j], lng, lnb, router)
        elif kind == 1:
            x_all, q, k, v = _qkv(pending, mods, fa_w_qkv[j], fa_b_qkv[j], cos_t, sin_t, FA_Q_HEADS,
                                  FA_KV_HEADS, n_lat, fa_q_norm[j], fa_k_norm[j])
            o = _full_attention(q, k, v, n_lat, n_ctx)
            post = _proj_post(o, fa_w_out[j], fa_b_out[j], x_all, mods, lng, lnb, router, n_tok, n_lat_tiles)
        elif kind == 2:
            post = _gmlp_layer(pending, mods, gm_w_in[j], gm_b_in[j], gm_v_norm_g[j], gm_v_norm_b[j],
                               gm_w_s[j], gm_b_s[j], gm_w_out[j], gm_b_out[j], lng, lnb, router, n_lat)
        else:
            x_all, q, k, v = _qkv(pending, mods, wa_w_qkv[j], wa_b_qkv[j], cos_t, sin_t, WA_Q_HEADS,
                                  WA_KV_HEADS, n_lat)
            o = _window_attention(q, k, v, wa_sink[j], n_lat, n_ctx)
            post = _proj_post(o, wa_w_out[j], wa_b_out[j], x_all, mods, lng, lnb, router, n_tok, n_lat_tiles)
        pending = _moe_layer(*post, mods, ln_ffn_g[i], ln_ffn_b[i], exp_w_gate_up, exp_b_gate_up, exp_w_down,
                             exp_b_down, i, n_tok)
    return _combine(pending, n_lat, n_lat)[None]
```

```python
import functools
import math

import numpy as np
import jax
import jax.numpy as jnp
from jax import lax
from jax.experimental import pallas as pl
from jax.experimental.pallas import tpu as pltpu

F32, BF16, I32 = jnp.float32, jnp.bfloat16, jnp.int32

D_MODEL = 1024
DEPTH = 4
GRID_W = 64
N_MOD = 6
FN_GROUPS = 4
HEAD_DIM = 64
FA_Q_HEADS, FA_KV_HEADS = 16, 4
WA_Q_HEADS, WA_KV_HEADS = 16, 2
WINDOW = 128
Q_BLOCK = 128
ROPE_THETA = 10000.0
GM_CHUNK = 128
GM_GROUPS = 8
N_EXPERTS = 32
TOP_K = 4
SWIGLU_LIMIT = 7.0
SWIGLU_ALPHA = 1.702
MOE_BLOCK = 256
LN_EPS = 1e-5
RMS_EPS = 1e-6
NEG = -1e30
ALPHA = (2 * DEPTH) ** 0.25
LOG2E = math.log2(math.e)
Q_SCALE = HEAD_DIM ** -0.5 * LOG2E

LANE = 128
SUBLANE = 8
ROW_TILE = D_MODEL // LANE
assert ROW_TILE == SUBLANE
MXU_WIDTH = 256
TM = 256
FLASH_TQ = 256
FLASH_TK = 1280
FLASH_PAIRS = 3
VMEM_LIMIT = 56 * 2 ** 20


def _params(*sem):
    return pltpu.CompilerParams(dimension_semantics=sem, vmem_limit_bytes=VMEM_LIMIT)


def _dot(a, b):
    return jnp.dot(a, b, preferred_element_type=F32)


def _split(a):
    hi = a.astype(BF16)
    lo = (a - hi.astype(F32)).astype(BF16)
    return hi, lo


def _dot3(a_hi, a_lo, b_hi, b_lo):
    return _dot(a_hi, b_hi) + (_dot(a_hi, b_lo) + _dot(a_lo, b_hi))


def _layer_norm(x, g, b):
    mu = jnp.mean(x, axis=-1, keepdims=True)
    xc = x - mu
    var = jnp.mean(xc * xc, axis=-1, keepdims=True)
    return xc * lax.rsqrt(var + LN_EPS) * g + b


def _top4(logits):
    lane = lax.broadcasted_iota(I32, logits.shape, 1).astype(F32)
    cur = logits
    vals, idxs = [], []
    for _ in range(TOP_K):
        m = jnp.max(cur, axis=-1, keepdims=True)
        i = jnp.min(jnp.where(cur == m, lane, float(LANE)), axis=-1, keepdims=True)
        vals.append(m)
        idxs.append(i)
        cur = jnp.where(lane == i, -jnp.inf, cur)
    exps = [jnp.exp(v - vals[0]) for v in vals]
    inv = 1.0 / (exps[0] + exps[1] + exps[2] + exps[3])
    idx_out = jnp.zeros_like(logits)
    gate_out = jnp.zeros_like(logits)
    picked = jnp.zeros_like(logits)
    for k in range(TOP_K):
        idx_out = jnp.where(lane == float(k), idxs[k], idx_out)
        gate_out = jnp.where(lane == float(k), exps[k] * inv, gate_out)
        picked = picked + jnp.where(lane == idxs[k], 1.0, 0.0)
    return idx_out.astype(I32), gate_out, jnp.sum(picked, axis=0, keepdims=True)


def _post(y, x, mod, lng, lnb, rw_hi, rw_lo, rb, out_refs):
    x1_ref, h2_ref, idx_ref, gate_ref, cnt_ref = out_refs
    x1 = _layer_norm(ALPHA * x + mod[2:3] * y, lng, lnb)
    h2 = x1 * (1.0 + mod[4:5]) + mod[3:4]
    hh, hl = _split(h2)
    logits = _dot3(hh, hl, rw_hi, rw_lo) + rb
    idx, gates, cnt = _top4(logits)
    x1_ref[...] = x1
    h2_ref[...] = h2
    idx_ref[...] = idx
    gate_ref[...] = gates

    @pl.when(pl.program_id(0) == 0)
    def _():
        cnt_ref[...] = jnp.zeros(cnt_ref.shape, F32)

    cnt_ref[...] = cnt_ref[...] + cnt


def _to_row_tiles(ref, x):
    n = x.shape[0]
    for s in range(ROW_TILE):
        ref[pl.ds(s, n, stride=ROW_TILE), :] = x[:, s * LANE:(s + 1) * LANE]


def _from_row_tiles(ref, start, n):
    return jnp.concatenate([ref[pl.ds(start * ROW_TILE + s, n, stride=ROW_TILE), :] for s in range(ROW_TILE)],
                           axis=1)


def _ada_kernel(cs_ref, w_ref, b_ref, o_ref):
    cs = cs_ref[...]
    s = cs * (1.0 / (1.0 + jnp.exp(-cs)))
    sh, sl = _split(s)
    wh, wl = _split(w_ref[0])
    o_ref[0] = _dot3(sh, sl, wh, wl) + b_ref[0]


def _ada(c, c_ctx, ada_w, ada_b):
    d = c.shape[-1]
    nm = ada_w.shape[-1]
    tn = nm // 4
    cs = jnp.zeros((SUBLANE, d), F32).at[0].set(c[0]).at[1].set(c_ctx)
    out = pl.pallas_call(
        _ada_kernel,
        grid=(DEPTH, nm // tn),
        in_specs=[
            pl.BlockSpec((SUBLANE, d), lambda i, j: (0, 0)),
            pl.BlockSpec((1, d, tn), lambda i, j: (i, 0, j)),
            pl.BlockSpec((1, 1, tn), lambda i, j: (i, 0, j)),
        ],
        out_specs=pl.BlockSpec((1, SUBLANE, tn), lambda i, j: (i, 0, j)),
        out_shape=jax.ShapeDtypeStruct((DEPTH, SUBLANE, nm), F32),
        compiler_params=_params("arbitrary", "arbitrary"),
        name="ada",
    )(cs, ada_w, ada_b.reshape(DEPTH, 1, nm))
    return out[:, :2].reshape(DEPTH, 2, N_MOD, d)


def _tile_specs(n_lat_tiles):
    tok = pl.BlockSpec((TM, D_MODEL), lambda t, *_: (t, 0))
    mod = pl.BlockSpec((1, N_MOD, D_MODEL), lambda t, *_: (jnp.where(t >= n_lat_tiles, 1, 0), 0, 0))
    return tok, mod


def _full(shape):
    nd = len(shape)
    return pl.BlockSpec(shape, lambda *_: (0,) * nd)


_COUNT_SHAPE = jax.ShapeDtypeStruct((SUBLANE, LANE), F32)
_COUNT_SPEC = pl.BlockSpec((SUBLANE, LANE), lambda *_: (0, 0))


def _post_out(n_rows):
    shapes = (
        jax.ShapeDtypeStruct((n_rows, D_MODEL), F32),
        jax.ShapeDtypeStruct((n_rows, D_MODEL), F32),
        jax.ShapeDtypeStruct((n_rows, LANE), I32),
        jax.ShapeDtypeStruct((n_rows, LANE), F32),
        _COUNT_SHAPE,
    )
    specs = (
        pl.BlockSpec((TM, D_MODEL), lambda t, *_: (t, 0)),
        pl.BlockSpec((TM, D_MODEL), lambda t, *_: (t, 0)),
        pl.BlockSpec((TM, LANE), lambda t, *_: (t, 0)),
        pl.BlockSpec((TM, LANE), lambda t, *_: (t, 0)),
        _COUNT_SPEC,
    )
    return shapes, specs


def _router_operands(router_w, router_b):
    rw = jnp.zeros((D_MODEL, LANE), F32).at[:, :N_EXPERTS].set(router_w)
    rw_hi = rw.astype(BF16)
    rw_lo = (rw - rw_hi.astype(F32)).astype(BF16)
    rb = jnp.full((1, LANE), NEG, F32).at[0, :N_EXPERTS].set(router_b)
    return rw_hi, rw_lo, rb


def _proj_post_kernel(a_ref, w_ref, b_ref, x_ref, mod_ref, lng_ref, lnb_ref, rwh_ref, rwl_ref, rb_ref, *out_refs):
    y = _dot(a_ref[...], w_ref[...]) + b_ref[...]
    _post(y, x_ref[...], mod_ref[0], lng_ref[...], lnb_ref[...], rwh_ref[...], rwl_ref[...], rb_ref[...], out_refs)


def _proj_post(a, w_out, b_out, x, mods, lng, lnb, router, n_rows, n_lat_tiles):
    k = a.shape[1]
    tok, mod = _tile_specs(n_lat_tiles)
    shapes, specs = _post_out(n_rows)
    return pl.pallas_call(
        _proj_post_kernel,
        grid=(n_rows // TM,),
        in_specs=[
            pl.BlockSpec((TM, k), lambda t: (t, 0)),
            _full((k, D_MODEL)), _full((1, D_MODEL)),
            tok, mod, _full((1, D_MODEL)), _full((1, D_MODEL)),
            _full((D_MODEL, LANE)), _full((D_MODEL, LANE)), _full((1, LANE)),
        ],
        out_specs=specs,
        out_shape=shapes,
        compiler_params=_params("arbitrary"),
        name="proj_post",
    )(a, w_out.astype(BF16), b_out.reshape(1, -1), x, mods, lng.reshape(1, -1), lnb.reshape(1, -1), *router)


def _dft_mats(n):
    jk = np.outer(np.arange(n), np.arange(n)) % n
    ang = 2.0 * np.pi * jk / n
    out = []
    for m in (np.cos(ang), np.sin(ang)):
        m32 = jnp.asarray(m, F32)
        hi = m32.astype(BF16)
        out += [hi, (m32 - hi.astype(F32)).astype(BF16)]
    return out


def _channel_dft(h, cc, sc):
    cw = cc[0].shape[0]
    a_parts, b_parts = [], []
    for g in range(h.shape[1] // cw):
        hh, hl = _split(h[:, g * cw:(g + 1) * cw])
        a_parts.append(_dot3(hh, hl, cc[0][...], cc[1][...]))
        b_parts.append(_dot3(hh, hl, sc[0][...], sc[1][...]))
    return jnp.concatenate(a_parts, axis=1), jnp.concatenate(b_parts, axis=1)


def _fourier1_kernel(x_ref, mod_ref, cch_ref, ccl_ref, sch_ref, scl_ref, tc_ref, ts_ref, ur_ref, ui_ref):
    mod = mod_ref[0]
    h = x_ref[...] * (1.0 + mod[1:2]) + mod[0:1]
    a, b = _channel_dft(h, (cch_ref, ccl_ref), (sch_ref, scl_ref))
    tch, tcl = _split(tc_ref[0])
    tsh, tsl = _split(ts_ref[0])
    ah, al = _split(a)
    bh, bl = _split(b)
    ur_ref[...] = _dot3(tch, tcl, ah, al) - _dot3(tsh, tsl, bh, bl)
    ui_ref[...] = -(_dot3(tch, tcl, bh, bl) + _dot3(tsh, tsl, ah, al))


def _fourier2_kernel(ur_ref, ui_ref, c2h_ref, c2l_ref, s2h_ref, s2l_ref, w_ref, b_ref, x_ref, xc_ref, mod_ref,
                     cch_ref, ccl_ref, sch_ref, scl_ref, cnh_ref, cnl_ref, snh_ref, snl_ref,
                     lng_ref, lnb_ref, rwh_ref, rwl_ref, rb_ref,
                     x1_hbm, h2_hbm, idx_hbm, gate_hbm, cnt_ref, bx1, bh2, bidx, bgate, sem,
                     *, n1, norm_lat, norm_ctx):
    k = pl.program_id(0)
    n2 = ur_ref.shape[1]
    slot = k % 2
    bufs = (bx1, bh2, bidx, bgate)
    outs = (x1_hbm, h2_hbm, idx_hbm, gate_hbm)
    common = (lng_ref[...], lnb_ref[...], rwh_ref[...], rwl_ref[...], rb_ref[...])

    def lat_copies(s, col):
        return [pltpu.make_async_copy(b.at[s, pl.ds(0, n2), :], o.at[pl.ds(0, n2), col, :], sem.at[s])
                for b, o in zip(bufs, outs)]

    def ctx_copies(s):
        return [pltpu.make_async_copy(b.at[s, pl.ds(j * n2, n2), :], o.at[n1 + j], sem.at[s])
                for b, o in zip(bufs, outs) for j in range(xc_ref.shape[0] // n2)]

    @pl.when(k >= 2)
    def _():
        for c in lat_copies(slot, 0):
            c.wait()

    @pl.when(k < n1)
    def _():
        urh, url = _split(ur_ref[0])
        uih, uil = _split(ui_ref[0])
        mixed = (_dot3(c2h_ref[...], c2l_ref[...], urh, url)
                 + _dot3(s2h_ref[...], s2l_ref[...], uih, uil)) * norm_lat
        y = _dot(mixed.astype(BF16), w_ref[...]) + b_ref[...]
        _post(y, x_ref[...], mod_ref[0], *common, tuple(b.at[slot, pl.ds(0, n2), :] for b in bufs) + (cnt_ref,))
        for c in lat_copies(slot, k):
            c.start()

    @pl.when(k == n1)
    def _():
        mod = mod_ref[1]
        x = xc_ref[...]
        h = x * (1.0 + mod[1:2]) + mod[0:1]
        a, b = _channel_dft(h, (cch_ref, ccl_ref), (sch_ref, scl_ref))
        ah, al = _split(a)
        bh, bl = _split(b)
        mixed = (_dot3(cnh_ref[...], cnl_ref[...], ah, al) - _dot3(snh_ref[...], snl_ref[...], bh, bl)) * norm_ctx
        y = _dot(mixed.astype(BF16), w_ref[...]) + b_ref[...]
        _post(y, x, mod, *common, tuple(b.at[slot] for b in bufs) + (cnt_ref,))
        for c in ctx_copies(slot):
            c.start()
        for c in lat_copies(1 - slot, 0) + ctx_copies(slot):
            c.wait()


def _fourier_layer(x_lat, x_ctx, mods, w_out, b_out, lng, lnb, router):
    n_lat, d = x_lat.shape
    n_ctx = x_ctx.shape[0]
    n2 = LANE
    n1 = n_lat // n2
    cw = d // FN_GROUPS
    xv = x_lat.reshape(n1, n2 * d)
    cmat = _dft_mats(cw)
    w_bf = w_out.astype(BF16)
    b2 = b_out.reshape(1, d)
    lng2, lnb2 = lng.reshape(1, d), lnb.reshape(1, d)

    k1 = jnp.arange(n1, dtype=I32)
    pos = jnp.arange(n1, dtype=I32)[None, None, :] * n2 + jnp.arange(n2, dtype=I32)[:, None, None]
    ang = ((k1[None, :, None] * pos) % n_lat).astype(F32) * (2.0 * math.pi / n_lat)
    tc, ts = jnp.cos(ang), jnp.sin(ang)

    mat = _full((cw, cw))
    ur, ui = pl.pallas_call(
        _fourier1_kernel,
        grid=(n2,),
        in_specs=[
            pl.BlockSpec((n1, d), lambda j: (0, j)),
            pl.BlockSpec((1, N_MOD, d), lambda j: (0, 0, 0)),
            mat, mat, mat, mat,
            pl.BlockSpec((1, n1, n1), lambda j: (j, 0, 0)),
            pl.BlockSpec((1, n1, n1), lambda j: (j, 0, 0)),
        ],
        out_specs=(pl.BlockSpec((n1, d), lambda j: (0, j)), pl.BlockSpec((n1, d), lambda j: (0, j))),
        out_shape=(jax.ShapeDtypeStruct((n1, n2 * d), F32), jax.ShapeDtypeStruct((n1, n2 * d), F32)),
        compiler_params=_params("arbitrary"),
        name="fourier1",
    )(xv, mods, *cmat, tc, ts)

    assert n_ctx % n2 == 0
    t_all = n_lat + n_ctx
    blocks = t_all // n2
    m2 = _dft_mats(n2)
    cn = _dft_mats(n_ctx)
    mat2 = _full((n2, n2))
    matn = _full((n_ctx, n_ctx))
    any_spec = pl.BlockSpec(memory_space=pl.ANY)
    lat_step = lambda k: jnp.minimum(k, n1 - 1)
    out_shapes = (
        jax.ShapeDtypeStruct((blocks, n2, d), F32),
        jax.ShapeDtypeStruct((blocks, n2, d), F32),
        jax.ShapeDtypeStruct((blocks, n2, LANE), I32),
        jax.ShapeDtypeStruct((blocks, n2, LANE), F32),
        _COUNT_SHAPE,
    )
    outs = pl.pallas_call(
        functools.partial(_fourier2_kernel, n1=n1, norm_lat=1.0 / math.sqrt(n_lat * cw),
                          norm_ctx=1.0 / math.sqrt(n_ctx * cw)),
        grid=(n1 + 1,),
        in_specs=[
            pl.BlockSpec((1, n2, d), lambda k: (lat_step(k), 0, 0)),
            pl.BlockSpec((1, n2, d), lambda k: (lat_step(k), 0, 0)),
            mat2, mat2, mat2, mat2,
            _full((d, d)), _full((1, d)),
            pl.BlockSpec((n2, d), lambda k: (0, lat_step(k))),
            _full((n_ctx, d)),
            _full((2, N_MOD, d)),
            mat, mat, mat, mat, matn, matn, matn, matn,
            _full((1, d)), _full((1, d)),
            _full((d, LANE)), _full((d, LANE)), _full((1, LANE)),
        ],
        out_specs=(any_spec, any_spec, any_spec, any_spec, _COUNT_SPEC),
        out_shape=out_shapes,
        scratch_shapes=[pltpu.VMEM((2, n_ctx, d), F32), pltpu.VMEM((2, n_ctx, d), F32),
                        pltpu.VMEM((2, n_ctx, LANE), I32), pltpu.VMEM((2, n_ctx, LANE), F32),
                        pltpu.SemaphoreType.DMA((2,))],
        compiler_params=_params("arbitrary"),
        name="fourier2",
    )(ur.reshape(n1, n2, d), ui.reshape(n1, n2, d), *m2, w_bf, b2, xv, x_ctx, mods, *cmat, *cn, lng2, lnb2,
      *router)
    return (outs[0].reshape(t_all, d), outs[1].reshape(t_all, d), outs[2].reshape(t_all, LANE),
            outs[3].reshape(t_all, LANE), outs[4])


def _qkv_kernel(pos_ref, *refs, n_qk, rms):
    mod_ref, w_ref, b_ref, cos_ref, sin_ref, *rest = refs[N_COMBINE_REFS:]
    if rms:
        gain_ref, ind_ref, indt_ref, x_ref, q_ref, k_ref, v_ref, ybuf, sem = rest
    else:
        x_ref, q_ref, k_ref, v_ref, ybuf, sem = rest
    x = _combine_tile(pos_ref, *refs[:N_COMBINE_REFS], ybuf, sem)
    x_ref[...] = x
    mod = mod_ref[0]
    h = x * (1.0 + mod[1:2]) + mod[0:1]
    y = _dot(h.astype(BF16), w_ref[...]) + b_ref[...]
    qk = y[:, :n_qk]
    if rms:
        sh, sl = _split(qk * qk)
        ms = _dot(sh, ind_ref[...]) + _dot(sl, ind_ref[...])
        mh, ml = _split(ms)
        msb = _dot(mh, indt_ref[...]) + _dot(ml, indt_ref[...])
        qk = qk * lax.rsqrt(msb + RMS_EPS) * gain_ref[...]
    cos = cos_ref[...]
    sin = sin_ref[...]
    even = (lax.broadcasted_iota(I32, cos.shape, 1) & 1) == 0
    parts = []
    for c in range(n_qk // LANE):
        z = qk[:, c * LANE:(c + 1) * LANE]
        swapped = jnp.where(even, pltpu.roll(z, LANE - 1, 1), pltpu.roll(z, 1, 1))
        parts.append(z * cos + swapped * sin)
    nq = q_ref.shape[1]
    q_ref[...] = (jnp.concatenate(parts[:nq // LANE], axis=1) * Q_SCALE).astype(BF16)
    k_ref[...] = jnp.concatenate(parts[nq // LANE:], axis=1).astype(BF16)
    v_ref[...] = y[:, n_qk:].astype(BF16)


def _qkv(pending, mods, w_qkv, b_qkv, cos_t, sin_t, n_q, n_kv, n_lat, q_norm=None, k_norm=None):
    t_all, d = pending[3].shape
    nq, nk = n_q * HEAD_DIM, n_kv * HEAD_DIM
    n_qk, n_all = nq + nk, nq + 2 * nk
    rms = q_norm is not None
    pos, args, in_specs, scratch = _combine_operands(pending, TM, n_lat)
    row = lambda w: pl.BlockSpec((TM, w), lambda t, *_: (t, 0))
    in_specs += [_tile_specs(n_lat // TM)[1], _full((d, n_all)), _full((1, n_all)), row(LANE), row(LANE)]
    args += [mods, w_qkv.astype(BF16), b_qkv.reshape(1, n_all), cos_t, sin_t]
    if rms:
        gain = jnp.concatenate([jnp.tile(q_norm, n_q), jnp.tile(k_norm, n_kv)]).reshape(1, n_qk)
        head = np.arange(n_qk) // HEAD_DIM
        ind = np.zeros((n_qk, LANE), np.float32)
        ind[np.arange(n_qk), head] = 1.0 / HEAD_DIM
        indt = np.zeros((LANE, n_qk), np.float32)
        indt[head, np.arange(n_qk)] = 1.0
        in_specs += [_full((1, n_qk)), _full((n_qk, LANE)), _full((LANE, n_qk))]
        args += [gain, jnp.asarray(ind, BF16), jnp.asarray(indt, BF16)]
    grid_spec = pltpu.PrefetchScalarGridSpec(
        num_scalar_prefetch=1,
        grid=(t_all // TM,),
        in_specs=in_specs,
        out_specs=(row(d), row(nq), row(nk), row(nk)),
        scratch_shapes=scratch,
    )
    return pl.pallas_call(
        functools.partial(_qkv_kernel, n_qk=n_qk, rms=rms),
        grid_spec=grid_spec,
        out_shape=(jax.ShapeDtypeStruct((t_all, d), F32), jax.ShapeDtypeStruct((t_all, nq), BF16),
                   jax.ShapeDtypeStruct((t_all, nk), BF16), jax.ShapeDtypeStruct((t_all, nk), BF16)),
        compiler_params=_params("arbitrary"),
        name="qkv",
    )(pos, *args)


def _rope_tables(n_lat, n_ctx):
    rows = n_lat // GRID_W
    row = jnp.repeat(jnp.arange(rows, dtype=F32), GRID_W)
    col = jnp.tile(jnp.arange(GRID_W, dtype=F32), rows)
    n_freq = HEAD_DIM // 4
    inv = ROPE_THETA ** (-jnp.arange(n_freq, dtype=F32) / n_freq)
    ang = jnp.concatenate([row[:, None] * inv, col[:, None] * inv], axis=-1)
    ang = jnp.concatenate([ang, jnp.zeros((n_ctx, HEAD_DIM // 2), F32)], axis=0)
    cos = jnp.tile(jnp.repeat(jnp.cos(ang), 2, axis=1), (1, LANE // HEAD_DIM))
    sin = jnp.tile(jnp.repeat(jnp.sin(ang), 2, axis=1), (1, LANE // HEAD_DIM))
    sign = jnp.where(jnp.arange(LANE) % 2 == 0, -1.0, 1.0).astype(F32)
    return cos, sin * sign


def _kv_layouts(k, v, n_kv):
    t_all = k.shape[0]
    kh = k.reshape(t_all, n_kv, HEAD_DIM).transpose(1, 0, 2)
    vt = v.reshape(t_all, n_kv, HEAD_DIM).transpose(1, 2, 0)
    pad = jnp.zeros((n_kv, LANE - HEAD_DIM, t_all), BF16).at[:, 0, :].set(1.0)
    return kh, jnp.concatenate([vt, pad], axis=1)


def _flash_kernel(q_ref, k_ref, vt_ref, o_ref, s_even, s_odd, *, grp, tk, n_lat, n_ctx):
    tq = q_ref.shape[0]
    q_t = q_ref[...].astype(F32).T
    qt = jnp.concatenate([q_t[g * HEAD_DIM:(g + 1) * HEAD_DIM] for g in range(grp)], axis=1).astype(BF16)
    cols = qt.shape[1]
    is_ctx = pl.program_id(1) == n_lat // FLASH_TQ
    n_chunks = (n_lat + n_ctx) // tk

    def finish(acc):
        o = acc[:HEAD_DIM] / acc[HEAD_DIM:HEAD_DIM + 1]
        o_t = jnp.concatenate([o[:, g * tq:(g + 1) * tq] for g in range(grp)], axis=0)
        o_ref[...] = o_t.T.astype(BF16)

    groups = [slice(c * MXU_WIDTH, (c + 1) * MXU_WIDTH) for c in range(cols // MXU_WIDTH)]

    def scores(j, buf, g):
        off = pl.multiple_of(j * tk, tk)
        s = _dot(k_ref[0, pl.ds(off, tk), :], qt[:, g])
        buf[:, g] = s
        return jnp.max(s, axis=0, keepdims=True)

    def accumulate(j, buf, g, m, acc, mc):
        off = pl.multiple_of(j * tk, tk)
        m_new = jnp.maximum(m, mc)
        p = jnp.exp2(buf[:, g] - m_new)
        acc = jnp.exp2(m - m_new) * acc + _dot(vt_ref[0, :, pl.ds(off, tk)], p.astype(BF16))
        return m_new, acc

    @pl.when(jnp.logical_not(is_ctx))
    def _():
        def step(j, cur, nxt, carry):
            out = []
            for g, (m, acc, mc) in zip(groups, carry):
                mc_next = scores(j + 1, nxt, g)
                out.append(accumulate(j, cur, g, m, acc, mc) + (mc_next,))
            return out

        def body(jj, carry):
            for pair in range(FLASH_PAIRS):
                j = 2 * (FLASH_PAIRS * jj + pair)
                carry = step(j, s_even, s_odd, carry)
                carry = step(j + 1, s_odd, s_even, carry)
            return carry

        init = [(jnp.full((1, MXU_WIDTH), NEG, F32), jnp.zeros((LANE, MXU_WIDTH), F32), scores(0, s_even, g))
                for g in groups]
        carry = lax.fori_loop(0, (n_chunks - 1) // (2 * FLASH_PAIRS), body, init)
        finish(jnp.concatenate([accumulate(n_chunks - 1, s_even, g, m, acc, mc)[1]
                                for g, (m, acc, mc) in zip(groups, carry)], axis=1))

    @pl.when(is_ctx)
    def _():
        s = _dot(k_ref[0, n_lat:n_lat + n_ctx, :], qt)
        p = jnp.exp2(s - jnp.max(s, axis=0, keepdims=True))
        finish(_dot(vt_ref[0, :, n_lat:n_lat + n_ctx], p.astype(BF16)))


def _full_attention(q, k, v, n_lat, n_ctx):
    t_all, nq = q.shape
    n_kv = k.shape[1] // HEAD_DIM
    grp = nq // HEAD_DIM // n_kv
    nt = t_all // FLASH_TQ
    cols = grp * FLASH_TQ
    assert n_ctx == FLASH_TQ and t_all % FLASH_TK == 0 and (t_all // FLASH_TK - 1) % (2 * FLASH_PAIRS) == 0
    kh, vt = _kv_layouts(k, v, n_kv)
    return pl.pallas_call(
        functools.partial(_flash_kernel, grp=grp, tk=FLASH_TK, n_lat=n_lat, n_ctx=n_ctx),
        grid=(n_kv, nt),
        in_specs=[
            pl.BlockSpec((FLASH_TQ, grp * HEAD_DIM), lambda h, i: (i, h)),
            pl.BlockSpec((1, t_all, HEAD_DIM), lambda h, i: (h, 0, 0)),
            pl.BlockSpec((1, LANE, t_all), lambda h, i: (h, 0, 0)),
        ],
        out_specs=pl.BlockSpec((FLASH_TQ, grp * HEAD_DIM), lambda h, i: (i, h)),
        out_shape=jax.ShapeDtypeStruct((t_all, nq), BF16),
        scratch_shapes=[pltpu.VMEM((FLASH_TK, cols), F32), pltpu.VMEM((FLASH_TK, cols), F32)],
        compiler_params=_params("arbitrary", "arbitrary"),
        name="flash",
    )(q, kh, vt)


def _window_kernel(q_ref, kp_ref, kc_ref, kn_ref, kx_ref, vp_ref, vc_ref, vn_ref, vx_ref, sink_ref, o_ref,
                   *, grp, nb):
    i = pl.program_id(1)
    q_t = q_ref[...].astype(F32).T
    qt = jnp.concatenate([q_t[g * HEAD_DIM:(g + 1) * HEAD_DIM] for g in range(grp)], axis=1).astype(BF16)
    cols = qt.shape[1]
    s = _dot(jnp.concatenate([kp_ref[0], kc_ref[0], kn_ref[0], kx_ref[0]], axis=0), qt)
    kj = lax.broadcasted_iota(I32, (Q_BLOCK, cols), 0)
    qi = lax.broadcasted_iota(I32, (Q_BLOCK, cols), 1) & (Q_BLOCK - 1)
    sp = jnp.where(kj >= jnp.where(i > 0, qi, Q_BLOCK), s[:Q_BLOCK], NEG)
    sn = jnp.where(kj <= jnp.where(i < nb - 1, qi, -1), s[2 * Q_BLOCK:3 * Q_BLOCK], NEG)
    s = jnp.concatenate([sp, s[Q_BLOCK:2 * Q_BLOCK], sn, s[3 * Q_BLOCK:]], axis=0)
    sink = sink_ref[0]
    m = jnp.maximum(jnp.max(s, axis=0, keepdims=True), sink)
    p = jnp.exp2(s - m).astype(BF16)
    acc = _dot(jnp.concatenate([vp_ref[0], vc_ref[0], vn_ref[0], vx_ref[0]], axis=1), p)
    o = acc[:HEAD_DIM] / (acc[HEAD_DIM:HEAD_DIM + 1] + jnp.exp2(sink - m))
    o_t = jnp.concatenate([o[:, g * Q_BLOCK:(g + 1) * Q_BLOCK] for g in range(grp)], axis=0)
    o_ref[...] = o_t.T.astype(BF16)


def _window_attention(q, k, v, sink, n_lat, n_ctx):
    nq = q.shape[1]
    n_kv = k.shape[1] // HEAD_DIM
    grp = nq // HEAD_DIM // n_kv
    gw = grp * HEAD_DIM
    nb = n_lat // Q_BLOCK
    cb = n_lat // n_ctx
    kh, vt = _kv_layouts(k, v, n_kv)
    sink_cols = jnp.repeat(sink.reshape(n_kv, 1, grp) * LOG2E, Q_BLOCK, axis=2)
    kspec = lambda f: pl.BlockSpec((1, Q_BLOCK, HEAD_DIM), lambda h, i: (h, f(i), 0))
    vspec = lambda f: pl.BlockSpec((1, LANE, Q_BLOCK), lambda h, i: (h, 0, f(i)))
    prev = lambda i: jnp.maximum(i - 1, 0)
    cur = lambda i: i
    nxt = lambda i: jnp.minimum(i + 1, nb - 1)
    return pl.pallas_call(
        functools.partial(_window_kernel, grp=grp, nb=nb),
        grid=(n_kv, nb),
        in_specs=[
            pl.BlockSpec((Q_BLOCK, gw), lambda h, i: (i, h)),
            kspec(prev), kspec(cur), kspec(nxt),
            pl.BlockSpec((1, n_ctx, HEAD_DIM), lambda h, i: (h, cb, 0)),
            vspec(prev), vspec(cur), vspec(nxt),
            pl.BlockSpec((1, LANE, n_ctx), lambda h, i: (h, 0, cb)),
            pl.BlockSpec((1, 1, grp * Q_BLOCK), lambda h, i: (h, 0, 0)),
        ],
        out_specs=pl.BlockSpec((Q_BLOCK, gw), lambda h, i: (i, h)),
        out_shape=jax.ShapeDtypeStruct((n_lat, nq), BF16),
        compiler_params=_params("arbitrary", "arbitrary"),
        name="window",
    )(q, kh, kh, kh, kh, vt, vt, vt, vt, sink_cols)


def _gmlp_kernel(pos_ref, *refs):
    (mod_ref, win_ref, bin_ref, vg_ref, vb_ref, ws_ref, bs_ref, wout_ref, bout_ref,
     lng_ref, lnb_ref, rwh_ref, rwl_ref, rb_ref, *rest) = refs[N_COMBINE_REFS:]
    out_refs, (ybuf, sem) = rest[:-2], rest[-2:]
    mod = mod_ref[0]
    x = _combine_tile(pos_ref, *refs[:N_COMBINE_REFS], ybuf, sem)
    h = x * (1.0 + mod[1:2]) + mod[0:1]
    z = _dot(h.astype(BF16), win_ref[...]) + bin_ref[...]
    z = 0.5 * z * (1.0 + lax.erf(z * (2.0 ** -0.5)))
    half = z.shape[1] // 2
    u = z[:, :half]
    v = _layer_norm(z[:, half:], vg_ref[...], vb_ref[...]).astype(BF16)
    cw = half // GM_GROUPS
    chunks = []
    for c in range(x.shape[0] // GM_CHUNK):
        vc = v[c * GM_CHUNK:(c + 1) * GM_CHUNK]
        chunks.append(jnp.concatenate(
            [_dot(ws_ref[g], vc[:, g * cw:(g + 1) * cw]) + bs_ref[g] for g in range(GM_GROUPS)], axis=1))
    gated = u * jnp.concatenate(chunks, axis=0)
    y = _dot(gated.astype(BF16), wout_ref[...]) + bout_ref[...]
    _post(y, x, mod, lng_ref[...], lnb_ref[...], rwh_ref[...], rwl_ref[...], rb_ref[...], out_refs)


def _gmlp_layer(pending, mods, w_in, b_in, vg, vb, w_s, b_s, w_out, b_out, lng, lnb, router, n_lat):
    t_all, d = pending[3].shape
    dffn = w_in.shape[1]
    half = dffn // 2
    cw = half // GM_GROUPS
    shapes, specs = _post_out(t_all)
    bs_full = jnp.broadcast_to(b_s[:, :, None], (GM_GROUPS, GM_CHUNK, cw))
    pos, args, in_specs, scratch = _combine_operands(pending, TM, n_lat)
    in_specs += [
        _tile_specs(n_lat // TM)[1], _full((d, dffn)), _full((1, dffn)), _full((1, half)), _full((1, half)),
        _full((GM_GROUPS, GM_CHUNK, GM_CHUNK)), _full((GM_GROUPS, GM_CHUNK, cw)),
        _full((half, d)), _full((1, d)), _full((1, d)), _full((1, d)),
        _full((d, LANE)), _full((d, LANE)), _full((1, LANE)),
    ]
    args += [mods, w_in.astype(BF16), b_in.reshape(1, dffn), vg.reshape(1, half), vb.reshape(1, half),
             w_s.astype(BF16), bs_full, w_out.astype(BF16), b_out.reshape(1, d), lng.reshape(1, d),
             lnb.reshape(1, d), *router]
    grid_spec = pltpu.PrefetchScalarGridSpec(
        num_scalar_prefetch=1,
        grid=(t_all // TM,),
        in_specs=in_specs,
        out_specs=specs,
        scratch_shapes=scratch,
    )
    return pl.pallas_call(
        _gmlp_kernel,
        grid_spec=grid_spec,
        out_shape=shapes,
        compiler_params=_params("arbitrary"),
        name="gmlp",
    )(pos, *args)


COMB_TM = 128


def _route_meta(count_blk, n_tok):
    n_assign = n_tok * TOP_K
    n_blocks = -(-n_assign // MOE_BLOCK) + N_EXPERTS
    ids = jnp.arange(N_EXPERTS, dtype=I32)
    counts = count_blk[0, :N_EXPERTS].astype(I32)
    padded = (counts + MOE_BLOCK - 1) // MOE_BLOCK * MOE_BLOCK
    ends_pad = jnp.cumsum(padded)
    base = ends_pad - padded
    starts = jnp.arange(n_blocks, dtype=I32) * MOE_BLOCK
    block_expert = jnp.minimum(jnp.sum((ends_pad[None, :] <= starts[:, None]).astype(I32), axis=1), N_EXPERTS - 1)
    last_blk = jnp.where(padded > 0, ends_pad - MOE_BLOCK, -1)
    tail = ends_pad[-1] + ids * MOE_BLOCK
    tail = jnp.where(tail < n_blocks * MOE_BLOCK, tail, -1)
    base_b = jnp.broadcast_to(base.astype(F32)[:, None], (N_EXPERTS, LANE))
    return block_expert.astype(I32), base_b, jnp.concatenate([last_blk, tail]).astype(I32), n_blocks


def _rank_kernel(idx_ref, base_ref, upper_ref, dest_ref, run_ref):
    @pl.when(pl.program_id(0) == 0)
    def _():
        run_ref[...] = jnp.zeros(run_ref.shape, F32)

    eid = lax.broadcasted_iota(I32, (N_EXPERTS, TM), 0)
    onehots = [(eid == idx_ref[k:k + 1, :]).astype(F32) for k in range(TOP_K)]
    cnt = onehots[0] + onehots[1] + onehots[2] + onehots[3]
    before = _dot(cnt.astype(BF16), upper_ref[...])
    slot = base_ref[:, :1] + run_ref[:, :1] + before
    for k in range(TOP_K):
        rows = (jnp.sum(onehots[k] * slot, axis=0, keepdims=True) * ROW_TILE).astype(I32)
        for j in range(TM // COMB_TM):
            dest_ref[j, k:k + 1, :] = rows[:, j * COMB_TM:(j + 1) * COMB_TM]
    run_ref[...] = run_ref[...] + jnp.sum(cnt, axis=1, keepdims=True)


def _rank(top_idx, base_b, n_tok):
    per_step = TM // COMB_TM
    idx_t = top_idx[:n_tok, :TOP_K].T
    upper = jnp.asarray(np.triu(np.ones((TM, TM), np.float32), 1), BF16)
    return pl.pallas_call(
        _rank_kernel,
        grid=(n_tok // TM,),
        in_specs=[pl.BlockSpec((TOP_K, TM), lambda t: (0, t)), _full((N_EXPERTS, LANE)), _full((TM, TM))],
        out_specs=pl.BlockSpec((per_step, TOP_K, COMB_TM), lambda t: (t, 0, 0)),
        out_shape=jax.ShapeDtypeStruct((n_tok // COMB_TM, TOP_K, COMB_TM), I32),
        scratch_shapes=[pltpu.VMEM((N_EXPERTS, LANE), F32)],
        compiler_params=_params("arbitrary"),
        name="rank",
    )(idx_t, base_b, upper)


def _dispatch_kernel(dest_ref, zrow_ref, h_ref, xs_hbm, hbuf, zbuf, sem, zsem, *, n_tiles):
    t = pl.program_id(0)
    slot = t % 2
    n_zero = 2 * N_EXPERTS

    def zero_copy(j):
        row = pl.multiple_of(zrow_ref[j] * ROW_TILE, MOE_BLOCK * ROW_TILE)
        return pltpu.make_async_copy(zbuf, xs_hbm.at[pl.ds(row, MOE_BLOCK * ROW_TILE), :], zsem.at[0])

    def wait_rows(s):
        for _ in range(TOP_K):
            pltpu.make_async_copy(hbuf.at[s], xs_hbm.at[pl.ds(0, COMB_TM * ROW_TILE), :], sem.at[s]).wait()

    @pl.when(t == 0)
    def _():
        zbuf[...] = jnp.zeros(zbuf.shape, F32)
        for j in range(n_zero):
            @pl.when(zrow_ref[j] >= 0)
            def _():
                zero_copy(j).start()
        for j in range(n_zero):
            @pl.when(zrow_ref[j] >= 0)
            def _():
                zero_copy(j).wait()

    @pl.when(t >= 2)
    def _():
        wait_rows(slot)

    _to_row_tiles(hbuf.at[slot], h_ref[...])
    for k in range(TOP_K):
        for r in range(COMB_TM):
            row = pl.multiple_of(dest_ref[(t * TOP_K + k) * COMB_TM + r], ROW_TILE)
            pltpu.make_async_copy(hbuf.at[slot, pl.ds(r * ROW_TILE, ROW_TILE), :],
                                  xs_hbm.at[pl.ds(row, ROW_TILE), :], sem.at[slot]).start(priority=r % 2)

    @pl.when(t == n_tiles - 1)
    def _():
        wait_rows(slot)
        if n_tiles > 1:
            wait_rows(1 - slot)


def _dispatch(h2, dest, zrows, n_tok, n_blocks):
    d = h2.shape[1]
    n_tiles = n_tok // COMB_TM
    grid_spec = pltpu.PrefetchScalarGridSpec(
        num_scalar_prefetch=2,
        grid=(n_tiles,),
        in_specs=[pl.BlockSpec((COMB_TM, d), lambda t, dest, zr: (t, 0))],
        out_specs=pl.BlockSpec(memory_space=pl.ANY),
        scratch_shapes=[pltpu.VMEM((2, COMB_TM * ROW_TILE, LANE), F32),
                        pltpu.VMEM((MOE_BLOCK * ROW_TILE, LANE), F32),
                        pltpu.SemaphoreType.DMA((2,)), pltpu.SemaphoreType.DMA((1,))],
    )
    return pl.pallas_call(
        functools.partial(_dispatch_kernel, n_tiles=n_tiles),
        grid_spec=grid_spec,
        out_shape=jax.ShapeDtypeStruct((n_blocks * MOE_BLOCK * ROW_TILE, LANE), F32),
        compiler_params=_params("arbitrary"),
        name="dispatch",
    )(dest.reshape(-1), zrows, h2)


EXP_CHUNK = MXU_WIDTH


def _expert_kernel(be_ref, x_ref, wgu_ref, bgu_ref, wd_ref, bd_ref, sel_ref, y_ref,
                   wg_s, wl_s, wd_s, bg_s, bl_s):
    b = pl.program_id(0)
    ff = wg_s.shape[1]
    half = EXP_CHUNK // 2

    @pl.when(jnp.logical_or(b == 0, be_ref[b] != be_ref[jnp.maximum(b - 1, 0)]))
    def _():
        for c in range(2 * ff // EXP_CHUNK):
            w = _dot(wgu_ref[0, 0, :, c * EXP_CHUNK:(c + 1) * EXP_CHUNK].astype(BF16), sel_ref[...])
            wg_s[:, c * half:(c + 1) * half] = w[:, :half].astype(BF16)
            wl_s[:, c * half:(c + 1) * half] = w[:, half:].astype(BF16)
            bh, bl = _split(jnp.broadcast_to(bgu_ref[0, 0, :, c * EXP_CHUNK:(c + 1) * EXP_CHUNK],
                                             (SUBLANE, EXP_CHUNK)))
            bias = _dot(bh, sel_ref[...]) + _dot(bl, sel_ref[...])
            bg_s[:, c * half:(c + 1) * half] = bias[:, :half]
            bl_s[:, c * half:(c + 1) * half] = bias[:, half:]
        wd_s[...] = wd_ref[0, 0].astype(BF16)

    x = _from_row_tiles(x_ref, 0, MOE_BLOCK).astype(BF16)
    glu = jnp.minimum(_dot(x, wg_s[...]) + bg_s[0:1, :], SWIGLU_LIMIT)
    lin = jnp.clip(_dot(x, wl_s[...]) + bl_s[0:1, :], -SWIGLU_LIMIT, SWIGLU_LIMIT)
    act = glu * (1.0 / (1.0 + jnp.exp(-SWIGLU_ALPHA * glu))) * (lin + 1.0)
    _to_row_tiles(y_ref, _dot(act.astype(BF16), wd_s[...]) + bd_ref[0, 0])


def _experts(xs, block_expert, n_blocks, w_gate_up, b_gate_up, w_down, b_down, layer):
    d = w_down.shape[3]
    ff = w_down.shape[2]
    half = EXP_CHUNK // 2
    sel = np.zeros((EXP_CHUNK, EXP_CHUNK), np.float32)
    sel[2 * np.arange(half), np.arange(half)] = 1.0
    sel[2 * np.arange(half) + 1, half + np.arange(half)] = 1.0
    wspec = lambda r, c: pl.BlockSpec((1, 1, r, c), lambda b, be: (layer, be[b], 0, 0))
    sspec = pl.BlockSpec((EXP_CHUNK, EXP_CHUNK), lambda b, be: (0, 0))
    grid_spec = pltpu.PrefetchScalarGridSpec(
        num_scalar_prefetch=1,
        grid=(n_blocks,),
        in_specs=[pl.BlockSpec((MOE_BLOCK * ROW_TILE, LANE), lambda b, be: (b, 0)), wspec(d, 2 * ff),
                  wspec(1, 2 * ff), wspec(ff, d), wspec(1, d), sspec],
        out_specs=pl.BlockSpec((MOE_BLOCK * ROW_TILE, LANE), lambda b, be: (b, 0)),
        scratch_shapes=[pltpu.VMEM((d, ff), BF16), pltpu.VMEM((d, ff), BF16), pltpu.VMEM((ff, d), BF16),
                        pltpu.VMEM((SUBLANE, ff), F32), pltpu.VMEM((SUBLANE, ff), F32)],
    )
    return pl.pallas_call(
        _expert_kernel,
        grid_spec=grid_spec,
        out_shape=jax.ShapeDtypeStruct((n_blocks * MOE_BLOCK * ROW_TILE, LANE), F32),
        compiler_params=_params("arbitrary"),
        name="experts",
    )(block_expert, xs, w_gate_up, b_gate_up[:, :, None, :], w_down, b_down[:, :, None, :],
      jnp.asarray(sel, BF16))


def _gather_rows(idx_ref, base, n, src_hbm, dst, sem):
    for r in range(n):
        row = pl.multiple_of(idx_ref[base + r], ROW_TILE)
        pltpu.make_async_copy(src_hbm.at[pl.ds(row, ROW_TILE), :], dst.at[pl.ds(r * ROW_TILE, ROW_TILE), :],
                              sem).start(priority=r % 2)


N_COMBINE_REFS = 6


def _combine_tile(pos_ref, y_hbm, gate_ref, x1_ref, mod_ref, lng_ref, lnb_ref, ybuf, sem):
    t = pl.program_id(0)
    slot = t % 2
    tm, d = x1_ref.shape
    n = tm * TOP_K

    @pl.when(t == 0)
    def _():
        _gather_rows(pos_ref, 0, n, y_hbm, ybuf.at[0], sem.at[0])

    @pl.when(t + 1 < pl.num_programs(0))
    def _():
        _gather_rows(pos_ref, (t + 1) * n, n, y_hbm, ybuf.at[1 - slot], sem.at[1 - slot])

    pltpu.make_async_copy(y_hbm.at[pl.ds(0, n * ROW_TILE), :], ybuf.at[slot], sem.at[slot]).wait()
    gates = gate_ref[...]
    parts = []
    for j in range(tm // COMB_TM):
        f = jnp.zeros((COMB_TM, d), F32)
        for k in range(TOP_K):
            rows = _from_row_tiles(ybuf.at[slot], (j * TOP_K + k) * COMB_TM, COMB_TM)
            f = f + gates[j * COMB_TM:(j + 1) * COMB_TM, k:k + 1] * rows
        parts.append(f)
    f = parts[0] if len(parts) == 1 else jnp.concatenate(parts, axis=0)
    return _layer_norm(ALPHA * x1_ref[...] + mod_ref[0][5:6] * f, lng_ref[...], lnb_ref[...])


def _combine_operands(pending, tm, n_lat):
    yb, dest, gates, x1, mods, lng, lnb = pending
    d = x1.shape[1]
    n_lat_tiles = n_lat // tm
    specs = [
        pl.BlockSpec(memory_space=pl.ANY),
        pl.BlockSpec((tm, LANE), lambda t, *_: (t, 0)),
        pl.BlockSpec((tm, d), lambda t, *_: (t, 0)),
        pl.BlockSpec((1, N_MOD, d), lambda t, *_: (jnp.where(t >= n_lat_tiles, 1, 0), 0, 0)),
        _full((1, d)), _full((1, d)),
    ]
    scratch = [pltpu.VMEM((2, tm * TOP_K * ROW_TILE, LANE), F32), pltpu.SemaphoreType.DMA((2,))]
    return dest.reshape(-1), [yb, gates, x1, mods, lng.reshape(1, d), lnb.reshape(1, d)], specs, scratch


def _combine_kernel(pos_ref, *refs):
    o_ref, ybuf, sem = refs[N_COMBINE_REFS:]
    o_ref[...] = _combine_tile(pos_ref, *refs[:N_COMBINE_REFS], ybuf, sem)


def _combine(pending, n_tok, n_lat):
    d = pending[3].shape[1]
    pos, args, specs, scratch = _combine_operands(pending, COMB_TM, n_lat)
    grid_spec = pltpu.PrefetchScalarGridSpec(
        num_scalar_prefetch=1,
        grid=(n_tok // COMB_TM,),
        in_specs=specs,
        out_specs=pl.BlockSpec((COMB_TM, d), lambda t, *_: (t, 0)),
        scratch_shapes=scratch,
    )
    return pl.pallas_call(
        _combine_kernel,
        grid_spec=grid_spec,
        out_shape=jax.ShapeDtypeStruct((n_tok, d), F32),
        compiler_params=_params("arbitrary"),
        name="combine",
    )(pos, *args)


def _moe_layer(x1, h2, top_idx, gates, counts, mods, lng, lnb, w_gate_up, b_gate_up, w_down, b_down, layer, n_tok):
    block_expert, base_b, zrows, n_blocks = _route_meta(counts, n_tok)
    dest = _rank(top_idx, base_b, n_tok)
    xs = _dispatch(h2, dest, zrows, n_tok, n_blocks)
    yb = _experts(xs, block_expert, n_blocks, w_gate_up, b_gate_up, w_down, b_down, layer)
    return yb, dest, gates, x1, mods, lng, lnb


def kernel(x, c, ctx, c_ctx, ada_w, ada_b, ln_mix_g, ln_mix_b, ln_ffn_g, ln_ffn_b, fn_w_out, fn_b_out, fa_w_qkv, fa_b_qkv, fa_q_norm, fa_k_norm, fa_w_out, fa_b_out, gm_w_in, gm_b_in, gm_v_norm_g, gm_v_norm_b, gm_w_s, gm_b_s, gm_w_out, gm_b_out, wa_w_qkv, wa_b_qkv, wa_sink, wa_w_out, wa_b_out, router_w, router_b, exp_w_gate_up, exp_b_gate_up, exp_w_down, exp_b_down):
    bsz, n_lat, d = x.shape
    n_ctx = ctx.shape[1]
    assert bsz == 1 and d == D_MODEL and n_lat == LANE * LANE and n_lat % n_ctx == 0 and n_ctx % TM == 0
    t_all = n_lat + n_ctx
    n_lat_tiles = n_lat // TM
    mods_all = _ada(c, c_ctx, ada_w, ada_b)
    cos_t, sin_t = _rope_tables(n_lat, n_ctx)

    for i in range(DEPTH):
        kind, j = i % 4, i // 4
        last = i == DEPTH - 1
        n_tok = n_lat if last else t_all
        mods = mods_all[i]
        router = _router_operands(router_w[i], router_b[i])
        lng, lnb = ln_mix_g[i], ln_mix_b[i]
        if kind == 0:
            post = _fourier_layer(x[0], ctx[0], mods, fn_w_out[j], fn_b_out[j], lng, lnb, router)
        elif kind == 1:
            x_all, q, k, v = _qkv(pending, mods, fa_w_qkv[j], fa_b_qkv[j], cos_t, sin_t, FA_Q_HEADS,
                                  FA_KV_HEADS, n_lat, fa_q_norm[j], fa_k_norm[j])
            o = _full_attention(q, k, v, n_lat, n_ctx)
            post = _proj_post(o, fa_w_out[j], fa_b_out[j], x_all, mods, lng, lnb, router, n_tok, n_lat_tiles)
        elif kind == 2:
            post = _gmlp_layer(pending, mods, gm_w_in[j], gm_b_in[j], gm_v_norm_g[j], gm_v_norm_b[j],
                               gm_w_s[j], gm_b_s[j], gm_w_out[j], gm_b_out[j], lng, lnb, router, n_lat)
        else:
            x_all, q, k, v = _qkv(pending, mods, wa_w_qkv[j], wa_b_qkv[j], cos_t, sin_t, WA_Q_HEADS,
                                  WA_KV_HEADS, n_lat)
            o = _window_attention(q, k, v, wa_sink[j], n_lat, n_ctx)
            post = _proj_post(o, wa_w_out[j], wa_b_out[j], x_all, mods, lng, lnb, router, n_tok, n_lat_tiles)
        pending = _moe_layer(*post, mods, ln_ffn_g[i], ln_ffn_b[i], exp_w_gate_up, exp_b_gate_up, exp_w_down,
                             exp_b_down, i, n_tok)
    return _combine(pending, n_lat, n_lat)[None]
```

```python
import functools
import math

import numpy as np
import jax
import jax.numpy as jnp
from jax import lax
from jax.experimental import pallas as pl
from jax.experimental.pallas import tpu as pltpu

F32, BF16, I32 = jnp.float32, jnp.bfloat16, jnp.int32

D_MODEL = 1024
DEPTH = 4
GRID_W = 64
N_MOD = 6
FN_GROUPS = 4
HEAD_DIM = 64
FA_Q_HEADS, FA_KV_HEADS = 16, 4
WA_Q_HEADS, WA_KV_HEADS = 16, 2
WINDOW = 128
Q_BLOCK = 128
ROPE_THETA = 10000.0
GM_CHUNK = 128
GM_GROUPS = 8
N_EXPERTS = 32
TOP_K = 4
SWIGLU_LIMIT = 7.0
SWIGLU_ALPHA = 1.702
MOE_BLOCK = 256
LN_EPS = 1e-5
RMS_EPS = 1e-6
NEG = -1e30
ALPHA = (2 * DEPTH) ** 0.25
LOG2E = math.log2(math.e)
Q_SCALE = HEAD_DIM ** -0.5 * LOG2E

LANE = 128
SUBLANE = 8
ROW_TILE = D_MODEL // LANE
assert ROW_TILE == SUBLANE
MXU_WIDTH = 256
TM = 256
FLASH_TQ = 256
FLASH_TK = 1280
FLASH_PAIRS = 3
VMEM_LIMIT = 56 * 2 ** 20


def _params(*sem):
    return pltpu.CompilerParams(dimension_semantics=sem, vmem_limit_bytes=VMEM_LIMIT)


def _dot(a, b):
    return jnp.dot(a, b, preferred_element_type=F32)


def _split(a):
    hi = a.astype(BF16)
    lo = (a - hi.astype(F32)).astype(BF16)
    return hi, lo


def _dot3(a_hi, a_lo, b_hi, b_lo):
    return _dot(a_hi, b_hi) + (_dot(a_hi, b_lo) + _dot(a_lo, b_hi))


def _layer_norm(x, g, b):
    mu = jnp.mean(x, axis=-1, keepdims=True)
    xc = x - mu
    var = jnp.mean(xc * xc, axis=-1, keepdims=True)
    return xc * lax.rsqrt(var + LN_EPS) * g + b


def _top4(logits):
    lane = lax.broadcasted_iota(I32, logits.shape, 1).astype(F32)
    cur = logits
    vals, idxs = [], []
    for _ in range(TOP_K):
        m = jnp.max(cur, axis=-1, keepdims=True)
        i = jnp.min(jnp.where(cur == m, lane, float(LANE)), axis=-1, keepdims=True)
        vals.append(m)
        idxs.append(i)
        cur = jnp.where(lane == i, -jnp.inf, cur)
    exps = [jnp.exp(v - vals[0]) for v in vals]
    inv = 1.0 / (exps[0] + exps[1] + exps[2] + exps[3])
    idx_out = jnp.zeros_like(logits)
    gate_out = jnp.zeros_like(logits)
    picked = jnp.zeros_like(logits)
    for k in range(TOP_K):
        idx_out = jnp.where(lane == float(k), idxs[k], idx_out)
        gate_out = jnp.where(lane == float(k), exps[k] * inv, gate_out)
        picked = picked + jnp.where(lane == idxs[k], 1.0, 0.0)
    return idx_out.astype(I32), gate_out, jnp.sum(picked, axis=0, keepdims=True)


def _post(y, x, mod, lng, lnb, rw_hi, rw_lo, rb, out_refs):
    x1_ref, h2_ref, idx_ref, gate_ref, cnt_ref = out_refs
    x1 = _layer_norm(ALPHA * x + mod[2:3] * y, lng, lnb)
    h2 = x1 * (1.0 + mod[4:5]) + mod[3:4]
    hh, hl = _split(h2)
    logits = _dot3(hh, hl, rw_hi, rw_lo) + rb
    idx, gates, cnt = _top4(logits)
    x1_ref[...] = x1
    h2_ref[...] = h2
    idx_ref[...] = idx
    gate_ref[...] = gates

    @pl.when(pl.program_id(0) == 0)
    def _():
        cnt_ref[...] = jnp.zeros(cnt_ref.shape, F32)

    cnt_ref[...] = cnt_ref[...] + cnt


def _to_row_tiles(ref, x):
    n = x.shape[0]
    for s in range(ROW_TILE):
        ref[pl.ds(s, n, stride=ROW_TILE), :] = x[:, s * LANE:(s + 1) * LANE]


def _from_row_tiles(ref, start, n):
    return jnp.concatenate([ref[pl.ds(start * ROW_TILE + s, n, stride=ROW_TILE), :] for s in range(ROW_TILE)],
                           axis=1)


def _ada_kernel(cs_ref, w_ref, b_ref, o_ref):
    cs = cs_ref[...]
    s = cs * (1.0 / (1.0 + jnp.exp(-cs)))
    sh, sl = _split(s)
    wh, wl = _split(w_ref[0])
    o_ref[0] = _dot3(sh, sl, wh, wl) + b_ref[0]


def _ada(c, c_ctx, ada_w, ada_b):
    d = c.shape[-1]
    nm = ada_w.shape[-1]
    tn = nm // 4
    cs = jnp.zeros((SUBLANE, d), F32).at[0].set(c[0]).at[1].set(c_ctx)
    out = pl.pallas_call(
        _ada_kernel,
        grid=(DEPTH, nm // tn),
        in_specs=[
            pl.BlockSpec((SUBLANE, d), lambda i, j: (0, 0)),
            pl.BlockSpec((1, d, tn), lambda i, j: (i, 0, j)),
            pl.BlockSpec((1, 1, tn), lambda i, j: (i, 0, j)),
        ],
        out_specs=pl.BlockSpec((1, SUBLANE, tn), lambda i, j: (i, 0, j)),
        out_shape=jax.ShapeDtypeStruct((DEPTH, SUBLANE, nm), F32),
        compiler_params=_params("arbitrary", "arbitrary"),
        name="ada",
    )(cs, ada_w, ada_b.reshape(DEPTH, 1, nm))
    return out[:, :2].reshape(DEPTH, 2, N_MOD, d)


def _tile_specs(n_lat_tiles):
    tok = pl.BlockSpec((TM, D_MODEL), lambda t, *_: (t, 0))
    mod = pl.BlockSpec((1, N_MOD, D_MODEL), lambda t, *_: (jnp.where(t >= n_lat_tiles, 1, 0), 0, 0))
    return tok, mod


def _full(shape):
    nd = len(shape)
    return pl.BlockSpec(shape, lambda *_: (0,) * nd)


_COUNT_SHAPE = jax.ShapeDtypeStruct((SUBLANE, LANE), F32)
_COUNT_SPEC = pl.BlockSpec((SUBLANE, LANE), lambda *_: (0, 0))


def _post_out(n_rows):
    shapes = (
        jax.ShapeDtypeStruct((n_rows, D_MODEL), F32),
        jax.ShapeDtypeStruct((n_rows, D_MODEL), F32),
        jax.ShapeDtypeStruct((n_rows, LANE), I32),
        jax.ShapeDtypeStruct((n_rows, LANE), F32),
        _COUNT_SHAPE,
    )
    specs = (
        pl.BlockSpec((TM, D_MODEL), lambda t, *_: (t, 0)),
        pl.BlockSpec((TM, D_MODEL), lambda t, *_: (t, 0)),
        pl.BlockSpec((TM, LANE), lambda t, *_: (t, 0)),
        pl.BlockSpec((TM, LANE), lambda t, *_: (t, 0)),
        _COUNT_SPEC,
    )
    return shapes, specs


def _router_operands(router_w, router_b):
    rw = jnp.zeros((D_MODEL, LANE), F32).at[:, :N_EXPERTS].set(router_w)
    rw_hi = rw.astype(BF16)
    rw_lo = (rw - rw_hi.astype(F32)).astype(BF16)
    rb = jnp.full((1, LANE), NEG, F32).at[0, :N_EXPERTS].set(router_b)
    return rw_hi, rw_lo, rb


def _proj_post_kernel(a_ref, w_ref, b_ref, x_ref, mod_ref, lng_ref, lnb_ref, rwh_ref, rwl_ref, rb_ref, *out_refs):
    y = _dot(a_ref[...], w_ref[...]) + b_ref[...]
    _post(y, x_ref[...], mod_ref[0], lng_ref[...], lnb_ref[...], rwh_ref[...], rwl_ref[...], rb_ref[...], out_refs)


def _proj_post(a, w_out, b_out, x, mods, lng, lnb, router, n_rows, n_lat_tiles):
    k = a.shape[1]
    tok, mod = _tile_specs(n_lat_tiles)
    shapes, specs = _post_out(n_rows)
    return pl.pallas_call(
        _proj_post_kernel,
        grid=(n_rows // TM,),
        in_specs=[
            pl.BlockSpec((TM, k), lambda t: (t, 0)),
            _full((k, D_MODEL)), _full((1, D_MODEL)),
            tok, mod, _full((1, D_MODEL)), _full((1, D_MODEL)),
            _full((D_MODEL, LANE)), _full((D_MODEL, LANE)), _full((1, LANE)),
        ],
        out_specs=specs,
        out_shape=shapes,
        compiler_params=_params("arbitrary"),
        name="proj_post",
    )(a, w_out.astype(BF16), b_out.reshape(1, -1), x, mods, lng.reshape(1, -1), lnb.reshape(1, -1), *router)


def _dft_mats(n):
    jk = np.outer(np.arange(n), np.arange(n)) % n
    ang = 2.0 * np.pi * jk / n
    out = []
    for m in (np.cos(ang), np.sin(ang)):
        m32 = jnp.asarray(m, F32)
        hi = m32.astype(BF16)
        out += [hi, (m32 - hi.astype(F32)).astype(BF16)]
    return out


def _channel_dft(h, cc, sc):
    cw = cc[0].shape[0]
    a_parts, b_parts = [], []
    for g in range(h.shape[1] // cw):
        hh, hl = _split(h[:, g * cw:(g + 1) * cw])
        a_parts.append(_dot3(hh, hl, cc[0][...], cc[1][...]))
        b_parts.append(_dot3(hh, hl, sc[0][...], sc[1][...]))
    return jnp.concatenate(a_parts, axis=1), jnp.concatenate(b_parts, axis=1)


def _column_rows(x_hbm, buf, sem, n_steps):
    j = pl.program_id(0)
    slot = j % 2

    def copy(step, s):
        return pltpu.make_async_copy(x_hbm.at[pl.ds(0, buf.shape[1]), step, :], buf.at[s], sem.at[s])

    @pl.when(j == 0)
    def _():
        copy(0, 0).start()

    @pl.when(j + 1 < n_steps)
    def _():
        copy(j + 1, 1 - slot).start()

    copy(j, slot).wait()
    return buf[slot]


def _fourier1_kernel(x_hbm, mod_ref, cch_ref, ccl_ref, sch_ref, scl_ref, tc_ref, ts_ref, ur_ref, ui_ref, xbuf, xsem):
    mod = mod_ref[0]
    h = _column_rows(x_hbm, xbuf, xsem, pl.num_programs(0)) * (1.0 + mod[1:2]) + mod[0:1]
    a, b = _channel_dft(h, (cch_ref, ccl_ref), (sch_ref, scl_ref))
    tch, tcl = _split(tc_ref[0])
    tsh, tsl = _split(ts_ref[0])
    ah, al = _split(a)
    bh, bl = _split(b)
    ur_ref[...] = _dot3(tch, tcl, ah, al) - _dot3(tsh, tsl, bh, bl)
    ui_ref[...] = -(_dot3(tch, tcl, bh, bl) + _dot3(tsh, tsl, ah, al))


def _fourier2_kernel(ur_ref, ui_ref, c2h_ref, c2l_ref, s2h_ref, s2l_ref, w_ref, b_ref, x_hbm, xc_ref, mod_ref,
                     cch_ref, ccl_ref, sch_ref, scl_ref, cnh_ref, cnl_ref, snh_ref, snl_ref,
                     lng_ref, lnb_ref, rwh_ref, rwl_ref, rb_ref,
                     x1_hbm, h2_hbm, idx_hbm, gate_hbm, cnt_ref, bx1, bh2, bidx, bgate, sem, xbuf, xsem,
                     *, n1, norm_lat, norm_ctx):
    k = pl.program_id(0)
    n2 = ur_ref.shape[1]
    slot = k % 2
    bufs = (bx1, bh2, bidx, bgate)
    outs = (x1_hbm, h2_hbm, idx_hbm, gate_hbm)
    common = (lng_ref[...], lnb_ref[...], rwh_ref[...], rwl_ref[...], rb_ref[...])

    def lat_copies(s, col):
        return [pltpu.make_async_copy(b.at[s, pl.ds(0, n2), :], o.at[pl.ds(0, n2), col, :], sem.at[s])
                for b, o in zip(bufs, outs)]

    def ctx_copies(s):
        return [pltpu.make_async_copy(b.at[s, pl.ds(j * n2, n2), :], o.at[n1 + j], sem.at[s])
                for b, o in zip(bufs, outs) for j in range(xc_ref.shape[0] // n2)]

    @pl.when(k >= 2)
    def _():
        for c in lat_copies(slot, 0):
            c.wait()

    @pl.when(k < n1)
    def _():
        urh, url = _split(ur_ref[0])
        uih, uil = _split(ui_ref[0])
        mixed = (_dot3(c2h_ref[...], c2l_ref[...], urh, url)
                 + _dot3(s2h_ref[...], s2l_ref[...], uih, uil)) * norm_lat
        y = _dot(mixed.astype(BF16), w_ref[...]) + b_ref[...]
        x = _column_rows(x_hbm, xbuf, xsem, n1)
        _post(y, x, mod_ref[0], *common, tuple(b.at[slot, pl.ds(0, n2), :] for b in bufs) + (cnt_ref,))
        for c in lat_copies(slot, k):
            c.start()

    @pl.when(k == n1)
    def _():
        mod = mod_ref[1]
        x = xc_ref[...]
        h = x * (1.0 + mod[1:2]) + mod[0:1]
        a, b = _channel_dft(h, (cch_ref, ccl_ref), (sch_ref, scl_ref))
        ah, al = _split(a)
        bh, bl = _split(b)
        mixed = (_dot3(cnh_ref[...], cnl_ref[...], ah, al) - _dot3(snh_ref[...], snl_ref[...], bh, bl)) * norm_ctx
        y = _dot(mixed.astype(BF16), w_ref[...]) + b_ref[...]
        _post(y, x, mod, *common, tuple(b.at[slot] for b in bufs) + (cnt_ref,))
        for c in ctx_copies(slot):
            c.start()
        for c in lat_copies(1 - slot, 0) + ctx_copies(slot):
            c.wait()


def _fourier_layer(x_lat, x_ctx, mods, w_out, b_out, lng, lnb, router):
    n_lat, d = x_lat.shape
    n_ctx = x_ctx.shape[0]
    n2 = LANE
    n1 = n_lat // n2
    cw = d // FN_GROUPS
    x3 = x_lat.reshape(n1, n2, d)
    any_spec = pl.BlockSpec(memory_space=pl.ANY)
    col_scratch = [pltpu.VMEM((2, n1, d), F32), pltpu.SemaphoreType.DMA((2,))]
    cmat = _dft_mats(cw)
    w_bf = w_out.astype(BF16)
    b2 = b_out.reshape(1, d)
    lng2, lnb2 = lng.reshape(1, d), lnb.reshape(1, d)

    k1 = jnp.arange(n1, dtype=I32)
    pos = jnp.arange(n1, dtype=I32)[None, None, :] * n2 + jnp.arange(n2, dtype=I32)[:, None, None]
    ang = ((k1[None, :, None] * pos) % n_lat).astype(F32) * (2.0 * math.pi / n_lat)
    tc, ts = jnp.cos(ang), jnp.sin(ang)

    mat = _full((cw, cw))
    ur, ui = pl.pallas_call(
        _fourier1_kernel,
        grid=(n2,),
        in_specs=[
            any_spec,
            pl.BlockSpec((1, N_MOD, d), lambda j: (0, 0, 0)),
            mat, mat, mat, mat,
            pl.BlockSpec((1, n1, n1), lambda j: (j, 0, 0)),
            pl.BlockSpec((1, n1, n1), lambda j: (j, 0, 0)),
        ],
        out_specs=(pl.BlockSpec((n1, d), lambda j: (0, j)), pl.BlockSpec((n1, d), lambda j: (0, j))),
        out_shape=(jax.ShapeDtypeStruct((n1, n2 * d), F32), jax.ShapeDtypeStruct((n1, n2 * d), F32)),
        scratch_shapes=col_scratch,
        compiler_params=_params("arbitrary"),
        name="fourier1",
    )(x3, mods, *cmat, tc, ts)

    assert n_ctx % n2 == 0
    t_all = n_lat + n_ctx
    blocks = t_all // n2
    m2 = _dft_mats(n2)
    cn = _dft_mats(n_ctx)
    mat2 = _full((n2, n2))
    matn = _full((n_ctx, n_ctx))
    lat_step = lambda k: jnp.minimum(k, n1 - 1)
    out_shapes = (
        jax.ShapeDtypeStruct((blocks, n2, d), F32),
        jax.ShapeDtypeStruct((blocks, n2, d), F32),
        jax.ShapeDtypeStruct((blocks, n2, LANE), I32),
        jax.ShapeDtypeStruct((blocks, n2, LANE), F32),
        _COUNT_SHAPE,
    )
    outs = pl.pallas_call(
        functools.partial(_fourier2_kernel, n1=n1, norm_lat=1.0 / math.sqrt(n_lat * cw),
                          norm_ctx=1.0 / math.sqrt(n_ctx * cw)),
        grid=(n1 + 1,),
        in_specs=[
            pl.BlockSpec((1, n2, d), lambda k: (lat_step(k), 0, 0)),
            pl.BlockSpec((1, n2, d), lambda k: (lat_step(k), 0, 0)),
            mat2, mat2, mat2, mat2,
            _full((d, d)), _full((1, d)),
            any_spec,
            _full((n_ctx, d)),
            _full((2, N_MOD, d)),
            mat, mat, mat, mat, matn, matn, matn, matn,
            _full((1, d)), _full((1, d)),
            _full((d, LANE)), _full((d, LANE)), _full((1, LANE)),
        ],
        out_specs=(any_spec, any_spec, any_spec, any_spec, _COUNT_SPEC),
        out_shape=out_shapes,
        scratch_shapes=[pltpu.VMEM((2, n_ctx, d), F32), pltpu.VMEM((2, n_ctx, d), F32),
                        pltpu.VMEM((2, n_ctx, LANE), I32), pltpu.VMEM((2, n_ctx, LANE), F32),
                        pltpu.SemaphoreType.DMA((2,))] + col_scratch,
        compiler_params=_params("arbitrary"),
        name="fourier2",
    )(ur.reshape(n1, n2, d), ui.reshape(n1, n2, d), *m2, w_bf, b2, x3, x_ctx, mods, *cmat, *cn, lng2, lnb2,
      *router)
    return (outs[0].reshape(t_all, d), outs[1].reshape(t_all, d), outs[2].reshape(t_all, LANE),
            outs[3].reshape(t_all, LANE), outs[4])


def _qkv_kernel(pos_ref, *refs, n_qk, rms):
    mod_ref, w_ref, b_ref, cos_ref, sin_ref, *rest = refs[N_COMBINE_REFS:]
    if rms:
        gain_ref, ind_ref, indt_ref, x_ref, q_ref, k_ref, v_ref, ybuf, sem = rest
    else:
        x_ref, q_ref, k_ref, v_ref, ybuf, sem = rest
    x = _combine_tile(pos_ref, *refs[:N_COMBINE_REFS], ybuf, sem)
    x_ref[...] = x
    mod = mod_ref[0]
    h = x * (1.0 + mod[1:2]) + mod[0:1]
    y = _dot(h.astype(BF16), w_ref[...]) + b_ref[...]
    qk = y[:, :n_qk]
    if rms:
        sh, sl = _split(qk * qk)
        ms = _dot(sh, ind_ref[...]) + _dot(sl, ind_ref[...])
        mh, ml = _split(ms)
        msb = _dot(mh, indt_ref[...]) + _dot(ml, indt_ref[...])
        qk = qk * lax.rsqrt(msb + RMS_EPS) * gain_ref[...]
    cos = cos_ref[...]
    sin = sin_ref[...]
    even = (lax.broadcasted_iota(I32, cos.shape, 1) & 1) == 0
    parts = []
    for c in range(n_qk // LANE):
        z = qk[:, c * LANE:(c + 1) * LANE]
        swapped = jnp.where(even, pltpu.roll(z, LANE - 1, 1), pltpu.roll(z, 1, 1))
        parts.append(z * cos + swapped * sin)
    nq = q_ref.shape[1]
    q_ref[...] = (jnp.concatenate(parts[:nq // LANE], axis=1) * Q_SCALE).astype(BF16)
    k_ref[...] = jnp.concatenate(parts[nq // LANE:], axis=1).astype(BF16)
    v_ref[...] = y[:, n_qk:].astype(BF16)


def _qkv(pending, mods, w_qkv, b_qkv, cos_t, sin_t, n_q, n_kv, n_lat, q_norm=None, k_norm=None):
    t_all, d = pending[3].shape
    nq, nk = n_q * HEAD_DIM, n_kv * HEAD_DIM
    n_qk, n_all = nq + nk, nq + 2 * nk
    rms = q_norm is not None
    pos, args, in_specs, scratch = _combine_operands(pending, TM, n_lat)
    row = lambda w: pl.BlockSpec((TM, w), lambda t, *_: (t, 0))
    in_specs += [_tile_specs(n_lat // TM)[1], _full((d, n_all)), _full((1, n_all)), row(LANE), row(LANE)]
    args += [mods, w_qkv.astype(BF16), b_qkv.reshape(1, n_all), cos_t, sin_t]
    if rms:
        gain = jnp.concatenate([jnp.tile(q_norm, n_q), jnp.tile(k_norm, n_kv)]).reshape(1, n_qk)
        head = np.arange(n_qk) // HEAD_DIM
        ind = np.zeros((n_qk, LANE), np.float32)
        ind[np.arange(n_qk), head] = 1.0 / HEAD_DIM
        indt = np.zeros((LANE, n_qk), np.float32)
        indt[head, np.arange(n_qk)] = 1.0
        in_specs += [_full((1, n_qk)), _full((n_qk, LANE)), _full((LANE, n_qk))]
        args += [gain, jnp.asarray(ind, BF16), jnp.asarray(indt, BF16)]
    grid_spec = pltpu.PrefetchScalarGridSpec(
        num_scalar_prefetch=1,
        grid=(t_all // TM,),
        in_specs=in_specs,
        out_specs=(row(d), row(nq), row(nk), row(nk)),
        scratch_shapes=scratch,
    )
    return pl.pallas_call(
        functools.partial(_qkv_kernel, n_qk=n_qk, rms=rms),
        grid_spec=grid_spec,
        out_shape=(jax.ShapeDtypeStruct((t_all, d), F32), jax.ShapeDtypeStruct((t_all, nq), BF16),
                   jax.ShapeDtypeStruct((t_all, nk), BF16), jax.ShapeDtypeStruct((t_all, nk), BF16)),
        compiler_params=_params("arbitrary"),
        name="qkv",
    )(pos, *args)


def _rope_tables(n_lat, n_ctx):
    rows = n_lat // GRID_W
    row = jnp.repeat(jnp.arange(rows, dtype=F32), GRID_W)
    col = jnp.tile(jnp.arange(GRID_W, dtype=F32), rows)
    n_freq = HEAD_DIM // 4
    inv = ROPE_THETA ** (-jnp.arange(n_freq, dtype=F32) / n_freq)
    ang = jnp.concatenate([row[:, None] * inv, col[:, None] * inv], axis=-1)
    ang = jnp.concatenate([ang, jnp.zeros((n_ctx, HEAD_DIM // 2), F32)], axis=0)
    cos = jnp.tile(jnp.repeat(jnp.cos(ang), 2, axis=1), (1, LANE // HEAD_DIM))
    sin = jnp.tile(jnp.repeat(jnp.sin(ang), 2, axis=1), (1, LANE // HEAD_DIM))
    sign = jnp.where(jnp.arange(LANE) % 2 == 0, -1.0, 1.0).astype(F32)
    return cos, sin * sign


def _kv_layouts(k, v, n_kv):
    t_all = k.shape[0]
    kh = k.reshape(t_all, n_kv, HEAD_DIM).transpose(1, 0, 2)
    vt = v.reshape(t_all, n_kv, HEAD_DIM).transpose(1, 2, 0)
    pad = jnp.zeros((n_kv, LANE - HEAD_DIM, t_all), BF16).at[:, 0, :].set(1.0)
    return kh, jnp.concatenate([vt, pad], axis=1)


def _flash_kernel(q_ref, k_ref, vt_ref, o_ref, s_even, s_odd, *, grp, tk, n_lat, n_ctx):
    tq = q_ref.shape[0]
    q_t = q_ref[...].astype(F32).T
    qt = jnp.concatenate([q_t[g * HEAD_DIM:(g + 1) * HEAD_DIM] for g in range(grp)], axis=1).astype(BF16)
    cols = qt.shape[1]
    is_ctx = pl.program_id(1) == n_lat // FLASH_TQ
    n_chunks = (n_lat + n_ctx) // tk

    def finish(acc):
        o = acc[:HEAD_DIM] / acc[HEAD_DIM:HEAD_DIM + 1]
        o_t = jnp.concatenate([o[:, g * tq:(g + 1) * tq] for g in range(grp)], axis=0)
        o_ref[...] = o_t.T.astype(BF16)

    groups = [slice(c * MXU_WIDTH, (c + 1) * MXU_WIDTH) for c in range(cols // MXU_WIDTH)]

    def scores(j, buf, g):
        off = pl.multiple_of(j * tk, tk)
        s = _dot(k_ref[0, pl.ds(off, tk), :], qt[:, g])
        buf[:, g] = s
        return jnp.max(s, axis=0, keepdims=True)

    def accumulate(j, buf, g, m, acc, mc):
        off = pl.multiple_of(j * tk, tk)
        m_new = jnp.maximum(m, mc)
        p = jnp.exp2(buf[:, g] - m_new)
        acc = jnp.exp2(m - m_new) * acc + _dot(vt_ref[0, :, pl.ds(off, tk)], p.astype(BF16))
        return m_new, acc

    @pl.when(jnp.logical_not(is_ctx))
    def _():
        def step(j, cur, nxt, carry):
            out = []
            for g, (m, acc, mc) in zip(groups, carry):
                mc_next = scores(j + 1, nxt, g)
                out.append(accumulate(j, cur, g, m, acc, mc) + (mc_next,))
            return out

        def body(jj, carry):
            for pair in range(FLASH_PAIRS):
                j = 2 * (FLASH_PAIRS * jj + pair)
                carry = step(j, s_even, s_odd, carry)
                carry = step(j + 1, s_odd, s_even, carry)
            return carry

        init = [(jnp.full((1, MXU_WIDTH), NEG, F32), jnp.zeros((LANE, MXU_WIDTH), F32), scores(0, s_even, g))
                for g in groups]
        carry = lax.fori_loop(0, (n_chunks - 1) // (2 * FLASH_PAIRS), body, init)
        finish(jnp.concatenate([accumulate(n_chunks - 1, s_even, g, m, acc, mc)[1]
                                for g, (m, acc, mc) in zip(groups, carry)], axis=1))

    @pl.when(is_ctx)
    def _():
        s = _dot(k_ref[0, n_lat:n_lat + n_ctx, :], qt)
        p = jnp.exp2(s - jnp.max(s, axis=0, keepdims=True))
        finish(_dot(vt_ref[0, :, n_lat:n_lat + n_ctx], p.astype(BF16)))


def _full_attention(q, k, v, n_lat, n_ctx):
    t_all, nq = q.shape
    n_kv = k.shape[1] // HEAD_DIM
    grp = nq // HEAD_DIM // n_kv
    nt = t_all // FLASH_TQ
    cols = grp * FLASH_TQ
    assert n_ctx == FLASH_TQ and t_all % FLASH_TK == 0 and (t_all // FLASH_TK - 1) % (2 * FLASH_PAIRS) == 0
    kh, vt = _kv_layouts(k, v, n_kv)
    return pl.pallas_call(
        functools.partial(_flash_kernel, grp=grp, tk=FLASH_TK, n_lat=n_lat, n_ctx=n_ctx),
        grid=(n_kv, nt),
        in_specs=[
            pl.BlockSpec((FLASH_TQ, grp * HEAD_DIM), lambda h, i: (i, h)),
            pl.BlockSpec((1, t_all, HEAD_DIM), lambda h, i: (h, 0, 0)),
            pl.BlockSpec((1, LANE, t_all), lambda h, i: (h, 0, 0)),
        ],
        out_specs=pl.BlockSpec((FLASH_TQ, grp * HEAD_DIM), lambda h, i: (i, h)),
        out_shape=jax.ShapeDtypeStruct((t_all, nq), BF16),
        scratch_shapes=[pltpu.VMEM((FLASH_TK, cols), F32), pltpu.VMEM((FLASH_TK, cols), F32)],
        compiler_params=_params("arbitrary", "arbitrary"),
        name="flash",
    )(q, kh, vt)


def _window_kernel(q_ref, kp_ref, kc_ref, kn_ref, kx_ref, vp_ref, vc_ref, vn_ref, vx_ref, sink_ref, o_ref,
                   *, grp, nb):
    i = pl.program_id(1)
    q_t = q_ref[...].astype(F32).T
    qt = jnp.concatenate([q_t[g * HEAD_DIM:(g + 1) * HEAD_DIM] for g in range(grp)], axis=1).astype(BF16)
    cols = qt.shape[1]
    s = _dot(jnp.concatenate([kp_ref[0], kc_ref[0], kn_ref[0], kx_ref[0]], axis=0), qt)
    kj = lax.broadcasted_iota(I32, (Q_BLOCK, cols), 0)
    qi = lax.broadcasted_iota(I32, (Q_BLOCK, cols), 1) & (Q_BLOCK - 1)
    sp = jnp.where(kj >= jnp.where(i > 0, qi, Q_BLOCK), s[:Q_BLOCK], NEG)
    sn = jnp.where(kj <= jnp.where(i < nb - 1, qi, -1), s[2 * Q_BLOCK:3 * Q_BLOCK], NEG)
    s = jnp.concatenate([sp, s[Q_BLOCK:2 * Q_BLOCK], sn, s[3 * Q_BLOCK:]], axis=0)
    sink = sink_ref[0]
    m = jnp.maximum(jnp.max(s, axis=0, keepdims=True), sink)
    p = jnp.exp2(s - m).astype(BF16)
    acc = _dot(jnp.concatenate([vp_ref[0], vc_ref[0], vn_ref[0], vx_ref[0]], axis=1), p)
    o = acc[:HEAD_DIM] / (acc[HEAD_DIM:HEAD_DIM + 1] + jnp.exp2(sink - m))
    o_t = jnp.concatenate([o[:, g * Q_BLOCK:(g + 1) * Q_BLOCK] for g in range(grp)], axis=0)
    o_ref[...] = o_t.T.astype(BF16)


def _window_attention(q, k, v, sink, n_lat, n_ctx):
    nq = q.shape[1]
    n_kv = k.shape[1] // HEAD_DIM
    grp = nq // HEAD_DIM // n_kv
    gw = grp * HEAD_DIM
    nb = n_lat // Q_BLOCK
    cb = n_lat // n_ctx
    assert WINDOW == Q_BLOCK
    kh, vt = _kv_layouts(k, v, n_kv)
    sink_cols = jnp.repeat(sink.reshape(n_kv, 1, grp) * LOG2E, Q_BLOCK, axis=2)
    kspec = lambda f: pl.BlockSpec((1, Q_BLOCK, HEAD_DIM), lambda h, i: (h, f(i), 0))
    vspec = lambda f: pl.BlockSpec((1, LANE, Q_BLOCK), lambda h, i: (h, 0, f(i)))
    prev = lambda i: jnp.maximum(i - 1, 0)
    cur = lambda i: i
    nxt = lambda i: jnp.minimum(i + 1, nb - 1)
    return pl.pallas_call(
        functools.partial(_window_kernel, grp=grp, nb=nb),
        grid=(n_kv, nb),
        in_specs=[
            pl.BlockSpec((Q_BLOCK, gw), lambda h, i: (i, h)),
            kspec(prev), kspec(cur), kspec(nxt),
            pl.BlockSpec((1, n_ctx, HEAD_DIM), lambda h, i: (h, cb, 0)),
            vspec(prev), vspec(cur), vspec(nxt),
            pl.BlockSpec((1, LANE, n_ctx), lambda h, i: (h, 0, cb)),
            pl.BlockSpec((1, 1, grp * Q_BLOCK), lambda h, i: (h, 0, 0)),
        ],
        out_specs=pl.BlockSpec((Q_BLOCK, gw), lambda h, i: (i, h)),
        out_shape=jax.ShapeDtypeStruct((n_lat, nq), BF16),
        compiler_params=_params("arbitrary", "arbitrary"),
        name="window",
    )(q, kh, kh, kh, kh, vt, vt, vt, vt, sink_cols)


def _gmlp_kernel(pos_ref, *refs):
    (mod_ref, win_ref, bin_ref, vg_ref, vb_ref, ws_ref, bs_ref, wout_ref, bout_ref,
     lng_ref, lnb_ref, rwh_ref, rwl_ref, rb_ref, *rest) = refs[N_COMBINE_REFS:]
    out_refs, (ybuf, sem) = rest[:-2], rest[-2:]
    mod = mod_ref[0]
    x = _combine_tile(pos_ref, *refs[:N_COMBINE_REFS], ybuf, sem)
    h = x * (1.0 + mod[1:2]) + mod[0:1]
    z = _dot(h.astype(BF16), win_ref[...]) + bin_ref[...]
    z = 0.5 * z * (1.0 + lax.erf(z * (2.0 ** -0.5)))
    half = z.shape[1] // 2
    u = z[:, :half]
    v = _layer_norm(z[:, half:], vg_ref[...], vb_ref[...]).astype(BF16)
    cw = half // GM_GROUPS
    chunks = []
    for c in range(x.shape[0] // GM_CHUNK):
        vc = v[c * GM_CHUNK:(c + 1) * GM_CHUNK]
        chunks.append(jnp.concatenate(
            [_dot(ws_ref[g], vc[:, g * cw:(g + 1) * cw]) + bs_ref[g] for g in range(GM_GROUPS)], axis=1))
    gated = u * jnp.concatenate(chunks, axis=0)
    y = _dot(gated.astype(BF16), wout_ref[...]) + bout_ref[...]
    _post(y, x, mod, lng_ref[...], lnb_ref[...], rwh_ref[...], rwl_ref[...], rb_ref[...], out_refs)


def _gmlp_layer(pending, mods, w_in, b_in, vg, vb, w_s, b_s, w_out, b_out, lng, lnb, router, n_lat):
    t_all, d = pending[3].shape
    dffn = w_in.shape[1]
    half = dffn // 2
    cw = half // GM_GROUPS
    shapes, specs = _post_out(t_all)
    bs_full = jnp.broadcast_to(b_s[:, :, None], (GM_GROUPS, GM_CHUNK, cw))
    pos, args, in_specs, scratch = _combine_operands(pending, TM, n_lat)
    in_specs += [
        _tile_specs(n_lat // TM)[1], _full((d, dffn)), _full((1, dffn)), _full((1, half)), _full((1, half)),
        _full((GM_GROUPS, GM_CHUNK, GM_CHUNK)), _full((GM_GROUPS, GM_CHUNK, cw)),
        _full((half, d)), _full((1, d)), _full((1, d)), _full((1, d)),
        _full((d, LANE)), _full((d, LANE)), _full((1, LANE)),
    ]
    args += [mods, w_in.astype(BF16), b_in.reshape(1, dffn), vg.reshape(1, half), vb.reshape(1, half),
             w_s.astype(BF16), bs_full, w_out.astype(BF16), b_out.reshape(1, d), lng.reshape(1, d),
             lnb.reshape(1, d), *router]
    grid_spec = pltpu.PrefetchScalarGridSpec(
        num_scalar_prefetch=1,
        grid=(t_all // TM,),
        in_specs=in_specs,
        out_specs=specs,
        scratch_shapes=scratch,
    )
    return pl.pallas_call(
        _gmlp_kernel,
        grid_spec=grid_spec,
        out_shape=shapes,
        compiler_params=_params("arbitrary"),
        name="gmlp",
    )(pos, *args)


COMB_TM = 128


def _route_meta(count_blk, n_tok):
    n_assign = n_tok * TOP_K
    n_blocks = -(-n_assign // MOE_BLOCK) + N_EXPERTS
    ids = jnp.arange(N_EXPERTS, dtype=I32)
    counts = count_blk[0, :N_EXPERTS].astype(I32)
    padded = (counts + MOE_BLOCK - 1) // MOE_BLOCK * MOE_BLOCK
    ends_pad = jnp.cumsum(padded)
    base = ends_pad - padded
    starts = jnp.arange(n_blocks, dtype=I32) * MOE_BLOCK
    block_expert = jnp.minimum(jnp.sum((ends_pad[None, :] <= starts[:, None]).astype(I32), axis=1), N_EXPERTS - 1)
    last_blk = jnp.where(padded > 0, ends_pad - MOE_BLOCK, -1)
    tail = ends_pad[-1] + ids * MOE_BLOCK
    tail = jnp.where(tail < n_blocks * MOE_BLOCK, tail, -1)
    base_b = jnp.broadcast_to(base.astype(F32)[:, None], (N_EXPERTS, LANE))
    return block_expert.astype(I32), base_b, jnp.concatenate([last_blk, tail]).astype(I32), n_blocks


def _rank_kernel(idx_ref, base_ref, upper_ref, dest_ref, run_ref):
    @pl.when(pl.program_id(0) == 0)
    def _():
        run_ref[...] = jnp.zeros(run_ref.shape, F32)

    eid = lax.broadcasted_iota(I32, (N_EXPERTS, TM), 0)
    onehots = [(eid == idx_ref[k:k + 1, :]).astype(F32) for k in range(TOP_K)]
    cnt = onehots[0] + onehots[1] + onehots[2] + onehots[3]
    before = _dot(cnt.astype(BF16), upper_ref[...])
    slot = base_ref[:, :1] + run_ref[:, :1] + before
    for k in range(TOP_K):
        rows = (jnp.sum(onehots[k] * slot, axis=0, keepdims=True) * ROW_TILE).astype(I32)
        for j in range(TM // COMB_TM):
            dest_ref[j, k:k + 1, :] = rows[:, j * COMB_TM:(j + 1) * COMB_TM]
    run_ref[...] = run_ref[...] + jnp.sum(cnt, axis=1, keepdims=True)


def _rank(top_idx, base_b, n_tok):
    per_step = TM // COMB_TM
    idx_t = top_idx[:n_tok, :TOP_K].T
    upper = jnp.asarray(np.triu(np.ones((TM, TM), np.float32), 1), BF16)
    return pl.pallas_call(
        _rank_kernel,
        grid=(n_tok // TM,),
        in_specs=[pl.BlockSpec((TOP_K, TM), lambda t: (0, t)), _full((N_EXPERTS, LANE)), _full((TM, TM))],
        out_specs=pl.BlockSpec((per_step, TOP_K, COMB_TM), lambda t: (t, 0, 0)),
        out_shape=jax.ShapeDtypeStruct((n_tok // COMB_TM, TOP_K, COMB_TM), I32),
        scratch_shapes=[pltpu.VMEM((N_EXPERTS, LANE), F32)],
        compiler_params=_params("arbitrary"),
        name="rank",
    )(idx_t, base_b, upper)


def _dispatch_kernel(dest_ref, zrow_ref, h_ref, xs_hbm, hbuf, zbuf, sem, zsem, *, n_tiles):
    t = pl.program_id(0)
    slot = t % 2
    n_zero = 2 * N_EXPERTS

    def zero_copy(j):
        row = pl.multiple_of(zrow_ref[j] * ROW_TILE, MOE_BLOCK * ROW_TILE)
        return pltpu.make_async_copy(zbuf, xs_hbm.at[pl.ds(row, MOE_BLOCK * ROW_TILE), :], zsem.at[0])

    def wait_rows(s):
        for _ in range(TOP_K):
            pltpu.make_async_copy(hbuf.at[s], xs_hbm.at[pl.ds(0, COMB_TM * ROW_TILE), :], sem.at[s]).wait()

    @pl.when(t == 0)
    def _():
        zbuf[...] = jnp.zeros(zbuf.shape, F32)
        for j in range(n_zero):
            @pl.when(zrow_ref[j] >= 0)
            def _():
                zero_copy(j).start()
        for j in range(n_zero):
            @pl.when(zrow_ref[j] >= 0)
            def _():
                zero_copy(j).wait()

    @pl.when(t >= 2)
    def _():
        wait_rows(slot)

    _to_row_tiles(hbuf.at[slot], h_ref[...])
    for k in range(TOP_K):
        for r in range(COMB_TM):
            row = pl.multiple_of(dest_ref[(t * TOP_K + k) * COMB_TM + r], ROW_TILE)
            pltpu.make_async_copy(hbuf.at[slot, pl.ds(r * ROW_TILE, ROW_TILE), :],
                                  xs_hbm.at[pl.ds(row, ROW_TILE), :], sem.at[slot]).start(priority=r % 2)

    @pl.when(t == n_tiles - 1)
    def _():
        wait_rows(slot)
        if n_tiles > 1:
            wait_rows(1 - slot)


def _dispatch(h2, dest, zrows, n_tok, n_blocks):
    d = h2.shape[1]
    n_tiles = n_tok // COMB_TM
    grid_spec = pltpu.PrefetchScalarGridSpec(
        num_scalar_prefetch=2,
        grid=(n_tiles,),
        in_specs=[pl.BlockSpec((COMB_TM, d), lambda t, dest, zr: (t, 0))],
        out_specs=pl.BlockSpec(memory_space=pl.ANY),
        scratch_shapes=[pltpu.VMEM((2, COMB_TM * ROW_TILE, LANE), F32),
                        pltpu.VMEM((MOE_BLOCK * ROW_TILE, LANE), F32),
                        pltpu.SemaphoreType.DMA((2,)), pltpu.SemaphoreType.DMA((1,))],
    )
    return pl.pallas_call(
        functools.partial(_dispatch_kernel, n_tiles=n_tiles),
        grid_spec=grid_spec,
        out_shape=jax.ShapeDtypeStruct((n_blocks * MOE_BLOCK * ROW_TILE, LANE), F32),
        compiler_params=_params("arbitrary"),
        name="dispatch",
    )(dest.reshape(-1), zrows, h2)


EXP_CHUNK = MXU_WIDTH


def _expert_kernel(be_ref, x_ref, wgu_ref, bgu_ref, wd_ref, bd_ref, sel_ref, y_ref,
                   wg_s, wl_s, wd_s, bg_s, bl_s):
    b = pl.program_id(0)
    ff = wg_s.shape[1]
    half = EXP_CHUNK // 2

    @pl.when(jnp.logical_or(b == 0, be_ref[b] != be_ref[jnp.maximum(b - 1, 0)]))
    def _():
        for c in range(2 * ff // EXP_CHUNK):
            w = _dot(wgu_ref[0, 0, :, c * EXP_CHUNK:(c + 1) * EXP_CHUNK].astype(BF16), sel_ref[...])
            wg_s[:, c * half:(c + 1) * half] = w[:, :half].astype(BF16)
            wl_s[:, c * half:(c + 1) * half] = w[:, half:].astype(BF16)
            bh, bl = _split(jnp.broadcast_to(bgu_ref[0, 0, :, c * EXP_CHUNK:(c + 1) * EXP_CHUNK],
                                             (SUBLANE, EXP_CHUNK)))
            bias = _dot(bh, sel_ref[...]) + _dot(bl, sel_ref[...])
            bg_s[:, c * half:(c + 1) * half] = bias[:, :half]
            bl_s[:, c * half:(c + 1) * half] = bias[:, half:]
        wd_s[...] = wd_ref[0, 0].astype(BF16)

    x = _from_row_tiles(x_ref, 0, MOE_BLOCK).astype(BF16)
    glu = jnp.minimum(_dot(x, wg_s[...]) + bg_s[0:1, :], SWIGLU_LIMIT)
    lin = jnp.clip(_dot(x, wl_s[...]) + bl_s[0:1, :], -SWIGLU_LIMIT, SWIGLU_LIMIT)
    act = glu * (1.0 / (1.0 + jnp.exp(-SWIGLU_ALPHA * glu))) * (lin + 1.0)
    _to_row_tiles(y_ref, _dot(act.astype(BF16), wd_s[...]) + bd_ref[0, 0])


def _experts(xs, block_expert, n_blocks, w_gate_up, b_gate_up, w_down, b_down, layer):
    d = w_down.shape[3]
    ff = w_down.shape[2]
    half = EXP_CHUNK // 2
    sel = np.zeros((EXP_CHUNK, EXP_CHUNK), np.float32)
    sel[2 * np.arange(half), np.arange(half)] = 1.0
    sel[2 * np.arange(half) + 1, half + np.arange(half)] = 1.0
    wspec = lambda r, c: pl.BlockSpec((1, 1, r, c), lambda b, be: (layer, be[b], 0, 0))
    sspec = pl.BlockSpec((EXP_CHUNK, EXP_CHUNK), lambda b, be: (0, 0))
    grid_spec = pltpu.PrefetchScalarGridSpec(
        num_scalar_prefetch=1,
        grid=(n_blocks,),
        in_specs=[pl.BlockSpec((MOE_BLOCK * ROW_TILE, LANE), lambda b, be: (b, 0)), wspec(d, 2 * ff),
                  wspec(1, 2 * ff), wspec(ff, d), wspec(1, d), sspec],
        out_specs=pl.BlockSpec((MOE_BLOCK * ROW_TILE, LANE), lambda b, be: (b, 0)),
        scratch_shapes=[pltpu.VMEM((d, ff), BF16), pltpu.VMEM((d, ff), BF16), pltpu.VMEM((ff, d), BF16),
                        pltpu.VMEM((SUBLANE, ff), F32), pltpu.VMEM((SUBLANE, ff), F32)],
    )
    return pl.pallas_call(
        _expert_kernel,
        grid_spec=grid_spec,
        out_shape=jax.ShapeDtypeStruct((n_blocks * MOE_BLOCK * ROW_TILE, LANE), F32),
        compiler_params=_params("arbitrary"),
        name="experts",
    )(block_expert, xs, w_gate_up, b_gate_up[:, :, None, :], w_down, b_down[:, :, None, :],
      jnp.asarray(sel, BF16))


def _gather_rows(idx_ref, base, n, src_hbm, dst, sem):
    for r in range(n):
        row = pl.multiple_of(idx_ref[base + r], ROW_TILE)
        pltpu.make_async_copy(src_hbm.at[pl.ds(row, ROW_TILE), :], dst.at[pl.ds(r * ROW_TILE, ROW_TILE), :],
                              sem).start(priority=r % 2)


N_COMBINE_REFS = 6


def _combine_tile(pos_ref, y_hbm, gate_ref, x1_ref, mod_ref, lng_ref, lnb_ref, ybuf, sem):
    t = pl.program_id(0)
    slot = t % 2
    tm, d = x1_ref.shape
    n = tm * TOP_K

    @pl.when(t == 0)
    def _():
        _gather_rows(pos_ref, 0, n, y_hbm, ybuf.at[0], sem.at[0])

    @pl.when(t + 1 < pl.num_programs(0))
    def _():
        _gather_rows(pos_ref, (t + 1) * n, n, y_hbm, ybuf.at[1 - slot], sem.at[1 - slot])

    pltpu.make_async_copy(y_hbm.at[pl.ds(0, n * ROW_TILE), :], ybuf.at[slot], sem.at[slot]).wait()
    gates = gate_ref[...]
    parts = []
    for j in range(tm // COMB_TM):
        f = jnp.zeros((COMB_TM, d), F32)
        for k in range(TOP_K):
            rows = _from_row_tiles(ybuf.at[slot], (j * TOP_K + k) * COMB_TM, COMB_TM)
            f = f + gates[j * COMB_TM:(j + 1) * COMB_TM, k:k + 1] * rows
        parts.append(f)
    f = parts[0] if len(parts) == 1 else jnp.concatenate(parts, axis=0)
    return _layer_norm(ALPHA * x1_ref[...] + mod_ref[0][5:6] * f, lng_ref[...], lnb_ref[...])


def _combine_operands(pending, tm, n_lat):
    yb, dest, gates, x1, mods, lng, lnb = pending
    d = x1.shape[1]
    n_lat_tiles = n_lat // tm
    specs = [
        pl.BlockSpec(memory_space=pl.ANY),
        pl.BlockSpec((tm, LANE), lambda t, *_: (t, 0)),
        pl.BlockSpec((tm, d), lambda t, *_: (t, 0)),
        pl.BlockSpec((1, N_MOD, d), lambda t, *_: (jnp.where(t >= n_lat_tiles, 1, 0), 0, 0)),
        _full((1, d)), _full((1, d)),
    ]
    scratch = [pltpu.VMEM((2, tm * TOP_K * ROW_TILE, LANE), F32), pltpu.SemaphoreType.DMA((2,))]
    return dest.reshape(-1), [yb, gates, x1, mods, lng.reshape(1, d), lnb.reshape(1, d)], specs, scratch


def _combine_kernel(pos_ref, *refs):
    o_ref, ybuf, sem = refs[N_COMBINE_REFS:]
    o_ref[...] = _combine_tile(pos_ref, *refs[:N_COMBINE_REFS], ybuf, sem)


def _combine(pending, n_tok, n_lat):
    d = pending[3].shape[1]
    pos, args, specs, scratch = _combine_operands(pending, COMB_TM, n_lat)
    grid_spec = pltpu.PrefetchScalarGridSpec(
        num_scalar_prefetch=1,
        grid=(n_tok // COMB_TM,),
        in_specs=specs,
        out_specs=pl.BlockSpec((COMB_TM, d), lambda t, *_: (t, 0)),
        scratch_shapes=scratch,
    )
    return pl.pallas_call(
        _combine_kernel,
        grid_spec=grid_spec,
        out_shape=jax.ShapeDtypeStruct((n_tok, d), F32),
        compiler_params=_params("arbitrary"),
        name="combine",
    )(pos, *args)


def _moe_layer(x1, h2, top_idx, gates, counts, mods, lng, lnb, w_gate_up, b_gate_up, w_down, b_down, layer, n_tok):
    block_expert, base_b, zrows, n_blocks = _route_meta(counts, n_tok)
    dest = _rank(top_idx, base_b, n_tok)
    xs = _dispatch(h2, dest, zrows, n_tok, n_blocks)
    yb = _experts(xs, block_expert, n_blocks, w_gate_up, b_gate_up, w_down, b_down, layer)
    return yb, dest, gates, x1, mods, lng, lnb


def kernel(x, c, ctx, c_ctx, ada_w, ada_b, ln_mix_g, ln_mix_b, ln_ffn_g, ln_ffn_b, fn_w_out, fn_b_out, fa_w_qkv, fa_b_qkv, fa_q_norm, fa_k_norm, fa_w_out, fa_b_out, gm_w_in, gm_b_in, gm_v_norm_g, gm_v_norm_b, gm_w_s, gm_b_s, gm_w_out, gm_b_out, wa_w_qkv, wa_b_qkv, wa_sink, wa_w_out, wa_b_out, router_w, router_b, exp_w_gate_up, exp_b_gate_up, exp_w_down, exp_b_down):
    bsz, n_lat, d = x.shape
    n_ctx = ctx.shape[1]
    assert bsz == 1 and d == D_MODEL and n_lat == LANE * LANE and n_lat % n_ctx == 0 and n_ctx % TM == 0
    t_all = n_lat + n_ctx
    n_lat_tiles = n_lat // TM
    mods_all = _ada(c, c_ctx, ada_w, ada_b)
    cos_t, sin_t = _rope_tables(n_lat, n_ctx)

    for i in range(DEPTH):
        kind, j = i % 4, i // 4
        last = i == DEPTH - 1
        n_tok = n_lat if last else t_all
        mods = mods_all[i]
        router = _router_operands(router_w[i], router_b[i])
        lng, lnb = ln_mix_g[i], ln_mix_b[i]
        if kind == 0:
            post = _fourier_layer(x[0], ctx[0], mods, fn_w_out[j], fn_b_out[j], lng, lnb, router)
        elif kind == 1:
            x_all, q, k, v = _qkv(pending, mods, fa_w_qkv[j], fa_b_qkv[j], cos_t, sin_t, FA_Q_HEADS,
                                  FA_KV_HEADS, n_lat, fa_q_norm[j], fa_k_norm[j])
            o = _full_attention(q, k, v, n_lat, n_ctx)
            post = _proj_post(o, fa_w_out[j], fa_b_out[j], x_all, mods, lng, lnb, router, n_tok, n_lat_tiles)
        elif kind == 2:
            post = _gmlp_layer(pending, mods, gm_w_in[j], gm_b_in[j], gm_v_norm_g[j], gm_v_norm_b[j],
                               gm_w_s[j], gm_b_s[j], gm_w_out[j], gm_b_out[j], lng, lnb, router, n_lat)
        else:
            x_all, q, k, v = _qkv(pending, mods, wa_w_qkv[j], wa_b_qkv[j], cos_t, sin_t, WA_Q_HEADS,
                                  WA_KV_HEADS, n_lat)
            o = _window_attention(q, k, v, wa_sink[j], n_lat, n_ctx)
            post = _proj_post(o, wa_w_out[j], wa_b_out[j], x_all, mods, lng, lnb, router, n_tok, n_lat_tiles)
        pending = _moe_layer(*post, mods, ln_ffn_g[i], ln_ffn_b[i], exp_w_gate_up, exp_b_gate_up, exp_w_down,
                             exp_b_down, i, n_tok)
    return _combine(pending, n_lat, n_lat)[None]
```

```python
import functools
import math

import numpy as np
import jax
import jax.numpy as jnp
from jax import lax
from jax.experimental import pallas as pl
from jax.experimental.pallas import tpu as pltpu

F32, BF16, I32 = jnp.float32, jnp.bfloat16, jnp.int32

D_MODEL = 1024
DEPTH = 4
GRID_W = 64
N_MOD = 6
FN_GROUPS = 4
HEAD_DIM = 64
FA_Q_HEADS, FA_KV_HEADS = 16, 4
WA_Q_HEADS, WA_KV_HEADS = 16, 2
WINDOW = 128
Q_BLOCK = 128
ROPE_THETA = 10000.0
GM_CHUNK = 128
GM_GROUPS = 8
N_EXPERTS = 32
TOP_K = 4
SWIGLU_LIMIT = 7.0
SWIGLU_ALPHA = 1.702
MOE_BLOCK = 256
LN_EPS = 1e-5
RMS_EPS = 1e-6
NEG = -1e30
ALPHA = (2 * DEPTH) ** 0.25
LOG2E = math.log2(math.e)
Q_SCALE = HEAD_DIM ** -0.5 * LOG2E

LANE = 128
SUBLANE = 8
ROW_TILE = D_MODEL // LANE
assert ROW_TILE == SUBLANE
MXU_WIDTH = 256
TM = 256
FLASH_TQ = 256
FLASH_TK = 1280
FLASH_PAIRS = 3
VMEM_LIMIT = 56 * 2 ** 20


def _params(*sem):
    return pltpu.CompilerParams(dimension_semantics=sem, vmem_limit_bytes=VMEM_LIMIT)


def _dot(a, b):
    return jnp.dot(a, b, preferred_element_type=F32)


def _split(a):
    hi = a.astype(BF16)
    lo = (a - hi.astype(F32)).astype(BF16)
    return hi, lo


def _dot3(a_hi, a_lo, b_hi, b_lo):
    return _dot(a_hi, b_hi) + (_dot(a_hi, b_lo) + _dot(a_lo, b_hi))


def _layer_norm(x, g, b):
    mu = jnp.mean(x, axis=-1, keepdims=True)
    xc = x - mu
    var = jnp.mean(xc * xc, axis=-1, keepdims=True)
    return xc * lax.rsqrt(var + LN_EPS) * g + b


def _top4(logits):
    lane = lax.broadcasted_iota(I32, logits.shape, 1).astype(F32)
    cur = logits
    vals, idxs = [], []
    for _ in range(TOP_K):
        m = jnp.max(cur, axis=-1, keepdims=True)
        i = jnp.min(jnp.where(cur == m, lane, float(LANE)), axis=-1, keepdims=True)
        vals.append(m)
        idxs.append(i)
        cur = jnp.where(lane == i, -jnp.inf, cur)
    exps = [jnp.exp(v - vals[0]) for v in vals]
    inv = 1.0 / (exps[0] + exps[1] + exps[2] + exps[3])
    idx_out = jnp.zeros_like(logits)
    gate_out = jnp.zeros_like(logits)
    picked = jnp.zeros_like(logits)
    for k in range(TOP_K):
        idx_out = jnp.where(lane == float(k), idxs[k], idx_out)
        gate_out = jnp.where(lane == float(k), exps[k] * inv, gate_out)
        picked = picked + jnp.where(lane == idxs[k], 1.0, 0.0)
    return idx_out.astype(I32), gate_out, jnp.sum(picked, axis=0, keepdims=True)


def _post(y, x, mod, lng, lnb, rw_hi, rw_lo, rb, out_refs):
    x1_ref, h2_ref, idx_ref, gate_ref, cnt_ref = out_refs
    x1 = _layer_norm(ALPHA * x + mod[2:3] * y, lng, lnb)
    h2 = x1 * (1.0 + mod[4:5]) + mod[3:4]
    hh, hl = _split(h2)
    logits = _dot3(hh, hl, rw_hi, rw_lo) + rb
    idx, gates, cnt = _top4(logits)
    x1_ref[...] = x1
    h2_ref[...] = h2
    idx_ref[...] = idx
    gate_ref[...] = gates

    @pl.when(pl.program_id(0) == 0)
    def _():
        cnt_ref[...] = jnp.zeros(cnt_ref.shape, F32)

    cnt_ref[...] = cnt_ref[...] + cnt


def _to_row_tiles(ref, x):
    n = x.shape[0]
    for s in range(ROW_TILE):
        ref[pl.ds(s, n, stride=ROW_TILE), :] = x[:, s * LANE:(s + 1) * LANE]


def _from_row_tiles(ref, start, n):
    return jnp.concatenate([ref[pl.ds(start * ROW_TILE + s, n, stride=ROW_TILE), :] for s in range(ROW_TILE)],
                           axis=1)


def _ada_kernel(cs_ref, w_ref, b_ref, o_ref):
    cs = cs_ref[...]
    s = cs * (1.0 / (1.0 + jnp.exp(-cs)))
    sh, sl = _split(s)
    wh, wl = _split(w_ref[0])
    o_ref[0] = _dot3(sh, sl, wh, wl) + b_ref[0]


def _ada(c, c_ctx, ada_w, ada_b):
    d = c.shape[-1]
    nm = ada_w.shape[-1]
    tn = nm // 4
    cs = jnp.zeros((SUBLANE, d), F32).at[0].set(c[0]).at[1].set(c_ctx)
    out = pl.pallas_call(
        _ada_kernel,
        grid=(DEPTH, nm // tn),
        in_specs=[
            pl.BlockSpec((SUBLANE, d), lambda i, j: (0, 0)),
            pl.BlockSpec((1, d, tn), lambda i, j: (i, 0, j)),
            pl.BlockSpec((1, 1, tn), lambda i, j: (i, 0, j)),
        ],
        out_specs=pl.BlockSpec((1, SUBLANE, tn), lambda i, j: (i, 0, j)),
        out_shape=jax.ShapeDtypeStruct((DEPTH, SUBLANE, nm), F32),
        compiler_params=_params("arbitrary", "arbitrary"),
        name="ada",
    )(cs, ada_w, ada_b.reshape(DEPTH, 1, nm))
    return out[:, :2].reshape(DEPTH, 2, N_MOD, d)


def _tile_specs(n_lat_tiles):
    tok = pl.BlockSpec((TM, D_MODEL), lambda t, *_: (t, 0))
    mod = pl.BlockSpec((1, N_MOD, D_MODEL), lambda t, *_: (jnp.where(t >= n_lat_tiles, 1, 0), 0, 0))
    return tok, mod


def _full(shape):
    nd = len(shape)
    return pl.BlockSpec(shape, lambda *_: (0,) * nd)


_COUNT_SHAPE = jax.ShapeDtypeStruct((SUBLANE, LANE), F32)
_COUNT_SPEC = pl.BlockSpec((SUBLANE, LANE), lambda *_: (0, 0))


def _post_out(n_rows):
    shapes = (
        jax.ShapeDtypeStruct((n_rows, D_MODEL), F32),
        jax.ShapeDtypeStruct((n_rows, D_MODEL), F32),
        jax.ShapeDtypeStruct((n_rows, LANE), I32),
        jax.ShapeDtypeStruct((n_rows, LANE), F32),
        _COUNT_SHAPE,
    )
    specs = (
        pl.BlockSpec((TM, D_MODEL), lambda t, *_: (t, 0)),
        pl.BlockSpec((TM, D_MODEL), lambda t, *_: (t, 0)),
        pl.BlockSpec((TM, LANE), lambda t, *_: (t, 0)),
        pl.BlockSpec((TM, LANE), lambda t, *_: (t, 0)),
        _COUNT_SPEC,
    )
    return shapes, specs


def _router_operands(router_w, router_b):
    rw = jnp.zeros((D_MODEL, LANE), F32).at[:, :N_EXPERTS].set(router_w)
    rw_hi = rw.astype(BF16)
    rw_lo = (rw - rw_hi.astype(F32)).astype(BF16)
    rb = jnp.full((1, LANE), NEG, F32).at[0, :N_EXPERTS].set(router_b)
    return rw_hi, rw_lo, rb


def _proj_post_kernel(a_ref, w_ref, b_ref, x_ref, mod_ref, lng_ref, lnb_ref, rwh_ref, rwl_ref, rb_ref, *out_refs):
    y = _dot(a_ref[...], w_ref[...]) + b_ref[...]
    _post(y, x_ref[...], mod_ref[0], lng_ref[...], lnb_ref[...], rwh_ref[...], rwl_ref[...], rb_ref[...], out_refs)


def _proj_post(a, w_out, b_out, x, mods, lng, lnb, router, n_rows, n_lat_tiles):
    k = a.shape[1]
    tok, mod = _tile_specs(n_lat_tiles)
    shapes, specs = _post_out(n_rows)
    return pl.pallas_call(
        _proj_post_kernel,
        grid=(n_rows // TM,),
        in_specs=[
            pl.BlockSpec((TM, k), lambda t: (t, 0)),
            _full((k, D_MODEL)), _full((1, D_MODEL)),
            tok, mod, _full((1, D_MODEL)), _full((1, D_MODEL)),
            _full((D_MODEL, LANE)), _full((D_MODEL, LANE)), _full((1, LANE)),
        ],
        out_specs=specs,
        out_shape=shapes,
        compiler_params=_params("arbitrary"),
        name="proj_post",
    )(a, w_out.astype(BF16), b_out.reshape(1, -1), x, mods, lng.reshape(1, -1), lnb.reshape(1, -1), *router)


def _dft_mats(n):
    jk = np.outer(np.arange(n), np.arange(n)) % n
    ang = 2.0 * np.pi * jk / n
    out = []
    for m in (np.cos(ang), np.sin(ang)):
        m32 = jnp.asarray(m, F32)
        hi = m32.astype(BF16)
        out += [hi, (m32 - hi.astype(F32)).astype(BF16)]
    return out


def _channel_dft(h, cc, sc):
    cw = cc[0].shape[0]
    a_parts, b_parts = [], []
    for g in range(h.shape[1] // cw):
        hh, hl = _split(h[:, g * cw:(g + 1) * cw])
        a_parts.append(_dot3(hh, hl, cc[0][...], cc[1][...]))
        b_parts.append(_dot3(hh, hl, sc[0][...], sc[1][...]))
    return jnp.concatenate(a_parts, axis=1), jnp.concatenate(b_parts, axis=1)


def _column_rows(x_hbm, buf, sem, n_steps):
    j = pl.program_id(0)
    slot = j % 2

    def copy(step, s):
        return pltpu.make_async_copy(x_hbm.at[pl.ds(0, buf.shape[1]), step, :], buf.at[s], sem.at[s])

    @pl.when(j == 0)
    def _():
        copy(0, 0).start()

    @pl.when(j + 1 < n_steps)
    def _():
        copy(j + 1, 1 - slot).start()

    copy(j, slot).wait()
    return buf[slot]


def _fourier1_kernel(x_hbm, mod_ref, cch_ref, ccl_ref, sch_ref, scl_ref, tc_ref, ts_ref, ur_ref, ui_ref, xbuf, xsem):
    mod = mod_ref[0]
    h = _column_rows(x_hbm, xbuf, xsem, pl.num_programs(0)) * (1.0 + mod[1:2]) + mod[0:1]
    a, b = _channel_dft(h, (cch_ref, ccl_ref), (sch_ref, scl_ref))
    tch, tcl = _split(tc_ref[0])
    tsh, tsl = _split(ts_ref[0])
    ah, al = _split(a)
    bh, bl = _split(b)
    ur_ref[...] = _dot3(tch, tcl, ah, al) - _dot3(tsh, tsl, bh, bl)
    ui_ref[...] = -(_dot3(tch, tcl, bh, bl) + _dot3(tsh, tsl, ah, al))


def _fourier2_kernel(ur_ref, ui_ref, c2h_ref, c2l_ref, s2h_ref, s2l_ref, w_ref, b_ref, x_hbm, xc_ref, mod_ref,
                     cch_ref, ccl_ref, sch_ref, scl_ref, cnh_ref, cnl_ref, snh_ref, snl_ref,
                     lng_ref, lnb_ref, rwh_ref, rwl_ref, rb_ref,
                     x1_hbm, h2_hbm, idx_hbm, gate_hbm, cnt_ref, bx1, bh2, bidx, bgate, sem, xbuf, xsem,
                     *, n1, norm_lat, norm_ctx):
    k = pl.program_id(0)
    n2 = ur_ref.shape[1]
    slot = k % 2
    bufs = (bx1, bh2, bidx, bgate)
    outs = (x1_hbm, h2_hbm, idx_hbm, gate_hbm)
    common = (lng_ref[...], lnb_ref[...], rwh_ref[...], rwl_ref[...], rb_ref[...])

    def lat_copies(s, col):
        return [pltpu.make_async_copy(b.at[s, pl.ds(0, n2), :], o.at[pl.ds(0, n2), col, :], sem.at[s])
                for b, o in zip(bufs, outs)]

    def ctx_copies(s):
        return [pltpu.make_async_copy(b.at[s, pl.ds(j * n2, n2), :], o.at[n1 + j], sem.at[s])
                for b, o in zip(bufs, outs) for j in range(xc_ref.shape[0] // n2)]

    @pl.when(k >= 2)
    def _():
        for c in lat_copies(slot, 0):
            c.wait()

    @pl.when(k < n1)
    def _():
        urh, url = _split(ur_ref[0])
        uih, uil = _split(ui_ref[0])
        mixed = (_dot3(c2h_ref[...], c2l_ref[...], urh, url)
                 + _dot3(s2h_ref[...], s2l_ref[...], uih, uil)) * norm_lat
        y = _dot(mixed.astype(BF16), w_ref[...]) + b_ref[...]
        x = _column_rows(x_hbm, xbuf, xsem, n1)
        _post(y, x, mod_ref[0], *common, tuple(b.at[slot, pl.ds(0, n2), :] for b in bufs) + (cnt_ref,))
        for c in lat_copies(slot, k):
            c.start()

    @pl.when(k == n1)
    def _():
        mod = mod_ref[1]
        x = xc_ref[...]
        h = x * (1.0 + mod[1:2]) + mod[0:1]
        a, b = _channel_dft(h, (cch_ref, ccl_ref), (sch_ref, scl_ref))
        ah, al = _split(a)
        bh, bl = _split(b)
        mixed = (_dot3(cnh_ref[...], cnl_ref[...], ah, al) - _dot3(snh_ref[...], snl_ref[...], bh, bl)) * norm_ctx
        y = _dot(mixed.astype(BF16), w_ref[...]) + b_ref[...]
        _post(y, x, mod, *common, tuple(b.at[slot] for b in bufs) + (cnt_ref,))
        for c in ctx_copies(slot):
            c.start()
        for c in lat_copies(1 - slot, 0) + ctx_copies(slot):
            c.wait()


def _fourier_layer(x_lat, x_ctx, mods, w_out, b_out, lng, lnb, router):
    n_lat, d = x_lat.shape
    n_ctx = x_ctx.shape[0]
    n2 = LANE
    n1 = n_lat // n2
    cw = d // FN_GROUPS
    x3 = x_lat.reshape(n1, n2, d)
    any_spec = pl.BlockSpec(memory_space=pl.ANY)
    col_scratch = [pltpu.VMEM((2, n1, d), F32), pltpu.SemaphoreType.DMA((2,))]
    cmat = _dft_mats(cw)
    w_bf = w_out.astype(BF16)
    b2 = b_out.reshape(1, d)
    lng2, lnb2 = lng.reshape(1, d), lnb.reshape(1, d)

    k1 = jnp.arange(n1, dtype=I32)
    pos = jnp.arange(n1, dtype=I32)[None, None, :] * n2 + jnp.arange(n2, dtype=I32)[:, None, None]
    ang = ((k1[None, :, None] * pos) % n_lat).astype(F32) * (2.0 * math.pi / n_lat)
    tc, ts = jnp.cos(ang), jnp.sin(ang)

    mat = _full((cw, cw))
    ur, ui = pl.pallas_call(
        _fourier1_kernel,
        grid=(n2,),
        in_specs=[
            any_spec,
            pl.BlockSpec((1, N_MOD, d), lambda j: (0, 0, 0)),
            mat, mat, mat, mat,
            pl.BlockSpec((1, n1, n1), lambda j: (j, 0, 0)),
            pl.BlockSpec((1, n1, n1), lambda j: (j, 0, 0)),
        ],
        out_specs=(pl.BlockSpec((n1, d), lambda j: (0, j)), pl.BlockSpec((n1, d), lambda j: (0, j))),
        out_shape=(jax.ShapeDtypeStruct((n1, n2 * d), F32), jax.ShapeDtypeStruct((n1, n2 * d), F32)),
        scratch_shapes=col_scratch,
        compiler_params=_params("arbitrary"),
        name="fourier1",
    )(x3, mods, *cmat, tc, ts)

    assert n_ctx % n2 == 0
    t_all = n_lat + n_ctx
    blocks = t_all // n2
    m2 = _dft_mats(n2)
    cn = _dft_mats(n_ctx)
    mat2 = _full((n2, n2))
    matn = _full((n_ctx, n_ctx))
    lat_step = lambda k: jnp.minimum(k, n1 - 1)
    out_shapes = (
        jax.ShapeDtypeStruct((blocks, n2, d), F32),
        jax.ShapeDtypeStruct((blocks, n2, d), F32),
        jax.ShapeDtypeStruct((blocks, n2, LANE), I32),
        jax.ShapeDtypeStruct((blocks, n2, LANE), F32),
        _COUNT_SHAPE,
    )
    outs = pl.pallas_call(
        functools.partial(_fourier2_kernel, n1=n1, norm_lat=1.0 / math.sqrt(n_lat * cw),
                          norm_ctx=1.0 / math.sqrt(n_ctx * cw)),
        grid=(n1 + 1,),
        in_specs=[
            pl.BlockSpec((1, n2, d), lambda k: (lat_step(k), 0, 0)),
            pl.BlockSpec((1, n2, d), lambda k: (lat_step(k), 0, 0)),
            mat2, mat2, mat2, mat2,
            _full((d, d)), _full((1, d)),
            any_spec,
            _full((n_ctx, d)),
            _full((2, N_MOD, d)),
            mat, mat, mat, mat, matn, matn, matn, matn,
            _full((1, d)), _full((1, d)),
            _full((d, LANE)), _full((d, LANE)), _full((1, LANE)),
        ],
        out_specs=(any_spec, any_spec, any_spec, any_spec, _COUNT_SPEC),
        out_shape=out_shapes,
        scratch_shapes=[pltpu.VMEM((2, n_ctx, d), F32), pltpu.VMEM((2, n_ctx, d), F32),
                        pltpu.VMEM((2, n_ctx, LANE), I32), pltpu.VMEM((2, n_ctx, LANE), F32),
                        pltpu.SemaphoreType.DMA((2,))] + col_scratch,
        compiler_params=_params("arbitrary"),
        name="fourier2",
    )(ur.reshape(n1, n2, d), ui.reshape(n1, n2, d), *m2, w_bf, b2, x3, x_ctx, mods, *cmat, *cn, lng2, lnb2,
      *router)
    return (outs[0].reshape(t_all, d), outs[1].reshape(t_all, d), outs[2].reshape(t_all, LANE),
            outs[3].reshape(t_all, LANE), outs[4])


def _qkv_kernel(pos_ref, *refs, n_qk, rms):
    mod_ref, w_ref, b_ref, cos_ref, sin_ref, *rest = refs[N_COMBINE_REFS:]
    if rms:
        gain_ref, ind_ref, indt_ref, x_ref, q_ref, k_ref, v_ref, ybuf, sem = rest
    else:
        x_ref, q_ref, k_ref, v_ref, ybuf, sem = rest
    x = _combine_tile(pos_ref, *refs[:N_COMBINE_REFS], ybuf, sem)
    x_ref[...] = x
    mod = mod_ref[0]
    h = x * (1.0 + mod[1:2]) + mod[0:1]
    y = _dot(h.astype(BF16), w_ref[...]) + b_ref[...]
    qk = y[:, :n_qk]
    if rms:
        sh, sl = _split(qk * qk)
        ms = _dot(sh, ind_ref[...]) + _dot(sl, ind_ref[...])
        mh, ml = _split(ms)
        msb = _dot(mh, indt_ref[...]) + _dot(ml, indt_ref[...])
        qk = qk * lax.rsqrt(msb + RMS_EPS) * gain_ref[...]
    cos = cos_ref[...]
    sin = sin_ref[...]
    even = (lax.broadcasted_iota(I32, cos.shape, 1) & 1) == 0
    parts = []
    for c in range(n_qk // LANE):
        z = qk[:, c * LANE:(c + 1) * LANE]
        swapped = jnp.where(even, pltpu.roll(z, LANE - 1, 1), pltpu.roll(z, 1, 1))
        parts.append(z * cos + swapped * sin)
    nq = q_ref.shape[1]
    q_ref[...] = (jnp.concatenate(parts[:nq // LANE], axis=1) * Q_SCALE).astype(BF16)
    k_ref[...] = jnp.concatenate(parts[nq // LANE:], axis=1).astype(BF16)
    v_ref[...] = y[:, n_qk:].astype(BF16)


def _qkv(pending, mods, w_qkv, b_qkv, cos_t, sin_t, n_q, n_kv, n_lat, q_norm=None, k_norm=None):
    t_all, d = pending[3].shape
    nq, nk = n_q * HEAD_DIM, n_kv * HEAD_DIM
    n_qk, n_all = nq + nk, nq + 2 * nk
    rms = q_norm is not None
    pos, args, in_specs, scratch = _combine_operands(pending, TM, n_lat)
    row = lambda w: pl.BlockSpec((TM, w), lambda t, *_: (t, 0))
    in_specs += [_tile_specs(n_lat // TM)[1], _full((d, n_all)), _full((1, n_all)), row(LANE), row(LANE)]
    args += [mods, w_qkv.astype(BF16), b_qkv.reshape(1, n_all), cos_t, sin_t]
    if rms:
        gain = jnp.concatenate([jnp.tile(q_norm, n_q), jnp.tile(k_norm, n_kv)]).reshape(1, n_qk)
        head = np.arange(n_qk) // HEAD_DIM
        ind = np.zeros((n_qk, LANE), np.float32)
        ind[np.arange(n_qk), head] = 1.0 / HEAD_DIM
        indt = np.zeros((LANE, n_qk), np.float32)
        indt[head, np.arange(n_qk)] = 1.0
        in_specs += [_full((1, n_qk)), _full((n_qk, LANE)), _full((LANE, n_qk))]
        args += [gain, jnp.asarray(ind, BF16), jnp.asarray(indt, BF16)]
    grid_spec = pltpu.PrefetchScalarGridSpec(
        num_scalar_prefetch=1,
        grid=(t_all // TM,),
        in_specs=in_specs,
        out_specs=(row(d), row(nq), row(nk), row(nk)),
        scratch_shapes=scratch,
    )
    return pl.pallas_call(
        functools.partial(_qkv_kernel, n_qk=n_qk, rms=rms),
        grid_spec=grid_spec,
        out_shape=(jax.ShapeDtypeStruct((t_all, d), F32), jax.ShapeDtypeStruct((t_all, nq), BF16),
                   jax.ShapeDtypeStruct((t_all, nk), BF16), jax.ShapeDtypeStruct((t_all, nk), BF16)),
        compiler_params=_params("arbitrary"),
        name="qkv",
    )(pos, *args)


def _rope_tables(n_lat, n_ctx):
    rows = n_lat // GRID_W
    row = jnp.repeat(jnp.arange(rows, dtype=F32), GRID_W)
    col = jnp.tile(jnp.arange(GRID_W, dtype=F32), rows)
    n_freq = HEAD_DIM // 4
    inv = ROPE_THETA ** (-jnp.arange(n_freq, dtype=F32) / n_freq)
    ang = jnp.concatenate([row[:, None] * inv, col[:, None] * inv], axis=-1)
    ang = jnp.concatenate([ang, jnp.zeros((n_ctx, HEAD_DIM // 2), F32)], axis=0)
    cos = jnp.tile(jnp.repeat(jnp.cos(ang), 2, axis=1), (1, LANE // HEAD_DIM))
    sin = jnp.tile(jnp.repeat(jnp.sin(ang), 2, axis=1), (1, LANE // HEAD_DIM))
    sign = jnp.where(jnp.arange(LANE) % 2 == 0, -1.0, 1.0).astype(F32)
    return cos, sin * sign


def _kv_layouts(k, v, n_kv):
    t_all = k.shape[0]
    kh = k.reshape(t_all, n_kv, HEAD_DIM).transpose(1, 0, 2)
    vt = v.reshape(t_all, n_kv, HEAD_DIM).transpose(1, 2, 0)
    pad = jnp.zeros((n_kv, LANE - HEAD_DIM, t_all), BF16).at[:, 0, :].set(1.0)
    return kh, jnp.concatenate([vt, pad], axis=1)


def _flash_kernel(q_ref, k_ref, vt_ref, o_ref, s_even, s_odd, *, grp, tk, n_lat, n_ctx):
    tq = q_ref.shape[0]
    q_t = q_ref[...].astype(F32).T
    qt = jnp.concatenate([q_t[g * HEAD_DIM:(g + 1) * HEAD_DIM] for g in range(grp)], axis=1).astype(BF16)
    cols = qt.shape[1]
    is_ctx = pl.program_id(1) == n_lat // FLASH_TQ
    n_chunks = (n_lat + n_ctx) // tk

    def finish(acc):
        o = acc[:HEAD_DIM] / acc[HEAD_DIM:HEAD_DIM + 1]
        o_t = jnp.concatenate([o[:, g * tq:(g + 1) * tq] for g in range(grp)], axis=0)
        o_ref[...] = o_t.T.astype(BF16)

    groups = [slice(c * MXU_WIDTH, (c + 1) * MXU_WIDTH) for c in range(cols // MXU_WIDTH)]

    def scores(j, buf, g):
        off = pl.multiple_of(j * tk, tk)
        s = _dot(k_ref[0, pl.ds(off, tk), :], qt[:, g])
        buf[:, g] = s
        return jnp.max(s, axis=0, keepdims=True)

    def accumulate(j, buf, g, m, acc, mc):
        off = pl.multiple_of(j * tk, tk)
        m_new = jnp.maximum(m, mc)
        p = jnp.exp2(buf[:, g] - m_new)
        acc = jnp.exp2(m - m_new) * acc + _dot(vt_ref[0, :, pl.ds(off, tk)], p.astype(BF16))
        return m_new, acc

    @pl.when(jnp.logical_not(is_ctx))
    def _():
        def step(j, cur, nxt, carry):
            out = []
            for g, (m, acc, mc) in zip(groups, carry):
                mc_next = scores(j + 1, nxt, g)
                out.append(accumulate(j, cur, g, m, acc, mc) + (mc_next,))
            return out

        def body(jj, carry):
            for pair in range(FLASH_PAIRS):
                j = 2 * (FLASH_PAIRS * jj + pair)
                carry = step(j, s_even, s_odd, carry)
                carry = step(j + 1, s_odd, s_even, carry)
            return carry

        init = [(jnp.full((1, MXU_WIDTH), NEG, F32), jnp.zeros((LANE, MXU_WIDTH), F32), scores(0, s_even, g))
                for g in groups]
        carry = lax.fori_loop(0, (n_chunks - 1) // (2 * FLASH_PAIRS), body, init)
        finish(jnp.concatenate([accumulate(n_chunks - 1, s_even, g, m, acc, mc)[1]
                                for g, (m, acc, mc) in zip(groups, carry)], axis=1))

    @pl.when(is_ctx)
    def _():
        s = _dot(k_ref[0, n_lat:n_lat + n_ctx, :], qt)
        p = jnp.exp2(s - jnp.max(s, axis=0, keepdims=True))
        finish(_dot(vt_ref[0, :, n_lat:n_lat + n_ctx], p.astype(BF16)))


def _full_attention(q, k, v, n_lat, n_ctx):
    t_all, nq = q.shape
    n_kv = k.shape[1] // HEAD_DIM
    grp = nq // HEAD_DIM // n_kv
    nt = t_all // FLASH_TQ
    cols = grp * FLASH_TQ
    assert n_ctx == FLASH_TQ and t_all % FLASH_TK == 0 and (t_all // FLASH_TK - 1) % (2 * FLASH_PAIRS) == 0
    kh, vt = _kv_layouts(k, v, n_kv)
    return pl.pallas_call(
        functools.partial(_flash_kernel, grp=grp, tk=FLASH_TK, n_lat=n_lat, n_ctx=n_ctx),
        grid=(n_kv, nt),
        in_specs=[
            pl.BlockSpec((FLASH_TQ, grp * HEAD_DIM), lambda h, i: (i, h)),
            pl.BlockSpec((1, t_all, HEAD_DIM), lambda h, i: (h, 0, 0)),
            pl.BlockSpec((1, LANE, t_all), lambda h, i: (h, 0, 0)),
        ],
        out_specs=pl.BlockSpec((FLASH_TQ, grp * HEAD_DIM), lambda h, i: (i, h)),
        out_shape=jax.ShapeDtypeStruct((t_all, nq), BF16),
        scratch_shapes=[pltpu.VMEM((FLASH_TK, cols), F32), pltpu.VMEM((FLASH_TK, cols), F32)],
        compiler_params=_params("arbitrary", "arbitrary"),
        name="flash",
    )(q, kh, vt)


def _window_kernel(q_ref, kp_ref, kc_ref, kn_ref, kx_ref, vp_ref, vc_ref, vn_ref, vx_ref, sink_ref, o_ref,
                   *, grp, nb):
    i = pl.program_id(0)
    gw = grp * HEAD_DIM
    n_kv = kp_ref.shape[0]
    q_t = q_ref[...].astype(F32).T
    cols = grp * Q_BLOCK
    kj = lax.broadcasted_iota(I32, (Q_BLOCK, cols), 0)
    qi = lax.broadcasted_iota(I32, (Q_BLOCK, cols), 1) & (Q_BLOCK - 1)
    lo = jnp.where(i > 0, qi, Q_BLOCK)
    hi = jnp.where(i < nb - 1, qi, -1)
    outs = []
    for h in range(n_kv):
        qt = jnp.concatenate([q_t[h * gw + g * HEAD_DIM:h * gw + (g + 1) * HEAD_DIM] for g in range(grp)],
                             axis=1).astype(BF16)
        s = _dot(jnp.concatenate([kp_ref[h], kc_ref[h], kn_ref[h], kx_ref[h]], axis=0), qt)
        sp = jnp.where(kj >= lo, s[:Q_BLOCK], NEG)
        sn = jnp.where(kj <= hi, s[2 * Q_BLOCK:3 * Q_BLOCK], NEG)
        s = jnp.concatenate([sp, s[Q_BLOCK:2 * Q_BLOCK], sn, s[3 * Q_BLOCK:]], axis=0)
        sink = sink_ref[h]
        m = jnp.maximum(jnp.max(s, axis=0, keepdims=True), sink)
        p = jnp.exp2(s - m).astype(BF16)
        acc = _dot(jnp.concatenate([vp_ref[h], vc_ref[h], vn_ref[h], vx_ref[h]], axis=1), p)
        o = acc[:HEAD_DIM] / (acc[HEAD_DIM:HEAD_DIM + 1] + jnp.exp2(sink - m))
        outs += [o[:, g * Q_BLOCK:(g + 1) * Q_BLOCK] for g in range(grp)]
    o_ref[...] = jnp.concatenate(outs, axis=0).T.astype(BF16)


def _window_attention(q, k, v, sink, n_lat, n_ctx):
    nq = q.shape[1]
    n_kv = k.shape[1] // HEAD_DIM
    grp = nq // HEAD_DIM // n_kv
    nb = n_lat // Q_BLOCK
    cb = n_lat // n_ctx
    assert WINDOW == Q_BLOCK
    kh, vt = _kv_layouts(k, v, n_kv)
    sink_cols = jnp.repeat(sink.reshape(n_kv, 1, grp) * LOG2E, Q_BLOCK, axis=2)
    kspec = lambda f: pl.BlockSpec((n_kv, Q_BLOCK, HEAD_DIM), lambda i: (0, f(i), 0))
    vspec = lambda f: pl.BlockSpec((n_kv, LANE, Q_BLOCK), lambda i: (0, 0, f(i)))
    prev = lambda i: jnp.maximum(i - 1, 0)
    cur = lambda i: i
    nxt = lambda i: jnp.minimum(i + 1, nb - 1)
    return pl.pallas_call(
        functools.partial(_window_kernel, grp=grp, nb=nb),
        grid=(nb,),
        in_specs=[
            pl.BlockSpec((Q_BLOCK, nq), lambda i: (i, 0)),
            kspec(prev), kspec(cur), kspec(nxt),
            pl.BlockSpec((n_kv, n_ctx, HEAD_DIM), lambda i: (0, cb, 0)),
            vspec(prev), vspec(cur), vspec(nxt),
            pl.BlockSpec((n_kv, LANE, n_ctx), lambda i: (0, 0, cb)),
            _full((n_kv, 1, grp * Q_BLOCK)),
        ],
        out_specs=pl.BlockSpec((Q_BLOCK, nq), lambda i: (i, 0)),
        out_shape=jax.ShapeDtypeStruct((n_lat, nq), BF16),
        compiler_params=_params("arbitrary"),
        name="window",
    )(q, kh, kh, kh, kh, vt, vt, vt, vt, sink_cols)


def _gmlp_kernel(pos_ref, *refs):
    (mod_ref, win_ref, bin_ref, vg_ref, vb_ref, ws_ref, bs_ref, wout_ref, bout_ref,
     lng_ref, lnb_ref, rwh_ref, rwl_ref, rb_ref, *rest) = refs[N_COMBINE_REFS:]
    out_refs, (ybuf, sem) = rest[:-2], rest[-2:]
    mod = mod_ref[0]
    x = _combine_tile(pos_ref, *refs[:N_COMBINE_REFS], ybuf, sem)
    h = x * (1.0 + mod[1:2]) + mod[0:1]
    z = _dot(h.astype(BF16), win_ref[...]) + bin_ref[...]
    z = 0.5 * z * (1.0 + lax.erf(z * (2.0 ** -0.5)))
    half = z.shape[1] // 2
    u = z[:, :half]
    v = _layer_norm(z[:, half:], vg_ref[...], vb_ref[...]).astype(BF16)
    cw = half // GM_GROUPS
    chunks = []
    for c in range(x.shape[0] // GM_CHUNK):
        vc = v[c * GM_CHUNK:(c + 1) * GM_CHUNK]
        chunks.append(jnp.concatenate(
            [_dot(ws_ref[g], vc[:, g * cw:(g + 1) * cw]) + bs_ref[g] for g in range(GM_GROUPS)], axis=1))
    gated = u * jnp.concatenate(chunks, axis=0)
    y = _dot(gated.astype(BF16), wout_ref[...]) + bout_ref[...]
    _post(y, x, mod, lng_ref[...], lnb_ref[...], rwh_ref[...], rwl_ref[...], rb_ref[...], out_refs)


def _gmlp_layer(pending, mods, w_in, b_in, vg, vb, w_s, b_s, w_out, b_out, lng, lnb, router, n_lat):
    t_all, d = pending[3].shape
    dffn = w_in.shape[1]
    half = dffn // 2
    cw = half // GM_GROUPS
    shapes, specs = _post_out(t_all)
    bs_full = jnp.broadcast_to(b_s[:, :, None], (GM_GROUPS, GM_CHUNK, cw))
    pos, args, in_specs, scratch = _combine_operands(pending, TM, n_lat)
    in_specs += [
        _tile_specs(n_lat // TM)[1], _full((d, dffn)), _full((1, dffn)), _full((1, half)), _full((1, half)),
        _full((GM_GROUPS, GM_CHUNK, GM_CHUNK)), _full((GM_GROUPS, GM_CHUNK, cw)),
        _full((half, d)), _full((1, d)), _full((1, d)), _full((1, d)),
        _full((d, LANE)), _full((d, LANE)), _full((1, LANE)),
    ]
    args += [mods, w_in.astype(BF16), b_in.reshape(1, dffn), vg.reshape(1, half), vb.reshape(1, half),
             w_s.astype(BF16), bs_full, w_out.astype(BF16), b_out.reshape(1, d), lng.reshape(1, d),
             lnb.reshape(1, d), *router]
    grid_spec = pltpu.PrefetchScalarGridSpec(
        num_scalar_prefetch=1,
        grid=(t_all // TM,),
        in_specs=in_specs,
        out_specs=specs,
        scratch_shapes=scratch,
    )
    return pl.pallas_call(
        _gmlp_kernel,
        grid_spec=grid_spec,
        out_shape=shapes,
        compiler_params=_params("arbitrary"),
        name="gmlp",
    )(pos, *args)


COMB_TM = 128


def _route_meta(count_blk, n_tok):
    n_assign = n_tok * TOP_K
    n_blocks = -(-n_assign // MOE_BLOCK) + N_EXPERTS
    ids = jnp.arange(N_EXPERTS, dtype=I32)
    counts = count_blk[0, :N_EXPERTS].astype(I32)
    padded = (counts + MOE_BLOCK - 1) // MOE_BLOCK * MOE_BLOCK
    ends_pad = jnp.cumsum(padded)
    base = ends_pad - padded
    starts = jnp.arange(n_blocks, dtype=I32) * MOE_BLOCK
    block_expert = jnp.minimum(jnp.sum((ends_pad[None, :] <= starts[:, None]).astype(I32), axis=1), N_EXPERTS - 1)
    last_blk = jnp.where(padded > 0, ends_pad - MOE_BLOCK, -1)
    tail = ends_pad[-1] + ids * MOE_BLOCK
    tail = jnp.where(tail < n_blocks * MOE_BLOCK, tail, -1)
    base_b = jnp.broadcast_to(base.astype(F32)[:, None], (N_EXPERTS, LANE))
    return block_expert.astype(I32), base_b, jnp.concatenate([last_blk, tail]).astype(I32), n_blocks


def _rank_kernel(idx_ref, base_ref, upper_ref, dest_ref, run_ref):
    @pl.when(pl.program_id(0) == 0)
    def _():
        run_ref[...] = jnp.zeros(run_ref.shape, F32)

    eid = lax.broadcasted_iota(I32, (N_EXPERTS, TM), 0)
    onehots = [(eid == idx_ref[k:k + 1, :]).astype(F32) for k in range(TOP_K)]
    cnt = onehots[0] + onehots[1] + onehots[2] + onehots[3]
    before = _dot(cnt.astype(BF16), upper_ref[...])
    slot = base_ref[:, :1] + run_ref[:, :1] + before
    for k in range(TOP_K):
        rows = (jnp.sum(onehots[k] * slot, axis=0, keepdims=True) * ROW_TILE).astype(I32)
        for j in range(TM // COMB_TM):
            dest_ref[j, k:k + 1, :] = rows[:, j * COMB_TM:(j + 1) * COMB_TM]
    run_ref[...] = run_ref[...] + jnp.sum(cnt, axis=1, keepdims=True)


def _rank(top_idx, base_b, n_tok):
    per_step = TM // COMB_TM
    idx_t = top_idx[:n_tok, :TOP_K].T
    upper = jnp.asarray(np.triu(np.ones((TM, TM), np.float32), 1), BF16)
    return pl.pallas_call(
        _rank_kernel,
        grid=(n_tok // TM,),
        in_specs=[pl.BlockSpec((TOP_K, TM), lambda t: (0, t)), _full((N_EXPERTS, LANE)), _full((TM, TM))],
        out_specs=pl.BlockSpec((per_step, TOP_K, COMB_TM), lambda t: (t, 0, 0)),
        out_shape=jax.ShapeDtypeStruct((n_tok // COMB_TM, TOP_K, COMB_TM), I32),
        scratch_shapes=[pltpu.VMEM((N_EXPERTS, LANE), F32)],
        compiler_params=_params("arbitrary"),
        name="rank",
    )(idx_t, base_b, upper)


def _dispatch_kernel(dest_ref, zrow_ref, h_ref, xs_hbm, hbuf, zbuf, sem, zsem, *, n_tiles):
    t = pl.program_id(0)
    slot = t % 2
    n_zero = 2 * N_EXPERTS

    def zero_copy(j):
        row = pl.multiple_of(zrow_ref[j] * ROW_TILE, MOE_BLOCK * ROW_TILE)
        return pltpu.make_async_copy(zbuf, xs_hbm.at[pl.ds(row, MOE_BLOCK * ROW_TILE), :], zsem.at[0])

    def wait_rows(s):
        for _ in range(TOP_K):
            pltpu.make_async_copy(hbuf.at[s], xs_hbm.at[pl.ds(0, COMB_TM * ROW_TILE), :], sem.at[s]).wait()

    @pl.when(t == 0)
    def _():
        zbuf[...] = jnp.zeros(zbuf.shape, F32)
        for j in range(n_zero):
            @pl.when(zrow_ref[j] >= 0)
            def _():
                zero_copy(j).start()
        for j in range(n_zero):
            @pl.when(zrow_ref[j] >= 0)
            def _():
                zero_copy(j).wait()

    @pl.when(t >= 2)
    def _():
        wait_rows(slot)

    _to_row_tiles(hbuf.at[slot], h_ref[...])
    for k in range(TOP_K):
        for r in range(COMB_TM):
            row = pl.multiple_of(dest_ref[(t * TOP_K + k) * COMB_TM + r], ROW_TILE)
            pltpu.make_async_copy(hbuf.at[slot, pl.ds(r * ROW_TILE, ROW_TILE), :],
                                  xs_hbm.at[pl.ds(row, ROW_TILE), :], sem.at[slot]).start(priority=r % 2)

    @pl.when(t == n_tiles - 1)
    def _():
        wait_rows(slot)
        if n_tiles > 1:
            wait_rows(1 - slot)


def _dispatch(h2, dest, zrows, n_tok, n_blocks):
    d = h2.shape[1]
    n_tiles = n_tok // COMB_TM
    grid_spec = pltpu.PrefetchScalarGridSpec(
        num_scalar_prefetch=2,
        grid=(n_tiles,),
        in_specs=[pl.BlockSpec((COMB_TM, d), lambda t, dest, zr: (t, 0))],
        out_specs=pl.BlockSpec(memory_space=pl.ANY),
        scratch_shapes=[pltpu.VMEM((2, COMB_TM * ROW_TILE, LANE), F32),
                        pltpu.VMEM((MOE_BLOCK * ROW_TILE, LANE), F32),
                        pltpu.SemaphoreType.DMA((2,)), pltpu.SemaphoreType.DMA((1,))],
    )
    return pl.pallas_call(
        functools.partial(_dispatch_kernel, n_tiles=n_tiles),
        grid_spec=grid_spec,
        out_shape=jax.ShapeDtypeStruct((n_blocks * MOE_BLOCK * ROW_TILE, LANE), F32),
        compiler_params=_params("arbitrary"),
        name="dispatch",
    )(dest.reshape(-1), zrows, h2)


EXP_CHUNK = MXU_WIDTH


def _expert_kernel(be_ref, x_ref, wgu_ref, bgu_ref, wd_ref, bd_ref, sel_ref, y_ref,
                   wg_s, wl_s, wd_s, bg_s, bl_s):
    b = pl.program_id(0)
    ff = wg_s.shape[1]
    half = EXP_CHUNK // 2

    @pl.when(jnp.logical_or(b == 0, be_ref[b] != be_ref[jnp.maximum(b - 1, 0)]))
    def _():
        for c in range(2 * ff // EXP_CHUNK):
            w = _dot(wgu_ref[0, 0, :, c * EXP_CHUNK:(c + 1) * EXP_CHUNK].astype(BF16), sel_ref[...])
            wg_s[:, c * half:(c + 1) * half] = w[:, :half].astype(BF16)
            wl_s[:, c * half:(c + 1) * half] = w[:, half:].astype(BF16)
            bh, bl = _split(jnp.broadcast_to(bgu_ref[0, 0, :, c * EXP_CHUNK:(c + 1) * EXP_CHUNK],
                                             (SUBLANE, EXP_CHUNK)))
            bias = _dot(bh, sel_ref[...]) + _dot(bl, sel_ref[...])
            bg_s[:, c * half:(c + 1) * half] = bias[:, :half]
            bl_s[:, c * half:(c + 1) * half] = bias[:, half:]
        wd_s[...] = wd_ref[0, 0].astype(BF16)

    x = _from_row_tiles(x_ref, 0, MOE_BLOCK).astype(BF16)
    glu = jnp.minimum(_dot(x, wg_s[...]) + bg_s[0:1, :], SWIGLU_LIMIT)
    lin = jnp.clip(_dot(x, wl_s[...]) + bl_s[0:1, :], -SWIGLU_LIMIT, SWIGLU_LIMIT)
    act = glu * (1.0 / (1.0 + jnp.exp(-SWIGLU_ALPHA * glu))) * (lin + 1.0)
    _to_row_tiles(y_ref, _dot(act.astype(BF16), wd_s[...]) + bd_ref[0, 0])


def _experts(xs, block_expert, n_blocks, w_gate_up, b_gate_up, w_down, b_down, layer):
    d = w_down.shape[3]
    ff = w_down.shape[2]
    half = EXP_CHUNK // 2
    sel = np.zeros((EXP_CHUNK, EXP_CHUNK), np.float32)
    sel[2 * np.arange(half), np.arange(half)] = 1.0
    sel[2 * np.arange(half) + 1, half + np.arange(half)] = 1.0
    wspec = lambda r, c: pl.BlockSpec((1, 1, r, c), lambda b, be: (layer, be[b], 0, 0))
    sspec = pl.BlockSpec((EXP_CHUNK, EXP_CHUNK), lambda b, be: (0, 0))
    grid_spec = pltpu.PrefetchScalarGridSpec(
        num_scalar_prefetch=1,
        grid=(n_blocks,),
        in_specs=[pl.BlockSpec((MOE_BLOCK * ROW_TILE, LANE), lambda b, be: (b, 0)), wspec(d, 2 * ff),
                  wspec(1, 2 * ff), wspec(ff, d), wspec(1, d), sspec],
        out_specs=pl.BlockSpec((MOE_BLOCK * ROW_TILE, LANE), lambda b, be: (b, 0)),
        scratch_shapes=[pltpu.VMEM((d, ff), BF16), pltpu.VMEM((d, ff), BF16), pltpu.VMEM((ff, d), BF16),
                        pltpu.VMEM((SUBLANE, ff), F32), pltpu.VMEM((SUBLANE, ff), F32)],
    )
    return pl.pallas_call(
        _expert_kernel,
        grid_spec=grid_spec,
        out_shape=jax.ShapeDtypeStruct((n_blocks * MOE_BLOCK * ROW_TILE, LANE), F32),
        compiler_params=_params("arbitrary"),
        name="experts",
    )(block_expert, xs, w_gate_up, b_gate_up[:, :, None, :], w_down, b_down[:, :, None, :],
      jnp.asarray(sel, BF16))


def _gather_rows(idx_ref, base, n, src_hbm, dst, sem):
    for r in range(n):
        row = pl.multiple_of(idx_ref[base + r], ROW_TILE)
        pltpu.make_async_copy(src_hbm.at[pl.ds(row, ROW_TILE), :], dst.at[pl.ds(r * ROW_TILE, ROW_TILE), :],
                              sem).start(priority=r % 2)


N_COMBINE_REFS = 6


def _combine_tile(pos_ref, y_hbm, gate_ref, x1_ref, mod_ref, lng_ref, lnb_ref, ybuf, sem):
    t = pl.program_id(0)
    slot = t % 2
    tm, d = x1_ref.shape
    n = tm * TOP_K

    @pl.when(t == 0)
    def _():
        _gather_rows(pos_ref, 0, n, y_hbm, ybuf.at[0], sem.at[0])

    @pl.when(t + 1 < pl.num_programs(0))
    def _():
        _gather_rows(pos_ref, (t + 1) * n, n, y_hbm, ybuf.at[1 - slot], sem.at[1 - slot])

    pltpu.make_async_copy(y_hbm.at[pl.ds(0, n * ROW_TILE), :], ybuf.at[slot], sem.at[slot]).wait()
    gates = gate_ref[...]
    parts = []
    for j in range(tm // COMB_TM):
        f = jnp.zeros((COMB_TM, d), F32)
        for k in range(TOP_K):
            rows = _from_row_tiles(ybuf.at[slot], (j * TOP_K + k) * COMB_TM, COMB_TM)
            f = f + gates[j * COMB_TM:(j + 1) * COMB_TM, k:k + 1] * rows
        parts.append(f)
    f = parts[0] if len(parts) == 1 else jnp.concatenate(parts, axis=0)
    return _layer_norm(ALPHA * x1_ref[...] + mod_ref[0][5:6] * f, lng_ref[...], lnb_ref[...])


def _combine_operands(pending, tm, n_lat):
    yb, dest, gates, x1, mods, lng, lnb = pending
    d = x1.shape[1]
    n_lat_tiles = n_lat // tm
    specs = [
        pl.BlockSpec(memory_space=pl.ANY),
        pl.BlockSpec((tm, LANE), lambda t, *_: (t, 0)),
        pl.BlockSpec((tm, d), lambda t, *_: (t, 0)),
        pl.BlockSpec((1, N_MOD, d), lambda t, *_: (jnp.where(t >= n_lat_tiles, 1, 0), 0, 0)),
        _full((1, d)), _full((1, d)),
    ]
    scratch = [pltpu.VMEM((2, tm * TOP_K * ROW_TILE, LANE), F32), pltpu.SemaphoreType.DMA((2,))]
    return dest.reshape(-1), [yb, gates, x1, mods, lng.reshape(1, d), lnb.reshape(1, d)], specs, scratch


def _combine_kernel(pos_ref, *refs):
    o_ref, ybuf, sem = refs[N_COMBINE_REFS:]
    o_ref[...] = _combine_tile(pos_ref, *refs[:N_COMBINE_REFS], ybuf, sem)


def _combine(pending, n_tok, n_lat):
    d = pending[3].shape[1]
    pos, args, specs, scratch = _combine_operands(pending, COMB_TM, n_lat)
    grid_spec = pltpu.PrefetchScalarGridSpec(
        num_scalar_prefetch=1,
        grid=(n_tok // COMB_TM,),
        in_specs=specs,
        out_specs=pl.BlockSpec((COMB_TM, d), lambda t, *_: (t, 0)),
        scratch_shapes=scratch,
    )
    return pl.pallas_call(
        _combine_kernel,
        grid_spec=grid_spec,
        out_shape=jax.ShapeDtypeStruct((n_tok, d), F32),
        compiler_params=_params("arbitrary"),
        name="combine",
    )(pos, *args)


def _moe_layer(x1, h2, top_idx, gates, counts, mods, lng, lnb, w_gate_up, b_gate_up, w_down, b_down, layer, n_tok):
    block_expert, base_b, zrows, n_blocks = _route_meta(counts, n_tok)
    dest = _rank(top_idx, base_b, n_tok)
    xs = _dispatch(h2, dest, zrows, n_tok, n_blocks)
    yb = _experts(xs, block_expert, n_blocks, w_gate_up, b_gate_up, w_down, b_down, layer)
    return yb, dest, gates, x1, mods, lng, lnb


def kernel(x, c, ctx, c_ctx, ada_w, ada_b, ln_mix_g, ln_mix_b, ln_ffn_g, ln_ffn_b, fn_w_out, fn_b_out, fa_w_qkv, fa_b_qkv, fa_q_norm, fa_k_norm, fa_w_out, fa_b_out, gm_w_in, gm_b_in, gm_v_norm_g, gm_v_norm_b, gm_w_s, gm_b_s, gm_w_out, gm_b_out, wa_w_qkv, wa_b_qkv, wa_sink, wa_w_out, wa_b_out, router_w, router_b, exp_w_gate_up, exp_b_gate_up, exp_w_down, exp_b_down):
    bsz, n_lat, d = x.shape
    n_ctx = ctx.shape[1]
    assert bsz == 1 and d == D_MODEL and n_lat == LANE * LANE and n_lat % n_ctx == 0 and n_ctx % TM == 0
    t_all = n_lat + n_ctx
    n_lat_tiles = n_lat // TM
    mods_all = _ada(c, c_ctx, ada_w, ada_b)
    cos_t, sin_t = _rope_tables(n_lat, n_ctx)

    for i in range(DEPTH):
        kind, j = i % 4, i // 4
        last = i == DEPTH - 1
        n_tok = n_lat if last else t_all
        mods = mods_all[i]
        router = _router_operands(router_w[i], router_b[i])
        lng, lnb = ln_mix_g[i], ln_mix_b[i]
        if kind == 0:
            post = _fourier_layer(x[0], ctx[0], mods, fn_w_out[j], fn_b_out[j], lng, lnb, router)
        elif kind == 1:
            x_all, q, k, v = _qkv(pending, mods, fa_w_qkv[j], fa_b_qkv[j], cos_t, sin_t, FA_Q_HEADS,
                                  FA_KV_HEADS, n_lat, fa_q_norm[j], fa_k_norm[j])
            o = _full_attention(q, k, v, n_lat, n_ctx)
            post = _proj_post(o, fa_w_out[j], fa_b_out[j], x_all, mods, lng, lnb, router, n_tok, n_lat_tiles)
        elif kind == 2:
            post = _gmlp_layer(pending, mods, gm_w_in[j], gm_b_in[j], gm_v_norm_g[j], gm_v_norm_b[j],
                               gm_w_s[j], gm_b_s[j], gm_w_out[j], gm_b_out[j], lng, lnb, router, n_lat)
        else:
            x_all, q, k, v = _qkv(pending, mods, wa_w_qkv[j], wa_b_qkv[j], cos_t, sin_t, WA_Q_HEADS,
                                  WA_KV_HEADS, n_lat)
            o = _window_attention(q, k, v, wa_sink[j], n_lat, n_ctx)
            post = _proj_post(o, wa_w_out[j], wa_b_out[j], x_all, mods, lng, lnb, router, n_tok, n_lat_tiles)
        pending = _moe_layer(*post, mods, ln_ffn_g[i], ln_ffn_b[i], exp_w_gate_up, exp_b_gate_up, exp_w_down,
                             exp_b_down, i, n_tok)
    return _combine(pending, n_lat, n_lat)[None]
```

```python
import functools
import math

import numpy as np
import jax
import jax.numpy as jnp
from jax import lax
from jax.experimental import pallas as pl
from jax.experimental.pallas import tpu as pltpu

F32, BF16, I32 = jnp.float32, jnp.bfloat16, jnp.int32

D_MODEL = 1024
DEPTH = 4
GRID_W = 64
N_MOD = 6
FN_GROUPS = 4
HEAD_DIM = 64
FA_Q_HEADS, FA_KV_HEADS = 16, 4
WA_Q_HEADS, WA_KV_HEADS = 16, 2
WINDOW = 128
Q_BLOCK = 128
ROPE_THETA = 10000.0
GM_CHUNK = 128
GM_GROUPS = 8
N_EXPERTS = 32
TOP_K = 4
SWIGLU_LIMIT = 7.0
SWIGLU_ALPHA = 1.702
MOE_BLOCK = 256
LN_EPS = 1e-5
RMS_EPS = 1e-6
NEG = -1e30
ALPHA = (2 * DEPTH) ** 0.25
LOG2E = math.log2(math.e)
Q_SCALE = HEAD_DIM ** -0.5 * LOG2E

LANE = 128
SUBLANE = 8
ROW_TILE = D_MODEL // LANE
assert ROW_TILE == SUBLANE
MXU_WIDTH = 256
TM = 256
FLASH_TQ = 256
FLASH_TK = 1280
FLASH_PAIRS = 3
VMEM_LIMIT = 56 * 2 ** 20


def _params(*sem):
    return pltpu.CompilerParams(dimension_semantics=sem, vmem_limit_bytes=VMEM_LIMIT)


def _dot(a, b):
    return jnp.dot(a, b, preferred_element_type=F32)


def _split(a):
    hi = a.astype(BF16)
    lo = (a - hi.astype(F32)).astype(BF16)
    return hi, lo


def _dot3(a_hi, a_lo, b_hi, b_lo):
    return _dot(a_hi, b_hi) + (_dot(a_hi, b_lo) + _dot(a_lo, b_hi))


def _layer_norm(x, g, b):
    mu = jnp.mean(x, axis=-1, keepdims=True)
    xc = x - mu
    var = jnp.mean(xc * xc, axis=-1, keepdims=True)
    return xc * lax.rsqrt(var + LN_EPS) * g + b


def _top4(logits):
    lane = lax.broadcasted_iota(I32, logits.shape, 1).astype(F32)
    cur = logits
    vals, idxs = [], []
    for _ in range(TOP_K):
        m = jnp.max(cur, axis=-1, keepdims=True)
        i = jnp.min(jnp.where(cur == m, lane, float(LANE)), axis=-1, keepdims=True)
        vals.append(m)
        idxs.append(i)
        cur = jnp.where(lane == i, -jnp.inf, cur)
    exps = [jnp.exp(v - vals[0]) for v in vals]
    inv = 1.0 / (exps[0] + exps[1] + exps[2] + exps[3])
    idx_out = jnp.zeros_like(logits)
    gate_out = jnp.zeros_like(logits)
    picked = jnp.zeros_like(logits)
    for k in range(TOP_K):
        idx_out = jnp.where(lane == float(k), idxs[k], idx_out)
        gate_out = jnp.where(lane == float(k), exps[k] * inv, gate_out)
        picked = picked + jnp.where(lane == idxs[k], 1.0, 0.0)
    return idx_out.astype(I32), gate_out, jnp.sum(picked, axis=0, keepdims=True)


def _post(y, x, mod, lng, lnb, rw_hi, rw_lo, rb, out_refs):
    x1_ref, h2_ref, idx_ref, gate_ref, cnt_ref = out_refs
    x1 = _layer_norm(ALPHA * x + mod[2:3] * y, lng, lnb)
    h2 = x1 * (1.0 + mod[4:5]) + mod[3:4]
    hh, hl = _split(h2)
    logits = _dot3(hh, hl, rw_hi, rw_lo) + rb
    idx, gates, cnt = _top4(logits)
    x1_ref[...] = x1
    h2_ref[...] = h2
    idx_ref[...] = idx
    gate_ref[...] = gates

    @pl.when(pl.program_id(0) == 0)
    def _():
        cnt_ref[...] = jnp.zeros(cnt_ref.shape, F32)

    cnt_ref[...] = cnt_ref[...] + cnt


def _to_row_tiles(ref, x):
    n = x.shape[0]
    for s in range(ROW_TILE):
        ref[pl.ds(s, n, stride=ROW_TILE), :] = x[:, s * LANE:(s + 1) * LANE]


def _from_row_tiles(ref, start, n):
    return jnp.concatenate([ref[pl.ds(start * ROW_TILE + s, n, stride=ROW_TILE), :] for s in range(ROW_TILE)],
                           axis=1)


def _ada_kernel(cs_ref, w_ref, b_ref, o_ref):
    cs = cs_ref[...]
    s = cs * (1.0 / (1.0 + jnp.exp(-cs)))
    sh, sl = _split(s)
    wh, wl = _split(w_ref[0])
    o_ref[0] = _dot3(sh, sl, wh, wl) + b_ref[0]


def _ada(c, c_ctx, ada_w, ada_b):
    d = c.shape[-1]
    nm = ada_w.shape[-1]
    tn = nm // 4
    cs = jnp.zeros((SUBLANE, d), F32).at[0].set(c[0]).at[1].set(c_ctx)
    out = pl.pallas_call(
        _ada_kernel,
        grid=(DEPTH, nm // tn),
        in_specs=[
            pl.BlockSpec((SUBLANE, d), lambda i, j: (0, 0)),
            pl.BlockSpec((1, d, tn), lambda i, j: (i, 0, j)),
            pl.BlockSpec((1, 1, tn), lambda i, j: (i, 0, j)),
        ],
        out_specs=pl.BlockSpec((1, SUBLANE, tn), lambda i, j: (i, 0, j)),
        out_shape=jax.ShapeDtypeStruct((DEPTH, SUBLANE, nm), F32),
        compiler_params=_params("arbitrary", "arbitrary"),
        name="ada",
    )(cs, ada_w, ada_b.reshape(DEPTH, 1, nm))
    return out[:, :2].reshape(DEPTH, 2, N_MOD, d)


def _tile_specs(n_lat_tiles):
    tok = pl.BlockSpec((TM, D_MODEL), lambda t, *_: (t, 0))
    mod = pl.BlockSpec((1, N_MOD, D_MODEL), lambda t, *_: (jnp.where(t >= n_lat_tiles, 1, 0), 0, 0))
    return tok, mod


def _full(shape):
    nd = len(shape)
    return pl.BlockSpec(shape, lambda *_: (0,) * nd)


_COUNT_SHAPE = jax.ShapeDtypeStruct((SUBLANE, LANE), F32)
_COUNT_SPEC = pl.BlockSpec((SUBLANE, LANE), lambda *_: (0, 0))


def _post_out(n_rows):
    shapes = (
        jax.ShapeDtypeStruct((n_rows, D_MODEL), F32),
        jax.ShapeDtypeStruct((n_rows, D_MODEL), F32),
        jax.ShapeDtypeStruct((n_rows, LANE), I32),
        jax.ShapeDtypeStruct((n_rows, LANE), F32),
        _COUNT_SHAPE,
    )
    specs = (
        pl.BlockSpec((TM, D_MODEL), lambda t, *_: (t, 0)),
        pl.BlockSpec((TM, D_MODEL), lambda t, *_: (t, 0)),
        pl.BlockSpec((TM, LANE), lambda t, *_: (t, 0)),
        pl.BlockSpec((TM, LANE), lambda t, *_: (t, 0)),
        _COUNT_SPEC,
    )
    return shapes, specs


def _router_operands(router_w, router_b):
    rw = jnp.zeros((D_MODEL, LANE), F32).at[:, :N_EXPERTS].set(router_w)
    rw_hi = rw.astype(BF16)
    rw_lo = (rw - rw_hi.astype(F32)).astype(BF16)
    rb = jnp.full((1, LANE), NEG, F32).at[0, :N_EXPERTS].set(router_b)
    return rw_hi, rw_lo, rb


def _proj_post_kernel(a_ref, w_ref, b_ref, x_ref, mod_ref, lng_ref, lnb_ref, rwh_ref, rwl_ref, rb_ref, *out_refs):
    y = _dot(a_ref[...], w_ref[...]) + b_ref[...]
    _post(y, x_ref[...], mod_ref[0], lng_ref[...], lnb_ref[...], rwh_ref[...], rwl_ref[...], rb_ref[...], out_refs)


def _proj_post(a, w_out, b_out, x, mods, lng, lnb, router, n_rows, n_lat_tiles):
    k = a.shape[1]
    tok, mod = _tile_specs(n_lat_tiles)
    shapes, specs = _post_out(n_rows)
    return pl.pallas_call(
        _proj_post_kernel,
        grid=(n_rows // TM,),
        in_specs=[
            pl.BlockSpec((TM, k), lambda t: (t, 0)),
            _full((k, D_MODEL)), _full((1, D_MODEL)),
            tok, mod, _full((1, D_MODEL)), _full((1, D_MODEL)),
            _full((D_MODEL, LANE)), _full((D_MODEL, LANE)), _full((1, LANE)),
        ],
        out_specs=specs,
        out_shape=shapes,
        compiler_params=_params("arbitrary"),
        name="proj_post",
    )(a, w_out.astype(BF16), b_out.reshape(1, -1), x, mods, lng.reshape(1, -1), lnb.reshape(1, -1), *router)


def _dft_mats(n):
    jk = np.outer(np.arange(n), np.arange(n)) % n
    ang = 2.0 * np.pi * jk / n
    out = []
    for m in (np.cos(ang), np.sin(ang)):
        m32 = jnp.asarray(m, F32)
        hi = m32.astype(BF16)
        out += [hi, (m32 - hi.astype(F32)).astype(BF16)]
    return out


def _channel_dft(h, cc, sc):
    cw = cc[0].shape[0]
    a_parts, b_parts = [], []
    for g in range(h.shape[1] // cw):
        hh, hl = _split(h[:, g * cw:(g + 1) * cw])
        a_parts.append(_dot3(hh, hl, cc[0][...], cc[1][...]))
        b_parts.append(_dot3(hh, hl, sc[0][...], sc[1][...]))
    return jnp.concatenate(a_parts, axis=1), jnp.concatenate(b_parts, axis=1)


def _column_rows(x_hbm, buf, sem, n_steps):
    j = pl.program_id(0)
    slot = j % 2

    def copy(step, s):
        return pltpu.make_async_copy(x_hbm.at[pl.ds(0, buf.shape[1]), step, :], buf.at[s], sem.at[s])

    @pl.when(j == 0)
    def _():
        copy(0, 0).start()

    @pl.when(j + 1 < n_steps)
    def _():
        copy(j + 1, 1 - slot).start()

    copy(j, slot).wait()
    return buf[slot]


def _fourier1_kernel(x_hbm, mod_ref, cch_ref, ccl_ref, sch_ref, scl_ref, tc_ref, ts_ref, ur_ref, ui_ref, xbuf, xsem):
    mod = mod_ref[0]
    h = _column_rows(x_hbm, xbuf, xsem, pl.num_programs(0)) * (1.0 + mod[1:2]) + mod[0:1]
    a, b = _channel_dft(h, (cch_ref, ccl_ref), (sch_ref, scl_ref))
    tch, tcl = _split(tc_ref[0])
    tsh, tsl = _split(ts_ref[0])
    ah, al = _split(a)
    bh, bl = _split(b)
    ur_ref[...] = _dot3(tch, tcl, ah, al) - _dot3(tsh, tsl, bh, bl)
    ui_ref[...] = -(_dot3(tch, tcl, bh, bl) + _dot3(tsh, tsl, ah, al))


def _fourier2_kernel(ur_ref, ui_ref, c2h_ref, c2l_ref, s2h_ref, s2l_ref, w_ref, b_ref, x_hbm, xc_ref, mod_ref,
                     cch_ref, ccl_ref, sch_ref, scl_ref, cnh_ref, cnl_ref, snh_ref, snl_ref,
                     lng_ref, lnb_ref, rwh_ref, rwl_ref, rb_ref,
                     x1_hbm, h2_hbm, idx_hbm, gate_hbm, cnt_ref, bx1, bh2, bidx, bgate, sem, xbuf, xsem,
                     *, n1, norm_lat, norm_ctx):
    k = pl.program_id(0)
    n2 = ur_ref.shape[1]
    slot = k % 2
    bufs = (bx1, bh2, bidx, bgate)
    outs = (x1_hbm, h2_hbm, idx_hbm, gate_hbm)
    common = (lng_ref[...], lnb_ref[...], rwh_ref[...], rwl_ref[...], rb_ref[...])

    def lat_copies(s, col):
        return [pltpu.make_async_copy(b.at[s, pl.ds(0, n2), :], o.at[pl.ds(0, n2), col, :], sem.at[s])
                for b, o in zip(bufs, outs)]

    def ctx_copies(s):
        return [pltpu.make_async_copy(b.at[s, pl.ds(j * n2, n2), :], o.at[n1 + j], sem.at[s])
                for b, o in zip(bufs, outs) for j in range(xc_ref.shape[0] // n2)]

    @pl.when(k >= 2)
    def _():
        for c in lat_copies(slot, 0):
            c.wait()

    @pl.when(k < n1)
    def _():
        urh, url = _split(ur_ref[0])
        uih, uil = _split(ui_ref[0])
        mixed = (_dot3(c2h_ref[...], c2l_ref[...], urh, url)
                 + _dot3(s2h_ref[...], s2l_ref[...], uih, uil)) * norm_lat
        y = _dot(mixed.astype(BF16), w_ref[...]) + b_ref[...]
        x = _column_rows(x_hbm, xbuf, xsem, n1)
        _post(y, x, mod_ref[0], *common, tuple(b.at[slot, pl.ds(0, n2), :] for b in bufs) + (cnt_ref,))
        for c in lat_copies(slot, k):
            c.start()

    @pl.when(k == n1)
    def _():
        mod = mod_ref[1]
        x = xc_ref[...]
        h = x * (1.0 + mod[1:2]) + mod[0:1]
        a, b = _channel_dft(h, (cch_ref, ccl_ref), (sch_ref, scl_ref))
        ah, al = _split(a)
        bh, bl = _split(b)
        mixed = (_dot3(cnh_ref[...], cnl_ref[...], ah, al) - _dot3(snh_ref[...], snl_ref[...], bh, bl)) * norm_ctx
        y = _dot(mixed.astype(BF16), w_ref[...]) + b_ref[...]
        _post(y, x, mod, *common, tuple(b.at[slot] for b in bufs) + (cnt_ref,))
        for c in ctx_copies(slot):
            c.start()
        for c in lat_copies(1 - slot, 0) + ctx_copies(slot):
            c.wait()


def _fourier_layer(x_lat, x_ctx, mods, w_out, b_out, lng, lnb, router):
    n_lat, d = x_lat.shape
    n_ctx = x_ctx.shape[0]
    n2 = LANE
    n1 = n_lat // n2
    cw = d // FN_GROUPS
    x3 = x_lat.reshape(n1, n2, d)
    any_spec = pl.BlockSpec(memory_space=pl.ANY)
    col_scratch = [pltpu.VMEM((2, n1, d), F32), pltpu.SemaphoreType.DMA((2,))]
    cmat = _dft_mats(cw)
    w_bf = w_out.astype(BF16)
    b2 = b_out.reshape(1, d)
    lng2, lnb2 = lng.reshape(1, d), lnb.reshape(1, d)

    k1 = jnp.arange(n1, dtype=I32)
    pos = jnp.arange(n1, dtype=I32)[None, None, :] * n2 + jnp.arange(n2, dtype=I32)[:, None, None]
    ang = ((k1[None, :, None] * pos) % n_lat).astype(F32) * (2.0 * math.pi / n_lat)
    tc, ts = jnp.cos(ang), jnp.sin(ang)

    mat = _full((cw, cw))
    ur, ui = pl.pallas_call(
        _fourier1_kernel,
        grid=(n2,),
        in_specs=[
            any_spec,
            pl.BlockSpec((1, N_MOD, d), lambda j: (0, 0, 0)),
            mat, mat, mat, mat,
            pl.BlockSpec((1, n1, n1), lambda j: (j, 0, 0)),
            pl.BlockSpec((1, n1, n1), lambda j: (j, 0, 0)),
        ],
        out_specs=(pl.BlockSpec((n1, d), lambda j: (0, j)), pl.BlockSpec((n1, d), lambda j: (0, j))),
        out_shape=(jax.ShapeDtypeStruct((n1, n2 * d), F32), jax.ShapeDtypeStruct((n1, n2 * d), F32)),
        scratch_shapes=col_scratch,
        compiler_params=_params("arbitrary"),
        name="fourier1",
    )(x3, mods, *cmat, tc, ts)

    assert n_ctx % n2 == 0
    t_all = n_lat + n_ctx
    blocks = t_all // n2
    m2 = _dft_mats(n2)
    cn = _dft_mats(n_ctx)
    mat2 = _full((n2, n2))
    matn = _full((n_ctx, n_ctx))
    lat_step = lambda k: jnp.minimum(k, n1 - 1)
    out_shapes = (
        jax.ShapeDtypeStruct((blocks, n2, d), F32),
        jax.ShapeDtypeStruct((blocks, n2, d), F32),
        jax.ShapeDtypeStruct((blocks, n2, LANE), I32),
        jax.ShapeDtypeStruct((blocks, n2, LANE), F32),
        _COUNT_SHAPE,
    )
    outs = pl.pallas_call(
        functools.partial(_fourier2_kernel, n1=n1, norm_lat=1.0 / math.sqrt(n_lat * cw),
                          norm_ctx=1.0 / math.sqrt(n_ctx * cw)),
        grid=(n1 + 1,),
        in_specs=[
            pl.BlockSpec((1, n2, d), lambda k: (lat_step(k), 0, 0)),
            pl.BlockSpec((1, n2, d), lambda k: (lat_step(k), 0, 0)),
            mat2, mat2, mat2, mat2,
            _full((d, d)), _full((1, d)),
            any_spec,
            _full((n_ctx, d)),
            _full((2, N_MOD, d)),
            mat, mat, mat, mat, matn, matn, matn, matn,
            _full((1, d)), _full((1, d)),
            _full((d, LANE)), _full((d, LANE)), _full((1, LANE)),
        ],
        out_specs=(any_spec, any_spec, any_spec, any_spec, _COUNT_SPEC),
        out_shape=out_shapes,
        scratch_shapes=[pltpu.VMEM((2, n_ctx, d), F32), pltpu.VMEM((2, n_ctx, d), F32),
                        pltpu.VMEM((2, n_ctx, LANE), I32), pltpu.VMEM((2, n_ctx, LANE), F32),
                        pltpu.SemaphoreType.DMA((2,))] + col_scratch,
        compiler_params=_params("arbitrary"),
        name="fourier2",
    )(ur.reshape(n1, n2, d), ui.reshape(n1, n2, d), *m2, w_bf, b2, x3, x_ctx, mods, *cmat, *cn, lng2, lnb2,
      *router)
    return (outs[0].reshape(t_all, d), outs[1].reshape(t_all, d), outs[2].reshape(t_all, LANE),
            outs[3].reshape(t_all, LANE), outs[4])


def _qkv_kernel(pos_ref, *refs, n_qk, rms):
    mod_ref, w_ref, b_ref, cos_ref, sin_ref, *rest = refs[N_COMBINE_REFS:]
    if rms:
        gain_ref, ind_ref, indt_ref, x_ref, q_ref, k_ref, v_ref, ybuf, sem = rest
    else:
        x_ref, q_ref, k_ref, v_ref, ybuf, sem = rest
    x = _combine_tile(pos_ref, *refs[:N_COMBINE_REFS], ybuf, sem)
    x_ref[...] = x
    mod = mod_ref[0]
    h = x * (1.0 + mod[1:2]) + mod[0:1]
    y = _dot(h.astype(BF16), w_ref[...]) + b_ref[...]
    qk = y[:, :n_qk]
    if rms:
        sh, sl = _split(qk * qk)
        ms = _dot(sh, ind_ref[...]) + _dot(sl, ind_ref[...])
        mh, ml = _split(ms)
        msb = _dot(mh, indt_ref[...]) + _dot(ml, indt_ref[...])
        qk = qk * lax.rsqrt(msb + RMS_EPS) * gain_ref[...]
    cos = cos_ref[...]
    sin = sin_ref[...]
    even = (lax.broadcasted_iota(I32, cos.shape, 1) & 1) == 0
    parts = []
    for c in range(n_qk // LANE):
        z = qk[:, c * LANE:(c + 1) * LANE]
        swapped = jnp.where(even, pltpu.roll(z, LANE - 1, 1), pltpu.roll(z, 1, 1))
        parts.append(z * cos + swapped * sin)
    nq = q_ref.shape[1]
    q_ref[...] = (jnp.concatenate(parts[:nq // LANE], axis=1) * Q_SCALE).astype(BF16)
    k_ref[...] = jnp.concatenate(parts[nq // LANE:], axis=1).astype(BF16)
    v_ref[...] = y[:, n_qk:].astype(BF16)


def _qkv(pending, mods, w_qkv, b_qkv, cos_t, sin_t, n_q, n_kv, n_lat, q_norm=None, k_norm=None):
    t_all, d = pending[3].shape
    nq, nk = n_q * HEAD_DIM, n_kv * HEAD_DIM
    n_qk, n_all = nq + nk, nq + 2 * nk
    rms = q_norm is not None
    pos, args, in_specs, scratch = _combine_operands(pending, TM, n_lat)
    row = lambda w: pl.BlockSpec((TM, w), lambda t, *_: (t, 0))
    in_specs += [_tile_specs(n_lat // TM)[1], _full((d, n_all)), _full((1, n_all)), row(LANE), row(LANE)]
    args += [mods, w_qkv.astype(BF16), b_qkv.reshape(1, n_all), cos_t, sin_t]
    if rms:
        gain = jnp.concatenate([jnp.tile(q_norm, n_q), jnp.tile(k_norm, n_kv)]).reshape(1, n_qk)
        head = np.arange(n_qk) // HEAD_DIM
        ind = np.zeros((n_qk, LANE), np.float32)
        ind[np.arange(n_qk), head] = 1.0 / HEAD_DIM
        indt = np.zeros((LANE, n_qk), np.float32)
        indt[head, np.arange(n_qk)] = 1.0
        in_specs += [_full((1, n_qk)), _full((n_qk, LANE)), _full((LANE, n_qk))]
        args += [gain, jnp.asarray(ind, BF16), jnp.asarray(indt, BF16)]
    grid_spec = pltpu.PrefetchScalarGridSpec(
        num_scalar_prefetch=1,
        grid=(t_all // TM,),
        in_specs=in_specs,
        out_specs=(row(d), row(nq), row(nk), row(nk)),
        scratch_shapes=scratch,
    )
    return pl.pallas_call(
        functools.partial(_qkv_kernel, n_qk=n_qk, rms=rms),
        grid_spec=grid_spec,
        out_shape=(jax.ShapeDtypeStruct((t_all, d), F32), jax.ShapeDtypeStruct((t_all, nq), BF16),
                   jax.ShapeDtypeStruct((t_all, nk), BF16), jax.ShapeDtypeStruct((t_all, nk), BF16)),
        compiler_params=_params("arbitrary"),
        name="qkv",
    )(pos, *args)


def _rope_tables(n_lat, n_ctx):
    rows = n_lat // GRID_W
    row = jnp.repeat(jnp.arange(rows, dtype=F32), GRID_W)
    col = jnp.tile(jnp.arange(GRID_W, dtype=F32), rows)
    n_freq = HEAD_DIM // 4
    inv = ROPE_THETA ** (-jnp.arange(n_freq, dtype=F32) / n_freq)
    ang = jnp.concatenate([row[:, None] * inv, col[:, None] * inv], axis=-1)
    ang = jnp.concatenate([ang, jnp.zeros((n_ctx, HEAD_DIM // 2), F32)], axis=0)
    cos = jnp.tile(jnp.repeat(jnp.cos(ang), 2, axis=1), (1, LANE // HEAD_DIM))
    sin = jnp.tile(jnp.repeat(jnp.sin(ang), 2, axis=1), (1, LANE // HEAD_DIM))
    sign = jnp.where(jnp.arange(LANE) % 2 == 0, -1.0, 1.0).astype(F32)
    return cos, sin * sign


def _kv_layouts(k, v, n_kv):
    t_all = k.shape[0]
    kh = k.reshape(t_all, n_kv, HEAD_DIM).transpose(1, 0, 2)
    vt = v.reshape(t_all, n_kv, HEAD_DIM).transpose(1, 2, 0)
    pad = jnp.zeros((n_kv, LANE - HEAD_DIM, t_all), BF16).at[:, 0, :].set(1.0)
    return kh, jnp.concatenate([vt, pad], axis=1)


def _flash_kernel(q_ref, k_ref, vt_ref, o_ref, s_even, s_odd, *, grp, tk, n_lat, n_ctx):
    tq = q_ref.shape[0]
    q_t = q_ref[...].astype(F32).T
    qt = jnp.concatenate([q_t[g * HEAD_DIM:(g + 1) * HEAD_DIM] for g in range(grp)], axis=1).astype(BF16)
    cols = qt.shape[1]
    is_ctx = pl.program_id(1) == n_lat // FLASH_TQ
    n_chunks = (n_lat + n_ctx) // tk

    def finish(acc):
        o = acc[:HEAD_DIM] / acc[HEAD_DIM:HEAD_DIM + 1]
        o_t = jnp.concatenate([o[:, g * tq:(g + 1) * tq] for g in range(grp)], axis=0)
        o_ref[...] = o_t.T.astype(BF16)

    groups = [slice(c * MXU_WIDTH, (c + 1) * MXU_WIDTH) for c in range(cols // MXU_WIDTH)]

    def scores(j, buf, g):
        off = pl.multiple_of(j * tk, tk)
        s = _dot(k_ref[0, pl.ds(off, tk), :], qt[:, g])
        buf[:, g] = s
        return jnp.max(s, axis=0, keepdims=True)

    def accumulate(j, buf, g, m, acc, mc):
        off = pl.multiple_of(j * tk, tk)
        m_new = jnp.maximum(m, mc)
        p = jnp.exp2(buf[:, g] - m_new)
        acc = jnp.exp2(m - m_new) * acc + _dot(vt_ref[0, :, pl.ds(off, tk)], p.astype(BF16))
        return m_new, acc

    @pl.when(jnp.logical_not(is_ctx))
    def _():
        def step(j, cur, nxt, carry):
            mcs = [scores(j + 1, nxt, groups[0])]
            out = []
            for gi, (g, (m, acc, mc)) in enumerate(zip(groups, carry)):
                if gi + 1 < len(groups):
                    mcs.append(scores(j + 1, nxt, groups[gi + 1]))
                out.append(accumulate(j, cur, g, m, acc, mc) + (mcs[gi],))
            return out

        def body(jj, carry):
            for pair in range(FLASH_PAIRS):
                j = 2 * (FLASH_PAIRS * jj + pair)
                carry = step(j, s_even, s_odd, carry)
                carry = step(j + 1, s_odd, s_even, carry)
            return carry

        init = [(jnp.full((1, MXU_WIDTH), NEG, F32), jnp.zeros((LANE, MXU_WIDTH), F32), scores(0, s_even, g))
                for g in groups]
        carry = lax.fori_loop(0, (n_chunks - 1) // (2 * FLASH_PAIRS), body, init)
        finish(jnp.concatenate([accumulate(n_chunks - 1, s_even, g, m, acc, mc)[1]
                                for g, (m, acc, mc) in zip(groups, carry)], axis=1))

    @pl.when(is_ctx)
    def _():
        s = _dot(k_ref[0, n_lat:n_lat + n_ctx, :], qt)
        p = jnp.exp2(s - jnp.max(s, axis=0, keepdims=True))
        finish(_dot(vt_ref[0, :, n_lat:n_lat + n_ctx], p.astype(BF16)))


def _full_attention(q, k, v, n_lat, n_ctx):
    t_all, nq = q.shape
    n_kv = k.shape[1] // HEAD_DIM
    grp = nq // HEAD_DIM // n_kv
    nt = t_all // FLASH_TQ
    cols = grp * FLASH_TQ
    assert n_ctx == FLASH_TQ and t_all % FLASH_TK == 0 and (t_all // FLASH_TK - 1) % (2 * FLASH_PAIRS) == 0
    kh, vt = _kv_layouts(k, v, n_kv)
    return pl.pallas_call(
        functools.partial(_flash_kernel, grp=grp, tk=FLASH_TK, n_lat=n_lat, n_ctx=n_ctx),
        grid=(n_kv, nt),
        in_specs=[
            pl.BlockSpec((FLASH_TQ, grp * HEAD_DIM), lambda h, i: (i, h)),
            pl.BlockSpec((1, t_all, HEAD_DIM), lambda h, i: (h, 0, 0)),
            pl.BlockSpec((1, LANE, t_all), lambda h, i: (h, 0, 0)),
        ],
        out_specs=pl.BlockSpec((FLASH_TQ, grp * HEAD_DIM), lambda h, i: (i, h)),
        out_shape=jax.ShapeDtypeStruct((t_all, nq), BF16),
        scratch_shapes=[pltpu.VMEM((FLASH_TK, cols), F32), pltpu.VMEM((FLASH_TK, cols), F32)],
        compiler_params=_params("arbitrary", "arbitrary"),
        name="flash",
    )(q, kh, vt)


def _window_kernel(q_ref, kp_ref, kc_ref, kn_ref, kx_ref, vp_ref, vc_ref, vn_ref, vx_ref, sink_ref, o_ref,
                   *, grp, nb):
    i = pl.program_id(0)
    gw = grp * HEAD_DIM
    n_kv = kp_ref.shape[0]
    q_t = q_ref[...].astype(F32).T
    cols = grp * Q_BLOCK
    kj = lax.broadcasted_iota(I32, (Q_BLOCK, cols), 0)
    qi = lax.broadcasted_iota(I32, (Q_BLOCK, cols), 1) & (Q_BLOCK - 1)
    lo = jnp.where(i > 0, qi, Q_BLOCK)
    hi = jnp.where(i < nb - 1, qi, -1)
    outs = []
    for h in range(n_kv):
        qt = jnp.concatenate([q_t[h * gw + g * HEAD_DIM:h * gw + (g + 1) * HEAD_DIM] for g in range(grp)],
                             axis=1).astype(BF16)
        s = _dot(jnp.concatenate([kp_ref[h], kc_ref[h], kn_ref[h], kx_ref[h]], axis=0), qt)
        sp = jnp.where(kj >= lo, s[:Q_BLOCK], NEG)
        sn = jnp.where(kj <= hi, s[2 * Q_BLOCK:3 * Q_BLOCK], NEG)
        s = jnp.concatenate([sp, s[Q_BLOCK:2 * Q_BLOCK], sn, s[3 * Q_BLOCK:]], axis=0)
        sink = sink_ref[h]
        m = jnp.maximum(jnp.max(s, axis=0, keepdims=True), sink)
        p = jnp.exp2(s - m).astype(BF16)
        acc = _dot(jnp.concatenate([vp_ref[h], vc_ref[h], vn_ref[h], vx_ref[h]], axis=1), p)
        o = acc[:HEAD_DIM] / (acc[HEAD_DIM:HEAD_DIM + 1] + jnp.exp2(sink - m))
        outs += [o[:, g * Q_BLOCK:(g + 1) * Q_BLOCK] for g in range(grp)]
    o_ref[...] = jnp.concatenate(outs, axis=0).T.astype(BF16)


def _window_attention(q, k, v, sink, n_lat, n_ctx):
    nq = q.shape[1]
    n_kv = k.shape[1] // HEAD_DIM
    grp = nq // HEAD_DIM // n_kv
    nb = n_lat // Q_BLOCK
    cb = n_lat // n_ctx
    assert WINDOW == Q_BLOCK
    kh, vt = _kv_layouts(k, v, n_kv)
    sink_cols = jnp.repeat(sink.reshape(n_kv, 1, grp) * LOG2E, Q_BLOCK, axis=2)
    kspec = lambda f: pl.BlockSpec((n_kv, Q_BLOCK, HEAD_DIM), lambda i: (0, f(i), 0))
    vspec = lambda f: pl.BlockSpec((n_kv, LANE, Q_BLOCK), lambda i: (0, 0, f(i)))
    prev = lambda i: jnp.maximum(i - 1, 0)
    cur = lambda i: i
    nxt = lambda i: jnp.minimum(i + 1, nb - 1)
    return pl.pallas_call(
        functools.partial(_window_kernel, grp=grp, nb=nb),
        grid=(nb,),
        in_specs=[
            pl.BlockSpec((Q_BLOCK, nq), lambda i: (i, 0)),
            kspec(prev), kspec(cur), kspec(nxt),
            pl.BlockSpec((n_kv, n_ctx, HEAD_DIM), lambda i: (0, cb, 0)),
            vspec(prev), vspec(cur), vspec(nxt),
            pl.BlockSpec((n_kv, LANE, n_ctx), lambda i: (0, 0, cb)),
            _full((n_kv, 1, grp * Q_BLOCK)),
        ],
        out_specs=pl.BlockSpec((Q_BLOCK, nq), lambda i: (i, 0)),
        out_shape=jax.ShapeDtypeStruct((n_lat, nq), BF16),
        compiler_params=_params("arbitrary"),
        name="window",
    )(q, kh, kh, kh, kh, vt, vt, vt, vt, sink_cols)


def _gmlp_kernel(pos_ref, *refs):
    (mod_ref, win_ref, bin_ref, vg_ref, vb_ref, ws_ref, bs_ref, wout_ref, bout_ref,
     lng_ref, lnb_ref, rwh_ref, rwl_ref, rb_ref, *rest) = refs[N_COMBINE_REFS:]
    out_refs, (ybuf, sem) = rest[:-2], rest[-2:]
    mod = mod_ref[0]
    x = _combine_tile(pos_ref, *refs[:N_COMBINE_REFS], ybuf, sem)
    h = x * (1.0 + mod[1:2]) + mod[0:1]
    z = _dot(h.astype(BF16), win_ref[...]) + bin_ref[...]
    z = 0.5 * z * (1.0 + lax.erf(z * (2.0 ** -0.5)))
    half = z.shape[1] // 2
    u = z[:, :half]
    v = _layer_norm(z[:, half:], vg_ref[...], vb_ref[...]).astype(BF16)
    cw = half // GM_GROUPS
    chunks = []
    for c in range(x.shape[0] // GM_CHUNK):
        vc = v[c * GM_CHUNK:(c + 1) * GM_CHUNK]
        chunks.append(jnp.concatenate(
            [_dot(ws_ref[g], vc[:, g * cw:(g + 1) * cw]) + bs_ref[g] for g in range(GM_GROUPS)], axis=1))
    gated = u * jnp.concatenate(chunks, axis=0)
    y = _dot(gated.astype(BF16), wout_ref[...]) + bout_ref[...]
    _post(y, x, mod, lng_ref[...], lnb_ref[...], rwh_ref[...], rwl_ref[...], rb_ref[...], out_refs)


def _gmlp_layer(pending, mods, w_in, b_in, vg, vb, w_s, b_s, w_out, b_out, lng, lnb, router, n_lat):
    t_all, d = pending[3].shape
    dffn = w_in.shape[1]
    half = dffn // 2
    cw = half // GM_GROUPS
    shapes, specs = _post_out(t_all)
    bs_full = jnp.broadcast_to(b_s[:, :, None], (GM_GROUPS, GM_CHUNK, cw))
    pos, args, in_specs, scratch = _combine_operands(pending, TM, n_lat)
    in_specs += [
        _tile_specs(n_lat // TM)[1], _full((d, dffn)), _full((1, dffn)), _full((1, half)), _full((1, half)),
        _full((GM_GROUPS, GM_CHUNK, GM_CHUNK)), _full((GM_GROUPS, GM_CHUNK, cw)),
        _full((half, d)), _full((1, d)), _full((1, d)), _full((1, d)),
        _full((d, LANE)), _full((d, LANE)), _full((1, LANE)),
    ]
    args += [mods, w_in.astype(BF16), b_in.reshape(1, dffn), vg.reshape(1, half), vb.reshape(1, half),
             w_s.astype(BF16), bs_full, w_out.astype(BF16), b_out.reshape(1, d), lng.reshape(1, d),
             lnb.reshape(1, d), *router]
    grid_spec = pltpu.PrefetchScalarGridSpec(
        num_scalar_prefetch=1,
        grid=(t_all // TM,),
        in_specs=in_specs,
        out_specs=specs,
        scratch_shapes=scratch,
    )
    return pl.pallas_call(
        _gmlp_kernel,
        grid_spec=grid_spec,
        out_shape=shapes,
        compiler_params=_params("arbitrary"),
        name="gmlp",
    )(pos, *args)


COMB_TM = 128


def _route_meta(count_blk, n_tok):
    n_assign = n_tok * TOP_K
    n_blocks = -(-n_assign // MOE_BLOCK) + N_EXPERTS
    ids = jnp.arange(N_EXPERTS, dtype=I32)
    counts = count_blk[0, :N_EXPERTS].astype(I32)
    padded = (counts + MOE_BLOCK - 1) // MOE_BLOCK * MOE_BLOCK
    ends_pad = jnp.cumsum(padded)
    base = ends_pad - padded
    starts = jnp.arange(n_blocks, dtype=I32) * MOE_BLOCK
    block_expert = jnp.minimum(jnp.sum((ends_pad[None, :] <= starts[:, None]).astype(I32), axis=1), N_EXPERTS - 1)
    last_blk = jnp.where(padded > 0, ends_pad - MOE_BLOCK, -1)
    tail = ends_pad[-1] + ids * MOE_BLOCK
    tail = jnp.where(tail < n_blocks * MOE_BLOCK, tail, -1)
    base_b = jnp.broadcast_to(base.astype(F32)[:, None], (N_EXPERTS, LANE))
    return block_expert.astype(I32), base_b, jnp.concatenate([last_blk, tail]).astype(I32), n_blocks


def _rank_kernel(idx_ref, base_ref, upper_ref, dest_ref, run_ref):
    @pl.when(pl.program_id(0) == 0)
    def _():
        run_ref[...] = jnp.zeros(run_ref.shape, F32)

    eid = lax.broadcasted_iota(I32, (N_EXPERTS, TM), 0)
    onehots = [(eid == idx_ref[k:k + 1, :]).astype(F32) for k in range(TOP_K)]
    cnt = onehots[0] + onehots[1] + onehots[2] + onehots[3]
    before = _dot(cnt.astype(BF16), upper_ref[...])
    slot = base_ref[:, :1] + run_ref[:, :1] + before
    for k in range(TOP_K):
        rows = (jnp.sum(onehots[k] * slot, axis=0, keepdims=True) * ROW_TILE).astype(I32)
        for j in range(TM // COMB_TM):
            dest_ref[j, k:k + 1, :] = rows[:, j * COMB_TM:(j + 1) * COMB_TM]
    run_ref[...] = run_ref[...] + jnp.sum(cnt, axis=1, keepdims=True)


def _rank(top_idx, base_b, n_tok):
    per_step = TM // COMB_TM
    idx_t = top_idx[:n_tok, :TOP_K].T
    upper = jnp.asarray(np.triu(np.ones((TM, TM), np.float32), 1), BF16)
    return pl.pallas_call(
        _rank_kernel,
        grid=(n_tok // TM,),
        in_specs=[pl.BlockSpec((TOP_K, TM), lambda t: (0, t)), _full((N_EXPERTS, LANE)), _full((TM, TM))],
        out_specs=pl.BlockSpec((per_step, TOP_K, COMB_TM), lambda t: (t, 0, 0)),
        out_shape=jax.ShapeDtypeStruct((n_tok // COMB_TM, TOP_K, COMB_TM), I32),
        scratch_shapes=[pltpu.VMEM((N_EXPERTS, LANE), F32)],
        compiler_params=_params("arbitrary"),
        name="rank",
    )(idx_t, base_b, upper)


def _dispatch_kernel(dest_ref, zrow_ref, h_ref, xs_hbm, hbuf, zbuf, sem, zsem, *, n_tiles):
    t = pl.program_id(0)
    slot = t % 2
    n_zero = 2 * N_EXPERTS

    def zero_copy(j):
        row = pl.multiple_of(zrow_ref[j] * ROW_TILE, MOE_BLOCK * ROW_TILE)
        return pltpu.make_async_copy(zbuf, xs_hbm.at[pl.ds(row, MOE_BLOCK * ROW_TILE), :], zsem.at[0])

    def wait_rows(s):
        for _ in range(TOP_K):
            pltpu.make_async_copy(hbuf.at[s], xs_hbm.at[pl.ds(0, COMB_TM * ROW_TILE), :], sem.at[s]).wait()

    @pl.when(t == 0)
    def _():
        zbuf[...] = jnp.zeros(zbuf.shape, F32)
        for j in range(n_zero):
            @pl.when(zrow_ref[j] >= 0)
            def _():
                zero_copy(j).start()
        for j in range(n_zero):
            @pl.when(zrow_ref[j] >= 0)
            def _():
                zero_copy(j).wait()

    @pl.when(t >= 2)
    def _():
        wait_rows(slot)

    _to_row_tiles(hbuf.at[slot], h_ref[...])
    for k in range(TOP_K):
        for r in range(COMB_TM):
            row = pl.multiple_of(dest_ref[(t * TOP_K + k) * COMB_TM + r], ROW_TILE)
            pltpu.make_async_copy(hbuf.at[slot, pl.ds(r * ROW_TILE, ROW_TILE), :],
                                  xs_hbm.at[pl.ds(row, ROW_TILE), :], sem.at[slot]).start(priority=r % 2)

    @pl.when(t == n_tiles - 1)
    def _():
        wait_rows(slot)
        if n_tiles > 1:
            wait_rows(1 - slot)


def _dispatch(h2, dest, zrows, n_tok, n_blocks):
    d = h2.shape[1]
    n_tiles = n_tok // COMB_TM
    grid_spec = pltpu.PrefetchScalarGridSpec(
        num_scalar_prefetch=2,
        grid=(n_tiles,),
        in_specs=[pl.BlockSpec((COMB_TM, d), lambda t, dest, zr: (t, 0))],
        out_specs=pl.BlockSpec(memory_space=pl.ANY),
        scratch_shapes=[pltpu.VMEM((2, COMB_TM * ROW_TILE, LANE), F32),
                        pltpu.VMEM((MOE_BLOCK * ROW_TILE, LANE), F32),
                        pltpu.SemaphoreType.DMA((2,)), pltpu.SemaphoreType.DMA((1,))],
    )
    return pl.pallas_call(
        functools.partial(_dispatch_kernel, n_tiles=n_tiles),
        grid_spec=grid_spec,
        out_shape=jax.ShapeDtypeStruct((n_blocks * MOE_BLOCK * ROW_TILE, LANE), F32),
        compiler_params=_params("arbitrary"),
        name="dispatch",
    )(dest.reshape(-1), zrows, h2)


EXP_CHUNK = MXU_WIDTH


def _expert_kernel(be_ref, x_ref, wgu_ref, bgu_ref, wd_ref, bd_ref, sel_ref, y_ref,
                   wg_s, wl_s, wd_s, bg_s, bl_s):
    b = pl.program_id(0)
    ff = wg_s.shape[1]
    half = EXP_CHUNK // 2

    @pl.when(jnp.logical_or(b == 0, be_ref[b] != be_ref[jnp.maximum(b - 1, 0)]))
    def _():
        for c in range(2 * ff // EXP_CHUNK):
            w = _dot(wgu_ref[0, 0, :, c * EXP_CHUNK:(c + 1) * EXP_CHUNK].astype(BF16), sel_ref[...])
            wg_s[:, c * half:(c + 1) * half] = w[:, :half].astype(BF16)
            wl_s[:, c * half:(c + 1) * half] = w[:, half:].astype(BF16)
            bh, bl = _split(jnp.broadcast_to(bgu_ref[0, 0, :, c * EXP_CHUNK:(c + 1) * EXP_CHUNK],
                                             (SUBLANE, EXP_CHUNK)))
            bias = _dot(bh, sel_ref[...]) + _dot(bl, sel_ref[...])
            bg_s[:, c * half:(c + 1) * half] = bias[:, :half]
            bl_s[:, c * half:(c + 1) * half] = bias[:, half:]
        wd_s[...] = wd_ref[0, 0].astype(BF16)

    x = _from_row_tiles(x_ref, 0, MOE_BLOCK).astype(BF16)
    glu = jnp.minimum(_dot(x, wg_s[...]) + bg_s[0:1, :], SWIGLU_LIMIT)
    lin = jnp.clip(_dot(x, wl_s[...]) + bl_s[0:1, :], -SWIGLU_LIMIT, SWIGLU_LIMIT)
    act = glu * (1.0 / (1.0 + jnp.exp(-SWIGLU_ALPHA * glu))) * (lin + 1.0)
    _to_row_tiles(y_ref, _dot(act.astype(BF16), wd_s[...]) + bd_ref[0, 0])


def _experts(xs, block_expert, n_blocks, w_gate_up, b_gate_up, w_down, b_down, layer):
    d = w_down.shape[3]
    ff = w_down.shape[2]
    half = EXP_CHUNK // 2
    sel = np.zeros((EXP_CHUNK, EXP_CHUNK), np.float32)
    sel[2 * np.arange(half), np.arange(half)] = 1.0
    sel[2 * np.arange(half) + 1, half + np.arange(half)] = 1.0
    wspec = lambda r, c: pl.BlockSpec((1, 1, r, c), lambda b, be: (layer, be[b], 0, 0))
    sspec = pl.BlockSpec((EXP_CHUNK, EXP_CHUNK), lambda b, be: (0, 0))
    grid_spec = pltpu.PrefetchScalarGridSpec(
        num_scalar_prefetch=1,
        grid=(n_blocks,),
        in_specs=[pl.BlockSpec((MOE_BLOCK * ROW_TILE, LANE), lambda b, be: (b, 0)), wspec(d, 2 * ff),
                  wspec(1, 2 * ff), wspec(ff, d), wspec(1, d), sspec],
        out_specs=pl.BlockSpec((MOE_BLOCK * ROW_TILE, LANE), lambda b, be: (b, 0)),
        scratch_shapes=[pltpu.VMEM((d, ff), BF16), pltpu.VMEM((d, ff), BF16), pltpu.VMEM((ff, d), BF16),
                        pltpu.VMEM((SUBLANE, ff), F32), pltpu.VMEM((SUBLANE, ff), F32)],
    )
    return pl.pallas_call(
        _expert_kernel,
        grid_spec=grid_spec,
        out_shape=jax.ShapeDtypeStruct((n_blocks * MOE_BLOCK * ROW_TILE, LANE), F32),
        compiler_params=_params("arbitrary"),
        name="experts",
    )(block_expert, xs, w_gate_up, b_gate_up[:, :, None, :], w_down, b_down[:, :, None, :],
      jnp.asarray(sel, BF16))


def _gather_rows(idx_ref, base, n, src_hbm, dst, sem):
    for r in range(n):
        row = pl.multiple_of(idx_ref[base + r], ROW_TILE)
        pltpu.make_async_copy(src_hbm.at[pl.ds(row, ROW_TILE), :], dst.at[pl.ds(r * ROW_TILE, ROW_TILE), :],
                              sem).start(priority=r % 2)


N_COMBINE_REFS = 6


def _combine_tile(pos_ref, y_hbm, gate_ref, x1_ref, mod_ref, lng_ref, lnb_ref, ybuf, sem):
    t = pl.program_id(0)
    slot = t % 2
    tm, d = x1_ref.shape
    n = tm * TOP_K

    @pl.when(t == 0)
    def _():
        _gather_rows(pos_ref, 0, n, y_hbm, ybuf.at[0], sem.at[0])

    @pl.when(t + 1 < pl.num_programs(0))
    def _():
        _gather_rows(pos_ref, (t + 1) * n, n, y_hbm, ybuf.at[1 - slot], sem.at[1 - slot])

    pltpu.make_async_copy(y_hbm.at[pl.ds(0, n * ROW_TILE), :], ybuf.at[slot], sem.at[slot]).wait()
    gates = gate_ref[...]
    parts = []
    for j in range(tm // COMB_TM):
        f = jnp.zeros((COMB_TM, d), F32)
        for k in range(TOP_K):
            rows = _from_row_tiles(ybuf.at[slot], (j * TOP_K + k) * COMB_TM, COMB_TM)
            f = f + gates[j * COMB_TM:(j + 1) * COMB_TM, k:k + 1] * rows
        parts.append(f)
    f = parts[0] if len(parts) == 1 else jnp.concatenate(parts, axis=0)
    return _layer_norm(ALPHA * x1_ref[...] + mod_ref[0][5:6] * f, lng_ref[...], lnb_ref[...])


def _combine_operands(pending, tm, n_lat):
    yb, dest, gates, x1, mods, lng, lnb = pending
    d = x1.shape[1]
    n_lat_tiles = n_lat // tm
    specs = [
        pl.BlockSpec(memory_space=pl.ANY),
        pl.BlockSpec((tm, LANE), lambda t, *_: (t, 0)),
        pl.BlockSpec((tm, d), lambda t, *_: (t, 0)),
        pl.BlockSpec((1, N_MOD, d), lambda t, *_: (jnp.where(t >= n_lat_tiles, 1, 0), 0, 0)),
        _full((1, d)), _full((1, d)),
    ]
    scratch = [pltpu.VMEM((2, tm * TOP_K * ROW_TILE, LANE), F32), pltpu.SemaphoreType.DMA((2,))]
    return dest.reshape(-1), [yb, gates, x1, mods, lng.reshape(1, d), lnb.reshape(1, d)], specs, scratch


def _combine_kernel(pos_ref, *refs):
    o_ref, ybuf, sem = refs[N_COMBINE_REFS:]
    o_ref[...] = _combine_tile(pos_ref, *refs[:N_COMBINE_REFS], ybuf, sem)


def _combine(pending, n_tok, n_lat):
    d = pending[3].shape[1]
    pos, args, specs, scratch = _combine_operands(pending, COMB_TM, n_lat)
    grid_spec = pltpu.PrefetchScalarGridSpec(
        num_scalar_prefetch=1,
        grid=(n_tok // COMB_TM,),
        in_specs=specs,
        out_specs=pl.BlockSpec((COMB_TM, d), lambda t, *_: (t, 0)),
        scratch_shapes=scratch,
    )
    return pl.pallas_call(
        _combine_kernel,
        grid_spec=grid_spec,
        out_shape=jax.ShapeDtypeStruct((n_tok, d), F32),
        compiler_params=_params("arbitrary"),
        name="combine",
    )(pos, *args)


def _moe_layer(x1, h2, top_idx, gates, counts, mods, lng, lnb, w_gate_up, b_gate_up, w_down, b_down, layer, n_tok):
    block_expert, base_b, zrows, n_blocks = _route_meta(counts, n_tok)
    dest = _rank(top_idx, base_b, n_tok)
    xs = _dispatch(h2, dest, zrows, n_tok, n_blocks)
    yb = _experts(xs, block_expert, n_blocks, w_gate_up, b_gate_up, w_down, b_down, layer)
    return yb, dest, gates, x1, mods, lng, lnb


def kernel(x, c, ctx, c_ctx, ada_w, ada_b, ln_mix_g, ln_mix_b, ln_ffn_g, ln_ffn_b, fn_w_out, fn_b_out, fa_w_qkv, fa_b_qkv, fa_q_norm, fa_k_norm, fa_w_out, fa_b_out, gm_w_in, gm_b_in, gm_v_norm_g, gm_v_norm_b, gm_w_s, gm_b_s, gm_w_out, gm_b_out, wa_w_qkv, wa_b_qkv, wa_sink, wa_w_out, wa_b_out, router_w, router_b, exp_w_gate_up, exp_b_gate_up, exp_w_down, exp_b_down):
    bsz, n_lat, d = x.shape
    n_ctx = ctx.shape[1]
    assert bsz == 1 and d == D_MODEL and n_lat == LANE * LANE and n_lat % n_ctx == 0 and n_ctx % TM == 0
    t_all = n_lat + n_ctx
    n_lat_tiles = n_lat // TM
    mods_all = _ada(c, c_ctx, ada_w, ada_b)
    cos_t, sin_t = _rope_tables(n_lat, n_ctx)

    for i in range(DEPTH):
        kind, j = i % 4, i // 4
        last = i == DEPTH - 1
        n_tok = n_lat if last else t_all
        mods = mods_all[i]
        router = _router_operands(router_w[i], router_b[i])
        lng, lnb = ln_mix_g[i], ln_mix_b[i]
        if kind == 0:
            post = _fourier_layer(x[0], ctx[0], mods, fn_w_out[j], fn_b_out[j], lng, lnb, router)
        elif kind == 1:
            x_all, q, k, v = _qkv(pending, mods, fa_w_qkv[j], fa_b_qkv[j], cos_t, sin_t, FA_Q_HEADS,
                                  FA_KV_HEADS, n_lat, fa_q_norm[j], fa_k_norm[j])
            o = _full_attention(q, k, v, n_lat, n_ctx)
            post = _proj_post(o, fa_w_out[j], fa_b_out[j], x_all, mods, lng, lnb, router, n_tok, n_lat_tiles)
        elif kind == 2:
            post = _gmlp_layer(pending, mods, gm_w_in[j], gm_b_in[j], gm_v_norm_g[j], gm_v_norm_b[j],
                               gm_w_s[j], gm_b_s[j], gm_w_out[j], gm_b_out[j], lng, lnb, router, n_lat)
        else:
            x_all, q, k, v = _qkv(pending, mods, wa_w_qkv[j], wa_b_qkv[j], cos_t, sin_t, WA_Q_HEADS,
                                  WA_KV_HEADS, n_lat)
            o = _window_attention(q, k, v, wa_sink[j], n_lat, n_ctx)
            post = _proj_post(o, wa_w_out[j], wa_b_out[j], x_all, mods, lng, lnb, router, n_tok, n_lat_tiles)
        pending = _moe_layer(*post, mods, ln_ffn_g[i], ln_ffn_b[i], exp_w_gate_up, exp_b_gate_up, exp_w_down,
                             exp_b_down, i, n_tok)
    return _combine(pending, n_lat, n_lat)[None]
```

```python
import functools
import math

import numpy as np
import jax
import jax.numpy as jnp
from jax import lax
from jax.experimental import pallas as pl
from jax.experimental.pallas import tpu as pltpu

F32, BF16, I32 = jnp.float32, jnp.bfloat16, jnp.int32

D_MODEL = 1024
DEPTH = 4
GRID_W = 64
N_MOD = 6
FN_GROUPS = 4
HEAD_DIM = 64
FA_Q_HEADS, FA_KV_HEADS = 16, 4
WA_Q_HEADS, WA_KV_HEADS = 16, 2
WINDOW = 128
Q_BLOCK = 128
ROPE_THETA = 10000.0
GM_CHUNK = 128
GM_GROUPS = 8
N_EXPERTS = 32
TOP_K = 4
SWIGLU_LIMIT = 7.0
SWIGLU_ALPHA = 1.702
MOE_BLOCK = 256
LN_EPS = 1e-5
RMS_EPS = 1e-6
NEG = -1e30
ALPHA = (2 * DEPTH) ** 0.25
LOG2E = math.log2(math.e)
Q_SCALE = HEAD_DIM ** -0.5 * LOG2E

LANE = 128
SUBLANE = 8
ROW_TILE = D_MODEL // LANE
assert ROW_TILE == SUBLANE
MXU_WIDTH = 256
TM = 256
FLASH_TQ = 256
FLASH_TK = 1280
FLASH_PAIRS = 3
VMEM_LIMIT = 56 * 2 ** 20


def _params(*sem):
    return pltpu.CompilerParams(dimension_semantics=sem, vmem_limit_bytes=VMEM_LIMIT)


def _dot(a, b):
    return jnp.dot(a, b, preferred_element_type=F32)


def _split(a):
    hi = a.astype(BF16)
    lo = (a - hi.astype(F32)).astype(BF16)
    return hi, lo


def _dot3(a_hi, a_lo, b_hi, b_lo):
    return _dot(a_hi, b_hi) + (_dot(a_hi, b_lo) + _dot(a_lo, b_hi))


def _layer_norm(x, g, b):
    mu = jnp.mean(x, axis=-1, keepdims=True)
    xc = x - mu
    var = jnp.mean(xc * xc, axis=-1, keepdims=True)
    return xc * lax.rsqrt(var + LN_EPS) * g + b


def _top4(logits):
    lane = lax.broadcasted_iota(I32, logits.shape, 1).astype(F32)
    cur = logits
    vals, idxs = [], []
    for _ in range(TOP_K):
        m = jnp.max(cur, axis=-1, keepdims=True)
        i = jnp.min(jnp.where(cur == m, lane, float(LANE)), axis=-1, keepdims=True)
        vals.append(m)
        idxs.append(i)
        cur = jnp.where(lane == i, -jnp.inf, cur)
    exps = [jnp.exp(v - vals[0]) for v in vals]
    inv = 1.0 / (exps[0] + exps[1] + exps[2] + exps[3])
    idx_out = jnp.zeros_like(logits)
    gate_out = jnp.zeros_like(logits)
    picked = jnp.zeros_like(logits)
    for k in range(TOP_K):
        idx_out = jnp.where(lane == float(k), idxs[k], idx_out)
        gate_out = jnp.where(lane == float(k), exps[k] * inv, gate_out)
        picked = picked + jnp.where(lane == idxs[k], 1.0, 0.0)
    return idx_out.astype(I32), gate_out, jnp.sum(picked, axis=0, keepdims=True)


def _post(y, x, mod, lng, lnb, rw_hi, rw_lo, rb, out_refs):
    x1_ref, h2_ref, idx_ref, gate_ref, cnt_ref = out_refs
    x1 = _layer_norm(ALPHA * x + mod[2:3] * y, lng, lnb)
    h2 = x1 * (1.0 + mod[4:5]) + mod[3:4]
    hh, hl = _split(h2)
    logits = _dot3(hh, hl, rw_hi, rw_lo) + rb
    idx, gates, cnt = _top4(logits)
    x1_ref[...] = x1
    h2_ref[...] = h2
    idx_ref[...] = idx
    gate_ref[...] = gates

    @pl.when(pl.program_id(0) == 0)
    def _():
        cnt_ref[...] = jnp.zeros(cnt_ref.shape, F32)

    cnt_ref[...] = cnt_ref[...] + cnt


def _to_row_tiles(ref, x):
    n = x.shape[0]
    for s in range(ROW_TILE):
        ref[pl.ds(s, n, stride=ROW_TILE), :] = x[:, s * LANE:(s + 1) * LANE]


def _from_row_tiles(ref, start, n):
    return jnp.concatenate([ref[pl.ds(start * ROW_TILE + s, n, stride=ROW_TILE), :] for s in range(ROW_TILE)],
                           axis=1)


def _ada_kernel(cs_ref, w_ref, b_ref, o_ref):
    cs = cs_ref[...]
    s = cs * (1.0 / (1.0 + jnp.exp(-cs)))
    sh, sl = _split(s)
    wh, wl = _split(w_ref[0])
    o_ref[0] = _dot3(sh, sl, wh, wl) + b_ref[0]


def _ada(c, c_ctx, ada_w, ada_b):
    d = c.shape[-1]
    nm = ada_w.shape[-1]
    tn = nm // 4
    cs = jnp.zeros((SUBLANE, d), F32).at[0].set(c[0]).at[1].set(c_ctx)
    out = pl.pallas_call(
        _ada_kernel,
        grid=(DEPTH, nm // tn),
        in_specs=[
            pl.BlockSpec((SUBLANE, d), lambda i, j: (0, 0)),
            pl.BlockSpec((1, d, tn), lambda i, j: (i, 0, j)),
            pl.BlockSpec((1, 1, tn), lambda i, j: (i, 0, j)),
        ],
        out_specs=pl.BlockSpec((1, SUBLANE, tn), lambda i, j: (i, 0, j)),
        out_shape=jax.ShapeDtypeStruct((DEPTH, SUBLANE, nm), F32),
        compiler_params=_params("arbitrary", "arbitrary"),
        name="ada",
    )(cs, ada_w, ada_b.reshape(DEPTH, 1, nm))
    return out[:, :2].reshape(DEPTH, 2, N_MOD, d)


def _tile_specs(n_lat_tiles):
    tok = pl.BlockSpec((TM, D_MODEL), lambda t, *_: (t, 0))
    mod = pl.BlockSpec((1, N_MOD, D_MODEL), lambda t, *_: (jnp.where(t >= n_lat_tiles, 1, 0), 0, 0))
    return tok, mod


def _full(shape):
    nd = len(shape)
    return pl.BlockSpec(shape, lambda *_: (0,) * nd)


_COUNT_SHAPE = jax.ShapeDtypeStruct((SUBLANE, LANE), F32)
_COUNT_SPEC = pl.BlockSpec((SUBLANE, LANE), lambda *_: (0, 0))


def _post_out(n_rows):
    shapes = (
        jax.ShapeDtypeStruct((n_rows, D_MODEL), F32),
        jax.ShapeDtypeStruct((n_rows, D_MODEL), F32),
        jax.ShapeDtypeStruct((n_rows, LANE), I32),
        jax.ShapeDtypeStruct((n_rows, LANE), F32),
        _COUNT_SHAPE,
    )
    specs = (
        pl.BlockSpec((TM, D_MODEL), lambda t, *_: (t, 0)),
        pl.BlockSpec((TM, D_MODEL), lambda t, *_: (t, 0)),
        pl.BlockSpec((TM, LANE), lambda t, *_: (t, 0)),
        pl.BlockSpec((TM, LANE), lambda t, *_: (t, 0)),
        _COUNT_SPEC,
    )
    return shapes, specs


def _router_operands(router_w, router_b):
    rw = jnp.zeros((D_MODEL, LANE), F32).at[:, :N_EXPERTS].set(router_w)
    rw_hi = rw.astype(BF16)
    rw_lo = (rw - rw_hi.astype(F32)).astype(BF16)
    rb = jnp.full((1, LANE), NEG, F32).at[0, :N_EXPERTS].set(router_b)
    return rw_hi, rw_lo, rb


def _proj_post_kernel(a_ref, w_ref, b_ref, x_ref, mod_ref, lng_ref, lnb_ref, rwh_ref, rwl_ref, rb_ref, *out_refs):
    y = _dot(a_ref[...], w_ref[...]) + b_ref[...]
    _post(y, x_ref[...], mod_ref[0], lng_ref[...], lnb_ref[...], rwh_ref[...], rwl_ref[...], rb_ref[...], out_refs)


def _proj_post(a, w_out, b_out, x, mods, lng, lnb, router, n_rows, n_lat_tiles):
    k = a.shape[1]
    tok, mod = _tile_specs(n_lat_tiles)
    shapes, specs = _post_out(n_rows)
    return pl.pallas_call(
        _proj_post_kernel,
        grid=(n_rows // TM,),
        in_specs=[
            pl.BlockSpec((TM, k), lambda t: (t, 0)),
            _full((k, D_MODEL)), _full((1, D_MODEL)),
            tok, mod, _full((1, D_MODEL)), _full((1, D_MODEL)),
            _full((D_MODEL, LANE)), _full((D_MODEL, LANE)), _full((1, LANE)),
        ],
        out_specs=specs,
        out_shape=shapes,
        compiler_params=_params("arbitrary"),
        name="proj_post",
    )(a, w_out.astype(BF16), b_out.reshape(1, -1), x, mods, lng.reshape(1, -1), lnb.reshape(1, -1), *router)


def _dft_mats(n):
    jk = np.outer(np.arange(n), np.arange(n)) % n
    ang = 2.0 * np.pi * jk / n
    out = []
    for m in (np.cos(ang), np.sin(ang)):
        m32 = jnp.asarray(m, F32)
        hi = m32.astype(BF16)
        out += [hi, (m32 - hi.astype(F32)).astype(BF16)]
    return out


def _channel_dft(h, cc, sc):
    cw = cc[0].shape[0]
    a_parts, b_parts = [], []
    for g in range(h.shape[1] // cw):
        hh, hl = _split(h[:, g * cw:(g + 1) * cw])
        a_parts.append(_dot3(hh, hl, cc[0][...], cc[1][...]))
        b_parts.append(_dot3(hh, hl, sc[0][...], sc[1][...]))
    return jnp.concatenate(a_parts, axis=1), jnp.concatenate(b_parts, axis=1)


def _column_rows(x_hbm, buf, sem, n_steps):
    j = pl.program_id(0)
    slot = j % 2

    def copy(step, s):
        return pltpu.make_async_copy(x_hbm.at[pl.ds(0, buf.shape[1]), step, :], buf.at[s], sem.at[s])

    @pl.when(j == 0)
    def _():
        copy(0, 0).start()

    @pl.when(j + 1 < n_steps)
    def _():
        copy(j + 1, 1 - slot).start()

    copy(j, slot).wait()
    return buf[slot]


def _fourier1_kernel(x_hbm, mod_ref, cch_ref, ccl_ref, sch_ref, scl_ref, tc_ref, ts_ref, ur_ref, ui_ref, xbuf, xsem):
    mod = mod_ref[0]
    h = _column_rows(x_hbm, xbuf, xsem, pl.num_programs(0)) * (1.0 + mod[1:2]) + mod[0:1]
    a, b = _channel_dft(h, (cch_ref, ccl_ref), (sch_ref, scl_ref))
    tch, tcl = _split(tc_ref[0])
    tsh, tsl = _split(ts_ref[0])
    ah, al = _split(a)
    bh, bl = _split(b)
    ur_ref[...] = _dot3(tch, tcl, ah, al) - _dot3(tsh, tsl, bh, bl)
    ui_ref[...] = -(_dot3(tch, tcl, bh, bl) + _dot3(tsh, tsl, ah, al))


def _fourier2_kernel(ur_ref, ui_ref, c2h_ref, c2l_ref, s2h_ref, s2l_ref, w_ref, b_ref, x_hbm, xc_ref, mod_ref,
                     cch_ref, ccl_ref, sch_ref, scl_ref, cnh_ref, cnl_ref, snh_ref, snl_ref,
                     lng_ref, lnb_ref, rwh_ref, rwl_ref, rb_ref,
                     x1_hbm, h2_hbm, idx_hbm, gate_hbm, cnt_ref, bx1, bh2, bidx, bgate, sem, xbuf, xsem,
                     *, n1, norm_lat, norm_ctx):
    k = pl.program_id(0)
    n2 = ur_ref.shape[1]
    slot = k % 2
    bufs = (bx1, bh2, bidx, bgate)
    outs = (x1_hbm, h2_hbm, idx_hbm, gate_hbm)
    common = (lng_ref[...], lnb_ref[...], rwh_ref[...], rwl_ref[...], rb_ref[...])

    def lat_copies(s, col):
        return [pltpu.make_async_copy(b.at[s, pl.ds(0, n2), :], o.at[pl.ds(0, n2), col, :], sem.at[s])
                for b, o in zip(bufs, outs)]

    def ctx_copies(s):
        return [pltpu.make_async_copy(b.at[s, pl.ds(j * n2, n2), :], o.at[n1 + j], sem.at[s])
                for b, o in zip(bufs, outs) for j in range(xc_ref.shape[0] // n2)]

    @pl.when(k >= 2)
    def _():
        for c in lat_copies(slot, 0):
            c.wait()

    @pl.when(k < n1)
    def _():
        urh, url = _split(ur_ref[0])
        uih, uil = _split(ui_ref[0])
        mixed = (_dot3(c2h_ref[...], c2l_ref[...], urh, url)
                 + _dot3(s2h_ref[...], s2l_ref[...], uih, uil)) * norm_lat
        y = _dot(mixed.astype(BF16), w_ref[...]) + b_ref[...]
        x = _column_rows(x_hbm, xbuf, xsem, n1)
        _post(y, x, mod_ref[0], *common, tuple(b.at[slot, pl.ds(0, n2), :] for b in bufs) + (cnt_ref,))
        for c in lat_copies(slot, k):
            c.start()

    @pl.when(k == n1)
    def _():
        mod = mod_ref[1]
        x = xc_ref[...]
        h = x * (1.0 + mod[1:2]) + mod[0:1]
        a, b = _channel_dft(h, (cch_ref, ccl_ref), (sch_ref, scl_ref))
        ah, al = _split(a)
        bh, bl = _split(b)
        mixed = (_dot3(cnh_ref[...], cnl_ref[...], ah, al) - _dot3(snh_ref[...], snl_ref[...], bh, bl)) * norm_ctx
        y = _dot(mixed.astype(BF16), w_ref[...]) + b_ref[...]
        _post(y, x, mod, *common, tuple(b.at[slot] for b in bufs) + (cnt_ref,))
        for c in ctx_copies(slot):
            c.start()
        for c in lat_copies(1 - slot, 0) + ctx_copies(slot):
            c.wait()


def _fourier_layer(x_lat, x_ctx, mods, w_out, b_out, lng, lnb, router):
    n_lat, d = x_lat.shape
    n_ctx = x_ctx.shape[0]
    n2 = LANE
    n1 = n_lat // n2
    cw = d // FN_GROUPS
    x3 = x_lat.reshape(n1, n2, d)
    any_spec = pl.BlockSpec(memory_space=pl.ANY)
    col_scratch = [pltpu.VMEM((2, n1, d), F32), pltpu.SemaphoreType.DMA((2,))]
    cmat = _dft_mats(cw)
    w_bf = w_out.astype(BF16)
    b2 = b_out.reshape(1, d)
    lng2, lnb2 = lng.reshape(1, d), lnb.reshape(1, d)

    k1 = jnp.arange(n1, dtype=I32)
    pos = jnp.arange(n1, dtype=I32)[None, None, :] * n2 + jnp.arange(n2, dtype=I32)[:, None, None]
    ang = ((k1[None, :, None] * pos) % n_lat).astype(F32) * (2.0 * math.pi / n_lat)
    tc, ts = jnp.cos(ang), jnp.sin(ang)

    mat = _full((cw, cw))
    ur, ui = pl.pallas_call(
        _fourier1_kernel,
        grid=(n2,),
        in_specs=[
            any_spec,
            pl.BlockSpec((1, N_MOD, d), lambda j: (0, 0, 0)),
            mat, mat, mat, mat,
            pl.BlockSpec((1, n1, n1), lambda j: (j, 0, 0)),
            pl.BlockSpec((1, n1, n1), lambda j: (j, 0, 0)),
        ],
        out_specs=(pl.BlockSpec((n1, d), lambda j: (0, j)), pl.BlockSpec((n1, d), lambda j: (0, j))),
        out_shape=(jax.ShapeDtypeStruct((n1, n2 * d), F32), jax.ShapeDtypeStruct((n1, n2 * d), F32)),
        scratch_shapes=col_scratch,
        compiler_params=_params("arbitrary"),
        name="fourier1",
    )(x3, mods, *cmat, tc, ts)

    assert n_ctx % n2 == 0
    t_all = n_lat + n_ctx
    blocks = t_all // n2
    m2 = _dft_mats(n2)
    cn = _dft_mats(n_ctx)
    mat2 = _full((n2, n2))
    matn = _full((n_ctx, n_ctx))
    lat_step = lambda k: jnp.minimum(k, n1 - 1)
    out_shapes = (
        jax.ShapeDtypeStruct((blocks, n2, d), F32),
        jax.ShapeDtypeStruct((blocks, n2, d), F32),
        jax.ShapeDtypeStruct((blocks, n2, LANE), I32),
        jax.ShapeDtypeStruct((blocks, n2, LANE), F32),
        _COUNT_SHAPE,
    )
    outs = pl.pallas_call(
        functools.partial(_fourier2_kernel, n1=n1, norm_lat=1.0 / math.sqrt(n_lat * cw),
                          norm_ctx=1.0 / math.sqrt(n_ctx * cw)),
        grid=(n1 + 1,),
        in_specs=[
            pl.BlockSpec((1, n2, d), lambda k: (lat_step(k), 0, 0)),
            pl.BlockSpec((1, n2, d), lambda k: (lat_step(k), 0, 0)),
            mat2, mat2, mat2, mat2,
            _full((d, d)), _full((1, d)),
            any_spec,
            _full((n_ctx, d)),
            _full((2, N_MOD, d)),
            mat, mat, mat, mat, matn, matn, matn, matn,
            _full((1, d)), _full((1, d)),
            _full((d, LANE)), _full((d, LANE)), _full((1, LANE)),
        ],
        out_specs=(any_spec, any_spec, any_spec, any_spec, _COUNT_SPEC),
        out_shape=out_shapes,
        scratch_shapes=[pltpu.VMEM((2, n_ctx, d), F32), pltpu.VMEM((2, n_ctx, d), F32),
                        pltpu.VMEM((2, n_ctx, LANE), I32), pltpu.VMEM((2, n_ctx, LANE), F32),
                        pltpu.SemaphoreType.DMA((2,))] + col_scratch,
        compiler_params=_params("arbitrary"),
        name="fourier2",
    )(ur.reshape(n1, n2, d), ui.reshape(n1, n2, d), *m2, w_bf, b2, x3, x_ctx, mods, *cmat, *cn, lng2, lnb2,
      *router)
    return (outs[0].reshape(t_all, d), outs[1].reshape(t_all, d), outs[2].reshape(t_all, LANE),
            outs[3].reshape(t_all, LANE), outs[4])


def _qkv_kernel(pos_ref, *refs, n_qk, rms):
    mod_ref, w_ref, b_ref, cos_ref, sin_ref, *rest = refs[N_COMBINE_REFS:]
    if rms:
        gain_ref, ind_ref, indt_ref, x_ref, q_ref, k_ref, v_ref, ybuf, sem = rest
    else:
        x_ref, q_ref, k_ref, v_ref, ybuf, sem = rest
    x = _combine_tile(pos_ref, *refs[:N_COMBINE_REFS], ybuf, sem)
    x_ref[...] = x
    mod = mod_ref[0]
    h = x * (1.0 + mod[1:2]) + mod[0:1]
    y = _dot(h.astype(BF16), w_ref[...]) + b_ref[...]
    qk = y[:, :n_qk]
    if rms:
        sh, sl = _split(qk * qk)
        ms = _dot(sh, ind_ref[...]) + _dot(sl, ind_ref[...])
        mh, ml = _split(ms)
        msb = _dot(mh, indt_ref[...]) + _dot(ml, indt_ref[...])
        qk = qk * lax.rsqrt(msb + RMS_EPS) * gain_ref[...]
    cos = cos_ref[...]
    sin = sin_ref[...]
    even = (lax.broadcasted_iota(I32, cos.shape, 1) & 1) == 0
    parts = []
    for c in range(n_qk // LANE):
        z = qk[:, c * LANE:(c + 1) * LANE]
        swapped = jnp.where(even, pltpu.roll(z, LANE - 1, 1), pltpu.roll(z, 1, 1))
        parts.append(z * cos + swapped * sin)
    nq = q_ref.shape[1]
    q_ref[...] = (jnp.concatenate(parts[:nq // LANE], axis=1) * Q_SCALE).astype(BF16)
    k_ref[...] = jnp.concatenate(parts[nq // LANE:], axis=1).astype(BF16)
    v_ref[...] = y[:, n_qk:].astype(BF16)


def _qkv(pending, mods, w_qkv, b_qkv, cos_t, sin_t, n_q, n_kv, n_lat, q_norm=None, k_norm=None):
    t_all, d = pending[3].shape
    nq, nk = n_q * HEAD_DIM, n_kv * HEAD_DIM
    n_qk, n_all = nq + nk, nq + 2 * nk
    rms = q_norm is not None
    pos, args, in_specs, scratch = _combine_operands(pending, TM, n_lat)
    row = lambda w: pl.BlockSpec((TM, w), lambda t, *_: (t, 0))
    in_specs += [_tile_specs(n_lat // TM)[1], _full((d, n_all)), _full((1, n_all)), row(LANE), row(LANE)]
    args += [mods, w_qkv.astype(BF16), b_qkv.reshape(1, n_all), cos_t, sin_t]
    if rms:
        gain = jnp.concatenate([jnp.tile(q_norm, n_q), jnp.tile(k_norm, n_kv)]).reshape(1, n_qk)
        head = np.arange(n_qk) // HEAD_DIM
        ind = np.zeros((n_qk, LANE), np.float32)
        ind[np.arange(n_qk), head] = 1.0 / HEAD_DIM
        indt = np.zeros((LANE, n_qk), np.float32)
        indt[head, np.arange(n_qk)] = 1.0
        in_specs += [_full((1, n_qk)), _full((n_qk, LANE)), _full((LANE, n_qk))]
        args += [gain, jnp.asarray(ind, BF16), jnp.asarray(indt, BF16)]
    grid_spec = pltpu.PrefetchScalarGridSpec(
        num_scalar_prefetch=1,
        grid=(t_all // TM,),
        in_specs=in_specs,
        out_specs=(row(d), row(nq), row(nk), row(nk)),
        scratch_shapes=scratch,
    )
    return pl.pallas_call(
        functools.partial(_qkv_kernel, n_qk=n_qk, rms=rms),
        grid_spec=grid_spec,
        out_shape=(jax.ShapeDtypeStruct((t_all, d), F32), jax.ShapeDtypeStruct((t_all, nq), BF16),
                   jax.ShapeDtypeStruct((t_all, nk), BF16), jax.ShapeDtypeStruct((t_all, nk), BF16)),
        compiler_params=_params("arbitrary"),
        name="qkv",
    )(pos, *args)


def _rope_tables(n_lat, n_ctx):
    rows = n_lat // GRID_W
    row = jnp.repeat(jnp.arange(rows, dtype=F32), GRID_W)
    col = jnp.tile(jnp.arange(GRID_W, dtype=F32), rows)
    n_freq = HEAD_DIM // 4
    inv = ROPE_THETA ** (-jnp.arange(n_freq, dtype=F32) / n_freq)
    ang = jnp.concatenate([row[:, None] * inv, col[:, None] * inv], axis=-1)
    ang = jnp.concatenate([ang, jnp.zeros((n_ctx, HEAD_DIM // 2), F32)], axis=0)
    cos = jnp.tile(jnp.repeat(jnp.cos(ang), 2, axis=1), (1, LANE // HEAD_DIM))
    sin = jnp.tile(jnp.repeat(jnp.sin(ang), 2, axis=1), (1, LANE // HEAD_DIM))
    sign = jnp.where(jnp.arange(LANE) % 2 == 0, -1.0, 1.0).astype(F32)
    return cos, sin * sign


def _kv_layouts(k, v, n_kv):
    t_all = k.shape[0]
    kh = k.reshape(t_all, n_kv, HEAD_DIM).transpose(1, 0, 2)
    vt = v.reshape(t_all, n_kv, HEAD_DIM).transpose(1, 2, 0)
    pad = jnp.zeros((n_kv, LANE - HEAD_DIM, t_all), BF16).at[:, 0, :].set(1.0)
    return kh, jnp.concatenate([vt, pad], axis=1)


def _flash_kernel(q_ref, k_ref, vt_ref, o_ref, s_even, s_odd, *, grp, tk, n_lat, n_ctx):
    tq = q_ref.shape[0]
    q_t = q_ref[...].astype(F32).T
    qt = jnp.concatenate([q_t[g * HEAD_DIM:(g + 1) * HEAD_DIM] for g in range(grp)], axis=1).astype(BF16)
    cols = qt.shape[1]
    is_ctx = pl.program_id(1) == n_lat // FLASH_TQ
    n_chunks = (n_lat + n_ctx) // tk

    def finish(acc):
        o = acc[:HEAD_DIM] / acc[HEAD_DIM:HEAD_DIM + 1]
        o_t = jnp.concatenate([o[:, g * tq:(g + 1) * tq] for g in range(grp)], axis=0)
        o_ref[...] = o_t.T.astype(BF16)

    groups = [slice(c * MXU_WIDTH, (c + 1) * MXU_WIDTH) for c in range(cols // MXU_WIDTH)]

    def scores(j, buf, g):
        off = pl.multiple_of(j * tk, tk)
        s = _dot(k_ref[0, pl.ds(off, tk), :], qt[:, g])
        buf[:, g] = s
        return jnp.max(s, axis=0, keepdims=True)

    def accumulate(j, buf, g, m, acc, mc):
        off = pl.multiple_of(j * tk, tk)
        m_new = jnp.maximum(m, mc)
        p = jnp.exp2(buf[:, g] - m_new)
        acc = jnp.exp2(m - m_new) * acc + _dot(vt_ref[0, :, pl.ds(off, tk)], p.astype(BF16))
        return m_new, acc

    @pl.when(jnp.logical_not(is_ctx))
    def _():
        def step(j, cur, nxt, carry):
            mcs = [scores(j + 1, nxt, groups[0])]
            out = []
            for gi, (g, (m, acc, mc)) in enumerate(zip(groups, carry)):
                if gi + 1 < len(groups):
                    mcs.append(scores(j + 1, nxt, groups[gi + 1]))
                out.append(accumulate(j, cur, g, m, acc, mc) + (mcs[gi],))
            return out

        def body(jj, carry):
            for pair in range(FLASH_PAIRS):
                j = 2 * (FLASH_PAIRS * jj + pair)
                carry = step(j, s_even, s_odd, carry)
                carry = step(j + 1, s_odd, s_even, carry)
            return carry

        init = [(jnp.full((1, MXU_WIDTH), NEG, F32), jnp.zeros((LANE, MXU_WIDTH), F32), scores(0, s_even, g))
                for g in groups]
        carry = lax.fori_loop(0, (n_chunks - 1) // (2 * FLASH_PAIRS), body, init)
        finish(jnp.concatenate([accumulate(n_chunks - 1, s_even, g, m, acc, mc)[1]
                                for g, (m, acc, mc) in zip(groups, carry)], axis=1))

    @pl.when(is_ctx)
    def _():
        s = _dot(k_ref[0, n_lat:n_lat + n_ctx, :], qt)
        p = jnp.exp2(s - jnp.max(s, axis=0, keepdims=True))
        finish(_dot(vt_ref[0, :, n_lat:n_lat + n_ctx], p.astype(BF16)))


def _full_attention(q, k, v, n_lat, n_ctx):
    t_all, nq = q.shape
    n_kv = k.shape[1] // HEAD_DIM
    grp = nq // HEAD_DIM // n_kv
    nt = t_all // FLASH_TQ
    cols = grp * FLASH_TQ
    assert n_ctx == FLASH_TQ and t_all % FLASH_TK == 0 and (t_all // FLASH_TK - 1) % (2 * FLASH_PAIRS) == 0
    kh, vt = _kv_layouts(k, v, n_kv)
    return pl.pallas_call(
        functools.partial(_flash_kernel, grp=grp, tk=FLASH_TK, n_lat=n_lat, n_ctx=n_ctx),
        grid=(n_kv, nt),
        in_specs=[
            pl.BlockSpec((FLASH_TQ, grp * HEAD_DIM), lambda h, i: (i, h)),
            pl.BlockSpec((1, t_all, HEAD_DIM), lambda h, i: (h, 0, 0)),
            pl.BlockSpec((1, LANE, t_all), lambda h, i: (h, 0, 0)),
        ],
        out_specs=pl.BlockSpec((FLASH_TQ, grp * HEAD_DIM), lambda h, i: (i, h)),
        out_shape=jax.ShapeDtypeStruct((t_all, nq), BF16),
        scratch_shapes=[pltpu.VMEM((FLASH_TK, cols), F32), pltpu.VMEM((FLASH_TK, cols), F32)],
        compiler_params=_params("arbitrary", "arbitrary"),
        name="flash",
    )(q, kh, vt)


def _window_kernel(q_ref, kp_ref, kc_ref, kn_ref, kx_ref, vp_ref, vc_ref, vn_ref, vx_ref, sink_ref, o_ref,
                   *, grp, nb):
    i = pl.program_id(0)
    gw = grp * HEAD_DIM
    n_kv = kp_ref.shape[0]
    q_t = q_ref[...].astype(F32).T
    cols = grp * Q_BLOCK
    kj = lax.broadcasted_iota(I32, (Q_BLOCK, cols), 0)
    qi = lax.broadcasted_iota(I32, (Q_BLOCK, cols), 1) & (Q_BLOCK - 1)
    lo = jnp.where(i > 0, qi, Q_BLOCK)
    hi = jnp.where(i < nb - 1, qi, -1)
    def scores(h):
        qt = jnp.concatenate([q_t[h * gw + g * HEAD_DIM:h * gw + (g + 1) * HEAD_DIM] for g in range(grp)],
                             axis=1).astype(BF16)
        return _dot(jnp.concatenate([kp_ref[h], kc_ref[h], kn_ref[h], kx_ref[h]], axis=0), qt)

    outs = []
    pending = [scores(0)]
    for h in range(n_kv):
        if h + 1 < n_kv:
            pending.append(scores(h + 1))
        s = pending[h]
        sp = jnp.where(kj >= lo, s[:Q_BLOCK], NEG)
        sn = jnp.where(kj <= hi, s[2 * Q_BLOCK:3 * Q_BLOCK], NEG)
        s = jnp.concatenate([sp, s[Q_BLOCK:2 * Q_BLOCK], sn, s[3 * Q_BLOCK:]], axis=0)
        sink = sink_ref[h]
        m = jnp.maximum(jnp.max(s, axis=0, keepdims=True), sink)
        p = jnp.exp2(s - m).astype(BF16)
        acc = _dot(jnp.concatenate([vp_ref[h], vc_ref[h], vn_ref[h], vx_ref[h]], axis=1), p)
        o = acc[:HEAD_DIM] / (acc[HEAD_DIM:HEAD_DIM + 1] + jnp.exp2(sink - m))
        outs += [o[:, g * Q_BLOCK:(g + 1) * Q_BLOCK] for g in range(grp)]
    o_ref[...] = jnp.concatenate(outs, axis=0).T.astype(BF16)


def _window_attention(q, k, v, sink, n_lat, n_ctx):
    nq = q.shape[1]
    n_kv = k.shape[1] // HEAD_DIM
    grp = nq // HEAD_DIM // n_kv
    nb = n_lat // Q_BLOCK
    cb = n_lat // n_ctx
    assert WINDOW == Q_BLOCK
    kh, vt = _kv_layouts(k, v, n_kv)
    sink_cols = jnp.repeat(sink.reshape(n_kv, 1, grp) * LOG2E, Q_BLOCK, axis=2)
    kspec = lambda f: pl.BlockSpec((n_kv, Q_BLOCK, HEAD_DIM), lambda i: (0, f(i), 0))
    vspec = lambda f: pl.BlockSpec((n_kv, LANE, Q_BLOCK), lambda i: (0, 0, f(i)))
    prev = lambda i: jnp.maximum(i - 1, 0)
    cur = lambda i: i
    nxt = lambda i: jnp.minimum(i + 1, nb - 1)
    return pl.pallas_call(
        functools.partial(_window_kernel, grp=grp, nb=nb),
        grid=(nb,),
        in_specs=[
            pl.BlockSpec((Q_BLOCK, nq), lambda i: (i, 0)),
            kspec(prev), kspec(cur), kspec(nxt),
            pl.BlockSpec((n_kv, n_ctx, HEAD_DIM), lambda i: (0, cb, 0)),
            vspec(prev), vspec(cur), vspec(nxt),
            pl.BlockSpec((n_kv, LANE, n_ctx), lambda i: (0, 0, cb)),
            _full((n_kv, 1, grp * Q_BLOCK)),
        ],
        out_specs=pl.BlockSpec((Q_BLOCK, nq), lambda i: (i, 0)),
        out_shape=jax.ShapeDtypeStruct((n_lat, nq), BF16),
        compiler_params=_params("arbitrary"),
        name="window",
    )(q, kh, kh, kh, kh, vt, vt, vt, vt, sink_cols)


def _gmlp_kernel(pos_ref, *refs):
    (mod_ref, win_ref, bin_ref, vg_ref, vb_ref, ws_ref, bs_ref, wout_ref, bout_ref,
     lng_ref, lnb_ref, rwh_ref, rwl_ref, rb_ref, *rest) = refs[N_COMBINE_REFS:]
    out_refs, (ybuf, sem) = rest[:-2], rest[-2:]
    mod = mod_ref[0]
    x = _combine_tile(pos_ref, *refs[:N_COMBINE_REFS], ybuf, sem)
    h = x * (1.0 + mod[1:2]) + mod[0:1]
    z = _dot(h.astype(BF16), win_ref[...]) + bin_ref[...]
    z = 0.5 * z * (1.0 + lax.erf(z * (2.0 ** -0.5)))
    half = z.shape[1] // 2
    u = z[:, :half]
    v = _layer_norm(z[:, half:], vg_ref[...], vb_ref[...]).astype(BF16)
    cw = half // GM_GROUPS
    chunks = []
    for c in range(x.shape[0] // GM_CHUNK):
        vc = v[c * GM_CHUNK:(c + 1) * GM_CHUNK]
        chunks.append(jnp.concatenate(
            [_dot(ws_ref[g], vc[:, g * cw:(g + 1) * cw]) + bs_ref[g] for g in range(GM_GROUPS)], axis=1))
    gated = u * jnp.concatenate(chunks, axis=0)
    y = _dot(gated.astype(BF16), wout_ref[...]) + bout_ref[...]
    _post(y, x, mod, lng_ref[...], lnb_ref[...], rwh_ref[...], rwl_ref[...], rb_ref[...], out_refs)


def _gmlp_layer(pending, mods, w_in, b_in, vg, vb, w_s, b_s, w_out, b_out, lng, lnb, router, n_lat):
    t_all, d = pending[3].shape
    dffn = w_in.shape[1]
    half = dffn // 2
    cw = half // GM_GROUPS
    shapes, specs = _post_out(t_all)
    bs_full = jnp.broadcast_to(b_s[:, :, None], (GM_GROUPS, GM_CHUNK, cw))
    pos, args, in_specs, scratch = _combine_operands(pending, TM, n_lat)
    in_specs += [
        _tile_specs(n_lat // TM)[1], _full((d, dffn)), _full((1, dffn)), _full((1, half)), _full((1, half)),
        _full((GM_GROUPS, GM_CHUNK, GM_CHUNK)), _full((GM_GROUPS, GM_CHUNK, cw)),
        _full((half, d)), _full((1, d)), _full((1, d)), _full((1, d)),
        _full((d, LANE)), _full((d, LANE)), _full((1, LANE)),
    ]
    args += [mods, w_in.astype(BF16), b_in.reshape(1, dffn), vg.reshape(1, half), vb.reshape(1, half),
             w_s.astype(BF16), bs_full, w_out.astype(BF16), b_out.reshape(1, d), lng.reshape(1, d),
             lnb.reshape(1, d), *router]
    grid_spec = pltpu.PrefetchScalarGridSpec(
        num_scalar_prefetch=1,
        grid=(t_all // TM,),
        in_specs=in_specs,
        out_specs=specs,
        scratch_shapes=scratch,
    )
    return pl.pallas_call(
        _gmlp_kernel,
        grid_spec=grid_spec,
        out_shape=shapes,
        compiler_params=_params("arbitrary"),
        name="gmlp",
    )(pos, *args)


COMB_TM = 128


def _route_meta(count_blk, n_tok):
    n_assign = n_tok * TOP_K
    n_blocks = -(-n_assign // MOE_BLOCK) + N_EXPERTS
    ids = jnp.arange(N_EXPERTS, dtype=I32)
    counts = count_blk[0, :N_EXPERTS].astype(I32)
    padded = (counts + MOE_BLOCK - 1) // MOE_BLOCK * MOE_BLOCK
    ends_pad = jnp.cumsum(padded)
    base = ends_pad - padded
    starts = jnp.arange(n_blocks, dtype=I32) * MOE_BLOCK
    block_expert = jnp.minimum(jnp.sum((ends_pad[None, :] <= starts[:, None]).astype(I32), axis=1), N_EXPERTS - 1)
    last_blk = jnp.where(padded > 0, ends_pad - MOE_BLOCK, -1)
    tail = ends_pad[-1] + ids * MOE_BLOCK
    tail = jnp.where(tail < n_blocks * MOE_BLOCK, tail, -1)
    base_b = jnp.broadcast_to(base.astype(F32)[:, None], (N_EXPERTS, LANE))
    return block_expert.astype(I32), base_b, jnp.concatenate([last_blk, tail]).astype(I32), n_blocks


def _rank_kernel(idx_ref, base_ref, upper_ref, dest_ref, run_ref):
    @pl.when(pl.program_id(0) == 0)
    def _():
        run_ref[...] = jnp.zeros(run_ref.shape, F32)

    eid = lax.broadcasted_iota(I32, (N_EXPERTS, TM), 0)
    onehots = [(eid == idx_ref[k:k + 1, :]).astype(F32) for k in range(TOP_K)]
    cnt = onehots[0] + onehots[1] + onehots[2] + onehots[3]
    before = _dot(cnt.astype(BF16), upper_ref[...])
    slot = base_ref[:, :1] + run_ref[:, :1] + before
    for k in range(TOP_K):
        rows = (jnp.sum(onehots[k] * slot, axis=0, keepdims=True) * ROW_TILE).astype(I32)
        for j in range(TM // COMB_TM):
            dest_ref[j, k:k + 1, :] = rows[:, j * COMB_TM:(j + 1) * COMB_TM]
    run_ref[...] = run_ref[...] + jnp.sum(cnt, axis=1, keepdims=True)


def _rank(top_idx, base_b, n_tok):
    per_step = TM // COMB_TM
    idx_t = top_idx[:n_tok, :TOP_K].T
    upper = jnp.asarray(np.triu(np.ones((TM, TM), np.float32), 1), BF16)
    return pl.pallas_call(
        _rank_kernel,
        grid=(n_tok // TM,),
        in_specs=[pl.BlockSpec((TOP_K, TM), lambda t: (0, t)), _full((N_EXPERTS, LANE)), _full((TM, TM))],
        out_specs=pl.BlockSpec((per_step, TOP_K, COMB_TM), lambda t: (t, 0, 0)),
        out_shape=jax.ShapeDtypeStruct((n_tok // COMB_TM, TOP_K, COMB_TM), I32),
        scratch_shapes=[pltpu.VMEM((N_EXPERTS, LANE), F32)],
        compiler_params=_params("arbitrary"),
        name="rank",
    )(idx_t, base_b, upper)


def _dispatch_kernel(dest_ref, zrow_ref, h_ref, xs_hbm, hbuf, zbuf, sem, zsem, *, n_tiles):
    t = pl.program_id(0)
    slot = t % 2
    n_zero = 2 * N_EXPERTS

    def zero_copy(j):
        row = pl.multiple_of(zrow_ref[j] * ROW_TILE, MOE_BLOCK * ROW_TILE)
        return pltpu.make_async_copy(zbuf, xs_hbm.at[pl.ds(row, MOE_BLOCK * ROW_TILE), :], zsem.at[0])

    def wait_rows(s):
        for _ in range(TOP_K):
            pltpu.make_async_copy(hbuf.at[s], xs_hbm.at[pl.ds(0, COMB_TM * ROW_TILE), :], sem.at[s]).wait()

    @pl.when(t == 0)
    def _():
        zbuf[...] = jnp.zeros(zbuf.shape, F32)
        for j in range(n_zero):
            @pl.when(zrow_ref[j] >= 0)
            def _():
                zero_copy(j).start()
        for j in range(n_zero):
            @pl.when(zrow_ref[j] >= 0)
            def _():
                zero_copy(j).wait()

    @pl.when(t >= 2)
    def _():
        wait_rows(slot)

    _to_row_tiles(hbuf.at[slot], h_ref[...])
    for k in range(TOP_K):
        for r in range(COMB_TM):
            row = pl.multiple_of(dest_ref[(t * TOP_K + k) * COMB_TM + r], ROW_TILE)
            pltpu.make_async_copy(hbuf.at[slot, pl.ds(r * ROW_TILE, ROW_TILE), :],
                                  xs_hbm.at[pl.ds(row, ROW_TILE), :], sem.at[slot]).start(priority=r % 2)

    @pl.when(t == n_tiles - 1)
    def _():
        wait_rows(slot)
        if n_tiles > 1:
            wait_rows(1 - slot)


def _dispatch(h2, dest, zrows, n_tok, n_blocks):
    d = h2.shape[1]
    n_tiles = n_tok // COMB_TM
    grid_spec = pltpu.PrefetchScalarGridSpec(
        num_scalar_prefetch=2,
        grid=(n_tiles,),
        in_specs=[pl.BlockSpec((COMB_TM, d), lambda t, dest, zr: (t, 0))],
        out_specs=pl.BlockSpec(memory_space=pl.ANY),
        scratch_shapes=[pltpu.VMEM((2, COMB_TM * ROW_TILE, LANE), F32),
                        pltpu.VMEM((MOE_BLOCK * ROW_TILE, LANE), F32),
                        pltpu.SemaphoreType.DMA((2,)), pltpu.SemaphoreType.DMA((1,))],
    )
    return pl.pallas_call(
        functools.partial(_dispatch_kernel, n_tiles=n_tiles),
        grid_spec=grid_spec,
        out_shape=jax.ShapeDtypeStruct((n_blocks * MOE_BLOCK * ROW_TILE, LANE), F32),
        compiler_params=_params("arbitrary"),
        name="dispatch",
    )(dest.reshape(-1), zrows, h2)


EXP_CHUNK = MXU_WIDTH


def _expert_kernel(be_ref, x_ref, wgu_ref, bgu_ref, wd_ref, bd_ref, sel_ref, y_ref,
                   wg_s, wl_s, wd_s, bg_s, bl_s):
    b = pl.program_id(0)
    ff = wg_s.shape[1]
    half = EXP_CHUNK // 2

    @pl.when(jnp.logical_or(b == 0, be_ref[b] != be_ref[jnp.maximum(b - 1, 0)]))
    def _():
        for c in range(2 * ff // EXP_CHUNK):
            w = _dot(wgu_ref[0, 0, :, c * EXP_CHUNK:(c + 1) * EXP_CHUNK].astype(BF16), sel_ref[...])
            wg_s[:, c * half:(c + 1) * half] = w[:, :half].astype(BF16)
            wl_s[:, c * half:(c + 1) * half] = w[:, half:].astype(BF16)
            bh, bl = _split(jnp.broadcast_to(bgu_ref[0, 0, :, c * EXP_CHUNK:(c + 1) * EXP_CHUNK],
                                             (SUBLANE, EXP_CHUNK)))
            bias = _dot(bh, sel_ref[...]) + _dot(bl, sel_ref[...])
            bg_s[:, c * half:(c + 1) * half] = bias[:, :half]
            bl_s[:, c * half:(c + 1) * half] = bias[:, half:]
        wd_s[...] = wd_ref[0, 0].astype(BF16)

    x = _from_row_tiles(x_ref, 0, MOE_BLOCK).astype(BF16)
    glu = jnp.minimum(_dot(x, wg_s[...]) + bg_s[0:1, :], SWIGLU_LIMIT)
    lin = jnp.clip(_dot(x, wl_s[...]) + bl_s[0:1, :], -SWIGLU_LIMIT, SWIGLU_LIMIT)
    act = glu * (1.0 / (1.0 + jnp.exp(-SWIGLU_ALPHA * glu))) * (lin + 1.0)
    _to_row_tiles(y_ref, _dot(act.astype(BF16), wd_s[...]) + bd_ref[0, 0])


def _experts(xs, block_expert, n_blocks, w_gate_up, b_gate_up, w_down, b_down, layer):
    d = w_down.shape[3]
    ff = w_down.shape[2]
    half = EXP_CHUNK // 2
    sel = np.zeros((EXP_CHUNK, EXP_CHUNK), np.float32)
    sel[2 * np.arange(half), np.arange(half)] = 1.0
    sel[2 * np.arange(half) + 1, half + np.arange(half)] = 1.0
    wspec = lambda r, c: pl.BlockSpec((1, 1, r, c), lambda b, be: (layer, be[b], 0, 0))
    sspec = pl.BlockSpec((EXP_CHUNK, EXP_CHUNK), lambda b, be: (0, 0))
    grid_spec = pltpu.PrefetchScalarGridSpec(
        num_scalar_prefetch=1,
        grid=(n_blocks,),
        in_specs=[pl.BlockSpec((MOE_BLOCK * ROW_TILE, LANE), lambda b, be: (b, 0)), wspec(d, 2 * ff),
                  wspec(1, 2 * ff), wspec(ff, d), wspec(1, d), sspec],
        out_specs=pl.BlockSpec((MOE_BLOCK * ROW_TILE, LANE), lambda b, be: (b, 0)),
        scratch_shapes=[pltpu.VMEM((d, ff), BF16), pltpu.VMEM((d, ff), BF16), pltpu.VMEM((ff, d), BF16),
                        pltpu.VMEM((SUBLANE, ff), F32), pltpu.VMEM((SUBLANE, ff), F32)],
    )
    return pl.pallas_call(
        _expert_kernel,
        grid_spec=grid_spec,
        out_shape=jax.ShapeDtypeStruct((n_blocks * MOE_BLOCK * ROW_TILE, LANE), F32),
        compiler_params=_params("arbitrary"),
        name="experts",
    )(block_expert, xs, w_gate_up, b_gate_up[:, :, None, :], w_down, b_down[:, :, None, :],
      jnp.asarray(sel, BF16))


def _gather_rows(idx_ref, base, n, src_hbm, dst, sem):
    for r in range(n):
        row = pl.multiple_of(idx_ref[base + r], ROW_TILE)
        pltpu.make_async_copy(src_hbm.at[pl.ds(row, ROW_TILE), :], dst.at[pl.ds(r * ROW_TILE, ROW_TILE), :],
                              sem).start(priority=r % 2)


N_COMBINE_REFS = 6


def _combine_tile(pos_ref, y_hbm, gate_ref, x1_ref, mod_ref, lng_ref, lnb_ref, ybuf, sem):
    t = pl.program_id(0)
    slot = t % 2
    tm, d = x1_ref.shape
    n = tm * TOP_K

    @pl.when(t == 0)
    def _():
        _gather_rows(pos_ref, 0, n, y_hbm, ybuf.at[0], sem.at[0])

    @pl.when(t + 1 < pl.num_programs(0))
    def _():
        _gather_rows(pos_ref, (t + 1) * n, n, y_hbm, ybuf.at[1 - slot], sem.at[1 - slot])

    pltpu.make_async_copy(y_hbm.at[pl.ds(0, n * ROW_TILE), :], ybuf.at[slot], sem.at[slot]).wait()
    gates = gate_ref[...]
    parts = []
    for j in range(tm // COMB_TM):
        f = jnp.zeros((COMB_TM, d), F32)
        for k in range(TOP_K):
            rows = _from_row_tiles(ybuf.at[slot], (j * TOP_K + k) * COMB_TM, COMB_TM)
            f = f + gates[j * COMB_TM:(j + 1) * COMB_TM, k:k + 1] * rows
        parts.append(f)
    f = parts[0] if len(parts) == 1 else jnp.concatenate(parts, axis=0)
    return _layer_norm(ALPHA * x1_ref[...] + mod_ref[0][5:6] * f, lng_ref[...], lnb_ref[...])


def _combine_operands(pending, tm, n_lat):
    yb, dest, gates, x1, mods, lng, lnb = pending
    d = x1.shape[1]
    n_lat_tiles = n_lat // tm
    specs = [
        pl.BlockSpec(memory_space=pl.ANY),
        pl.BlockSpec((tm, LANE), lambda t, *_: (t, 0)),
        pl.BlockSpec((tm, d), lambda t, *_: (t, 0)),
        pl.BlockSpec((1, N_MOD, d), lambda t, *_: (jnp.where(t >= n_lat_tiles, 1, 0), 0, 0)),
        _full((1, d)), _full((1, d)),
    ]
    scratch = [pltpu.VMEM((2, tm * TOP_K * ROW_TILE, LANE), F32), pltpu.SemaphoreType.DMA((2,))]
    return dest.reshape(-1), [yb, gates, x1, mods, lng.reshape(1, d), lnb.reshape(1, d)], specs, scratch


def _combine_kernel(pos_ref, *refs):
    o_ref, ybuf, sem = refs[N_COMBINE_REFS:]
    o_ref[...] = _combine_tile(pos_ref, *refs[:N_COMBINE_REFS], ybuf, sem)


def _combine(pending, n_tok, n_lat):
    d = pending[3].shape[1]
    pos, args, specs, scratch = _combine_operands(pending, COMB_TM, n_lat)
    grid_spec = pltpu.PrefetchScalarGridSpec(
        num_scalar_prefetch=1,
        grid=(n_tok // COMB_TM,),
        in_specs=specs,
        out_specs=pl.BlockSpec((COMB_TM, d), lambda t, *_: (t, 0)),
        scratch_shapes=scratch,
    )
    return pl.pallas_call(
        _combine_kernel,
        grid_spec=grid_spec,
        out_shape=jax.ShapeDtypeStruct((n_tok, d), F32),
        compiler_params=_params("arbitrary"),
        name="combine",
    )(pos, *args)


def _moe_layer(x1, h2, top_idx, gates, counts, mods, lng, lnb, w_gate_up, b_gate_up, w_down, b_down, layer, n_tok):
    block_expert, base_b, zrows, n_blocks = _route_meta(counts, n_tok)
    dest = _rank(top_idx, base_b, n_tok)
    xs = _dispatch(h2, dest, zrows, n_tok, n_blocks)
    yb = _experts(xs, block_expert, n_blocks, w_gate_up, b_gate_up, w_down, b_down, layer)
    return yb, dest, gates, x1, mods, lng, lnb


def kernel(x, c, ctx, c_ctx, ada_w, ada_b, ln_mix_g, ln_mix_b, ln_ffn_g, ln_ffn_b, fn_w_out, fn_b_out, fa_w_qkv, fa_b_qkv, fa_q_norm, fa_k_norm, fa_w_out, fa_b_out, gm_w_in, gm_b_in, gm_v_norm_g, gm_v_norm_b, gm_w_s, gm_b_s, gm_w_out, gm_b_out, wa_w_qkv, wa_b_qkv, wa_sink, wa_w_out, wa_b_out, router_w, router_b, exp_w_gate_up, exp_b_gate_up, exp_w_down, exp_b_down):
    bsz, n_lat, d = x.shape
    n_ctx = ctx.shape[1]
    assert bsz == 1 and d == D_MODEL and n_lat == LANE * LANE and n_lat % n_ctx == 0 and n_ctx % TM == 0
    t_all = n_lat + n_ctx
    n_lat_tiles = n_lat // TM
    mods_all = _ada(c, c_ctx, ada_w, ada_b)
    cos_t, sin_t = _rope_tables(n_lat, n_ctx)

    for i in range(DEPTH):
        kind, j = i % 4, i // 4
        last = i == DEPTH - 1
        n_tok = n_lat if last else t_all
        mods = mods_all[i]
        router = _router_operands(router_w[i], router_b[i])
        lng, lnb = ln_mix_g[i], ln_mix_b[i]
        if kind == 0:
            post = _fourier_layer(x[0], ctx[0], mods, fn_w_out[j], fn_b_out[j], lng, lnb, router)
        elif kind == 1:
            x_all, q, k, v = _qkv(pending, mods, fa_w_qkv[j], fa_b_qkv[j], cos_t, sin_t, FA_Q_HEADS,
                                  FA_KV_HEADS, n_lat, fa_q_norm[j], fa_k_norm[j])
            o = _full_attention(q, k, v, n_lat, n_ctx)
            post = _proj_post(o, fa_w_out[j], fa_b_out[j], x_all, mods, lng, lnb, router, n_tok, n_lat_tiles)
        elif kind == 2:
            post = _gmlp_layer(pending, mods, gm_w_in[j], gm_b_in[j], gm_v_norm_g[j], gm_v_norm_b[j],
                               gm_w_s[j], gm_b_s[j], gm_w_out[j], gm_b_out[j], lng, lnb, router, n_lat)
        else:
            x_all, q, k, v = _qkv(pending, mods, wa_w_qkv[j], wa_b_qkv[j], cos_t, sin_t, WA_Q_HEADS,
                                  WA_KV_HEADS, n_lat)
            o = _window_attention(q, k, v, wa_sink[j], n_lat, n_ctx)
            post = _proj_post(o, wa_w_out[j], wa_b_out[j], x_all, mods, lng, lnb, router, n_tok, n_lat_tiles)
        pending = _moe_layer(*post, mods, ln_ffn_g[i], ln_ffn_b[i], exp_w_gate_up, exp_b_gate_up, exp_w_down,
                             exp_b_down, i, n_tok)
    return _combine(pending, n_lat, n_lat)[None]
```
